```python
import jax, jax.numpy as jnp
from jax import lax
import numpy as np

D_MODEL = 1024
BATCH = 4
SEQ = 4096
DEPTH = 2

GRID_W = 64
CTX_LEN = 256
N_MOD = 6
EPS = 1e-6

HEAD_DIM = 64
N_Q_HEADS = 8
N_KV_HEADS = 2
Q_PER_KV = N_Q_HEADS // N_KV_HEADS
ATT_W = N_Q_HEADS * HEAD_DIM
KV_W = N_KV_HEADS * HEAD_DIM
WINDOW = 128
ATT_BLOCK = 128
ROPE_PAIRS = HEAD_DIM // 4
ROPE_BASE = 10000.0

SC_W = D_MODEL // 2
SC_WIDTH = 3
SC_PAD = (SC_WIDTH // 2, SC_WIDTH - 1 - SC_WIDTH // 2)
IN_AB = 2 * KV_W + ATT_W + 3 * SC_W
AB_SPLITS = [KV_W, 2 * KV_W, 2 * KV_W + ATT_W, 2 * KV_W + ATT_W + SC_W, 2 * KV_W + ATT_W + 2 * SC_W]

POOL_W = D_MODEL // 2
POOL_WINDOWS = (2, 4, 8, 16)
POOL_GROUPS = len(POOL_WINDOWS)
POOL_GW = POOL_W // POOL_GROUPS

LRU_W = D_MODEL // 2
LRU_BLOCKS = 8
LRU_BW = LRU_W // LRU_BLOCKS
LRU_CONV = 4
LRU_PAD = (LRU_CONV // 2, LRU_CONV - 1 - LRU_CONV // 2)
LRU_C = 8.0
IN_CD = 2 * LRU_W + POOL_W
CD_SPLITS = [LRU_W, 2 * LRU_W]

MIX_W = D_MODEL

N_GROUPS = 4
EXPERTS_PER_GROUP = 8
N_EXPERTS = N_GROUPS * EXPERTS_PER_GROUP
TOP_K = 2
D_EXPERT = D_MODEL // 2
MOE_BLOCK = 128

kernel_name = "hybrid_diffusion_prefix_backbone"


def rmsnorm(x, g):
    xf = x.astype(jnp.float32)
    y = xf * lax.rsqrt(jnp.mean(xf * xf, axis=-1, keepdims=True) + EPS)
    return (y * g.astype(jnp.float32)).astype(x.dtype)


def modulate(x, g, shift, scale):
    return rmsnorm(x, g) * (1 + scale) + shift


def depthwise_conv(u, w, b, pad):
    y = lax.conv_general_dilated(u, w[:, None, :], window_strides=(1,), padding=[pad],
                                 dimension_numbers=("NWC", "WIO", "NWC"),
                                 feature_group_count=u.shape[-1])
    return y + b


def axial_rope_tables(n_tokens):
    rows = n_tokens // GRID_W
    row = jnp.repeat(jnp.arange(rows), GRID_W)
    col = jnp.tile(jnp.arange(GRID_W), rows)
    inv_freq = ROPE_BASE ** (-jnp.arange(ROPE_PAIRS, dtype=jnp.float32) / ROPE_PAIRS)
    ang = jnp.stack([row, col], axis=-1).astype(jnp.float32)[..., None] * inv_freq
    return jnp.cos(ang), jnp.sin(ang)


def apply_rope(x, cos, sin):
    shp = x.shape
    bshape = (1, shp[1]) + (1,) * (x.ndim - 3) + cos.shape[1:]
    cos, sin = cos.reshape(bshape), sin.reshape(bshape)
    xr = x.astype(jnp.float32).reshape(shp[:-1] + (2, 2, ROPE_PAIRS))
    x1, x2 = xr[..., 0, :], xr[..., 1, :]
    out = jnp.stack([x1 * cos - x2 * sin, x2 * cos + x1 * sin], axis=-2)
    return out.reshape(shp).astype(x.dtype)


def window_attention(q, k, v, kc, vc, sink):
    Bn, L = q.shape[:2]
    nb = L // ATT_BLOCK
    qb = q.reshape(Bn, nb, ATT_BLOCK, N_KV_HEADS, Q_PER_KV, HEAD_DIM)

    def band(t):
        tp = jnp.pad(t, ((0, 0), (ATT_BLOCK, ATT_BLOCK), (0, 0), (0, 0)))
        tp = tp.reshape(Bn, nb + 2, ATT_BLOCK, N_KV_HEADS, HEAD_DIM)
        return jnp.concatenate([tp[:, :-2], tp[:, 1:-1], tp[:, 2:]], axis=2)

    kb, vb = band(k), band(v)
    scale = HEAD_DIM ** -0.5
    s_loc = jnp.einsum("bnqhgd,bnkhd->bnhgqk", qb, kb).astype(jnp.float32) * scale
    s_ctx = jnp.einsum("bnqhgd,bchd->bnhgqc", qb, kc).astype(jnp.float32) * scale
    i = jnp.arange(ATT_BLOCK)[:, None]
    j = jnp.arange(3 * ATT_BLOCK)[None, :]
    kpos = (jnp.arange(nb)[:, None, None] - 1) * ATT_BLOCK + j
    valid = (jnp.abs(i - j + ATT_BLOCK) <= WINDOW) & (kpos >= 0) & (kpos < L)
    s_loc = jnp.where(valid[None, :, None, None], s_loc, -jnp.inf)
    snk = jnp.broadcast_to(sink.astype(jnp.float32).reshape(1, 1, N_KV_HEADS, Q_PER_KV, 1, 1),
                           s_loc.shape[:-1] + (1,))
    p = jax.nn.softmax(jnp.concatenate([s_loc, s_ctx, snk], axis=-1), axis=-1).astype(v.dtype)
    n_loc, n_ctx = 3 * ATT_BLOCK, kc.shape[1]
    o = (jnp.einsum("bnhgqk,bnkhd->bnqhgd", p[..., :n_loc], vb)
         + jnp.einsum("bnhgqc,bchd->bnqhgd", p[..., n_loc:n_loc + n_ctx], vc))
    return o.reshape(Bn, L, ATT_W)


def context_attention(qc, kc, vc, sink):
    Bn, Lc = qc.shape[:2]
    s = jnp.einsum("bqhgd,bkhd->bhgqk", qc, kc).astype(jnp.float32) * HEAD_DIM ** -0.5
    snk = jnp.broadcast_to(sink.astype(jnp.float32).reshape(1, N_KV_HEADS, Q_PER_KV, 1, 1),
                           s.shape[:-1] + (1,))
    p = jax.nn.softmax(jnp.concatenate([s, snk], axis=-1), axis=-1).astype(vc.dtype)
    o = jnp.einsum("bhgqk,bkhd->bqhgd", p[..., :-1], vc)
    return o.reshape(Bn, Lc, ATT_W)


def multiscale_pool(u, w_pool, scale):
    Bn, L, _ = u.shape
    uf = u.astype(jnp.float32)
    cs = jnp.pad(lax.cumsum(uf, axis=1), ((0, 0), (1, 0), (0, 0)))
    t = jnp.arange(L)
    means = []
    for g, w in enumerate(POOL_WINDOWS):
        lo = jnp.clip(t - w // 2, 0, L)
        hi = jnp.clip(t + w - w // 2, 0, L)
        csg = cs[..., g * POOL_GW:(g + 1) * POOL_GW]
        means.append((csg[:, hi] - csg[:, lo]) / (hi - lo).astype(jnp.float32)[None, :, None])
    d = (jnp.concatenate(means, axis=-1) - uf).astype(u.dtype)
    d = d.reshape(Bn, L, POOL_GROUPS, POOL_GW)
    y = jnp.einsum("blgi,gij->blgj", d, w_pool).reshape(Bn, L, POOL_W)
    return y * scale


def rglru_coeffs(u, wa, ba, wx, bx, lam):
    ub = u.reshape(u.shape[:-1] + (LRU_BLOCKS, LRU_BW))
    r = jax.nn.sigmoid((jnp.einsum("blhi,hij->blhj", ub, wa).reshape(u.shape) + ba).astype(jnp.float32))
    gi = jax.nn.sigmoid((jnp.einsum("blhi,hij->blhj", ub, wx).reshape(u.shape) + bx).astype(jnp.float32))
    log_a = -LRU_C * jax.nn.softplus(-lam.astype(jnp.float32)) * r
    a = jnp.exp(log_a)
    b = jnp.sqrt(-jnp.expm1(2 * log_a)) * gi * u.astype(jnp.float32)
    return a, b


def linear_scan(a, b, h0, reverse):
    idx = -1 if reverse else 0
    b = b.at[:, idx].add(a[:, idx] * h0)

    def combine(l, r):
        return l[0] * r[0], r[0] * l[1] + r[1]

    _, h = lax.associative_scan(combine, (a, b), axis=1, reverse=reverse)
    return h


def merge_branches(branches, g_out, w_out):
    splits = list(np.cumsum([br.shape[-1] for br in branches])[:-1])
    gains = jnp.split(g_out, splits)
    y = jnp.concatenate([rmsnorm(br, g) for br, g in zip(branches, gains)], axis=-1)
    return y @ w_out


def expert_dispatch(h, e_idx, w, w1, w3, w2):
    T, D = h.shape
    n = T * TOP_K
    flat_e = e_idx.reshape(-1)
    order = jnp.argsort(flat_e)
    sorted_e = flat_e[order]
    counts = jnp.bincount(flat_e, length=N_EXPERTS)
    padded = (counts + MOE_BLOCK - 1) // MOE_BLOCK * MOE_BLOCK
    pad_end = jnp.cumsum(padded)
    pad_start = pad_end - padded
    seg_start = jnp.cumsum(counts) - counts
    dest_sorted = pad_start[sorted_e] + jnp.arange(n) - seg_start[sorted_e]
    dest = jnp.zeros((n,), jnp.int32).at[order].set(dest_sorted.astype(jnp.int32))
    n_blocks = (n + N_EXPERTS * (MOE_BLOCK - 1) + MOE_BLOCK - 1) // MOE_BLOCK
    n_slots = n_blocks * MOE_BLOCK
    slot_tok = jnp.full((n_slots,), T, jnp.int32).at[dest].set(jnp.arange(n, dtype=jnp.int32) // TOP_K)
    block_e = jnp.minimum(jnp.searchsorted(pad_end, jnp.arange(n_blocks) * MOE_BLOCK, side="right"),
                          N_EXPERTS - 1)
    xs = jnp.concatenate([h, jnp.zeros((1, D), h.dtype)], axis=0)[slot_tok]
    xs = xs.reshape(n_blocks, MOE_BLOCK, D)

    def run_block(args):
        xb, e = args
        return (jax.nn.silu(xb @ w1[e]) * (xb @ w3[e])) @ w2[e]

    ys = lax.map(run_block, (xs, block_e)).reshape(n_slots, D)
    return jnp.einsum("tkd,tk->td", ys[dest].reshape(T, TOP_K, D), w)


def hier_moe(h, gw, gb, ew, eb, w1, w3, w2):
    T = h.shape[0]
    hf = h.astype(jnp.float32)
    p_group = jax.nn.softmax(hf @ gw.astype(jnp.float32) + gb.astype(jnp.float32), axis=-1)
    p_sel, g_idx = lax.top_k(p_group, 1)
    e_logits = (hf @ ew.astype(jnp.float32) + eb.astype(jnp.float32)).reshape(T, N_GROUPS, EXPERTS_PER_GROUP)
    e_logits = jnp.take_along_axis(e_logits, g_idx[:, :, None], axis=1)[:, 0]
    p_top, e_local = lax.top_k(jax.nn.softmax(e_logits, axis=-1), TOP_K)
    w = p_sel * p_top / jnp.sum(p_top, axis=-1, keepdims=True)
    e_idx = g_idx * EXPERTS_PER_GROUP + e_local
    return expert_dispatch(h, e_idx, w.astype(h.dtype), w1, w3, w2)


def even_mixer(h, hc, w_in, q_g, k_g, sink, cw, cb, cos, sin, need_ctx):
    z = h @ w_in
    k, v, q, gb, gc, gx = jnp.split(z, AB_SPLITS, axis=-1)
    zc = hc @ (w_in if need_ctx else w_in[:, :2 * KV_W])
    kc, vc = zc[..., :KV_W], zc[..., KV_W:2 * KV_W]

    def kv_heads(t):
        return t.reshape(t.shape[:-1] + (N_KV_HEADS, HEAD_DIM))

    def q_heads(t):
        return t.reshape(t.shape[:-1] + (N_KV_HEADS, Q_PER_KV, HEAD_DIM))

    q = apply_rope(rmsnorm(q_heads(q), q_g), cos, sin)
    k = apply_rope(rmsnorm(kv_heads(k), k_g), cos, sin)
    kc = rmsnorm(kv_heads(kc), k_g)
    vc = kv_heads(vc)
    att = window_attention(q, k, kv_heads(v), kc, vc, sink)
    conv = gb * depthwise_conv(gc * gx, cw, cb, SC_PAD)
    if not need_ctx:
        return [att, conv], None
    _, _, qc, gbc, gcc, gxc = jnp.split(zc, AB_SPLITS, axis=-1)
    att_c = context_attention(rmsnorm(q_heads(qc), q_g), kc, vc, sink)
    conv_c = gbc * depthwise_conv(gcc * gxc, cw, cb, SC_PAD)
    return [att, conv], [att_c, conv_c]


def odd_mixer(h, hc, w_in, pool_w, pool_scale, conv_w, conv_b, wa, ba, wx, bx, lam, need_ctx):
    z = h @ w_in
    ux, ug, up = jnp.split(z, CD_SPLITS, axis=-1)
    zc = hc @ (w_in if need_ctx else w_in[:, :LRU_W])
    xl = depthwise_conv(ux, conv_w, conv_b, LRU_PAD)
    xc = depthwise_conv(zc[..., :LRU_W], conv_w, conv_b, LRU_PAD)
    h_lat = jnp.zeros(xl.shape, jnp.float32)
    ctx_states = []
    for d, reverse in enumerate((False, True)):
        a_c, b_c = rglru_coeffs(xc, wa[d], ba[d], wx[d], bx[d], lam[d])
        hs_c = linear_scan(a_c, b_c, jnp.zeros_like(b_c[:, 0]), reverse)
        h0 = hs_c[:, 0] if reverse else hs_c[:, -1]
        a_l, b_l = rglru_coeffs(xl, wa[d], ba[d], wx[d], bx[d], lam[d])
        h_lat = h_lat + linear_scan(a_l, b_l, h0, reverse)
        ctx_states.append(hs_c)
    lat = [multiscale_pool(up, pool_w, pool_scale), h_lat.astype(ug.dtype) * jax.nn.gelu(ug)]
    if not need_ctx:
        return lat, None
    _, ugc, upc = jnp.split(zc, CD_SPLITS, axis=-1)
    h_ctx = (ctx_states[0] + ctx_states[1]).astype(ugc.dtype)
    return lat, [multiscale_pool(upc, pool_w, pool_scale), h_ctx * jax.nn.gelu(ugc)]


def setup_inputs(seed: int = 0) -> dict:
    key = jax.random.key(seed)
    ks = iter(jax.random.split(key, 48))
    n_even, n_odd = (DEPTH + 1) // 2, DEPTH // 2

    def nrm(shape, std):
        return std * jax.random.normal(next(ks), shape, jnp.float32)

    def gain(shape):
        return 1.0 + 0.05 * jax.random.normal(next(ks), shape, jnp.float32)

    u = jax.random.uniform(next(ks), (n_odd, 2, LRU_W), jnp.float32, 0.9, 0.999)
    a0 = u ** (1.0 / LRU_C)
    lru_lambda = jnp.log(a0) - jnp.log1p(-a0)
    return {
        "x": nrm((BATCH, SEQ, D_MODEL), 1.0),
        "c": nrm((BATCH, D_MODEL), 1.0),
        "ctx": nrm((BATCH, CTX_LEN, D_MODEL), 1.0),
        "c_ctx": nrm((D_MODEL,), 1.0),
        "ada_w": nrm((DEPTH, D_MODEL, N_MOD * D_MODEL), 0.5 * D_MODEL ** -0.5),
        "ada_b": nrm((DEPTH, N_MOD * D_MODEL), 0.02),
        "norm_mix": gain((DEPTH, D_MODEL)),
        "norm_ffn": gain((DEPTH, D_MODEL)),
        "out_norm": gain((DEPTH, MIX_W)),
        "w_out": nrm((DEPTH, MIX_W, D_MODEL), MIX_W ** -0.5),
        "w_in_ab": nrm((n_even, D_MODEL, IN_AB), D_MODEL ** -0.5),
        "q_norm": gain((n_even, HEAD_DIM)),
        "k_norm": gain((n_even, HEAD_DIM)),
        "attn_sink": nrm((n_even, N_Q_HEADS), 0.5),
        "sconv_w": nrm((n_even, SC_WIDTH, SC_W), SC_WIDTH ** -0.5),
        "sconv_b": nrm((n_even, SC_W), 0.02),
        "w_in_cd": nrm((n_odd, D_MODEL, IN_CD), D_MODEL ** -0.5),
        "pool_w": nrm((n_odd, POOL_GROUPS, POOL_GW, POOL_GW), POOL_GW ** -0.5),
        "pool_scale": gain((n_odd, POOL_W)),
        "lru_conv_w": nrm((n_odd, LRU_CONV, LRU_W), LRU_CONV ** -0.5),
        "lru_conv_b": nrm((n_odd, LRU_W), 0.02),
        "lru_wa": nrm((n_odd, 2, LRU_BLOCKS, LRU_BW, LRU_BW), LRU_BW ** -0.5),
        "lru_ba": nrm((n_odd, 2, LRU_W), 0.02),
        "lru_wx": nrm((n_odd, 2, LRU_BLOCKS, LRU_BW, LRU_BW), LRU_BW ** -0.5),
        "lru_bx": nrm((n_odd, 2, LRU_W), 0.02),
        "lru_lambda": lru_lambda,
        "router_gw": nrm((DEPTH, D_MODEL, N_GROUPS), D_MODEL ** -0.5),
        "router_gb": nrm((DEPTH, N_GROUPS), 0.01),
        "router_ew": nrm((DEPTH, D_MODEL, N_EXPERTS), D_MODEL ** -0.5),
        "router_eb": nrm((DEPTH, N_EXPERTS), 0.01),
        "exp_w1": nrm((DEPTH, N_EXPERTS, D_MODEL, D_EXPERT), D_MODEL ** -0.5),
        "exp_w3": nrm((DEPTH, N_EXPERTS, D_MODEL, D_EXPERT), D_MODEL ** -0.5),
        "exp_w2": nrm((DEPTH, N_EXPERTS, D_EXPERT, D_MODEL), D_EXPERT ** -0.5),
    }


def reference(x, c, ctx, c_ctx, ada_w, ada_b, norm_mix, norm_ffn, out_norm, w_out,
              w_in_ab, q_norm, k_norm, attn_sink, sconv_w, sconv_b,
              w_in_cd, pool_w, pool_scale, lru_conv_w, lru_conv_b, lru_wa, lru_ba, lru_wx, lru_bx, lru_lambda,
              router_gw, router_gb, router_ew, router_eb, exp_w1, exp_w3, exp_w2):
    Bn, L, D = x.shape
    cos, sin = axial_rope_tables(L)
    for i in range(DEPTH):
        need_ctx = i < DEPTH - 1
        j = i // 2
        mod = jnp.split((jax.nn.silu(c) @ ada_w[i] + ada_b[i])[:, None, :], N_MOD, axis=-1)
        mod_c = jnp.split(jax.nn.silu(c_ctx) @ ada_w[i] + ada_b[i], N_MOD, axis=-1)
        h = modulate(x, norm_mix[i], mod[0], mod[1])
        hc = modulate(ctx, norm_mix[i], mod_c[0], mod_c[1])
        if i % 2 == 0:
            lat, cx = even_mixer(h, hc, w_in_ab[j], q_norm[j], k_norm[j], attn_sink[j],
                                 sconv_w[j], sconv_b[j], cos, sin, need_ctx)
        else:
            lat, cx = odd_mixer(h, hc, w_in_cd[j], pool_w[j], pool_scale[j], lru_conv_w[j], lru_conv_b[j],
                                lru_wa[j], lru_ba[j], lru_wx[j], lru_bx[j], lru_lambda[j], need_ctx)
        x = x + mod[2] * merge_branches(lat, out_norm[i], w_out[i])
        tokens = modulate(x, norm_ffn[i], mod[3], mod[4]).reshape(-1, D)
        if need_ctx:
            ctx = ctx + mod_c[2] * merge_branches(cx, out_norm[i], w_out[i])
            hc = modulate(ctx, norm_ffn[i], mod_c[3], mod_c[4])
            tokens = jnp.concatenate([tokens, hc.reshape(-1, D)], axis=0)
        y = hier_moe(tokens, router_gw[i], router_gb[i], router_ew[i], router_eb[i],
                     exp_w1[i], exp_w3[i], exp_w2[i])
        x = x + mod[5] * y[:Bn * L].reshape(Bn, L, D)
        if need_ctx:
            ctx = ctx + mod_c[5] * y[Bn * L:].reshape(ctx.shape)
    return x
```

```python
import functools

import jax
import jax.numpy as jnp
from jax import lax
from jax.experimental import pallas as pl
from jax.experimental.pallas import tpu as pltpu

F32 = jnp.float32
BF16 = jnp.bfloat16
HIGHEST = lax.Precision.HIGHEST

GRID_W = 64
EPS = 1e-6
HEAD_DIM = 64
N_Q_HEADS = 8
N_KV_HEADS = 2
Q_PER_KV = N_Q_HEADS // N_KV_HEADS
WINDOW = 128
ATT_BLOCK = 128
ROPE_PAIRS = HEAD_DIM // 4
ROPE_BASE = 10000.0
POOL_WINDOWS = (2, 4, 8, 16)
LRU_C = 8.0
N_GROUPS = 4
EXPERTS_PER_GROUP = 8
N_EXPERTS = N_GROUPS * EXPERTS_PER_GROUP
TOP_K = 2
MOE_BLOCK = 128
N_MOD = 6

SUBLANES = 8
LANES = 128
SEQ_TILE = 256
TOK_TILE = 512
NEG = -1e30
VMEM_LIMIT = 48 * 1024 * 1024


def _cparams(*sem):
    return pltpu.CompilerParams(dimension_semantics=sem, vmem_limit_bytes=VMEM_LIMIT)


def _rms(x, g):
    ms = jnp.mean(x * x, axis=-1, keepdims=True)
    return x * lax.rsqrt(ms + EPS) * g


def _modulate(x, g, shift, scale):
    return _rms(x, g) * (1.0 + scale) + shift


def _bdot(a, b):
    return jnp.dot(a.astype(BF16), b, preferred_element_type=F32)


def _silu(x):
    return x * jax.nn.sigmoid(x)


def _ada_kernel(c_ref, w_ref, b_ref, o_ref):
    s = _silu(c_ref[...])
    o_ref[0] = jnp.dot(s, w_ref[0], precision=HIGHEST, preferred_element_type=F32) + b_ref[0]


def _ada(cc, ada_w, ada_b):
    depth, d, n = ada_w.shape
    tn = 1536
    return pl.pallas_call(
        _ada_kernel,
        grid=(depth, n // tn),
        in_specs=[pl.BlockSpec((SUBLANES, d), lambda l, j: (0, 0)),
                  pl.BlockSpec((1, d, tn), lambda l, j: (l, 0, j)),
                  pl.BlockSpec((1, 1, tn), lambda l, j: (l, 0, j))],
        out_specs=pl.BlockSpec((1, SUBLANES, tn), lambda l, j: (l, 0, j)),
        out_shape=jax.ShapeDtypeStruct((depth, SUBLANES, n), F32),
        compiler_params=_cparams("arbitrary", "arbitrary"),
        name="ada",
    )(cc, ada_w, ada_b.reshape(depth, 1, n))


def _head_norm_rope(t, pm, g, cos, sin):
    ms = jnp.dot(t * t, pm, precision=HIGHEST, preferred_element_type=F32)
    t = t * lax.rsqrt(ms + EPS) * g
    lane = lax.broadcasted_iota(jnp.int32, t.shape, 1)
    first_half = (lane % (2 * ROPE_PAIRS)) < ROPE_PAIRS
    partner = jnp.where(first_half, pltpu.roll(t, LANES - ROPE_PAIRS, 1), pltpu.roll(t, ROPE_PAIRS, 1))
    return t * cos + partner * sin


def _inproj_even_kernel(x_ref, mod_ref, g_ref, w_ref, qg_ref, kg_ref, cos_ref, sin_ref, pm_ref,
                        q_ref, k_ref, v_ref, gb_ref, u_ref):
    mod = mod_ref[0]
    h = _modulate(x_ref[...], g_ref[...], mod[0:1], mod[1:2]).astype(BF16)
    cos, sin, pm = cos_ref[...], sin_ref[...], pm_ref[...]
    kv_w = N_KV_HEADS * HEAD_DIM
    att_w = N_Q_HEADS * HEAD_DIM
    sc_w = gb_ref.shape[1]
    kv = jnp.dot(h, w_ref[:, 0:2 * kv_w], preferred_element_type=F32)
    k_ref[...] = _head_norm_rope(kv[:, :kv_w], pm, kg_ref[...], cos, sin).astype(BF16)
    v_ref[...] = kv[:, kv_w:].astype(BF16)
    q = jnp.dot(h, w_ref[:, 2 * kv_w:2 * kv_w + att_w], preferred_element_type=F32)
    qscale = HEAD_DIM ** -0.5
    for c in range(att_w // LANES):
        qc = _head_norm_rope(q[:, c * LANES:(c + 1) * LANES], pm, qg_ref[...], cos, sin)
        q_ref[:, c * LANES:(c + 1) * LANES] = (qc * qscale).astype(BF16)
    o = 2 * kv_w + att_w
    gb_ref[...] = jnp.dot(h, w_ref[:, o:o + sc_w], preferred_element_type=F32)
    gc = jnp.dot(h, w_ref[:, o + sc_w:o + 2 * sc_w], preferred_element_type=F32)
    gx = jnp.dot(h, w_ref[:, o + 2 * sc_w:o + 3 * sc_w], preferred_element_type=F32)
    u_ref[...] = gc * gx


def _tile_maps(n_lat_tiles, tiles_per_seq, n_batch):
    def mod_map(j):
        return (jnp.minimum(j // tiles_per_seq, n_batch), 0, 0)

    def pos_map(j):
        return (jnp.where(j < n_lat_tiles, j % tiles_per_seq, tiles_per_seq), 0)

    return mod_map, pos_map


def _inproj_even(xf, modl, g, w_bf, q_g, k_g, cos_t, sin_t, pm, dims):
    n_tok, d = xf.shape
    n_batch, seq = dims
    tps = seq // SEQ_TILE
    mod_map, pos_map = _tile_maps(n_batch * tps, tps, n_batch)
    kv_w, att_w = N_KV_HEADS * HEAD_DIM, N_Q_HEADS * HEAD_DIM
    sc_w = (w_bf.shape[1] - 2 * kv_w - att_w) // 3
    row = lambda j: (j, 0)
    const = lambda j: (0, 0)
    return pl.pallas_call(
        _inproj_even_kernel,
        grid=(n_tok // SEQ_TILE,),
        in_specs=[pl.BlockSpec((SEQ_TILE, d), row),
                  pl.BlockSpec((1, SUBLANES, d), mod_map),
                  pl.BlockSpec((1, d), const),
                  pl.BlockSpec(w_bf.shape, const),
                  pl.BlockSpec((1, LANES), const),
                  pl.BlockSpec((1, LANES), const),
                  pl.BlockSpec((SEQ_TILE, LANES), pos_map),
                  pl.BlockSpec((SEQ_TILE, LANES), pos_map),
                  pl.BlockSpec((LANES, LANES), const)],
        out_specs=[pl.BlockSpec((SEQ_TILE, att_w), row),
                   pl.BlockSpec((SEQ_TILE, kv_w), row),
                   pl.BlockSpec((SEQ_TILE, kv_w), row),
                   pl.BlockSpec((SEQ_TILE, sc_w), row),
                   pl.BlockSpec((SEQ_TILE, sc_w), row)],
        out_shape=[jax.ShapeDtypeStruct((n_tok, att_w), BF16),
                   jax.ShapeDtypeStruct((n_tok, kv_w), BF16),
                   jax.ShapeDtypeStruct((n_tok, kv_w), BF16),
                   jax.ShapeDtypeStruct((n_tok, sc_w), F32),
                   jax.ShapeDtypeStruct((n_tok, sc_w), F32)],
        compiler_params=_cparams("arbitrary"),
        name="inproj_even",
    )(xf, modl, g, w_bf, q_g, k_g, cos_t, sin_t, pm)


def _attn_heads(q, kb, vb, bias, sink_ref, o_ref):
    outs = []
    for hq in range(N_Q_HEADS):
        hk = hq // Q_PER_KV
        qh = q[:, hq * HEAD_DIM:(hq + 1) * HEAD_DIM]
        kh = kb[:, hk * HEAD_DIM:(hk + 1) * HEAD_DIM]
        vh = vb[:, hk * HEAD_DIM:(hk + 1) * HEAD_DIM]
        s = lax.dot_general(qh, kh, (((1,), (1,)), ((), ())), preferred_element_type=F32)
        if bias is not None:
            s = s + bias
        snk = sink_ref[hq]
        m = jnp.maximum(jnp.max(s, axis=-1, keepdims=True), snk)
        p = jnp.exp(s - m)
        den = jnp.sum(p, axis=-1, keepdims=True) + jnp.exp(snk - m)
        o = jnp.dot(p.astype(BF16), vh, preferred_element_type=F32)
        outs.append(o / den)
    o_ref[...] = jnp.concatenate(outs, axis=-1)


def _attn_lat_kernel(sink_ref, q_ref, kp_ref, kc_ref, kn_ref, kx_ref, vp_ref, vc_ref, vn_ref, vx_ref,
                     o_ref):
    i = pl.program_id(1)
    nb = pl.num_programs(1)
    kb = jnp.concatenate([kp_ref[...], kc_ref[...], kn_ref[...], kx_ref[...]], axis=0)
    vb = jnp.concatenate([vp_ref[...], vc_ref[...], vn_ref[...], vx_ref[...]], axis=0)
    m, n = ATT_BLOCK, kb.shape[0]
    row = lax.broadcasted_iota(jnp.int32, (m, n), 0)
    col = lax.broadcasted_iota(jnp.int32, (m, n), 1)
    rel = col - row
    lo = jnp.where(i == 0, ATT_BLOCK, 0)
    hi = jnp.where(i == nb - 1, 2 * ATT_BLOCK, 3 * ATT_BLOCK)
    local_ok = (rel >= 0) & (rel <= 2 * WINDOW) & (col >= lo) & (col < hi)
    valid = (col >= 3 * ATT_BLOCK) | local_ok
    bias = jnp.where(valid, 0.0, NEG)
    _attn_heads(q_ref[...], kb, vb, bias, sink_ref, o_ref)


def _attn_ctx_kernel(sink_ref, att_in_ref, q_ref, kx_ref, vx_ref, o_ref):
    del att_in_ref
    _attn_heads(q_ref[...], kx_ref[...], vx_ref[...], None, sink_ref, o_ref)


def _attention(q, k, v, sink, dims, need_ctx):
    n_tok = q.shape[0]
    n_batch, seq = dims
    nb = seq // ATT_BLOCK
    n_lat = n_batch * seq
    ctx_len = (n_tok - n_lat) // n_batch
    att_w, kv_w = q.shape[1], k.shape[1]
    ctx_blk0 = n_lat // ctx_len
    cur = lambda b, i: (b * nb + i, 0)
    prev = lambda b, i: (b * nb + jnp.maximum(i - 1, 0), 0)
    nxt = lambda b, i: (b * nb + jnp.minimum(i + 1, nb - 1), 0)
    cx = lambda b, i: (ctx_blk0 + b, 0)
    smem = pl.BlockSpec(memory_space=pltpu.SMEM)
    kvb = lambda f: pl.BlockSpec((ATT_BLOCK, kv_w), f)
    att = pl.pallas_call(
        _attn_lat_kernel,
        grid=(n_batch, nb),
        in_specs=[smem, pl.BlockSpec((ATT_BLOCK, att_w), cur),
                  kvb(prev), kvb(cur), kvb(nxt), pl.BlockSpec((ctx_len, kv_w), cx),
                  kvb(prev), kvb(cur), kvb(nxt), pl.BlockSpec((ctx_len, kv_w), cx)],
        out_specs=pl.BlockSpec((ATT_BLOCK, att_w), cur),
        out_shape=jax.ShapeDtypeStruct((n_tok, att_w), F32),
        compiler_params=_cparams("arbitrary", "arbitrary"),
        name="attn_latent",
    )(sink, q, k, k, k, k, v, v, v, v)
    if not need_ctx:
        return att
    cxb = lambda b: (ctx_blk0 + b, 0)
    return pl.pallas_call(
        _attn_ctx_kernel,
        grid=(n_batch,),
        in_specs=[smem, pl.BlockSpec(memory_space=pl.ANY),
                  pl.BlockSpec((ctx_len, att_w), cxb),
                  pl.BlockSpec((ctx_len, kv_w), cxb), pl.BlockSpec((ctx_len, kv_w), cxb)],
        out_specs=pl.BlockSpec((ctx_len, att_w), cxb),
        out_shape=jax.ShapeDtypeStruct((n_tok, att_w), F32),
        input_output_aliases={1: 0},
        compiler_params=_cparams("arbitrary"),
        name="attn_context",
    )(sink, att, q, k, v)


def _route(logits):
    lane = lax.broadcasted_iota(jnp.int32, logits.shape, 1).astype(F32)
    big = float(LANES)
    gl = jnp.where((lane >= N_EXPERTS) & (lane < N_EXPERTS + N_GROUPS), logits, NEG)
    gmax = jnp.max(gl, axis=-1, keepdims=True)
    gidx = jnp.min(jnp.where(gl == gmax, lane, big), axis=-1, keepdims=True) - N_EXPERTS
    p_sel = 1.0 / jnp.sum(jnp.exp(gl - gmax), axis=-1, keepdims=True)
    lo = gidx * EXPERTS_PER_GROUP
    el = jnp.where((lane >= lo) & (lane < lo + EXPERTS_PER_GROUP), logits, NEG)
    m1 = jnp.max(el, axis=-1, keepdims=True)
    i1 = jnp.min(jnp.where(el == m1, lane, big), axis=-1, keepdims=True)
    el2 = jnp.where(lane == i1, NEG, el)
    m2 = jnp.max(el2, axis=-1, keepdims=True)
    i2 = jnp.min(jnp.where(el2 == m2, lane, big), axis=-1, keepdims=True)
    t = jnp.exp(m2 - m1)
    w1 = p_sel / (1.0 + t)
    w2 = p_sel * t / (1.0 + t)
    return jnp.where(lane == 0, i1, jnp.where(lane == 1, i2, jnp.where(lane == 2, w1,
                     jnp.where(lane == 3, w2, 0.0))))


def _tail(n1, n2, gain, wo_ref, x, mod, nf_ref, wr_ref, br_ref, xo_ref, tok_ref, route_ref):
    half = n1.shape[1]
    y = (_bdot(_rms(n1, gain[:, :half]), wo_ref[0:half, :])
         + _bdot(_rms(n2, gain[:, half:]), wo_ref[half:, :]))
    xn = x + mod[2:3] * y
    xo_ref[...] = xn
    tok = _modulate(xn, nf_ref[...], mod[3:4], mod[4:5])
    tok_ref[...] = tok
    logits = jnp.dot(tok, wr_ref[...], precision=HIGHEST, preferred_element_type=F32) + br_ref[...]
    route_ref[...] = _route(logits)


def _seq_flags(j, n_lat_tiles, tiles_per_seq):
    is_ctx = j >= n_lat_tiles
    first = is_ctx | (j % tiles_per_seq == 0)
    last = is_ctx | (j % tiles_per_seq == tiles_per_seq - 1)
    return first, last


def _with_halo(prev_ref, tile, next_ref, first, last):
    p = jnp.where(first, 0.0, prev_ref[...])
    n = jnp.where(last, 0.0, next_ref[...])
    return jnp.concatenate([p, tile, n], axis=0)


def _shift_rows(e, k):
    return pltpu.roll(e, (-k) % e.shape[0], 0)


def _merge_even_kernel(att_ref, gb_ref, u_ref, up_ref, un_ref, x_ref, mod_ref, cw_ref, cb_ref,
                       on_ref, wo_ref, nf_ref, wr_ref, br_ref,
                       xo_ref, tok_ref, route_ref, *, n_lat_tiles, tiles_per_seq):
    j = pl.program_id(0)
    first, last = _seq_flags(j, n_lat_tiles, tiles_per_seq)
    e = _with_halo(up_ref, u_ref[...], un_ref, first, last)
    cw = cw_ref[...]
    lo, hi = SUBLANES, SUBLANES + SEQ_TILE
    conv = (cw[0:1] * _shift_rows(e, -1)[lo:hi] + cw[1:2] * e[lo:hi]
            + cw[2:3] * _shift_rows(e, 1)[lo:hi] + cb_ref[...])
    conv = gb_ref[...] * conv
    _tail(att_ref[...], conv, on_ref[...], wo_ref, x_ref[...], mod_ref[0], nf_ref, wr_ref, br_ref,
          xo_ref, tok_ref, route_ref)


def _halo_specs(width, n_rows):
    r = SEQ_TILE // SUBLANES
    n_blk = n_rows // SUBLANES
    prev = pl.BlockSpec((SUBLANES, width), lambda j: (jnp.maximum(j * r - 1, 0), 0))
    nxt = pl.BlockSpec((SUBLANES, width), lambda j: (jnp.minimum((j + 1) * r, n_blk - 1), 0))
    return prev, nxt


def _tail_specs(d, n_out_tok):
    row = lambda j: (j, 0)
    out_specs = [pl.BlockSpec((SEQ_TILE, d), row), pl.BlockSpec((SEQ_TILE, d), row),
                 pl.BlockSpec((SEQ_TILE, LANES), row)]
    out_shape = [jax.ShapeDtypeStruct((n_out_tok, d), F32), jax.ShapeDtypeStruct((n_out_tok, d), F32),
                 jax.ShapeDtypeStruct((n_out_tok, LANES), F32)]
    return out_specs, out_shape


def _merge_even(att, gb, u, xf, modl, cw, cb, on, wo_bf, nf, wr, br, dims, n_out_tok):
    n_tok, d = xf.shape
    n_batch, seq = dims
    tps = seq // SEQ_TILE
    mod_map, _ = _tile_maps(n_batch * tps, tps, n_batch)
    sc_w = u.shape[1]
    row = lambda j: (j, 0)
    const = lambda j: (0, 0)
    prev, nxt = _halo_specs(sc_w, n_tok)
    out_specs, out_shape = _tail_specs(d, n_out_tok)
    return pl.pallas_call(
        functools.partial(_merge_even_kernel, n_lat_tiles=n_batch * tps, tiles_per_seq=tps),
        grid=(n_out_tok // SEQ_TILE,),
        in_specs=[pl.BlockSpec((SEQ_TILE, att.shape[1]), row),
                  pl.BlockSpec((SEQ_TILE, sc_w), row),
                  pl.BlockSpec((SEQ_TILE, sc_w), row), prev, nxt,
                  pl.BlockSpec((SEQ_TILE, d), row),
                  pl.BlockSpec((1, SUBLANES, d), mod_map),
                  pl.BlockSpec(cw.shape, const), pl.BlockSpec((1, sc_w), const),
                  pl.BlockSpec((1, d), const), pl.BlockSpec(wo_bf.shape, const),
                  pl.BlockSpec((1, d), const), pl.BlockSpec(wr.shape, const),
                  pl.BlockSpec((1, LANES), const)],
        out_specs=out_specs, out_shape=out_shape,
        compiler_params=_cparams("arbitrary"),
        name="merge_even",
    )(att, gb, u, u, u, xf, modl, cw, cb, on, wo_bf, nf, wr, br)


def _inproj_odd_kernel(x_ref, mod_ref, g_ref, w_ref, ux_ref, gg_ref, up_ref):
    mod = mod_ref[0]
    h = _modulate(x_ref[...], g_ref[...], mod[0:1], mod[1:2]).astype(BF16)
    w = ux_ref.shape[1]
    ux_ref[...] = jnp.dot(h, w_ref[:, 0:w], preferred_element_type=F32)
    ug = jnp.dot(h, w_ref[:, w:2 * w], preferred_element_type=F32)
    gg_ref[...] = jax.nn.gelu(ug, approximate=True)
    up_ref[...] = jnp.dot(h, w_ref[:, 2 * w:3 * w], preferred_element_type=F32)


def _inproj_odd(xf, modl, g, w_bf, dims):
    n_tok, d = xf.shape
    n_batch, seq = dims
    tps = seq // SEQ_TILE
    mod_map, _ = _tile_maps(n_batch * tps, tps, n_batch)
    w = w_bf.shape[1] // 3
    row = lambda j: (j, 0)
    const = lambda j: (0, 0)
    return pl.pallas_call(
        _inproj_odd_kernel,
        grid=(n_tok // SEQ_TILE,),
        in_specs=[pl.BlockSpec((SEQ_TILE, d), row),
                  pl.BlockSpec((1, SUBLANES, d), mod_map),
                  pl.BlockSpec((1, d), const),
                  pl.BlockSpec(w_bf.shape, const)],
        out_specs=[pl.BlockSpec((SEQ_TILE, w), row)] * 3,
        out_shape=[jax.ShapeDtypeStruct((n_tok, w), F32)] * 3,
        compiler_params=_cparams("arbitrary"),
        name="inproj_odd",
    )(xf, modl, g, w_bf)


def _lru_conv(e, cw, cb):
    lo, hi = SUBLANES, SUBLANES + SEQ_TILE
    return (cw[0:1] * _shift_rows(e, -2)[lo:hi] + cw[1:2] * _shift_rows(e, -1)[lo:hi]
            + cw[2:3] * e[lo:hi] + cw[3:4] * _shift_rows(e, 1)[lo:hi] + cb)


def _lru_coeffs(xl, d, wa_ref, wx_ref, ba_ref, bx_ref, lam_ref, a_ref, b_ref):
    xb = xl.astype(BF16)
    chunk = wa_ref.shape[-1]
    n_chunk = xl.shape[1] // chunk
    pre_a = jnp.concatenate([jnp.dot(xb[:, c * chunk:(c + 1) * chunk], wa_ref[d, c],
                                     preferred_element_type=F32) for c in range(n_chunk)], axis=-1)
    pre_x = jnp.concatenate([jnp.dot(xb[:, c * chunk:(c + 1) * chunk], wx_ref[d, c],
                                     preferred_element_type=F32) for c in range(n_chunk)], axis=-1)
    r = jax.nn.sigmoid(pre_a + ba_ref[d])
    gi = jax.nn.sigmoid(pre_x + bx_ref[d])
    log_a = (-LRU_C * jax.nn.softplus(-lam_ref[d])) * r
    a_ref[...] = jnp.exp(log_a)
    th = jnp.tanh(log_a)
    b_ref[...] = jnp.sqrt(-2.0 * th / (1.0 - th)) * gi * xl


def _scan_tile(a_ref, b_ref, h_ref, carry, reverse):
    n_grp = a_ref.shape[0] // SUBLANES
    row = lax.broadcasted_iota(jnp.int32, (SUBLANES, a_ref.shape[1]), 0)

    def body(g, carry):
        gi = (n_grp - 1 - g) if reverse else g
        r0 = pl.multiple_of(gi * SUBLANES, SUBLANES)
        a8 = a_ref[pl.ds(r0, SUBLANES), :]
        b8 = b_ref[pl.ds(r0, SUBLANES), :]
        for s in (1, 2, 4):
            if reverse:
                ok = row < SUBLANES - s
                sh = SUBLANES - s
            else:
                ok = row >= s
                sh = s
            a_sh = jnp.where(ok, pltpu.roll(a8, sh, 0), 1.0)
            b_sh = jnp.where(ok, pltpu.roll(b8, sh, 0), 0.0)
            b8 = a8 * b_sh + b8
            a8 = a8 * a_sh
        h8 = a8 * carry + b8
        if h_ref is not None:
            h_ref[pl.ds(r0, SUBLANES), :] = h8
        return h8[0:1] if reverse else h8[SUBLANES - 1:SUBLANES]

    return lax.fori_loop(0, n_grp, body, carry)


def _lru_kernel(uf_ref, ufp_ref, ufn_ref, ur_ref, urp_ref, urn_ref, uc_ref, cw_ref, cb_ref,
                wa_ref, wx_ref, ba_ref, bx_ref, lam_ref, hf_ref, hr_ref,
                cf_ref, cr_ref, a_ref, b_ref):
    i = pl.program_id(1)
    nt = pl.num_programs(1)
    cw, cb = cw_ref[...], cb_ref[...]
    coeffs = functools.partial(_lru_coeffs, wa_ref=wa_ref, wx_ref=wx_ref, ba_ref=ba_ref,
                               bx_ref=bx_ref, lam_ref=lam_ref, a_ref=a_ref, b_ref=b_ref)

    @pl.when(i == 0)
    def _():
        z = jnp.zeros((SUBLANES, uc_ref.shape[1]), F32)
        xc = _lru_conv(jnp.concatenate([z, uc_ref[...], z], axis=0), cw, cb)
        zero = jnp.zeros((1, uc_ref.shape[1]), F32)
        coeffs(xc, 0)
        cf_ref[...] = _scan_tile(a_ref, b_ref, None, zero, False)
        coeffs(xc, 1)
        cr_ref[...] = _scan_tile(a_ref, b_ref, None, zero, True)

    xf = _lru_conv(_with_halo(ufp_ref, uf_ref[...], ufn_ref, i == 0, i == nt - 1), cw, cb)
    coeffs(xf, 0)
    cf_ref[...] = _scan_tile(a_ref, b_ref, hf_ref, cf_ref[...], False)
    xr = _lru_conv(_with_halo(urp_ref, ur_ref[...], urn_ref, i == nt - 1, i == 0), cw, cb)
    coeffs(xr, 1)
    cr_ref[...] = _scan_tile(a_ref, b_ref, hr_ref, cr_ref[...], True)


def _lru(ux, cw, cb, wa_bd, wx_bd, ba, bx, lam, dims):
    n_tok, w = ux.shape
    n_batch, seq = dims
    nt = seq // SEQ_TILE
    n_lat = n_batch * seq
    r = SEQ_TILE // SUBLANES
    n_blk = n_tok // SUBLANES
    fwd = lambda b, i: b * nt + i
    rev = lambda b, i: b * nt + nt - 1 - i

    def specs(tile):
        return [pl.BlockSpec((SEQ_TILE, w), lambda b, i: (tile(b, i), 0)),
                pl.BlockSpec((SUBLANES, w), lambda b, i: (jnp.maximum(tile(b, i) * r - 1, 0), 0)),
                pl.BlockSpec((SUBLANES, w), lambda b, i: (jnp.minimum((tile(b, i) + 1) * r, n_blk - 1), 0))]

    const2 = lambda b, i: (0, 0)
    const3 = lambda b, i: (0, 0, 0)
    const4 = lambda b, i: (0, 0, 0, 0)
    return pl.pallas_call(
        _lru_kernel,
        grid=(n_batch, nt),
        in_specs=specs(fwd) + specs(rev) + [
            pl.BlockSpec((SEQ_TILE, w), lambda b, i: (n_lat // SEQ_TILE + b, 0)),
            pl.BlockSpec(cw.shape, const2), pl.BlockSpec((1, w), const2),
            pl.BlockSpec(wa_bd.shape, const4), pl.BlockSpec(wx_bd.shape, const4),
            pl.BlockSpec(ba.shape, const3), pl.BlockSpec(bx.shape, const3),
            pl.BlockSpec(lam.shape, const3)],
        out_specs=[pl.BlockSpec((SEQ_TILE, w), lambda b, i: (fwd(b, i), 0)),
                   pl.BlockSpec((SEQ_TILE, w), lambda b, i: (rev(b, i), 0))],
        out_shape=[jax.ShapeDtypeStruct((n_lat, w), F32)] * 2,
        scratch_shapes=[pltpu.VMEM((1, w), F32), pltpu.VMEM((1, w), F32),
                        pltpu.VMEM((SEQ_TILE, w), F32), pltpu.VMEM((SEQ_TILE, w), F32)],
        compiler_params=_cparams("arbitrary", "arbitrary"),
        name="rglru",
    )(ux, ux, ux, ux, ux, ux, ux, cw, cb, wa_bd, wx_bd, ba, bx, lam)


def _merge_odd_kernel(up_ref, upp_ref, upn_ref, hf_ref, hr_ref, gg_ref, x_ref, mod_ref,
                      pw_ref, ps_ref, on_ref, wo_ref, nf_ref, wr_ref, br_ref,
                      xo_ref, tok_ref, route_ref, *, tiles_per_seq, seq):
    j = pl.program_id(0)
    ts = j % tiles_per_seq
    e = _with_halo(upp_ref, up_ref[...], upn_ref, ts == 0, ts == tiles_per_seq - 1)
    gw = pw_ref.shape[-1]
    lo, hi = SUBLANES, SUBLANES + SEQ_TILE
    tpos = ts * SEQ_TILE + lax.broadcasted_iota(jnp.int32, (SEQ_TILE, 1), 0)
    parts = []
    for g, win in enumerate(POOL_WINDOWS):
        eg = e[:, g * gw:(g + 1) * gw]
        p = eg
        span = 1
        while span < win:
            p = p + _shift_rows(p, span)
            span *= 2
        half = win // 2
        s = _shift_rows(p, -half)[lo:hi]
        cnt = (jnp.minimum(tpos + half, seq) - jnp.maximum(tpos - half, 0)).astype(F32)
        dlt = s / cnt - eg[lo:hi]
        parts.append(_bdot(dlt, pw_ref[g]))
    pool = jnp.concatenate(parts, axis=-1) * ps_ref[...]
    lru = (hf_ref[...] + hr_ref[...]) * gg_ref[...]
    _tail(pool, lru, on_ref[...], wo_ref, x_ref[...], mod_ref[0], nf_ref, wr_ref, br_ref,
          xo_ref, tok_ref, route_ref)


def _merge_odd(up, hf, hr, gg, xf, modl, pw_bf, ps, on, wo_bf, nf, wr, br, dims):
    n_tok, d = xf.shape
    n_batch, seq = dims
    tps = seq // SEQ_TILE
    n_lat = n_batch * seq
    mod_map, _ = _tile_maps(n_batch * tps, tps, n_batch)
    w = up.shape[1]
    row = lambda j: (j, 0)
    const = lambda j: (0, 0)
    prev, nxt = _halo_specs(w, n_tok)
    out_specs, out_shape = _tail_specs(d, n_lat)
    return pl.pallas_call(
        functools.partial(_merge_odd_kernel, tiles_per_seq=tps, seq=seq),
        grid=(n_lat // SEQ_TILE,),
        in_specs=[pl.BlockSpec((SEQ_TILE, w), row), prev, nxt,
                  pl.BlockSpec((SEQ_TILE, w), row), pl.BlockSpec((SEQ_TILE, w), row),
                  pl.BlockSpec((SEQ_TILE, w), row),
                  pl.BlockSpec((SEQ_TILE, d), row),
                  pl.BlockSpec((1, SUBLANES, d), mod_map),
                  pl.BlockSpec(pw_bf.shape, lambda j: (0, 0, 0)), pl.BlockSpec((1, w), const),
                  pl.BlockSpec((1, d), const), pl.BlockSpec(wo_bf.shape, const),
                  pl.BlockSpec((1, d), const), pl.BlockSpec(wr.shape, const),
                  pl.BlockSpec((1, LANES), const)],
        out_specs=out_specs, out_shape=out_shape,
        compiler_params=_cparams("arbitrary"),
        name="merge_odd",
    )(up, up, up, hf, hr, gg, xf, modl, pw_bf, ps, on, wo_bf, nf, wr, br)


def _slots_kernel(route_ref, dest_ref, cnt_ref, tot_ref, run_ref, start_ref):
    phase = pl.program_id(0)
    i = pl.program_id(1)
    rt = route_ref[...]
    m = rt.shape[0]
    lane = lax.broadcasted_iota(jnp.int32, rt.shape, 1).astype(F32)
    oh1 = lane == rt[:, 0:1]
    oh2 = lane == rt[:, 1:2]
    s = jnp.where(oh1 | oh2, 1.0, 0.0)
    colsum = jnp.sum(s, axis=0, keepdims=True)

    @pl.when((phase == 0) & (i == 0))
    def _():
        tot_ref[...] = jnp.zeros_like(tot_ref)

    @pl.when(phase == 0)
    def _():
        tot_ref[...] += colsum

    @pl.when((phase == 1) & (i == 0))
    def _():
        tot = tot_ref[...]
        padded = jnp.floor((tot + (MOE_BLOCK - 1)) / MOE_BLOCK) * MOE_BLOCK
        r = lax.broadcasted_iota(jnp.int32, (LANES, LANES), 0)
        c = lax.broadcasted_iota(jnp.int32, (LANES, LANES), 1)
        upper = jnp.where(r < c, 1.0, 0.0)
        p8 = jnp.broadcast_to(padded, (SUBLANES, LANES))
        start_ref[...] = jnp.dot(p8, upper, precision=HIGHEST, preferred_element_type=F32)[0:1]
        run_ref[...] = jnp.zeros_like(run_ref)
        cnt_ref[...] = jnp.broadcast_to(tot, cnt_ref.shape)

    @pl.when(phase == 1)
    def _():
        r = lax.broadcasted_iota(jnp.int32, (m, m), 0)
        c = lax.broadcasted_iota(jnp.int32, (m, m), 1)
        lower = jnp.where(c < r, 1.0, 0.0).astype(BF16)
        prefix = jnp.dot(lower, s.astype(BF16), preferred_element_type=F32)
        base = prefix + run_ref[...] + start_ref[...]
        d1 = jnp.sum(jnp.where(oh1, base, 0.0), axis=-1, keepdims=True)
        d2 = jnp.sum(jnp.where(oh2, base, 0.0), axis=-1, keepdims=True)
        dest_ref[...] = jnp.where(lane == 0, d1, jnp.where(lane == 1, d2, 0.0)).astype(jnp.int32)
        run_ref[...] += colsum


def _slots(route):
    n_tok = route.shape[0]
    nt = n_tok // TOK_TILE
    return pl.pallas_call(
        _slots_kernel,
        grid=(2, nt),
        in_specs=[pl.BlockSpec((TOK_TILE, LANES), lambda p, i: (i, 0))],
        out_specs=[pl.BlockSpec((TOK_TILE, LANES), lambda p, i: (i * p, 0)),
                   pl.BlockSpec((SUBLANES, LANES), lambda p, i: (0, 0))],
        out_shape=[jax.ShapeDtypeStruct((n_tok, LANES), jnp.int32),
                   jax.ShapeDtypeStruct((SUBLANES, LANES), F32)],
        scratch_shapes=[pltpu.VMEM((1, LANES), F32)] * 3,
        compiler_params=_cparams("arbitrary", "arbitrary"),
        name="moe_slots",
    )(route)


def _dispatch_kernel(dest_ref, tok_ref, xs_in_ref, xs_ref, sem):
    del xs_in_ref
    base = pl.program_id(0) * TOK_TILE

    def body(r, c):
        for k in range(TOP_K):
            d = dest_ref[(base + r) * TOP_K + k]
            pltpu.make_async_copy(tok_ref.at[pl.ds(r, 1)], xs_ref.at[pl.ds(d, 1)], sem).start()
        return c

    lax.fori_loop(0, TOK_TILE, body, 0)
    for k in range(TOP_K):
        pltpu.make_async_copy(tok_ref, xs_ref.at[pl.ds(0, TOK_TILE)], sem).wait()


def _dispatch(dest_flat, tok, n_slots):
    n_tok, d = tok.shape
    xs0 = jnp.zeros((n_slots, d), F32)
    return pl.pallas_call(
        _dispatch_kernel,
        grid_spec=pltpu.PrefetchScalarGridSpec(
            num_scalar_prefetch=1,
            grid=(n_tok // TOK_TILE,),
            in_specs=[pl.BlockSpec((TOK_TILE, d), lambda i, dest: (i, 0)),
                      pl.BlockSpec(memory_space=pl.ANY)],
            out_specs=pl.BlockSpec(memory_space=pl.ANY),
            scratch_shapes=[pltpu.SemaphoreType.DMA(())]),
        out_shape=jax.ShapeDtypeStruct((n_slots, d), F32),
        input_output_aliases={2: 0},
        compiler_params=_cparams("arbitrary"),
        name="moe_dispatch",
    )(dest_flat, tok, xs0)


def _expert_kernel(be_ref, nu_ref, x_ref, w1_ref, w3_ref, w2_ref, y_ref, w1b, w3b, w2b):
    i = pl.program_id(0)

    @pl.when(i < nu_ref[0])
    def _():
        changed = (i == 0) | (be_ref[i] != be_ref[jnp.maximum(i - 1, 0)])

        @pl.when(changed)
        def _():
            w1b[...] = w1_ref[0].astype(BF16)
            w3b[...] = w3_ref[0].astype(BF16)
            w2b[...] = w2_ref[0].astype(BF16)

        x = x_ref[...].astype(BF16)
        a = jnp.dot(x, w1b[...], preferred_element_type=F32)
        b = jnp.dot(x, w3b[...], preferred_element_type=F32)
        y_ref[...] = _bdot(_silu(a) * b, w2b[...])


def _experts(block_e, n_used, xs, w1, w3, w2):
    n_slots, d = xs.shape
    de = w1.shape[-1]
    n_blocks = n_slots // MOE_BLOCK
    blk = lambda i, be, nu: (jnp.minimum(i, nu[0] - 1), 0)
    wmap = lambda i, be, nu: (be[i], 0, 0)
    return pl.pallas_call(
        _expert_kernel,
        grid_spec=pltpu.PrefetchScalarGridSpec(
            num_scalar_prefetch=2,
            grid=(n_blocks,),
            in_specs=[pl.BlockSpec((MOE_BLOCK, d), blk),
                      pl.BlockSpec((1, d, de), wmap), pl.BlockSpec((1, d, de), wmap),
                      pl.BlockSpec((1, de, d), wmap)],
            out_specs=pl.BlockSpec((MOE_BLOCK, d), blk),
            scratch_shapes=[pltpu.VMEM((d, de), BF16), pltpu.VMEM((d, de), BF16),
                            pltpu.VMEM((de, d), BF16)]),
        out_shape=jax.ShapeDtypeStruct((n_slots, d), F32),
        compiler_params=_cparams("arbitrary"),
        name="moe_experts",
    )(block_e, n_used, xs, w1, w3, w2)


def _combine_kernel(dest_ref, x_ref, route_ref, mod_ref, ys_ref, o_ref, ybuf, sem):
    base = pl.program_id(0) * TOK_TILE

    def body(r, c):
        for k in range(TOP_K):
            d = dest_ref[(base + r) * TOP_K + k]
            pltpu.make_async_copy(ys_ref.at[pl.ds(d, 1)], ybuf.at[k, pl.ds(r, 1)], sem).start()
        return c

    lax.fori_loop(0, TOK_TILE, body, 0)
    for k in range(TOP_K):
        pltpu.make_async_copy(ys_ref.at[pl.ds(0, TOK_TILE)], ybuf.at[k], sem).wait()
    rt = route_ref[...]
    y = rt[:, 2:3] * ybuf[0] + rt[:, 3:4] * ybuf[1]
    o_ref[...] = x_ref[...] + mod_ref[0][5:6] * y


def _combine(dest_flat, xn, route, modl, ys, dims):
    n_tok, d = xn.shape
    n_batch, seq = dims
    tpb = seq // TOK_TILE
    return pl.pallas_call(
        _combine_kernel,
        grid_spec=pltpu.PrefetchScalarGridSpec(
            num_scalar_prefetch=1,
            grid=(n_tok // TOK_TILE,),
            in_specs=[pl.BlockSpec((TOK_TILE, d), lambda i, dest: (i, 0)),
                      pl.BlockSpec((TOK_TILE, LANES), lambda i, dest: (i, 0)),
                      pl.BlockSpec((1, SUBLANES, d), lambda i, dest: (jnp.minimum(i // tpb, n_batch), 0, 0)),
                      pl.BlockSpec(memory_space=pl.ANY)],
            out_specs=pl.BlockSpec((TOK_TILE, d), lambda i, dest: (i, 0)),
            scratch_shapes=[pltpu.VMEM((TOP_K, TOK_TILE, d), F32), pltpu.SemaphoreType.DMA(())]),
        out_shape=jax.ShapeDtypeStruct((n_tok, d), F32),
        compiler_params=_cparams("arbitrary"),
        name="moe_combine",
    )(dest_flat, xn, route, modl, ys)


def _moe(xn, tok, route, modl, w1, w3, w2, dims):
    n_tok = tok.shape[0]
    n = n_tok * TOP_K
    n_blocks = (n + N_EXPERTS * (MOE_BLOCK - 1) + MOE_BLOCK - 1) // MOE_BLOCK
    dest, counts = _slots(route)
    dest_flat = dest[:, :TOP_K].reshape(-1)
    cnt = counts[0, :N_EXPERTS].astype(jnp.int32)
    blocks_per_e = (cnt + MOE_BLOCK - 1) // MOE_BLOCK
    blk_end = jnp.cumsum(blocks_per_e)
    n_used = blk_end[-1:]
    blk = jnp.minimum(jnp.arange(n_blocks, dtype=jnp.int32), n_used[0] - 1)
    block_e = jnp.minimum(jnp.searchsorted(blk_end, blk, side="right"), N_EXPERTS - 1).astype(jnp.int32)
    xs = _dispatch(dest_flat, tok, n_blocks * MOE_BLOCK)
    ys = _experts(block_e, n_used.astype(jnp.int32), xs, w1, w3, w2)
    return _combine(dest_flat, xn, route, modl, ys, dims)


def _rope_tables(seq):
    rows = seq // GRID_W
    row = jnp.repeat(jnp.arange(rows), GRID_W).astype(F32)
    col = jnp.tile(jnp.arange(GRID_W), rows).astype(F32)
    inv_freq = ROPE_BASE ** (-jnp.arange(ROPE_PAIRS, dtype=F32) / ROPE_PAIRS)
    ar, ac = row[:, None] * inv_freq, col[:, None] * inv_freq
    cos = jnp.concatenate([jnp.cos(ar), jnp.cos(ar), jnp.cos(ac), jnp.cos(ac)], axis=-1)
    sin = jnp.concatenate([-jnp.sin(ar), jnp.sin(ar), -jnp.sin(ac), jnp.sin(ac)], axis=-1)
    reps = LANES // HEAD_DIM
    cos, sin = jnp.tile(cos, (1, reps)), jnp.tile(sin, (1, reps))
    cos = jnp.concatenate([cos, jnp.ones((SEQ_TILE, LANES), F32)], axis=0)
    sin = jnp.concatenate([sin, jnp.zeros((SEQ_TILE, LANES), F32)], axis=0)
    return cos, sin


def _head_mean_matrix():
    r = jnp.arange(LANES)
    return jnp.where((r[:, None] // HEAD_DIM) == (r[None, :] // HEAD_DIM), 1.0 / HEAD_DIM, 0.0).astype(F32)


def _block_diag_chunks(w, chunk):
    dirs, nblk, bw, _ = w.shape
    per = chunk // bw
    w = w.reshape(dirs, nblk // per, per, bw, bw)
    eye = jnp.eye(per, dtype=w.dtype)
    out = jnp.einsum("dcpij,pq->dcpiqj", w, eye)
    return out.reshape(dirs, nblk // per, chunk, chunk).astype(BF16)


def _router_matrix(gw, gb, ew, eb):
    d = gw.shape[0]
    pad = LANES - N_EXPERTS - N_GROUPS
    wr = jnp.concatenate([ew, gw, jnp.zeros((d, pad), F32)], axis=1)
    br = jnp.concatenate([eb, gb, jnp.zeros((pad,), F32)])[None, :]
    return wr, br


def kernel(x, c, ctx, c_ctx, ada_w, ada_b, norm_mix, norm_ffn, out_norm, w_out, w_in_ab, q_norm, k_norm, attn_sink, sconv_w, sconv_b, w_in_cd, pool_w, pool_scale, lru_conv_w, lru_conv_b, lru_wa, lru_ba, lru_wx, lru_bx, lru_lambda, router_gw, router_gb, router_ew, router_eb, exp_w1, exp_w3, exp_w2):
    n_batch, seq, d = x.shape
    ctx_len = ctx.shape[1]
    depth = ada_w.shape[0]
    assert ctx_len == SEQ_TILE and seq % TOK_TILE == 0 and seq % GRID_W == 0
    assert depth == 2, "context-side odd-layer outputs are not implemented"
    dims = (n_batch, seq)
    n_lat = n_batch * seq

    cc = jnp.concatenate([c, c_ctx[None, :], jnp.zeros((SUBLANES - n_batch - 1, d), F32)], axis=0)
    mods = _ada(cc, ada_w, ada_b)
    mods = mods[:, :n_batch + 1].reshape(depth, n_batch + 1, N_MOD, d)
    mods = jnp.pad(mods, ((0, 0), (0, 0), (0, SUBLANES - N_MOD), (0, 0)))

    xf = jnp.concatenate([x.reshape(n_lat, d), ctx.reshape(n_batch * ctx_len, d)], axis=0)
    cos_t, sin_t = _rope_tables(seq)
    pm = _head_mean_matrix()

    for i in range(depth):
        need_ctx = i < depth - 1
        j = i // 2
        modl = mods[i]
        n_out = xf.shape[0] if need_ctx else n_lat
        wr, br = _router_matrix(router_gw[i], router_gb[i], router_ew[i], router_eb[i])
        wo_bf = w_out[i].astype(BF16)
        nm, nf, on = norm_mix[i][None, :], norm_ffn[i][None, :], out_norm[i][None, :]
        if i % 2 == 0:
            reps = LANES // HEAD_DIM
            q, k, v, gb, u = _inproj_even(xf, modl, nm, w_in_ab[j].astype(BF16),
                                          jnp.tile(q_norm[j], reps)[None, :], jnp.tile(k_norm[j], reps)[None, :],
                                          cos_t, sin_t, pm, dims)
            att = _attention(q, k, v, attn_sink[j], dims, need_ctx)
            xn, tok, route = _merge_even(att, gb, u, xf, modl, sconv_w[j], sconv_b[j][None, :], on, wo_bf,
                                         nf, wr, br, dims, n_out)
        else:
            ux, gg, up = _inproj_odd(xf, modl, nm, w_in_cd[j].astype(BF16), dims)
            chunk = 2 * LANES
            hf, hr = _lru(ux, lru_conv_w[j], lru_conv_b[j][None, :],
                          _block_diag_chunks(lru_wa[j], chunk), _block_diag_chunks(lru_wx[j], chunk),
                          lru_ba[j][:, None, :], lru_bx[j][:, None, :], lru_lambda[j][:, None, :], dims)
            xn, tok, route = _merge_odd(up, hf, hr, gg, xf, modl, pool_w[j].astype(BF16),
                                        pool_scale[j][None, :], on, wo_bf, nf, wr, br, dims)
        xf = _moe(xn, tok, route, modl, exp_w1[i], exp_w3[i], exp_w2[i], dims)
    return xf[:n_lat].reshape(n_batch, seq, d)
```

```python
import functools

import jax
import jax.numpy as jnp
from jax import lax
from jax.experimental import pallas as pl
from jax.experimental.pallas import tpu as pltpu

F32 = jnp.float32
BF16 = jnp.bfloat16
HIGHEST = lax.Precision.HIGHEST

GRID_W = 64
EPS = 1e-6
HEAD_DIM = 64
N_Q_HEADS = 8
N_KV_HEADS = 2
Q_PER_KV = N_Q_HEADS // N_KV_HEADS
WINDOW = 128
ATT_BLOCK = 128
ROPE_PAIRS = HEAD_DIM // 4
ROPE_BASE = 10000.0
POOL_WINDOWS = (2, 4, 8, 16)
LRU_C = 8.0
N_GROUPS = 4
EXPERTS_PER_GROUP = 8
N_EXPERTS = N_GROUPS * EXPERTS_PER_GROUP
TOP_K = 2
MOE_BLOCK = 128
N_MOD = 6

SUBLANES = 8
LANES = 128
SEQ_TILE = 256
TOK_TILE = 512
NEG = -1e30
VMEM_LIMIT = 48 * 1024 * 1024


def _cparams(*sem):
    return pltpu.CompilerParams(dimension_semantics=sem, vmem_limit_bytes=VMEM_LIMIT)


def _rms(x, g):
    ms = jnp.mean(x * x, axis=-1, keepdims=True)
    return x * lax.rsqrt(ms + EPS) * g


def _modulate(x, g, shift, scale):
    return _rms(x, g) * (1.0 + scale) + shift


def _bdot(a, b):
    return jnp.dot(a.astype(BF16), b, preferred_element_type=F32)


def _silu(x):
    return x * jax.nn.sigmoid(x)


def _ada_kernel(c_ref, w_ref, b_ref, o_ref):
    s = _silu(c_ref[...])
    o_ref[0] = jnp.dot(s, w_ref[0], precision=HIGHEST, preferred_element_type=F32) + b_ref[0]


def _ada(cc, ada_w, ada_b):
    depth, d, n = ada_w.shape
    tn = 1536
    return pl.pallas_call(
        _ada_kernel,
        grid=(depth, n // tn),
        in_specs=[pl.BlockSpec((SUBLANES, d), lambda l, j: (0, 0)),
                  pl.BlockSpec((1, d, tn), lambda l, j: (l, 0, j)),
                  pl.BlockSpec((1, 1, tn), lambda l, j: (l, 0, j))],
        out_specs=pl.BlockSpec((1, SUBLANES, tn), lambda l, j: (l, 0, j)),
        out_shape=jax.ShapeDtypeStruct((depth, SUBLANES, n), F32),
        compiler_params=_cparams("arbitrary", "arbitrary"),
        name="ada",
    )(cc, ada_w, ada_b.reshape(depth, 1, n))


def _head_norm_rope(t, pm, g, cos, sin):
    ms = jnp.dot(t * t, pm, precision=HIGHEST, preferred_element_type=F32)
    t = t * lax.rsqrt(ms + EPS) * g
    lane = lax.broadcasted_iota(jnp.int32, t.shape, 1)
    first_half = (lane % (2 * ROPE_PAIRS)) < ROPE_PAIRS
    partner = jnp.where(first_half, pltpu.roll(t, LANES - ROPE_PAIRS, 1), pltpu.roll(t, ROPE_PAIRS, 1))
    return t * cos + partner * sin


def _token_operands(xparts, n_lat_tiles):
    d = xparts[0].shape[1]
    lat = pl.BlockSpec((SEQ_TILE, d), lambda j: (jnp.minimum(j, n_lat_tiles - 1), 0))
    if len(xparts) == 2:
        ctx = pl.BlockSpec((SEQ_TILE, d), lambda j: (jnp.maximum(j - n_lat_tiles, 0), 0))
        return list(xparts), [lat, ctx]
    ctx = pl.BlockSpec((SEQ_TILE, d), lambda j: (jnp.maximum(j, n_lat_tiles), 0))
    return [xparts[0], xparts[0]], [lat, ctx]


def _pick_tokens(xl_ref, xc_ref, n_lat_tiles):
    return jnp.where(pl.program_id(0) >= n_lat_tiles, xc_ref[...], xl_ref[...])


def _inproj_even_kernel(xl_ref, xc_ref, mod_ref, g_ref, w_ref, qg_ref, kg_ref, cos_ref, sin_ref, pm_ref,
                        q_ref, k_ref, v_ref, gb_ref, u_ref, *, n_lat_tiles):
    mod = mod_ref[0]
    x = _pick_tokens(xl_ref, xc_ref, n_lat_tiles)
    h = _modulate(x, g_ref[...], mod[0:1], mod[1:2]).astype(BF16)
    cos, sin, pm = cos_ref[...], sin_ref[...], pm_ref[...]
    kv_w = N_KV_HEADS * HEAD_DIM
    att_w = N_Q_HEADS * HEAD_DIM
    sc_w = gb_ref.shape[1]
    kv = jnp.dot(h, w_ref[:, 0:2 * kv_w], preferred_element_type=F32)
    k_ref[...] = _head_norm_rope(kv[:, :kv_w], pm, kg_ref[...], cos, sin).astype(BF16)
    v_ref[...] = kv[:, kv_w:].astype(BF16)
    q = jnp.dot(h, w_ref[:, 2 * kv_w:2 * kv_w + att_w], preferred_element_type=F32)
    qscale = HEAD_DIM ** -0.5
    for c in range(att_w // LANES):
        qc = _head_norm_rope(q[:, c * LANES:(c + 1) * LANES], pm, qg_ref[...], cos, sin)
        q_ref[:, c * LANES:(c + 1) * LANES] = (qc * qscale).astype(BF16)
    o = 2 * kv_w + att_w
    gb_ref[...] = jnp.dot(h, w_ref[:, o:o + sc_w], preferred_element_type=F32)
    gc = jnp.dot(h, w_ref[:, o + sc_w:o + 2 * sc_w], preferred_element_type=F32)
    gx = jnp.dot(h, w_ref[:, o + 2 * sc_w:o + 3 * sc_w], preferred_element_type=F32)
    u_ref[...] = gc * gx


def _tile_maps(n_lat_tiles, tiles_per_seq, n_batch):
    def mod_map(j):
        return (jnp.minimum(j // tiles_per_seq, n_batch), 0, 0)

    def pos_map(j):
        return (jnp.where(j < n_lat_tiles, j % tiles_per_seq, tiles_per_seq), 0)

    return mod_map, pos_map


def _inproj_even(xparts, modl, g, w_bf, q_g, k_g, cos_t, sin_t, pm, dims):
    n_tok = sum(p.shape[0] for p in xparts)
    d = xparts[0].shape[1]
    n_batch, seq = dims
    tps = seq // SEQ_TILE
    mod_map, pos_map = _tile_maps(n_batch * tps, tps, n_batch)
    xs, xspecs = _token_operands(xparts, n_batch * tps)
    kv_w, att_w = N_KV_HEADS * HEAD_DIM, N_Q_HEADS * HEAD_DIM
    sc_w = (w_bf.shape[1] - 2 * kv_w - att_w) // 3
    row = lambda j: (j, 0)
    const = lambda j: (0, 0)
    return pl.pallas_call(
        functools.partial(_inproj_even_kernel, n_lat_tiles=n_batch * tps),
        grid=(n_tok // SEQ_TILE,),
        in_specs=xspecs + [
                  pl.BlockSpec((1, SUBLANES, d), mod_map),
                  pl.BlockSpec((1, d), const),
                  pl.BlockSpec(w_bf.shape, const),
                  pl.BlockSpec((1, LANES), const),
                  pl.BlockSpec((1, LANES), const),
                  pl.BlockSpec((SEQ_TILE, LANES), pos_map),
                  pl.BlockSpec((SEQ_TILE, LANES), pos_map),
                  pl.BlockSpec((LANES, LANES), const)],
        out_specs=[pl.BlockSpec((SEQ_TILE, att_w), row),
                   pl.BlockSpec((SEQ_TILE, kv_w), row),
                   pl.BlockSpec((SEQ_TILE, kv_w), row),
                   pl.BlockSpec((SEQ_TILE, sc_w), row),
                   pl.BlockSpec((SEQ_TILE, sc_w), row)],
        out_shape=[jax.ShapeDtypeStruct((n_tok, att_w), BF16),
                   jax.ShapeDtypeStruct((n_tok, kv_w), BF16),
                   jax.ShapeDtypeStruct((n_tok, kv_w), BF16),
                   jax.ShapeDtypeStruct((n_tok, sc_w), F32),
                   jax.ShapeDtypeStruct((n_tok, sc_w), F32)],
        compiler_params=_cparams("arbitrary"),
        name="inproj_even",
    )(*xs, modl, g, w_bf, q_g, k_g, cos_t, sin_t, pm)


def _attn_heads(q, kb, vb, bias, sink_ref, o_ref):
    outs = []
    for hq in range(N_Q_HEADS):
        hk = hq // Q_PER_KV
        qh = q[:, hq * HEAD_DIM:(hq + 1) * HEAD_DIM]
        kh = kb[:, hk * HEAD_DIM:(hk + 1) * HEAD_DIM]
        vh = vb[:, hk * HEAD_DIM:(hk + 1) * HEAD_DIM]
        s = lax.dot_general(qh, kh, (((1,), (1,)), ((), ())), preferred_element_type=F32)
        if bias is not None:
            s = s + bias
        snk = sink_ref[hq]
        m = jnp.maximum(jnp.max(s, axis=-1, keepdims=True), snk)
        p = jnp.exp(s - m)
        den = jnp.sum(p, axis=-1, keepdims=True) + jnp.exp(snk - m)
        o = jnp.dot(p.astype(BF16), vh, preferred_element_type=F32)
        outs.append(o / den)
    o_ref[...] = jnp.concatenate(outs, axis=-1)


def _attn_lat_kernel(sink_ref, q_ref, kp_ref, kc_ref, kn_ref, kx_ref, vp_ref, vc_ref, vn_ref, vx_ref,
                     o_ref):
    i = pl.program_id(1)
    nb = pl.num_programs(1)
    kb = jnp.concatenate([kp_ref[...], kc_ref[...], kn_ref[...], kx_ref[...]], axis=0)
    vb = jnp.concatenate([vp_ref[...], vc_ref[...], vn_ref[...], vx_ref[...]], axis=0)
    m, n = ATT_BLOCK, kb.shape[0]
    row = lax.broadcasted_iota(jnp.int32, (m, n), 0)
    col = lax.broadcasted_iota(jnp.int32, (m, n), 1)
    rel = col - row
    lo = jnp.where(i == 0, ATT_BLOCK, 0)
    hi = jnp.where(i == nb - 1, 2 * ATT_BLOCK, 3 * ATT_BLOCK)
    local_ok = (rel >= 0) & (rel <= 2 * WINDOW) & (col >= lo) & (col < hi)
    valid = (col >= 3 * ATT_BLOCK) | local_ok
    bias = jnp.where(valid, 0.0, NEG)
    _attn_heads(q_ref[...], kb, vb, bias, sink_ref, o_ref)


def _attn_ctx_kernel(sink_ref, att_in_ref, q_ref, kx_ref, vx_ref, o_ref):
    del att_in_ref
    _attn_heads(q_ref[...], kx_ref[...], vx_ref[...], None, sink_ref, o_ref)


def _attention(q, k, v, sink, dims, need_ctx):
    n_tok = q.shape[0]
    n_batch, seq = dims
    nb = seq // ATT_BLOCK
    n_lat = n_batch * seq
    ctx_len = (n_tok - n_lat) // n_batch
    att_w, kv_w = q.shape[1], k.shape[1]
    ctx_blk0 = n_lat // ctx_len
    cur = lambda b, i: (b * nb + i, 0)
    prev = lambda b, i: (b * nb + jnp.maximum(i - 1, 0), 0)
    nxt = lambda b, i: (b * nb + jnp.minimum(i + 1, nb - 1), 0)
    cx = lambda b, i: (ctx_blk0 + b, 0)
    smem = pl.BlockSpec(memory_space=pltpu.SMEM)
    kvb = lambda f: pl.BlockSpec((ATT_BLOCK, kv_w), f)
    att = pl.pallas_call(
        _attn_lat_kernel,
        grid=(n_batch, nb),
        in_specs=[smem, pl.BlockSpec((ATT_BLOCK, att_w), cur),
                  kvb(prev), kvb(cur), kvb(nxt), pl.BlockSpec((ctx_len, kv_w), cx),
                  kvb(prev), kvb(cur), kvb(nxt), pl.BlockSpec((ctx_len, kv_w), cx)],
        out_specs=pl.BlockSpec((ATT_BLOCK, att_w), cur),
        out_shape=jax.ShapeDtypeStruct((n_tok, att_w), F32),
        compiler_params=_cparams("arbitrary", "arbitrary"),
        name="attn_latent",
    )(sink, q, k, k, k, k, v, v, v, v)
    if not need_ctx:
        return att
    cxb = lambda b: (ctx_blk0 + b, 0)
    return pl.pallas_call(
        _attn_ctx_kernel,
        grid=(n_batch,),
        in_specs=[smem, pl.BlockSpec(memory_space=pl.ANY),
                  pl.BlockSpec((ctx_len, att_w), cxb),
                  pl.BlockSpec((ctx_len, kv_w), cxb), pl.BlockSpec((ctx_len, kv_w), cxb)],
        out_specs=pl.BlockSpec((ctx_len, att_w), cxb),
        out_shape=jax.ShapeDtypeStruct((n_tok, att_w), F32),
        input_output_aliases={1: 0},
        compiler_params=_cparams("arbitrary"),
        name="attn_context",
    )(sink, att, q, k, v)


def _route(logits):
    lane = lax.broadcasted_iota(jnp.int32, logits.shape, 1).astype(F32)
    big = float(LANES)
    gl = jnp.where((lane >= N_EXPERTS) & (lane < N_EXPERTS + N_GROUPS), logits, NEG)
    gmax = jnp.max(gl, axis=-1, keepdims=True)
    gidx = jnp.min(jnp.where(gl == gmax, lane, big), axis=-1, keepdims=True) - N_EXPERTS
    p_sel = 1.0 / jnp.sum(jnp.exp(gl - gmax), axis=-1, keepdims=True)
    lo = gidx * EXPERTS_PER_GROUP
    el = jnp.where((lane >= lo) & (lane < lo + EXPERTS_PER_GROUP), logits, NEG)
    m1 = jnp.max(el, axis=-1, keepdims=True)
    i1 = jnp.min(jnp.where(el == m1, lane, big), axis=-1, keepdims=True)
    el2 = jnp.where(lane == i1, NEG, el)
    m2 = jnp.max(el2, axis=-1, keepdims=True)
    i2 = jnp.min(jnp.where(el2 == m2, lane, big), axis=-1, keepdims=True)
    t = jnp.exp(m2 - m1)
    w1 = p_sel / (1.0 + t)
    w2 = p_sel * t / (1.0 + t)
    return jnp.where(lane == 0, i1, jnp.where(lane == 1, i2, jnp.where(lane == 2, w1,
                     jnp.where(lane == 3, w2, 0.0))))


def _tail(n1, n2, gain, wo_ref, x, mod, nf_ref, wr_ref, br_ref, xo_ref, tok_ref, route_ref):
    half = n1.shape[1]
    y = (_bdot(_rms(n1, gain[:, :half]), wo_ref[0:half, :])
         + _bdot(_rms(n2, gain[:, half:]), wo_ref[half:, :]))
    xn = x + mod[2:3] * y
    xo_ref[...] = xn
    tok = _modulate(xn, nf_ref[...], mod[3:4], mod[4:5])
    tok_ref[...] = tok
    logits = jnp.dot(tok, wr_ref[...], precision=HIGHEST, preferred_element_type=F32) + br_ref[...]
    route_ref[...] = _route(logits)


def _seq_flags(j, n_lat_tiles, tiles_per_seq):
    is_ctx = j >= n_lat_tiles
    first = is_ctx | (j % tiles_per_seq == 0)
    last = is_ctx | (j % tiles_per_seq == tiles_per_seq - 1)
    return first, last


def _with_halo(prev_ref, tile, next_ref, first, last):
    p = jnp.where(first, 0.0, prev_ref[...])
    n = jnp.where(last, 0.0, next_ref[...])
    return jnp.concatenate([p, tile, n], axis=0)


def _shift_rows(e, k):
    return pltpu.roll(e, (-k) % e.shape[0], 0)


def _merge_even_kernel(att_ref, gb_ref, u_ref, up_ref, un_ref, xl_ref, xc_ref, mod_ref, cw_ref, cb_ref,
                       on_ref, wo_ref, nf_ref, wr_ref, br_ref,
                       xo_ref, tok_ref, route_ref, *, n_lat_tiles, tiles_per_seq):
    j = pl.program_id(0)
    first, last = _seq_flags(j, n_lat_tiles, tiles_per_seq)
    e = _with_halo(up_ref, u_ref[...], un_ref, first, last)
    cw = cw_ref[...]
    lo, hi = SUBLANES, SUBLANES + SEQ_TILE
    conv = (cw[0:1] * _shift_rows(e, -1)[lo:hi] + cw[1:2] * e[lo:hi]
            + cw[2:3] * _shift_rows(e, 1)[lo:hi] + cb_ref[...])
    conv = gb_ref[...] * conv
    x = _pick_tokens(xl_ref, xc_ref, n_lat_tiles)
    _tail(att_ref[...], conv, on_ref[...], wo_ref, x, mod_ref[0], nf_ref, wr_ref, br_ref,
          xo_ref, tok_ref, route_ref)


def _halo_specs(width, n_rows):
    r = SEQ_TILE // SUBLANES
    n_blk = n_rows // SUBLANES
    prev = pl.BlockSpec((SUBLANES, width), lambda j: (jnp.maximum(j * r - 1, 0), 0))
    nxt = pl.BlockSpec((SUBLANES, width), lambda j: (jnp.minimum((j + 1) * r, n_blk - 1), 0))
    return prev, nxt


def _tail_specs(d, n_out_tok):
    row = lambda j: (j, 0)
    out_specs = [pl.BlockSpec((SEQ_TILE, d), row), pl.BlockSpec((SEQ_TILE, d), row),
                 pl.BlockSpec((SEQ_TILE, LANES), row)]
    out_shape = [jax.ShapeDtypeStruct((n_out_tok, d), F32), jax.ShapeDtypeStruct((n_out_tok, d), F32),
                 jax.ShapeDtypeStruct((n_out_tok, LANES), F32)]
    return out_specs, out_shape


def _merge_even(att, gb, u, xparts, modl, cw, cb, on, wo_bf, nf, wr, br, dims, n_out_tok):
    n_tok = u.shape[0]
    d = xparts[0].shape[1]
    n_batch, seq = dims
    tps = seq // SEQ_TILE
    mod_map, _ = _tile_maps(n_batch * tps, tps, n_batch)
    xs, xspecs = _token_operands(xparts, n_batch * tps)
    sc_w = u.shape[1]
    row = lambda j: (j, 0)
    const = lambda j: (0, 0)
    prev, nxt = _halo_specs(sc_w, n_tok)
    out_specs, out_shape = _tail_specs(d, n_out_tok)
    return pl.pallas_call(
        functools.partial(_merge_even_kernel, n_lat_tiles=n_batch * tps, tiles_per_seq=tps),
        grid=(n_out_tok // SEQ_TILE,),
        in_specs=[pl.BlockSpec((SEQ_TILE, att.shape[1]), row),
                  pl.BlockSpec((SEQ_TILE, sc_w), row),
                  pl.BlockSpec((SEQ_TILE, sc_w), row), prev, nxt] + xspecs + [
                  pl.BlockSpec((1, SUBLANES, d), mod_map),
                  pl.BlockSpec(cw.shape, const), pl.BlockSpec((1, sc_w), const),
                  pl.BlockSpec((1, d), const), pl.BlockSpec(wo_bf.shape, const),
                  pl.BlockSpec((1, d), const), pl.BlockSpec(wr.shape, const),
                  pl.BlockSpec((1, LANES), const)],
        out_specs=out_specs, out_shape=out_shape,
        compiler_params=_cparams("arbitrary"),
        name="merge_even",
    )(att, gb, u, u, u, *xs, modl, cw, cb, on, wo_bf, nf, wr, br)


def _inproj_odd_kernel(xl_ref, xc_ref, mod_ref, g_ref, w_ref, ux_ref, gg_ref, up_ref, *, n_lat_tiles):
    mod = mod_ref[0]
    x = _pick_tokens(xl_ref, xc_ref, n_lat_tiles)
    h = _modulate(x, g_ref[...], mod[0:1], mod[1:2]).astype(BF16)
    w = ux_ref.shape[1]
    ux_ref[...] = jnp.dot(h, w_ref[:, 0:w], preferred_element_type=F32)
    ug = jnp.dot(h, w_ref[:, w:2 * w], preferred_element_type=F32)
    gg_ref[...] = jax.nn.gelu(ug, approximate=True)
    up_ref[...] = jnp.dot(h, w_ref[:, 2 * w:3 * w], preferred_element_type=F32)


def _inproj_odd(xparts, modl, g, w_bf, dims):
    n_tok = sum(p.shape[0] for p in xparts)
    d = xparts[0].shape[1]
    n_batch, seq = dims
    tps = seq // SEQ_TILE
    mod_map, _ = _tile_maps(n_batch * tps, tps, n_batch)
    xs, xspecs = _token_operands(xparts, n_batch * tps)
    w = w_bf.shape[1] // 3
    row = lambda j: (j, 0)
    const = lambda j: (0, 0)
    return pl.pallas_call(
        functools.partial(_inproj_odd_kernel, n_lat_tiles=n_batch * tps),
        grid=(n_tok // SEQ_TILE,),
        in_specs=xspecs + [
                  pl.BlockSpec((1, SUBLANES, d), mod_map),
                  pl.BlockSpec((1, d), const),
                  pl.BlockSpec(w_bf.shape, const)],
        out_specs=[pl.BlockSpec((SEQ_TILE, w), row)] * 3,
        out_shape=[jax.ShapeDtypeStruct((n_tok, w), F32)] * 3,
        compiler_params=_cparams("arbitrary"),
        name="inproj_odd",
    )(*xs, modl, g, w_bf)


def _lru_conv(e, cw, cb):
    lo, hi = SUBLANES, SUBLANES + SEQ_TILE
    return (cw[0:1] * _shift_rows(e, -2)[lo:hi] + cw[1:2] * _shift_rows(e, -1)[lo:hi]
            + cw[2:3] * e[lo:hi] + cw[3:4] * _shift_rows(e, 1)[lo:hi] + cb)


def _lru_coeffs(xl, d, wa_ref, wx_ref, ba_ref, bx_ref, lam_ref, a_ref, b_ref):
    xb = xl.astype(BF16)
    chunk = wa_ref.shape[-1]
    n_chunk = xl.shape[1] // chunk
    pre_a = jnp.concatenate([jnp.dot(xb[:, c * chunk:(c + 1) * chunk], wa_ref[d, c],
                                     preferred_element_type=F32) for c in range(n_chunk)], axis=-1)
    pre_x = jnp.concatenate([jnp.dot(xb[:, c * chunk:(c + 1) * chunk], wx_ref[d, c],
                                     preferred_element_type=F32) for c in range(n_chunk)], axis=-1)
    r = jax.nn.sigmoid(pre_a + ba_ref[d])
    gi = jax.nn.sigmoid(pre_x + bx_ref[d])
    log_a = (-LRU_C * jax.nn.softplus(-lam_ref[d])) * r
    a_ref[...] = jnp.exp(log_a)
    th = jnp.tanh(log_a)
    b_ref[...] = jnp.sqrt(-2.0 * th / (1.0 - th)) * gi * xl


def _scan_tile(a_ref, b_ref, h_ref, carry, reverse):
    n_grp = a_ref.shape[0] // SUBLANES
    row = lax.broadcasted_iota(jnp.int32, (SUBLANES, a_ref.shape[1]), 0)

    def body(g, carry):
        gi = (n_grp - 1 - g) if reverse else g
        r0 = pl.multiple_of(gi * SUBLANES, SUBLANES)
        a8 = a_ref[pl.ds(r0, SUBLANES), :]
        b8 = b_ref[pl.ds(r0, SUBLANES), :]
        for s in (1, 2, 4):
            if reverse:
                ok = row < SUBLANES - s
                sh = SUBLANES - s
            else:
                ok = row >= s
                sh = s
            a_sh = jnp.where(ok, pltpu.roll(a8, sh, 0), 1.0)
            b_sh = jnp.where(ok, pltpu.roll(b8, sh, 0), 0.0)
            b8 = a8 * b_sh + b8
            a8 = a8 * a_sh
        h8 = a8 * carry + b8
        if h_ref is not None:
            h_ref[pl.ds(r0, SUBLANES), :] = h8
        return h8[0:1] if reverse else h8[SUBLANES - 1:SUBLANES]

    return lax.fori_loop(0, n_grp, body, carry)


def _lru_kernel(uf_ref, ufp_ref, ufn_ref, ur_ref, urp_ref, urn_ref, uc_ref, cw_ref, cb_ref,
                wa_ref, wx_ref, ba_ref, bx_ref, lam_ref, hf_ref, hr_ref,
                cf_ref, cr_ref, a_ref, b_ref):
    i = pl.program_id(1)
    nt = pl.num_programs(1)
    cw, cb = cw_ref[...], cb_ref[...]
    coeffs = functools.partial(_lru_coeffs, wa_ref=wa_ref, wx_ref=wx_ref, ba_ref=ba_ref,
                               bx_ref=bx_ref, lam_ref=lam_ref, a_ref=a_ref, b_ref=b_ref)

    @pl.when(i == 0)
    def _():
        z = jnp.zeros((SUBLANES, uc_ref.shape[1]), F32)
        xc = _lru_conv(jnp.concatenate([z, uc_ref[...], z], axis=0), cw, cb)
        zero = jnp.zeros((1, uc_ref.shape[1]), F32)
        coeffs(xc, 0)
        cf_ref[...] = _scan_tile(a_ref, b_ref, None, zero, False)
        coeffs(xc, 1)
        cr_ref[...] = _scan_tile(a_ref, b_ref, None, zero, True)

    xf = _lru_conv(_with_halo(ufp_ref, uf_ref[...], ufn_ref, i == 0, i == nt - 1), cw, cb)
    coeffs(xf, 0)
    cf_ref[...] = _scan_tile(a_ref, b_ref, hf_ref, cf_ref[...], False)
    xr = _lru_conv(_with_halo(urp_ref, ur_ref[...], urn_ref, i == nt - 1, i == 0), cw, cb)
    coeffs(xr, 1)
    cr_ref[...] = _scan_tile(a_ref, b_ref, hr_ref, cr_ref[...], True)


def _lru(ux, cw, cb, wa_bd, wx_bd, ba, bx, lam, dims):
    n_tok, w = ux.shape
    n_batch, seq = dims
    nt = seq // SEQ_TILE
    n_lat = n_batch * seq
    r = SEQ_TILE // SUBLANES
    n_blk = n_tok // SUBLANES
    fwd = lambda b, i: b * nt + i
    rev = lambda b, i: b * nt + nt - 1 - i

    def specs(tile):
        return [pl.BlockSpec((SEQ_TILE, w), lambda b, i: (tile(b, i), 0)),
                pl.BlockSpec((SUBLANES, w), lambda b, i: (jnp.maximum(tile(b, i) * r - 1, 0), 0)),
                pl.BlockSpec((SUBLANES, w), lambda b, i: (jnp.minimum((tile(b, i) + 1) * r, n_blk - 1), 0))]

    const2 = lambda b, i: (0, 0)
    const3 = lambda b, i: (0, 0, 0)
    const4 = lambda b, i: (0, 0, 0, 0)
    return pl.pallas_call(
        _lru_kernel,
        grid=(n_batch, nt),
        in_specs=specs(fwd) + specs(rev) + [
            pl.BlockSpec((SEQ_TILE, w), lambda b, i: (n_lat // SEQ_TILE + b, 0)),
            pl.BlockSpec(cw.shape, const2), pl.BlockSpec((1, w), const2),
            pl.BlockSpec(wa_bd.shape, const4), pl.BlockSpec(wx_bd.shape, const4),
            pl.BlockSpec(ba.shape, const3), pl.BlockSpec(bx.shape, const3),
            pl.BlockSpec(lam.shape, const3)],
        out_specs=[pl.BlockSpec((SEQ_TILE, w), lambda b, i: (fwd(b, i), 0)),
                   pl.BlockSpec((SEQ_TILE, w), lambda b, i: (rev(b, i), 0))],
        out_shape=[jax.ShapeDtypeStruct((n_lat, w), F32)] * 2,
        scratch_shapes=[pltpu.VMEM((1, w), F32), pltpu.VMEM((1, w), F32),
                        pltpu.VMEM((SEQ_TILE, w), F32), pltpu.VMEM((SEQ_TILE, w), F32)],
        compiler_params=_cparams("arbitrary", "arbitrary"),
        name="rglru",
    )(ux, ux, ux, ux, ux, ux, ux, cw, cb, wa_bd, wx_bd, ba, bx, lam)


def _merge_odd_kernel(up_ref, upp_ref, upn_ref, hf_ref, hr_ref, gg_ref, x_ref, mod_ref,
                      pw_ref, ps_ref, on_ref, wo_ref, nf_ref, wr_ref, br_ref,
                      xo_ref, tok_ref, route_ref, *, tiles_per_seq, seq):
    j = pl.program_id(0)
    ts = j % tiles_per_seq
    e = _with_halo(upp_ref, up_ref[...], upn_ref, ts == 0, ts == tiles_per_seq - 1)
    gw = pw_ref.shape[-1]
    lo, hi = SUBLANES, SUBLANES + SEQ_TILE
    tpos = ts * SEQ_TILE + lax.broadcasted_iota(jnp.int32, (SEQ_TILE, 1), 0)
    parts = []
    for g, win in enumerate(POOL_WINDOWS):
        eg = e[:, g * gw:(g + 1) * gw]
        p = eg
        span = 1
        while span < win:
            p = p + _shift_rows(p, span)
            span *= 2
        half = win // 2
        s = _shift_rows(p, -half)[lo:hi]
        cnt = (jnp.minimum(tpos + half, seq) - jnp.maximum(tpos - half, 0)).astype(F32)
        dlt = s / cnt - eg[lo:hi]
        parts.append(_bdot(dlt, pw_ref[g]))
    pool = jnp.concatenate(parts, axis=-1) * ps_ref[...]
    lru = (hf_ref[...] + hr_ref[...]) * gg_ref[...]
    _tail(pool, lru, on_ref[...], wo_ref, x_ref[...], mod_ref[0], nf_ref, wr_ref, br_ref,
          xo_ref, tok_ref, route_ref)


def _merge_odd(up, hf, hr, gg, xf, modl, pw_bf, ps, on, wo_bf, nf, wr, br, dims):
    d = xf.shape[1]
    n_tok, w = up.shape
    n_batch, seq = dims
    tps = seq // SEQ_TILE
    n_lat = n_batch * seq
    mod_map, _ = _tile_maps(n_batch * tps, tps, n_batch)
    row = lambda j: (j, 0)
    const = lambda j: (0, 0)
    prev, nxt = _halo_specs(w, n_tok)
    out_specs, out_shape = _tail_specs(d, n_lat)
    return pl.pallas_call(
        functools.partial(_merge_odd_kernel, tiles_per_seq=tps, seq=seq),
        grid=(n_lat // SEQ_TILE,),
        in_specs=[pl.BlockSpec((SEQ_TILE, w), row), prev, nxt,
                  pl.BlockSpec((SEQ_TILE, w), row), pl.BlockSpec((SEQ_TILE, w), row),
                  pl.BlockSpec((SEQ_TILE, w), row),
                  pl.BlockSpec((SEQ_TILE, d), row),
                  pl.BlockSpec((1, SUBLANES, d), mod_map),
                  pl.BlockSpec(pw_bf.shape, lambda j: (0, 0, 0)), pl.BlockSpec((1, w), const),
                  pl.BlockSpec((1, d), const), pl.BlockSpec(wo_bf.shape, const),
                  pl.BlockSpec((1, d), const), pl.BlockSpec(wr.shape, const),
                  pl.BlockSpec((1, LANES), const)],
        out_specs=out_specs, out_shape=out_shape,
        compiler_params=_cparams("arbitrary"),
        name="merge_odd",
    )(up, up, up, hf, hr, gg, xf, modl, pw_bf, ps, on, wo_bf, nf, wr, br)


def _slots_kernel(route_ref, dest_ref, cnt_ref, tot_ref, run_ref, start_ref):
    phase = pl.program_id(0)
    i = pl.program_id(1)
    rt = route_ref[...]
    m = rt.shape[0]
    lane = lax.broadcasted_iota(jnp.int32, rt.shape, 1).astype(F32)
    oh1 = lane == rt[:, 0:1]
    oh2 = lane == rt[:, 1:2]
    s = jnp.where(oh1 | oh2, 1.0, 0.0)
    colsum = jnp.sum(s, axis=0, keepdims=True)

    @pl.when((phase == 0) & (i == 0))
    def _():
        tot_ref[...] = jnp.zeros_like(tot_ref)

    @pl.when(phase == 0)
    def _():
        tot_ref[...] += colsum

    @pl.when((phase == 1) & (i == 0))
    def _():
        tot = tot_ref[...]
        padded = jnp.floor((tot + (MOE_BLOCK - 1)) / MOE_BLOCK) * MOE_BLOCK
        r = lax.broadcasted_iota(jnp.int32, (LANES, LANES), 0)
        c = lax.broadcasted_iota(jnp.int32, (LANES, LANES), 1)
        upper = jnp.where(r < c, 1.0, 0.0)
        p8 = jnp.broadcast_to(padded, (SUBLANES, LANES))
        start = jnp.dot(p8, upper, precision=HIGHEST, preferred_element_type=F32)[0:1]
        start_ref[...] = start
        run_ref[...] = jnp.zeros_like(run_ref)
        sub = lax.broadcasted_iota(jnp.int32, cnt_ref.shape, 0)
        cnt_ref[...] = jnp.where(sub == 0, tot, jnp.where(sub == 1, start, 0.0))

    @pl.when(phase == 1)
    def _():
        r = lax.broadcasted_iota(jnp.int32, (m, m), 0)
        c = lax.broadcasted_iota(jnp.int32, (m, m), 1)
        lower = jnp.where(c < r, 1.0, 0.0).astype(BF16)
        prefix = jnp.dot(lower, s.astype(BF16), preferred_element_type=F32)
        base = prefix + run_ref[...] + start_ref[...]
        d1 = jnp.sum(jnp.where(oh1, base, 0.0), axis=-1, keepdims=True)
        d2 = jnp.sum(jnp.where(oh2, base, 0.0), axis=-1, keepdims=True)
        dest_ref[...] = jnp.where(lane == 0, d1, jnp.where(lane == 1, d2, 0.0)).astype(jnp.int32)
        run_ref[...] += colsum


def _slots(route):
    n_tok = route.shape[0]
    nt = n_tok // TOK_TILE
    return pl.pallas_call(
        _slots_kernel,
        grid=(2, nt),
        in_specs=[pl.BlockSpec((TOK_TILE, LANES), lambda p, i: (i, 0))],
        out_specs=[pl.BlockSpec((TOK_TILE, LANES), lambda p, i: (i * p, 0)),
                   pl.BlockSpec((SUBLANES, LANES), lambda p, i: (0, 0))],
        out_shape=[jax.ShapeDtypeStruct((n_tok, LANES), jnp.int32),
                   jax.ShapeDtypeStruct((SUBLANES, LANES), F32)],
        scratch_shapes=[pltpu.VMEM((1, LANES), F32)] * 3,
        compiler_params=_cparams("arbitrary", "arbitrary"),
        name="moe_slots",
    )(route)


def _dispatch_kernel(dest_ref, pad_lo_ref, pad_hi_ref, tok_ref, xs_ref, zrow, sem, zsem):
    i = pl.program_id(0)
    base = i * TOK_TILE

    def zero_copy(r):
        return pltpu.make_async_copy(zrow.at[pl.ds(0, 1)], xs_ref.at[pl.ds(r, 1)], zsem)

    def for_each_pad_row(fn):
        def per_expert(e, c):
            return lax.fori_loop(pad_lo_ref[e], pad_hi_ref[e], fn, c)
        lax.fori_loop(0, N_EXPERTS, per_expert, 0)

    @pl.when(i == 0)
    def _():
        zrow[...] = jnp.zeros_like(zrow)

        def fill(r, c):
            zero_copy(r).start()
            return c
        for_each_pad_row(fill)

    def body(r, c):
        for k in range(TOP_K):
            d = dest_ref[(base + r) * TOP_K + k]
            pltpu.make_async_copy(tok_ref.at[pl.ds(r, 1)], xs_ref.at[pl.ds(d, 1)], sem).start()
        return c

    lax.fori_loop(0, TOK_TILE, body, 0)
    for k in range(TOP_K):
        pltpu.make_async_copy(tok_ref, xs_ref.at[pl.ds(0, TOK_TILE)], sem).wait()

    @pl.when(i == pl.num_programs(0) - 1)
    def _():
        def drain(r, c):
            zero_copy(r).wait()
            return c
        for_each_pad_row(drain)


def _dispatch(dest_flat, pad_lo, pad_hi, tok, n_slots):
    n_tok, d = tok.shape
    return pl.pallas_call(
        _dispatch_kernel,
        grid_spec=pltpu.PrefetchScalarGridSpec(
            num_scalar_prefetch=3,
            grid=(n_tok // TOK_TILE,),
            in_specs=[pl.BlockSpec((TOK_TILE, d), lambda i, *_: (i, 0))],
            out_specs=pl.BlockSpec(memory_space=pl.ANY),
            scratch_shapes=[pltpu.VMEM((SUBLANES, d), F32),
                            pltpu.SemaphoreType.DMA(()), pltpu.SemaphoreType.DMA(())]),
        out_shape=jax.ShapeDtypeStruct((n_slots, d), F32),
        compiler_params=_cparams("arbitrary"),
        name="moe_dispatch",
    )(dest_flat, pad_lo, pad_hi, tok)


def _expert_kernel(be_ref, nu_ref, x_ref, w1_ref, w3_ref, w2_ref, y_ref, w1b, w3b, w2b):
    i = pl.program_id(0)

    @pl.when(i < nu_ref[0])
    def _():
        changed = (i == 0) | (be_ref[i] != be_ref[jnp.maximum(i - 1, 0)])

        @pl.when(changed)
        def _():
            w1b[...] = w1_ref[0, 0].astype(BF16)
            w3b[...] = w3_ref[0, 0].astype(BF16)
            w2b[...] = w2_ref[0, 0].astype(BF16)

        x = x_ref[...].astype(BF16)
        a = jnp.dot(x, w1b[...], preferred_element_type=F32)
        b = jnp.dot(x, w3b[...], preferred_element_type=F32)
        y_ref[...] = _bdot(_silu(a) * b, w2b[...])


def _experts(block_e, n_used, xs, w1, w3, w2, layer):
    n_slots, d = xs.shape
    de = w1.shape[-1]
    n_blocks = n_slots // MOE_BLOCK
    blk = lambda i, be, nu: (jnp.minimum(i, nu[0] - 1), 0)
    wmap = lambda i, be, nu: (layer, be[i], 0, 0)
    return pl.pallas_call(
        _expert_kernel,
        grid_spec=pltpu.PrefetchScalarGridSpec(
            num_scalar_prefetch=2,
            grid=(n_blocks,),
            in_specs=[pl.BlockSpec((MOE_BLOCK, d), blk),
                      pl.BlockSpec((1, 1, d, de), wmap), pl.BlockSpec((1, 1, d, de), wmap),
                      pl.BlockSpec((1, 1, de, d), wmap)],
            out_specs=pl.BlockSpec((MOE_BLOCK, d), blk),
            scratch_shapes=[pltpu.VMEM((d, de), BF16), pltpu.VMEM((d, de), BF16),
                            pltpu.VMEM((de, d), BF16)]),
        out_shape=jax.ShapeDtypeStruct((n_slots, d), F32),
        compiler_params=_cparams("arbitrary"),
        name="moe_experts",
    )(block_e, n_used, xs, w1, w3, w2)


def _combine_kernel(dest_ref, x_ref, route_ref, mod_ref, ys_ref, o_ref, ybuf, sem):
    base = pl.program_id(0) * TOK_TILE

    def body(r, c):
        for k in range(TOP_K):
            d = dest_ref[(base + r) * TOP_K + k]
            pltpu.make_async_copy(ys_ref.at[pl.ds(d, 1)], ybuf.at[k, pl.ds(r, 1)], sem).start()
        return c

    lax.fori_loop(0, TOK_TILE, body, 0)
    for k in range(TOP_K):
        pltpu.make_async_copy(ys_ref.at[pl.ds(0, TOK_TILE)], ybuf.at[k], sem).wait()
    rt = route_ref[...]
    y = rt[:, 2:3] * ybuf[0] + rt[:, 3:4] * ybuf[1]
    o_ref[...] = x_ref[...] + mod_ref[0][5:6] * y


def _combine(dest_flat, xn, route, modl, ys, dims):
    n_tok, d = xn.shape
    n_batch, seq = dims
    tpb = seq // TOK_TILE
    return pl.pallas_call(
        _combine_kernel,
        grid_spec=pltpu.PrefetchScalarGridSpec(
            num_scalar_prefetch=1,
            grid=(n_tok // TOK_TILE,),
            in_specs=[pl.BlockSpec((TOK_TILE, d), lambda i, dest: (i, 0)),
                      pl.BlockSpec((TOK_TILE, LANES), lambda i, dest: (i, 0)),
                      pl.BlockSpec((1, SUBLANES, d), lambda i, dest: (jnp.minimum(i // tpb, n_batch), 0, 0)),
                      pl.BlockSpec(memory_space=pl.ANY)],
            out_specs=pl.BlockSpec((TOK_TILE, d), lambda i, dest: (i, 0)),
            scratch_shapes=[pltpu.VMEM((TOP_K, TOK_TILE, d), F32), pltpu.SemaphoreType.DMA(())]),
        out_shape=jax.ShapeDtypeStruct((n_tok, d), F32),
        compiler_params=_cparams("arbitrary"),
        name="moe_combine",
    )(dest_flat, xn, route, modl, ys)


def _moe(xn, tok, route, modl, w1, w3, w2, layer, dims):
    n_tok = tok.shape[0]
    n = n_tok * TOP_K
    n_blocks = (n + N_EXPERTS * (MOE_BLOCK - 1) + MOE_BLOCK - 1) // MOE_BLOCK
    dest, counts = _slots(route)
    dest_flat = dest[:, :TOP_K].reshape(-1)
    cnt = counts[0, :N_EXPERTS].astype(jnp.int32)
    start = counts[1, :N_EXPERTS].astype(jnp.int32)
    blocks_per_e = (cnt + MOE_BLOCK - 1) // MOE_BLOCK
    blk_end = (start + blocks_per_e * MOE_BLOCK) // MOE_BLOCK
    n_used = blk_end[-1:]
    blk = jnp.minimum(jnp.arange(n_blocks, dtype=jnp.int32), n_used[0] - 1)
    block_e = jnp.sum((blk[:, None] >= blk_end[None, :]).astype(jnp.int32), axis=1)
    block_e = jnp.minimum(block_e, N_EXPERTS - 1)
    xs = _dispatch(dest_flat, start + cnt, blk_end * MOE_BLOCK, tok, n_blocks * MOE_BLOCK)
    ys = _experts(block_e, n_used, xs, w1, w3, w2, layer)
    return _combine(dest_flat, xn, route, modl, ys, dims)


def _rope_tables(seq):
    rows = seq // GRID_W
    row = jnp.repeat(jnp.arange(rows), GRID_W).astype(F32)
    col = jnp.tile(jnp.arange(GRID_W), rows).astype(F32)
    inv_freq = ROPE_BASE ** (-jnp.arange(ROPE_PAIRS, dtype=F32) / ROPE_PAIRS)
    ar, ac = row[:, None] * inv_freq, col[:, None] * inv_freq
    cos = jnp.concatenate([jnp.cos(ar), jnp.cos(ar), jnp.cos(ac), jnp.cos(ac)], axis=-1)
    sin = jnp.concatenate([-jnp.sin(ar), jnp.sin(ar), -jnp.sin(ac), jnp.sin(ac)], axis=-1)
    reps = LANES // HEAD_DIM
    cos, sin = jnp.tile(cos, (1, reps)), jnp.tile(sin, (1, reps))
    cos = jnp.concatenate([cos, jnp.ones((SEQ_TILE, LANES), F32)], axis=0)
    sin = jnp.concatenate([sin, jnp.zeros((SEQ_TILE, LANES), F32)], axis=0)
    return cos, sin


def _head_mean_matrix():
    r = jnp.arange(LANES)
    return jnp.where((r[:, None] // HEAD_DIM) == (r[None, :] // HEAD_DIM), 1.0 / HEAD_DIM, 0.0).astype(F32)


def _block_diag_chunks(w, chunk):
    dirs, nblk, bw, _ = w.shape
    per = chunk // bw
    w = w.reshape(dirs, nblk // per, per, bw, bw)
    eye = jnp.eye(per, dtype=w.dtype)
    out = jnp.einsum("dcpij,pq->dcpiqj", w, eye)
    return out.reshape(dirs, nblk // per, chunk, chunk).astype(BF16)


def _router_matrix(gw, gb, ew, eb):
    d = gw.shape[0]
    pad = LANES - N_EXPERTS - N_GROUPS
    wr = jnp.concatenate([ew, gw, jnp.zeros((d, pad), F32)], axis=1)
    br = jnp.concatenate([eb, gb, jnp.zeros((pad,), F32)])[None, :]
    return wr, br


def kernel(x, c, ctx, c_ctx, ada_w, ada_b, norm_mix, norm_ffn, out_norm, w_out, w_in_ab, q_norm, k_norm, attn_sink, sconv_w, sconv_b, w_in_cd, pool_w, pool_scale, lru_conv_w, lru_conv_b, lru_wa, lru_ba, lru_wx, lru_bx, lru_lambda, router_gw, router_gb, router_ew, router_eb, exp_w1, exp_w3, exp_w2):
    n_batch, seq, d = x.shape
    ctx_len = ctx.shape[1]
    depth = ada_w.shape[0]
    assert ctx_len == SEQ_TILE and seq % TOK_TILE == 0 and seq % GRID_W == 0
    assert depth == 2, "context-side odd-layer outputs are not implemented"
    dims = (n_batch, seq)
    n_lat = n_batch * seq

    cc = jnp.concatenate([c, c_ctx[None, :], jnp.zeros((SUBLANES - n_batch - 1, d), F32)], axis=0)
    mods = _ada(cc, ada_w, ada_b)
    mods = mods[:, :n_batch + 1].reshape(depth, n_batch + 1, N_MOD, d)
    mods = jnp.pad(mods, ((0, 0), (0, 0), (0, SUBLANES - N_MOD), (0, 0)))

    xparts = (x.reshape(n_lat, d), ctx.reshape(n_batch * ctx_len, d))
    cos_t, sin_t = _rope_tables(seq)
    pm = _head_mean_matrix()

    for i in range(depth):
        need_ctx = i < depth - 1
        j = i // 2
        modl = mods[i]
        n_out = n_lat + n_batch * ctx_len if need_ctx else n_lat
        wr, br = _router_matrix(router_gw[i], router_gb[i], router_ew[i], router_eb[i])
        wo_bf = w_out[i].astype(BF16)
        nm, nf, on = norm_mix[i][None, :], norm_ffn[i][None, :], out_norm[i][None, :]
        if i % 2 == 0:
            reps = LANES // HEAD_DIM
            q, k, v, gb, u = _inproj_even(xparts, modl, nm, w_in_ab[j].astype(BF16),
                                          jnp.tile(q_norm[j], reps)[None, :], jnp.tile(k_norm[j], reps)[None, :],
                                          cos_t, sin_t, pm, dims)
            att = _attention(q, k, v, attn_sink[j], dims, need_ctx)
            xn, tok, route = _merge_even(att, gb, u, xparts, modl, sconv_w[j], sconv_b[j][None, :], on, wo_bf,
                                         nf, wr, br, dims, n_out)
        else:
            ux, gg, up = _inproj_odd(xparts, modl, nm, w_in_cd[j].astype(BF16), dims)
            chunk = 2 * LANES
            hf, hr = _lru(ux, lru_conv_w[j], lru_conv_b[j][None, :],
                          _block_diag_chunks(lru_wa[j], chunk), _block_diag_chunks(lru_wx[j], chunk),
                          lru_ba[j][:, None, :], lru_bx[j][:, None, :], lru_lambda[j][:, None, :], dims)
            xn, tok, route = _merge_odd(up, hf, hr, gg, xparts[0], modl, pool_w[j].astype(BF16),
                                        pool_scale[j][None, :], on, wo_bf, nf, wr, br, dims)
        xparts = (_moe(xn, tok, route, modl, exp_w1, exp_w3, exp_w2, i, dims),)
    return xparts[0][:n_lat].reshape(n_batch, seq, d)
```

```python
import functools

import jax
import jax.numpy as jnp
from jax import lax
from jax.experimental import pallas as pl
from jax.experimental.pallas import tpu as pltpu

F32 = jnp.float32
BF16 = jnp.bfloat16
HIGHEST = lax.Precision.HIGHEST

GRID_W = 64
EPS = 1e-6
HEAD_DIM = 64
N_Q_HEADS = 8
N_KV_HEADS = 2
Q_PER_KV = N_Q_HEADS // N_KV_HEADS
WINDOW = 128
ATT_BLOCK = 128
ROPE_PAIRS = HEAD_DIM // 4
ROPE_BASE = 10000.0
POOL_WINDOWS = (2, 4, 8, 16)
LRU_C = 8.0
N_GROUPS = 4
EXPERTS_PER_GROUP = 8
N_EXPERTS = N_GROUPS * EXPERTS_PER_GROUP
TOP_K = 2
MOE_BLOCK = 256
N_MOD = 6

SUBLANES = 8
LANES = 128
SEQ_TILE = 256
TOK_TILE = 512
NEG = -1e30
VMEM_LIMIT = 48 * 1024 * 1024


def _cparams(*sem):
    return pltpu.CompilerParams(dimension_semantics=sem, vmem_limit_bytes=VMEM_LIMIT)


def _rms(x, g):
    ms = jnp.mean(x * x, axis=-1, keepdims=True)
    return x * lax.rsqrt(ms + EPS) * g


def _modulate(x, g, shift, scale):
    return _rms(x, g) * (1.0 + scale) + shift


def _bdot(a, b):
    return jnp.dot(a.astype(BF16), b, preferred_element_type=F32)


def _silu(x):
    return x * jax.nn.sigmoid(x)


def _ada_kernel(c_ref, w_ref, b_ref, o_ref):
    s = _silu(c_ref[...])
    o_ref[0] = jnp.dot(s, w_ref[0], precision=HIGHEST, preferred_element_type=F32) + b_ref[0]


def _ada(cc, ada_w, ada_b):
    depth, d, n = ada_w.shape
    tn = 1536
    return pl.pallas_call(
        _ada_kernel,
        grid=(depth, n // tn),
        in_specs=[pl.BlockSpec((SUBLANES, d), lambda l, j: (0, 0)),
                  pl.BlockSpec((1, d, tn), lambda l, j: (l, 0, j)),
                  pl.BlockSpec((1, 1, tn), lambda l, j: (l, 0, j))],
        out_specs=pl.BlockSpec((1, SUBLANES, tn), lambda l, j: (l, 0, j)),
        out_shape=jax.ShapeDtypeStruct((depth, SUBLANES, n), F32),
        compiler_params=_cparams("arbitrary", "arbitrary"),
        name="ada",
    )(cc, ada_w, ada_b.reshape(depth, 1, n))


def _head_norm_rope(t, pm, g, cos, sin):
    ms = jnp.dot(t * t, pm, precision=HIGHEST, preferred_element_type=F32)
    t = t * lax.rsqrt(ms + EPS) * g
    lane = lax.broadcasted_iota(jnp.int32, t.shape, 1)
    first_half = (lane % (2 * ROPE_PAIRS)) < ROPE_PAIRS
    partner = jnp.where(first_half, pltpu.roll(t, LANES - ROPE_PAIRS, 1), pltpu.roll(t, ROPE_PAIRS, 1))
    return t * cos + partner * sin


def _token_operands(xparts, n_lat_tiles):
    d = xparts[0].shape[1]
    lat = pl.BlockSpec((SEQ_TILE, d), lambda j: (jnp.minimum(j, n_lat_tiles - 1), 0))
    if len(xparts) == 2:
        ctx = pl.BlockSpec((SEQ_TILE, d), lambda j: (jnp.maximum(j - n_lat_tiles, 0), 0))
        return list(xparts), [lat, ctx]
    ctx = pl.BlockSpec((SEQ_TILE, d), lambda j: (jnp.maximum(j, n_lat_tiles), 0))
    return [xparts[0], xparts[0]], [lat, ctx]


def _pick_tokens(xl_ref, xc_ref, n_lat_tiles):
    return jnp.where(pl.program_id(0) >= n_lat_tiles, xc_ref[...], xl_ref[...])


def _inproj_even_kernel(xl_ref, xc_ref, mod_ref, g_ref, w_ref, qg_ref, kg_ref, cos_ref, sin_ref, pm_ref,
                        q_ref, k_ref, v_ref, gb_ref, u_ref, *, n_lat_tiles):
    mod = mod_ref[0]
    x = _pick_tokens(xl_ref, xc_ref, n_lat_tiles)
    h = _modulate(x, g_ref[...], mod[0:1], mod[1:2]).astype(BF16)
    cos, sin, pm = cos_ref[...], sin_ref[...], pm_ref[...]
    kv_w = N_KV_HEADS * HEAD_DIM
    att_w = N_Q_HEADS * HEAD_DIM
    sc_w = gb_ref.shape[1]
    kv = jnp.dot(h, w_ref[:, 0:2 * kv_w], preferred_element_type=F32)
    k_ref[...] = _head_norm_rope(kv[:, :kv_w], pm, kg_ref[...], cos, sin).astype(BF16)
    v_ref[...] = kv[:, kv_w:].astype(BF16)
    q = jnp.dot(h, w_ref[:, 2 * kv_w:2 * kv_w + att_w], preferred_element_type=F32)
    qscale = HEAD_DIM ** -0.5
    for c in range(att_w // LANES):
        qc = _head_norm_rope(q[:, c * LANES:(c + 1) * LANES], pm, qg_ref[...], cos, sin)
        q_ref[:, c * LANES:(c + 1) * LANES] = (qc * qscale).astype(BF16)
    o = 2 * kv_w + att_w
    gb_ref[...] = jnp.dot(h, w_ref[:, o:o + sc_w], preferred_element_type=F32)
    gc = jnp.dot(h, w_ref[:, o + sc_w:o + 2 * sc_w], preferred_element_type=F32)
    gx = jnp.dot(h, w_ref[:, o + 2 * sc_w:o + 3 * sc_w], preferred_element_type=F32)
    u_ref[...] = gc * gx


def _tile_maps(n_lat_tiles, tiles_per_seq, n_batch):
    def mod_map(j):
        return (jnp.minimum(j // tiles_per_seq, n_batch), 0, 0)

    def pos_map(j):
        return (jnp.where(j < n_lat_tiles, j % tiles_per_seq, tiles_per_seq), 0)

    return mod_map, pos_map


def _inproj_even(xparts, modl, g, w_bf, q_g, k_g, cos_t, sin_t, pm, dims):
    n_tok = sum(p.shape[0] for p in xparts)
    d = xparts[0].shape[1]
    n_batch, seq = dims
    tps = seq // SEQ_TILE
    mod_map, pos_map = _tile_maps(n_batch * tps, tps, n_batch)
    xs, xspecs = _token_operands(xparts, n_batch * tps)
    kv_w, att_w = N_KV_HEADS * HEAD_DIM, N_Q_HEADS * HEAD_DIM
    sc_w = (w_bf.shape[1] - 2 * kv_w - att_w) // 3
    row = lambda j: (j, 0)
    const = lambda j: (0, 0)
    return pl.pallas_call(
        functools.partial(_inproj_even_kernel, n_lat_tiles=n_batch * tps),
        grid=(n_tok // SEQ_TILE,),
        in_specs=xspecs + [
                  pl.BlockSpec((1, SUBLANES, d), mod_map),
                  pl.BlockSpec((1, d), const),
                  pl.BlockSpec(w_bf.shape, const),
                  pl.BlockSpec((1, LANES), const),
                  pl.BlockSpec((1, LANES), const),
                  pl.BlockSpec((SEQ_TILE, LANES), pos_map),
                  pl.BlockSpec((SEQ_TILE, LANES), pos_map),
                  pl.BlockSpec((LANES, LANES), const)],
        out_specs=[pl.BlockSpec((SEQ_TILE, att_w), row),
                   pl.BlockSpec((SEQ_TILE, kv_w), row),
                   pl.BlockSpec((SEQ_TILE, kv_w), row),
                   pl.BlockSpec((SEQ_TILE, sc_w), row),
                   pl.BlockSpec((SEQ_TILE, sc_w), row)],
        out_shape=[jax.ShapeDtypeStruct((n_tok, att_w), BF16),
                   jax.ShapeDtypeStruct((n_tok, kv_w), BF16),
                   jax.ShapeDtypeStruct((n_tok, kv_w), BF16),
                   jax.ShapeDtypeStruct((n_tok, sc_w), F32),
                   jax.ShapeDtypeStruct((n_tok, sc_w), F32)],
        compiler_params=_cparams("arbitrary"),
        name="inproj_even",
    )(*xs, modl, g, w_bf, q_g, k_g, cos_t, sin_t, pm)


def _attn_heads(q, kb, vb, bias, sink_ref, o_ref):
    outs = []
    for hq in range(N_Q_HEADS):
        hk = hq // Q_PER_KV
        qh = q[:, hq * HEAD_DIM:(hq + 1) * HEAD_DIM]
        kh = kb[:, hk * HEAD_DIM:(hk + 1) * HEAD_DIM]
        vh = vb[:, hk * HEAD_DIM:(hk + 1) * HEAD_DIM]
        s = lax.dot_general(qh, kh, (((1,), (1,)), ((), ())), preferred_element_type=F32)
        if bias is not None:
            s = s + bias
        snk = sink_ref[hq]
        m = jnp.maximum(jnp.max(s, axis=-1, keepdims=True), snk)
        p = jnp.exp(s - m)
        den = jnp.sum(p, axis=-1, keepdims=True) + jnp.exp(snk - m)
        o = jnp.dot(p.astype(BF16), vh, preferred_element_type=F32)
        outs.append(o / den)
    o_ref[...] = jnp.concatenate(outs, axis=-1)


def _attn_lat_kernel(sink_ref, q_ref, kp_ref, kc_ref, kn_ref, kx_ref, vp_ref, vc_ref, vn_ref, vx_ref,
                     o_ref):
    i = pl.program_id(1)
    nb = pl.num_programs(1)
    kb = jnp.concatenate([kp_ref[...], kc_ref[...], kn_ref[...], kx_ref[...]], axis=0)
    vb = jnp.concatenate([vp_ref[...], vc_ref[...], vn_ref[...], vx_ref[...]], axis=0)
    m, n = ATT_BLOCK, kb.shape[0]
    row = lax.broadcasted_iota(jnp.int32, (m, n), 0)
    col = lax.broadcasted_iota(jnp.int32, (m, n), 1)
    rel = col - row
    lo = jnp.where(i == 0, ATT_BLOCK, 0)
    hi = jnp.where(i == nb - 1, 2 * ATT_BLOCK, 3 * ATT_BLOCK)
    local_ok = (rel >= 0) & (rel <= 2 * WINDOW) & (col >= lo) & (col < hi)
    valid = (col >= 3 * ATT_BLOCK) | local_ok
    bias = jnp.where(valid, 0.0, NEG)
    _attn_heads(q_ref[...], kb, vb, bias, sink_ref, o_ref)


def _attn_ctx_kernel(sink_ref, att_in_ref, q_ref, kx_ref, vx_ref, o_ref):
    del att_in_ref
    _attn_heads(q_ref[...], kx_ref[...], vx_ref[...], None, sink_ref, o_ref)


def _attention(q, k, v, sink, dims, need_ctx):
    n_tok = q.shape[0]
    n_batch, seq = dims
    nb = seq // ATT_BLOCK
    n_lat = n_batch * seq
    ctx_len = (n_tok - n_lat) // n_batch
    att_w, kv_w = q.shape[1], k.shape[1]
    ctx_blk0 = n_lat // ctx_len
    cur = lambda b, i: (b * nb + i, 0)
    prev = lambda b, i: (b * nb + jnp.maximum(i - 1, 0), 0)
    nxt = lambda b, i: (b * nb + jnp.minimum(i + 1, nb - 1), 0)
    cx = lambda b, i: (ctx_blk0 + b, 0)
    smem = pl.BlockSpec(memory_space=pltpu.SMEM)
    kvb = lambda f: pl.BlockSpec((ATT_BLOCK, kv_w), f)
    att = pl.pallas_call(
        _attn_lat_kernel,
        grid=(n_batch, nb),
        in_specs=[smem, pl.BlockSpec((ATT_BLOCK, att_w), cur),
                  kvb(prev), kvb(cur), kvb(nxt), pl.BlockSpec((ctx_len, kv_w), cx),
                  kvb(prev), kvb(cur), kvb(nxt), pl.BlockSpec((ctx_len, kv_w), cx)],
        out_specs=pl.BlockSpec((ATT_BLOCK, att_w), cur),
        out_shape=jax.ShapeDtypeStruct((n_tok, att_w), F32),
        compiler_params=_cparams("arbitrary", "arbitrary"),
        name="attn_latent",
    )(sink, q, k, k, k, k, v, v, v, v)
    if not need_ctx:
        return att
    cxb = lambda b: (ctx_blk0 + b, 0)
    return pl.pallas_call(
        _attn_ctx_kernel,
        grid=(n_batch,),
        in_specs=[smem, pl.BlockSpec(memory_space=pl.ANY),
                  pl.BlockSpec((ctx_len, att_w), cxb),
                  pl.BlockSpec((ctx_len, kv_w), cxb), pl.BlockSpec((ctx_len, kv_w), cxb)],
        out_specs=pl.BlockSpec((ctx_len, att_w), cxb),
        out_shape=jax.ShapeDtypeStruct((n_tok, att_w), F32),
        input_output_aliases={1: 0},
        compiler_params=_cparams("arbitrary"),
        name="attn_context",
    )(sink, att, q, k, v)


def _route(logits):
    lane = lax.broadcasted_iota(jnp.int32, logits.shape, 1).astype(F32)
    big = float(LANES)
    gl = jnp.where((lane >= N_EXPERTS) & (lane < N_EXPERTS + N_GROUPS), logits, NEG)
    gmax = jnp.max(gl, axis=-1, keepdims=True)
    gidx = jnp.min(jnp.where(gl == gmax, lane, big), axis=-1, keepdims=True) - N_EXPERTS
    p_sel = 1.0 / jnp.sum(jnp.exp(gl - gmax), axis=-1, keepdims=True)
    lo = gidx * EXPERTS_PER_GROUP
    el = jnp.where((lane >= lo) & (lane < lo + EXPERTS_PER_GROUP), logits, NEG)
    m1 = jnp.max(el, axis=-1, keepdims=True)
    i1 = jnp.min(jnp.where(el == m1, lane, big), axis=-1, keepdims=True)
    el2 = jnp.where(lane == i1, NEG, el)
    m2 = jnp.max(el2, axis=-1, keepdims=True)
    i2 = jnp.min(jnp.where(el2 == m2, lane, big), axis=-1, keepdims=True)
    t = jnp.exp(m2 - m1)
    w1 = p_sel / (1.0 + t)
    w2 = p_sel * t / (1.0 + t)
    return jnp.where(lane == 0, i1, jnp.where(lane == 1, i2, jnp.where(lane == 2, w1,
                     jnp.where(lane == 3, w2, 0.0))))


def _tail(n1, n2, gain, wo_ref, x, mod, nf_ref, wr_ref, br_ref, xo_ref, tok_ref, route_ref):
    half = n1.shape[1]
    y = (_bdot(_rms(n1, gain[:, :half]), wo_ref[0:half, :])
         + _bdot(_rms(n2, gain[:, half:]), wo_ref[half:, :]))
    xn = x + mod[2:3] * y
    xo_ref[...] = xn
    tok = _modulate(xn, nf_ref[...], mod[3:4], mod[4:5])
    tok_ref[...] = tok
    logits = jnp.dot(tok, wr_ref[...], precision=HIGHEST, preferred_element_type=F32) + br_ref[...]
    route_ref[...] = _route(logits)


def _seq_flags(j, n_lat_tiles, tiles_per_seq):
    is_ctx = j >= n_lat_tiles
    first = is_ctx | (j % tiles_per_seq == 0)
    last = is_ctx | (j % tiles_per_seq == tiles_per_seq - 1)
    return first, last


def _with_halo(prev_ref, tile, next_ref, first, last):
    p = jnp.where(first, 0.0, prev_ref[...])
    n = jnp.where(last, 0.0, next_ref[...])
    return jnp.concatenate([p, tile, n], axis=0)


def _shift_rows(e, k):
    return pltpu.roll(e, (-k) % e.shape[0], 0)


def _merge_even_kernel(att_ref, gb_ref, u_ref, up_ref, un_ref, xl_ref, xc_ref, mod_ref, cw_ref, cb_ref,
                       on_ref, wo_ref, nf_ref, wr_ref, br_ref,
                       xo_ref, tok_ref, route_ref, *, n_lat_tiles, tiles_per_seq):
    j = pl.program_id(0)
    first, last = _seq_flags(j, n_lat_tiles, tiles_per_seq)
    e = _with_halo(up_ref, u_ref[...], un_ref, first, last)
    cw = cw_ref[...]
    lo, hi = SUBLANES, SUBLANES + SEQ_TILE
    conv = (cw[0:1] * _shift_rows(e, -1)[lo:hi] + cw[1:2] * e[lo:hi]
            + cw[2:3] * _shift_rows(e, 1)[lo:hi] + cb_ref[...])
    conv = gb_ref[...] * conv
    x = _pick_tokens(xl_ref, xc_ref, n_lat_tiles)
    _tail(att_ref[...], conv, on_ref[...], wo_ref, x, mod_ref[0], nf_ref, wr_ref, br_ref,
          xo_ref, tok_ref, route_ref)


def _halo_specs(width, n_rows):
    r = SEQ_TILE // SUBLANES
    n_blk = n_rows // SUBLANES
    prev = pl.BlockSpec((SUBLANES, width), lambda j: (jnp.maximum(j * r - 1, 0), 0))
    nxt = pl.BlockSpec((SUBLANES, width), lambda j: (jnp.minimum((j + 1) * r, n_blk - 1), 0))
    return prev, nxt


def _tail_specs(d, n_out_tok):
    row = lambda j: (j, 0)
    out_specs = [pl.BlockSpec((SEQ_TILE, d), row), pl.BlockSpec((SEQ_TILE, d), row),
                 pl.BlockSpec((SEQ_TILE, LANES), row)]
    out_shape = [jax.ShapeDtypeStruct((n_out_tok, d), F32), jax.ShapeDtypeStruct((n_out_tok, d), F32),
                 jax.ShapeDtypeStruct((n_out_tok, LANES), F32)]
    return out_specs, out_shape


def _merge_even(att, gb, u, xparts, modl, cw, cb, on, wo_bf, nf, wr, br, dims, n_out_tok):
    n_tok = u.shape[0]
    d = xparts[0].shape[1]
    n_batch, seq = dims
    tps = seq // SEQ_TILE
    mod_map, _ = _tile_maps(n_batch * tps, tps, n_batch)
    xs, xspecs = _token_operands(xparts, n_batch * tps)
    sc_w = u.shape[1]
    row = lambda j: (j, 0)
    const = lambda j: (0, 0)
    prev, nxt = _halo_specs(sc_w, n_tok)
    out_specs, out_shape = _tail_specs(d, n_out_tok)
    return pl.pallas_call(
        functools.partial(_merge_even_kernel, n_lat_tiles=n_batch * tps, tiles_per_seq=tps),
        grid=(n_out_tok // SEQ_TILE,),
        in_specs=[pl.BlockSpec((SEQ_TILE, att.shape[1]), row),
                  pl.BlockSpec((SEQ_TILE, sc_w), row),
                  pl.BlockSpec((SEQ_TILE, sc_w), row), prev, nxt] + xspecs + [
                  pl.BlockSpec((1, SUBLANES, d), mod_map),
                  pl.BlockSpec(cw.shape, const), pl.BlockSpec((1, sc_w), const),
                  pl.BlockSpec((1, d), const), pl.BlockSpec(wo_bf.shape, const),
                  pl.BlockSpec((1, d), const), pl.BlockSpec(wr.shape, const),
                  pl.BlockSpec((1, LANES), const)],
        out_specs=out_specs, out_shape=out_shape,
        compiler_params=_cparams("arbitrary"),
        name="merge_even",
    )(att, gb, u, u, u, *xs, modl, cw, cb, on, wo_bf, nf, wr, br)


def _inproj_odd_kernel(xl_ref, xc_ref, mod_ref, g_ref, w_ref, ux_ref, gg_ref, up_ref, *, n_lat_tiles):
    mod = mod_ref[0]
    x = _pick_tokens(xl_ref, xc_ref, n_lat_tiles)
    h = _modulate(x, g_ref[...], mod[0:1], mod[1:2]).astype(BF16)
    w = ux_ref.shape[1]
    ux_ref[...] = jnp.dot(h, w_ref[:, 0:w], preferred_element_type=F32)
    ug = jnp.dot(h, w_ref[:, w:2 * w], preferred_element_type=F32)
    gg_ref[...] = jax.nn.gelu(ug, approximate=True)
    up_ref[...] = jnp.dot(h, w_ref[:, 2 * w:3 * w], preferred_element_type=F32)


def _inproj_odd(xparts, modl, g, w_bf, dims):
    n_tok = sum(p.shape[0] for p in xparts)
    d = xparts[0].shape[1]
    n_batch, seq = dims
    tps = seq // SEQ_TILE
    mod_map, _ = _tile_maps(n_batch * tps, tps, n_batch)
    xs, xspecs = _token_operands(xparts, n_batch * tps)
    w = w_bf.shape[1] // 3
    row = lambda j: (j, 0)
    const = lambda j: (0, 0)
    return pl.pallas_call(
        functools.partial(_inproj_odd_kernel, n_lat_tiles=n_batch * tps),
        grid=(n_tok // SEQ_TILE,),
        in_specs=xspecs + [
                  pl.BlockSpec((1, SUBLANES, d), mod_map),
                  pl.BlockSpec((1, d), const),
                  pl.BlockSpec(w_bf.shape, const)],
        out_specs=[pl.BlockSpec((SEQ_TILE, w), row)] * 3,
        out_shape=[jax.ShapeDtypeStruct((n_tok, w), F32)] * 3,
        compiler_params=_cparams("arbitrary"),
        name="inproj_odd",
    )(*xs, modl, g, w_bf)


def _lru_conv(e, cw, cb):
    lo, hi = SUBLANES, SUBLANES + SEQ_TILE
    return (cw[0:1] * _shift_rows(e, -2)[lo:hi] + cw[1:2] * _shift_rows(e, -1)[lo:hi]
            + cw[2:3] * e[lo:hi] + cw[3:4] * _shift_rows(e, 1)[lo:hi] + cb)


def _lru_coeffs(xl, d, wa_ref, wx_ref, ba_ref, bx_ref, lam_ref, a_ref, b_ref):
    xb = xl.astype(BF16)
    chunk = wa_ref.shape[-1]
    n_chunk = xl.shape[1] // chunk
    pre_a = jnp.concatenate([jnp.dot(xb[:, c * chunk:(c + 1) * chunk], wa_ref[d, c],
                                     preferred_element_type=F32) for c in range(n_chunk)], axis=-1)
    pre_x = jnp.concatenate([jnp.dot(xb[:, c * chunk:(c + 1) * chunk], wx_ref[d, c],
                                     preferred_element_type=F32) for c in range(n_chunk)], axis=-1)
    r = jax.nn.sigmoid(pre_a + ba_ref[d])
    gi = jax.nn.sigmoid(pre_x + bx_ref[d])
    log_a = (-LRU_C * jax.nn.softplus(-lam_ref[d])) * r
    a_ref[...] = jnp.exp(log_a)
    th = jnp.tanh(log_a)
    b_ref[...] = jnp.sqrt(-2.0 * th / (1.0 - th)) * gi * xl


def _scan_tile(a_ref, b_ref, h_ref, carry, reverse):
    n_grp = a_ref.shape[0] // SUBLANES
    row = lax.broadcasted_iota(jnp.int32, (SUBLANES, a_ref.shape[1]), 0)

    def body(g, carry):
        gi = (n_grp - 1 - g) if reverse else g
        r0 = pl.multiple_of(gi * SUBLANES, SUBLANES)
        a8 = a_ref[pl.ds(r0, SUBLANES), :]
        b8 = b_ref[pl.ds(r0, SUBLANES), :]
        for s in (1, 2, 4):
            if reverse:
                ok = row < SUBLANES - s
                sh = SUBLANES - s
            else:
                ok = row >= s
                sh = s
            a_sh = jnp.where(ok, pltpu.roll(a8, sh, 0), 1.0)
            b_sh = jnp.where(ok, pltpu.roll(b8, sh, 0), 0.0)
            b8 = a8 * b_sh + b8
            a8 = a8 * a_sh
        h8 = a8 * carry + b8
        if h_ref is not None:
            h_ref[pl.ds(r0, SUBLANES), :] = h8
        return h8[0:1] if reverse else h8[SUBLANES - 1:SUBLANES]

    return lax.fori_loop(0, n_grp, body, carry)


def _lru_kernel(uf_ref, ufp_ref, ufn_ref, ur_ref, urp_ref, urn_ref, uc_ref, cw_ref, cb_ref,
                wa_ref, wx_ref, ba_ref, bx_ref, lam_ref, hf_ref, hr_ref,
                cf_ref, cr_ref, a_ref, b_ref):
    i = pl.program_id(1)
    nt = pl.num_programs(1)
    cw, cb = cw_ref[...], cb_ref[...]
    coeffs = functools.partial(_lru_coeffs, wa_ref=wa_ref, wx_ref=wx_ref, ba_ref=ba_ref,
                               bx_ref=bx_ref, lam_ref=lam_ref, a_ref=a_ref, b_ref=b_ref)

    @pl.when(i == 0)
    def _():
        z = jnp.zeros((SUBLANES, uc_ref.shape[1]), F32)
        xc = _lru_conv(jnp.concatenate([z, uc_ref[...], z], axis=0), cw, cb)
        zero = jnp.zeros((1, uc_ref.shape[1]), F32)
        coeffs(xc, 0)
        cf_ref[...] = _scan_tile(a_ref, b_ref, None, zero, False)
        coeffs(xc, 1)
        cr_ref[...] = _scan_tile(a_ref, b_ref, None, zero, True)

    xf = _lru_conv(_with_halo(ufp_ref, uf_ref[...], ufn_ref, i == 0, i == nt - 1), cw, cb)
    coeffs(xf, 0)
    cf_ref[...] = _scan_tile(a_ref, b_ref, hf_ref, cf_ref[...], False)
    xr = _lru_conv(_with_halo(urp_ref, ur_ref[...], urn_ref, i == nt - 1, i == 0), cw, cb)
    coeffs(xr, 1)
    cr_ref[...] = _scan_tile(a_ref, b_ref, hr_ref, cr_ref[...], True)


def _lru(ux, cw, cb, wa_bd, wx_bd, ba, bx, lam, dims):
    n_tok, w = ux.shape
    n_batch, seq = dims
    nt = seq // SEQ_TILE
    n_lat = n_batch * seq
    r = SEQ_TILE // SUBLANES
    n_blk = n_tok // SUBLANES
    fwd = lambda b, i: b * nt + i
    rev = lambda b, i: b * nt + nt - 1 - i

    def specs(tile):
        return [pl.BlockSpec((SEQ_TILE, w), lambda b, i: (tile(b, i), 0)),
                pl.BlockSpec((SUBLANES, w), lambda b, i: (jnp.maximum(tile(b, i) * r - 1, 0), 0)),
                pl.BlockSpec((SUBLANES, w), lambda b, i: (jnp.minimum((tile(b, i) + 1) * r, n_blk - 1), 0))]

    const2 = lambda b, i: (0, 0)
    const3 = lambda b, i: (0, 0, 0)
    const4 = lambda b, i: (0, 0, 0, 0)
    return pl.pallas_call(
        _lru_kernel,
        grid=(n_batch, nt),
        in_specs=specs(fwd) + specs(rev) + [
            pl.BlockSpec((SEQ_TILE, w), lambda b, i: (n_lat // SEQ_TILE + b, 0)),
            pl.BlockSpec(cw.shape, const2), pl.BlockSpec((1, w), const2),
            pl.BlockSpec(wa_bd.shape, const4), pl.BlockSpec(wx_bd.shape, const4),
            pl.BlockSpec(ba.shape, const3), pl.BlockSpec(bx.shape, const3),
            pl.BlockSpec(lam.shape, const3)],
        out_specs=[pl.BlockSpec((SEQ_TILE, w), lambda b, i: (fwd(b, i), 0)),
                   pl.BlockSpec((SEQ_TILE, w), lambda b, i: (rev(b, i), 0))],
        out_shape=[jax.ShapeDtypeStruct((n_lat, w), F32)] * 2,
        scratch_shapes=[pltpu.VMEM((1, w), F32), pltpu.VMEM((1, w), F32),
                        pltpu.VMEM((SEQ_TILE, w), F32), pltpu.VMEM((SEQ_TILE, w), F32)],
        compiler_params=_cparams("arbitrary", "arbitrary"),
        name="rglru",
    )(ux, ux, ux, ux, ux, ux, ux, cw, cb, wa_bd, wx_bd, ba, bx, lam)


def _merge_odd_kernel(up_ref, upp_ref, upn_ref, hf_ref, hr_ref, gg_ref, x_ref, mod_ref,
                      pw_ref, ps_ref, on_ref, wo_ref, nf_ref, wr_ref, br_ref,
                      xo_ref, tok_ref, route_ref, *, tiles_per_seq, seq):
    j = pl.program_id(0)
    ts = j % tiles_per_seq
    e = _with_halo(upp_ref, up_ref[...], upn_ref, ts == 0, ts == tiles_per_seq - 1)
    gw = pw_ref.shape[-1]
    lo, hi = SUBLANES, SUBLANES + SEQ_TILE
    tpos = ts * SEQ_TILE + lax.broadcasted_iota(jnp.int32, (SEQ_TILE, 1), 0)
    parts = []
    for g, win in enumerate(POOL_WINDOWS):
        eg = e[:, g * gw:(g + 1) * gw]
        p = eg
        span = 1
        while span < win:
            p = p + _shift_rows(p, span)
            span *= 2
        half = win // 2
        s = _shift_rows(p, -half)[lo:hi]
        cnt = (jnp.minimum(tpos + half, seq) - jnp.maximum(tpos - half, 0)).astype(F32)
        dlt = s / cnt - eg[lo:hi]
        parts.append(_bdot(dlt, pw_ref[g]))
    pool = jnp.concatenate(parts, axis=-1) * ps_ref[...]
    lru = (hf_ref[...] + hr_ref[...]) * gg_ref[...]
    _tail(pool, lru, on_ref[...], wo_ref, x_ref[...], mod_ref[0], nf_ref, wr_ref, br_ref,
          xo_ref, tok_ref, route_ref)


def _merge_odd(up, hf, hr, gg, xf, modl, pw_bf, ps, on, wo_bf, nf, wr, br, dims):
    d = xf.shape[1]
    n_tok, w = up.shape
    n_batch, seq = dims
    tps = seq // SEQ_TILE
    n_lat = n_batch * seq
    mod_map, _ = _tile_maps(n_batch * tps, tps, n_batch)
    row = lambda j: (j, 0)
    const = lambda j: (0, 0)
    prev, nxt = _halo_specs(w, n_tok)
    out_specs, out_shape = _tail_specs(d, n_lat)
    return pl.pallas_call(
        functools.partial(_merge_odd_kernel, tiles_per_seq=tps, seq=seq),
        grid=(n_lat // SEQ_TILE,),
        in_specs=[pl.BlockSpec((SEQ_TILE, w), row), prev, nxt,
                  pl.BlockSpec((SEQ_TILE, w), row), pl.BlockSpec((SEQ_TILE, w), row),
                  pl.BlockSpec((SEQ_TILE, w), row),
                  pl.BlockSpec((SEQ_TILE, d), row),
                  pl.BlockSpec((1, SUBLANES, d), mod_map),
                  pl.BlockSpec(pw_bf.shape, lambda j: (0, 0, 0)), pl.BlockSpec((1, w), const),
                  pl.BlockSpec((1, d), const), pl.BlockSpec(wo_bf.shape, const),
                  pl.BlockSpec((1, d), const), pl.BlockSpec(wr.shape, const),
                  pl.BlockSpec((1, LANES), const)],
        out_specs=out_specs, out_shape=out_shape,
        compiler_params=_cparams("arbitrary"),
        name="merge_odd",
    )(up, up, up, hf, hr, gg, xf, modl, pw_bf, ps, on, wo_bf, nf, wr, br)


def _slots_kernel(route_ref, dest_ref, cnt_ref, tot_ref, run_ref, start_ref):
    phase = pl.program_id(0)
    i = pl.program_id(1)
    rt = route_ref[...]
    m = rt.shape[0]
    lane = lax.broadcasted_iota(jnp.int32, rt.shape, 1).astype(F32)
    oh1 = lane == rt[:, 0:1]
    oh2 = lane == rt[:, 1:2]
    s = jnp.where(oh1 | oh2, 1.0, 0.0)
    colsum = jnp.sum(s, axis=0, keepdims=True)

    @pl.when((phase == 0) & (i == 0))
    def _():
        tot_ref[...] = jnp.zeros_like(tot_ref)

    @pl.when(phase == 0)
    def _():
        tot_ref[...] += colsum

    @pl.when((phase == 1) & (i == 0))
    def _():
        tot = tot_ref[...]
        padded = jnp.floor((tot + (MOE_BLOCK - 1)) / MOE_BLOCK) * MOE_BLOCK
        r = lax.broadcasted_iota(jnp.int32, (LANES, LANES), 0)
        c = lax.broadcasted_iota(jnp.int32, (LANES, LANES), 1)
        upper = jnp.where(r < c, 1.0, 0.0)
        p8 = jnp.broadcast_to(padded, (SUBLANES, LANES))
        start = jnp.dot(p8, upper, precision=HIGHEST, preferred_element_type=F32)[0:1]
        start_ref[...] = start
        run_ref[...] = jnp.zeros_like(run_ref)
        sub = lax.broadcasted_iota(jnp.int32, cnt_ref.shape, 0)
        cnt_ref[...] = jnp.where(sub == 0, tot, jnp.where(sub == 1, start, 0.0))

    @pl.when(phase == 1)
    def _():
        r = lax.broadcasted_iota(jnp.int32, (m, m), 0)
        c = lax.broadcasted_iota(jnp.int32, (m, m), 1)
        lower = jnp.where(c < r, 1.0, 0.0).astype(BF16)
        prefix = jnp.dot(lower, s.astype(BF16), preferred_element_type=F32)
        base = prefix + run_ref[...] + start_ref[...]
        d1 = jnp.sum(jnp.where(oh1, base, 0.0), axis=-1, keepdims=True)
        d2 = jnp.sum(jnp.where(oh2, base, 0.0), axis=-1, keepdims=True)
        dest_ref[...] = jnp.where(lane == 0, d1, jnp.where(lane == 1, d2, 0.0)).astype(jnp.int32)
        run_ref[...] += colsum


def _slots(route):
    n_tok = route.shape[0]
    nt = n_tok // TOK_TILE
    return pl.pallas_call(
        _slots_kernel,
        grid=(2, nt),
        in_specs=[pl.BlockSpec((TOK_TILE, LANES), lambda p, i: (i, 0))],
        out_specs=[pl.BlockSpec((TOK_TILE, LANES), lambda p, i: (i * p, 0)),
                   pl.BlockSpec((SUBLANES, LANES), lambda p, i: (0, 0))],
        out_shape=[jax.ShapeDtypeStruct((n_tok, LANES), jnp.int32),
                   jax.ShapeDtypeStruct((SUBLANES, LANES), F32)],
        scratch_shapes=[pltpu.VMEM((1, LANES), F32)] * 3,
        compiler_params=_cparams("arbitrary", "arbitrary"),
        name="moe_slots",
    )(route)


def _for_each_row(n_rows, fn):
    def body(g, c):
        g8 = pl.multiple_of(g * SUBLANES, SUBLANES)
        for j in range(SUBLANES):
            fn(g8, j)
        return c
    lax.fori_loop(0, n_rows // SUBLANES, body, 0)


def _dispatch_kernel(dest_ref, pad_lo_ref, pad_hi_ref, tok_ref, xs_ref, zrow, sem, zsem):
    i = pl.program_id(0)
    base = i * TOK_TILE

    def zero_copy(r):
        return pltpu.make_async_copy(zrow.at[pl.ds(0, 1)], xs_ref.at[pl.ds(r, 1)], zsem)

    def for_each_pad_row(fn):
        def per_expert(e, c):
            return lax.fori_loop(pad_lo_ref[e], pad_hi_ref[e], fn, c)
        lax.fori_loop(0, N_EXPERTS, per_expert, 0)

    @pl.when(i == 0)
    def _():
        zrow[...] = jnp.zeros_like(zrow)

        def fill(r, c):
            zero_copy(r).start()
            return c
        for_each_pad_row(fill)

    def issue(g8, j):
        rows = tok_ref.at[pl.ds(g8, SUBLANES)]
        for k in range(TOP_K):
            d = dest_ref[(base + g8) * TOP_K + (j * TOP_K + k)]
            pltpu.make_async_copy(rows.at[pl.ds(j, 1)], xs_ref.at[pl.ds(d, 1)], sem).start()

    _for_each_row(TOK_TILE, issue)
    for k in range(TOP_K):
        pltpu.make_async_copy(tok_ref, xs_ref.at[pl.ds(0, TOK_TILE)], sem).wait()

    @pl.when(i == pl.num_programs(0) - 1)
    def _():
        def drain(r, c):
            zero_copy(r).wait()
            return c
        for_each_pad_row(drain)


def _dispatch(dest_flat, pad_lo, pad_hi, tok, n_slots):
    n_tok, d = tok.shape
    return pl.pallas_call(
        _dispatch_kernel,
        grid_spec=pltpu.PrefetchScalarGridSpec(
            num_scalar_prefetch=3,
            grid=(n_tok // TOK_TILE,),
            in_specs=[pl.BlockSpec((TOK_TILE, d), lambda i, *_: (i, 0))],
            out_specs=pl.BlockSpec(memory_space=pl.ANY),
            scratch_shapes=[pltpu.VMEM((SUBLANES, d), F32),
                            pltpu.SemaphoreType.DMA(()), pltpu.SemaphoreType.DMA(())]),
        out_shape=jax.ShapeDtypeStruct((n_slots, d), F32),
        compiler_params=_cparams("arbitrary"),
        name="moe_dispatch",
    )(dest_flat, pad_lo, pad_hi, tok)


def _expert_kernel(be_ref, nu_ref, x_ref, w1_ref, w3_ref, w2_ref, y_ref, w1b, w3b, w2b):
    i = pl.program_id(0)

    @pl.when(i < nu_ref[0])
    def _():
        changed = (i == 0) | (be_ref[i] != be_ref[jnp.maximum(i - 1, 0)])

        @pl.when(changed)
        def _():
            w1b[...] = w1_ref[0, 0].astype(BF16)
            w3b[...] = w3_ref[0, 0].astype(BF16)
            w2b[...] = w2_ref[0, 0].astype(BF16)

        x = x_ref[...].astype(BF16)
        a = jnp.dot(x, w1b[...], preferred_element_type=F32)
        b = jnp.dot(x, w3b[...], preferred_element_type=F32)
        y_ref[...] = _bdot(_silu(a) * b, w2b[...])


def _experts(block_e, n_used, xs, w1, w3, w2, layer):
    n_slots, d = xs.shape
    de = w1.shape[-1]
    n_blocks = n_slots // MOE_BLOCK
    blk = lambda i, be, nu: (jnp.minimum(i, nu[0] - 1), 0)
    wmap = lambda i, be, nu: (layer, be[i], 0, 0)
    return pl.pallas_call(
        _expert_kernel,
        grid_spec=pltpu.PrefetchScalarGridSpec(
            num_scalar_prefetch=2,
            grid=(n_blocks,),
            in_specs=[pl.BlockSpec((MOE_BLOCK, d), blk),
                      pl.BlockSpec((1, 1, d, de), wmap), pl.BlockSpec((1, 1, d, de), wmap),
                      pl.BlockSpec((1, 1, de, d), wmap)],
            out_specs=pl.BlockSpec((MOE_BLOCK, d), blk),
            scratch_shapes=[pltpu.VMEM((d, de), BF16), pltpu.VMEM((d, de), BF16),
                            pltpu.VMEM((de, d), BF16)]),
        out_shape=jax.ShapeDtypeStruct((n_slots, d), F32),
        compiler_params=_cparams("arbitrary"),
        name="moe_experts",
    )(block_e, n_used, xs, w1, w3, w2)


def _combine_kernel(dest_ref, x_ref, route_ref, mod_ref, ys_ref, o_ref, ybuf, sem):
    base = pl.program_id(0) * TOK_TILE

    def issue(g8, j):
        for k in range(TOP_K):
            d = dest_ref[(base + g8) * TOP_K + (j * TOP_K + k)]
            rows = ybuf.at[k, pl.ds(g8, SUBLANES)]
            pltpu.make_async_copy(ys_ref.at[pl.ds(d, 1)], rows.at[pl.ds(j, 1)], sem).start()

    _for_each_row(TOK_TILE, issue)
    for k in range(TOP_K):
        pltpu.make_async_copy(ys_ref.at[pl.ds(0, TOK_TILE)], ybuf.at[k], sem).wait()
    rt = route_ref[...]
    y = rt[:, 2:3] * ybuf[0] + rt[:, 3:4] * ybuf[1]
    o_ref[...] = x_ref[...] + mod_ref[0][5:6] * y


def _combine(dest_flat, xn, route, modl, ys, dims):
    n_tok, d = xn.shape
    n_batch, seq = dims
    tpb = seq // TOK_TILE
    return pl.pallas_call(
        _combine_kernel,
        grid_spec=pltpu.PrefetchScalarGridSpec(
            num_scalar_prefetch=1,
            grid=(n_tok // TOK_TILE,),
            in_specs=[pl.BlockSpec((TOK_TILE, d), lambda i, dest: (i, 0)),
                      pl.BlockSpec((TOK_TILE, LANES), lambda i, dest: (i, 0)),
                      pl.BlockSpec((1, SUBLANES, d), lambda i, dest: (jnp.minimum(i // tpb, n_batch), 0, 0)),
                      pl.BlockSpec(memory_space=pl.ANY)],
            out_specs=pl.BlockSpec((TOK_TILE, d), lambda i, dest: (i, 0)),
            scratch_shapes=[pltpu.VMEM((TOP_K, TOK_TILE, d), F32), pltpu.SemaphoreType.DMA(())]),
        out_shape=jax.ShapeDtypeStruct((n_tok, d), F32),
        compiler_params=_cparams("arbitrary"),
        name="moe_combine",
    )(dest_flat, xn, route, modl, ys)


def _moe(xn, tok, route, modl, w1, w3, w2, layer, dims):
    n_tok = tok.shape[0]
    n = n_tok * TOP_K
    n_blocks = (n + N_EXPERTS * (MOE_BLOCK - 1) + MOE_BLOCK - 1) // MOE_BLOCK
    dest, counts = _slots(route)
    dest_flat = dest[:, :TOP_K].reshape(-1)
    cnt = counts[0, :N_EXPERTS].astype(jnp.int32)
    start = counts[1, :N_EXPERTS].astype(jnp.int32)
    blocks_per_e = (cnt + MOE_BLOCK - 1) // MOE_BLOCK
    blk_end = (start + blocks_per_e * MOE_BLOCK) // MOE_BLOCK
    n_used = blk_end[-1:]
    blk = jnp.minimum(jnp.arange(n_blocks, dtype=jnp.int32), n_used[0] - 1)
    block_e = jnp.sum((blk[:, None] >= blk_end[None, :]).astype(jnp.int32), axis=1)
    block_e = jnp.minimum(block_e, N_EXPERTS - 1)
    xs = _dispatch(dest_flat, start + cnt, blk_end * MOE_BLOCK, tok, n_blocks * MOE_BLOCK)
    ys = _experts(block_e, n_used, xs, w1, w3, w2, layer)
    return _combine(dest_flat, xn, route, modl, ys, dims)


def _rope_tables(seq):
    rows = seq // GRID_W
    row = jnp.repeat(jnp.arange(rows), GRID_W).astype(F32)
    col = jnp.tile(jnp.arange(GRID_W), rows).astype(F32)
    inv_freq = ROPE_BASE ** (-jnp.arange(ROPE_PAIRS, dtype=F32) / ROPE_PAIRS)
    ar, ac = row[:, None] * inv_freq, col[:, None] * inv_freq
    cos = jnp.concatenate([jnp.cos(ar), jnp.cos(ar), jnp.cos(ac), jnp.cos(ac)], axis=-1)
    sin = jnp.concatenate([-jnp.sin(ar), jnp.sin(ar), -jnp.sin(ac), jnp.sin(ac)], axis=-1)
    reps = LANES // HEAD_DIM
    cos, sin = jnp.tile(cos, (1, reps)), jnp.tile(sin, (1, reps))
    cos = jnp.concatenate([cos, jnp.ones((SEQ_TILE, LANES), F32)], axis=0)
    sin = jnp.concatenate([sin, jnp.zeros((SEQ_TILE, LANES), F32)], axis=0)
    return cos, sin


def _head_mean_matrix():
    r = jnp.arange(LANES)
    return jnp.where((r[:, None] // HEAD_DIM) == (r[None, :] // HEAD_DIM), 1.0 / HEAD_DIM, 0.0).astype(F32)


def _block_diag_chunks(w, chunk):
    dirs, nblk, bw, _ = w.shape
    per = chunk // bw
    w = w.reshape(dirs, nblk // per, per, bw, bw)
    eye = jnp.eye(per, dtype=w.dtype)
    out = jnp.einsum("dcpij,pq->dcpiqj", w, eye)
    return out.reshape(dirs, nblk // per, chunk, chunk).astype(BF16)


def _router_matrix(gw, gb, ew, eb):
    d = gw.shape[0]
    pad = LANES - N_EXPERTS - N_GROUPS
    wr = jnp.concatenate([ew, gw, jnp.zeros((d, pad), F32)], axis=1)
    br = jnp.concatenate([eb, gb, jnp.zeros((pad,), F32)])[None, :]
    return wr, br


def kernel(x, c, ctx, c_ctx, ada_w, ada_b, norm_mix, norm_ffn, out_norm, w_out, w_in_ab, q_norm, k_norm, attn_sink, sconv_w, sconv_b, w_in_cd, pool_w, pool_scale, lru_conv_w, lru_conv_b, lru_wa, lru_ba, lru_wx, lru_bx, lru_lambda, router_gw, router_gb, router_ew, router_eb, exp_w1, exp_w3, exp_w2):
    n_batch, seq, d = x.shape
    ctx_len = ctx.shape[1]
    depth = ada_w.shape[0]
    assert ctx_len == SEQ_TILE and seq % TOK_TILE == 0 and seq % GRID_W == 0
    assert depth == 2, "context-side odd-layer outputs are not implemented"
    dims = (n_batch, seq)
    n_lat = n_batch * seq

    cc = jnp.concatenate([c, c_ctx[None, :], jnp.zeros((SUBLANES - n_batch - 1, d), F32)], axis=0)
    mods = _ada(cc, ada_w, ada_b)
    mods = mods[:, :n_batch + 1].reshape(depth, n_batch + 1, N_MOD, d)
    mods = jnp.pad(mods, ((0, 0), (0, 0), (0, SUBLANES - N_MOD), (0, 0)))

    xparts = (x.reshape(n_lat, d), ctx.reshape(n_batch * ctx_len, d))
    cos_t, sin_t = _rope_tables(seq)
    pm = _head_mean_matrix()

    for i in range(depth):
        need_ctx = i < depth - 1
        j = i // 2
        modl = mods[i]
        n_out = n_lat + n_batch * ctx_len if need_ctx else n_lat
        wr, br = _router_matrix(router_gw[i], router_gb[i], router_ew[i], router_eb[i])
        wo_bf = w_out[i].astype(BF16)
        nm, nf, on = norm_mix[i][None, :], norm_ffn[i][None, :], out_norm[i][None, :]
        if i % 2 == 0:
            reps = LANES // HEAD_DIM
            q, k, v, gb, u = _inproj_even(xparts, modl, nm, w_in_ab[j].astype(BF16),
                                          jnp.tile(q_norm[j], reps)[None, :], jnp.tile(k_norm[j], reps)[None, :],
                                          cos_t, sin_t, pm, dims)
            att = _attention(q, k, v, attn_sink[j], dims, need_ctx)
            xn, tok, route = _merge_even(att, gb, u, xparts, modl, sconv_w[j], sconv_b[j][None, :], on, wo_bf,
                                         nf, wr, br, dims, n_out)
        else:
            ux, gg, up = _inproj_odd(xparts, modl, nm, w_in_cd[j].astype(BF16), dims)
            chunk = 2 * LANES
            hf, hr = _lru(ux, lru_conv_w[j], lru_conv_b[j][None, :],
                          _block_diag_chunks(lru_wa[j], chunk), _block_diag_chunks(lru_wx[j], chunk),
                          lru_ba[j][:, None, :], lru_bx[j][:, None, :], lru_lambda[j][:, None, :], dims)
            xn, tok, route = _merge_odd(up, hf, hr, gg, xparts[0], modl, pool_w[j].astype(BF16),
                                        pool_scale[j][None, :], on, wo_bf, nf, wr, br, dims)
        xparts = (_moe(xn, tok, route, modl, exp_w1, exp_w3, exp_w2, i, dims),)
    return xparts[0][:n_lat].reshape(n_batch, seq, d)
```

```python
import functools

import jax
import jax.numpy as jnp
from jax import lax
from jax.experimental import pallas as pl
from jax.experimental.pallas import tpu as pltpu

F32 = jnp.float32
BF16 = jnp.bfloat16
HIGHEST = lax.Precision.HIGHEST
LOG2E = 1.4426950408889634

GRID_W = 64
EPS = 1e-6
HEAD_DIM = 64
N_Q_HEADS = 8
N_KV_HEADS = 2
Q_PER_KV = N_Q_HEADS // N_KV_HEADS
WINDOW = 128
ATT_BLOCK = 128
ROPE_PAIRS = HEAD_DIM // 4
ROPE_BASE = 10000.0
POOL_WINDOWS = (2, 4, 8, 16)
LRU_C = 8.0
N_GROUPS = 4
EXPERTS_PER_GROUP = 8
N_EXPERTS = N_GROUPS * EXPERTS_PER_GROUP
TOP_K = 2
MOE_BLOCK = 256
N_MOD = 6

SUBLANES = 8
LANES = 128
SEQ_TILE = 256
TOK_TILE = 512
NEG = -1e30
VMEM_LIMIT = 48 * 1024 * 1024


def _cparams(*sem):
    return pltpu.CompilerParams(dimension_semantics=sem, vmem_limit_bytes=VMEM_LIMIT)


def _rms(x, g):
    ms = jnp.mean(x * x, axis=-1, keepdims=True)
    return x * lax.rsqrt(ms + EPS) * g


def _modulate(x, g, shift, scale):
    return _rms(x, g) * (1.0 + scale) + shift


def _bdot(a, b):
    return jnp.dot(a.astype(BF16), b, preferred_element_type=F32)


def _silu(x):
    return x * jax.nn.sigmoid(x)


def _ada_kernel(c_ref, w_ref, b_ref, o_ref):
    s = _silu(c_ref[...])
    o_ref[0] = jnp.dot(s, w_ref[0], precision=HIGHEST, preferred_element_type=F32) + b_ref[0]


def _ada(cc, ada_w, ada_b):
    depth, d, n = ada_w.shape
    tn = 1536
    return pl.pallas_call(
        _ada_kernel,
        grid=(depth, n // tn),
        in_specs=[pl.BlockSpec((SUBLANES, d), lambda l, j: (0, 0)),
                  pl.BlockSpec((1, d, tn), lambda l, j: (l, 0, j)),
                  pl.BlockSpec((1, 1, tn), lambda l, j: (l, 0, j))],
        out_specs=pl.BlockSpec((1, SUBLANES, tn), lambda l, j: (l, 0, j)),
        out_shape=jax.ShapeDtypeStruct((depth, SUBLANES, n), F32),
        compiler_params=_cparams("arbitrary", "arbitrary"),
        name="ada",
    )(cc, ada_w, ada_b.reshape(depth, 1, n))


def _head_norm_rope(t, pm, g, cos, sin):
    ms = jnp.dot(t * t, pm, precision=HIGHEST, preferred_element_type=F32)
    t = t * lax.rsqrt(ms + EPS) * g
    lane = lax.broadcasted_iota(jnp.int32, t.shape, 1)
    first_half = (lane % (2 * ROPE_PAIRS)) < ROPE_PAIRS
    partner = jnp.where(first_half, pltpu.roll(t, LANES - ROPE_PAIRS, 1), pltpu.roll(t, ROPE_PAIRS, 1))
    return t * cos + partner * sin


def _token_operands(xparts, n_lat_tiles):
    d = xparts[0].shape[1]
    lat = pl.BlockSpec((SEQ_TILE, d), lambda j: (jnp.minimum(j, n_lat_tiles - 1), 0))
    if len(xparts) == 2:
        ctx = pl.BlockSpec((SEQ_TILE, d), lambda j: (jnp.maximum(j - n_lat_tiles, 0), 0))
        return list(xparts), [lat, ctx]
    ctx = pl.BlockSpec((SEQ_TILE, d), lambda j: (jnp.maximum(j, n_lat_tiles), 0))
    return [xparts[0], xparts[0]], [lat, ctx]


def _pick_tokens(xl_ref, xc_ref, n_lat_tiles):
    return jnp.where(pl.program_id(0) >= n_lat_tiles, xc_ref[...], xl_ref[...])


def _inproj_even_kernel(xl_ref, xc_ref, mod_ref, g_ref, w_ref, qg_ref, kg_ref, cos_ref, sin_ref, pm_ref,
                        q_ref, kvx_ref, gb_ref, u_ref, *, n_lat_tiles):
    mod = mod_ref[0]
    x = _pick_tokens(xl_ref, xc_ref, n_lat_tiles)
    h = _modulate(x, g_ref[...], mod[0:1], mod[1:2]).astype(BF16)
    cos, sin, pm = cos_ref[...], sin_ref[...], pm_ref[...]
    kv_w = N_KV_HEADS * HEAD_DIM
    att_w = N_Q_HEADS * HEAD_DIM
    sc_w = gb_ref.shape[1]
    kv = jnp.dot(h, w_ref[:, 0:2 * kv_w], preferred_element_type=F32)
    k = _head_norm_rope(kv[:, :kv_w], pm, kg_ref[...], cos, sin)
    v = kv[:, kv_w:]
    kvx_ref[...] = jnp.concatenate([k, pltpu.roll(k, HEAD_DIM, 1), v, pltpu.roll(v, HEAD_DIM, 1)],
                                   axis=-1).astype(BF16)
    q = jnp.dot(h, w_ref[:, 2 * kv_w:2 * kv_w + att_w], preferred_element_type=F32)
    qscale = HEAD_DIM ** -0.5 * LOG2E
    for c in range(att_w // LANES):
        qc = _head_norm_rope(q[:, c * LANES:(c + 1) * LANES], pm, qg_ref[...], cos, sin)
        q_ref[:, c * LANES:(c + 1) * LANES] = (qc * qscale).astype(BF16)
    o = 2 * kv_w + att_w
    gb_ref[...] = jnp.dot(h, w_ref[:, o:o + sc_w], preferred_element_type=F32)
    gc = jnp.dot(h, w_ref[:, o + sc_w:o + 2 * sc_w], preferred_element_type=F32)
    gx = jnp.dot(h, w_ref[:, o + 2 * sc_w:o + 3 * sc_w], preferred_element_type=F32)
    u_ref[...] = gc * gx


def _tile_maps(n_lat_tiles, tiles_per_seq, n_batch):
    def mod_map(j):
        return (jnp.minimum(j // tiles_per_seq, n_batch), 0, 0)

    def pos_map(j):
        return (jnp.where(j < n_lat_tiles, j % tiles_per_seq, tiles_per_seq), 0)

    return mod_map, pos_map


def _inproj_even(xparts, modl, g, w_bf, q_g, k_g, cos_t, sin_t, pm, dims):
    n_tok = sum(p.shape[0] for p in xparts)
    d = xparts[0].shape[1]
    n_batch, seq = dims
    tps = seq // SEQ_TILE
    mod_map, pos_map = _tile_maps(n_batch * tps, tps, n_batch)
    xs, xspecs = _token_operands(xparts, n_batch * tps)
    kv_w, att_w = N_KV_HEADS * HEAD_DIM, N_Q_HEADS * HEAD_DIM
    sc_w = (w_bf.shape[1] - 2 * kv_w - att_w) // 3
    row = lambda j: (j, 0)
    const = lambda j: (0, 0)
    return pl.pallas_call(
        functools.partial(_inproj_even_kernel, n_lat_tiles=n_batch * tps),
        grid=(n_tok // SEQ_TILE,),
        in_specs=xspecs + [
                  pl.BlockSpec((1, SUBLANES, d), mod_map),
                  pl.BlockSpec((1, d), const),
                  pl.BlockSpec(w_bf.shape, const),
                  pl.BlockSpec((1, LANES), const),
                  pl.BlockSpec((1, LANES), const),
                  pl.BlockSpec((SEQ_TILE, LANES), pos_map),
                  pl.BlockSpec((SEQ_TILE, LANES), pos_map),
                  pl.BlockSpec((LANES, LANES), const)],
        out_specs=[pl.BlockSpec((SEQ_TILE, att_w), row),
                   pl.BlockSpec((SEQ_TILE, 4 * kv_w), row),
                   pl.BlockSpec((SEQ_TILE, sc_w), row),
                   pl.BlockSpec((SEQ_TILE, sc_w), row)],
        out_shape=[jax.ShapeDtypeStruct((n_tok, att_w), BF16),
                   jax.ShapeDtypeStruct((n_tok, 4 * kv_w), BF16),
                   jax.ShapeDtypeStruct((n_tok, sc_w), F32),
                   jax.ShapeDtypeStruct((n_tok, sc_w), F32)],
        compiler_params=_cparams("arbitrary"),
        name="inproj_even",
    )(*xs, modl, g, w_bf, q_g, k_g, cos_t, sin_t, pm)


def _attn_heads(q, kvx, mask_fn, sink_ref, o_ref):
    m, n = q.shape[0], kvx.shape[0]
    k_nat, k_swp = kvx[:, 0:LANES], kvx[:, LANES:2 * LANES]
    v_nat, v_swp = kvx[:, 2 * LANES:3 * LANES], kvx[:, 3 * LANES:4 * LANES]
    lo_n = lax.broadcasted_iota(jnp.int32, (n, LANES), 1) < HEAD_DIM
    lo_m = lax.broadcasted_iota(jnp.int32, (2 * m, LANES), 1) < HEAD_DIM
    top = lax.broadcasted_iota(jnp.int32, (2 * m, 1), 0) < m
    zero, one = jnp.zeros_like(k_nat), jnp.ones_like(v_nat)
    outs = []
    for h in range(N_KV_HEADS):
        k_lo, k_hi = (k_nat, k_swp) if h == 0 else (k_swp, k_nat)
        v_lo, v_hi = (v_nat, v_swp) if h == 0 else (v_swp, v_nat)
        g0 = 2 * h
        q2 = jnp.concatenate([q[:, g0 * LANES:(g0 + 1) * LANES],
                              q[:, (g0 + 1) * LANES:(g0 + 2) * LANES]], axis=0)
        parts = []
        for par in range(2):
            kz = jnp.where(lo_n, k_lo, zero) if par == 0 else jnp.where(lo_n, zero, k_hi)
            vz = jnp.where(lo_n, v_lo, one) if par == 0 else jnp.where(lo_n, one, v_hi)
            s = lax.dot_general(q2, kz, (((1,), (1,)), ((), ())), preferred_element_type=F32)
            s = mask_fn(s)
            hq = Q_PER_KV * h + par
            snk = jnp.where(top, sink_ref[hq], sink_ref[hq + 2]) * LOG2E
            mx = jnp.maximum(jnp.max(s, axis=-1, keepdims=True), snk)
            p = jnp.exp2(s - mx)
            parts.append((jnp.dot(p.astype(BF16), vz, preferred_element_type=F32), jnp.exp2(snk - mx)))
        (o_lo, e_lo), (o_hi, e_hi) = parts
        num = jnp.where(lo_m, o_lo, o_hi)
        den = pltpu.roll(jnp.where(lo_m, o_hi, o_lo), HEAD_DIM, 1) + jnp.where(lo_m, e_lo, e_hi)
        o = num / den
        outs += [o[0:m], o[m:2 * m]]
    o_ref[...] = jnp.concatenate(outs, axis=-1)


def _attn_lat_kernel(sink_ref, q_ref, kp_ref, kc_ref, kn_ref, kx_ref, o_ref):
    i = pl.program_id(1)
    nb = pl.num_programs(1)
    kvx = jnp.concatenate([kp_ref[...], kc_ref[...], kn_ref[...], kx_ref[...]], axis=0)
    m = ATT_BLOCK
    tok = lax.broadcasted_iota(jnp.int32, (2 * m, m), 0) % m
    col = lax.broadcasted_iota(jnp.int32, (2 * m, m), 1)
    ok_prev = (col >= tok) & (i > 0)
    ok_next = (col <= tok) & (i < nb - 1)

    def mask_fn(s):
        return jnp.concatenate([jnp.where(ok_prev, s[:, 0:m], NEG), s[:, m:2 * m],
                                jnp.where(ok_next, s[:, 2 * m:3 * m], NEG), s[:, 3 * m:]], axis=1)

    _attn_heads(q_ref[...], kvx, mask_fn, sink_ref, o_ref)


def _attn_ctx_kernel(sink_ref, att_in_ref, q_ref, kx_ref, o_ref):
    del att_in_ref
    _attn_heads(q_ref[...], kx_ref[...], lambda s: s, sink_ref, o_ref)


def _attention(q, kvx, sink, dims, need_ctx):
    n_tok = q.shape[0]
    n_batch, seq = dims
    nb = seq // ATT_BLOCK
    n_lat = n_batch * seq
    ctx_len = (n_tok - n_lat) // n_batch
    att_w, kvx_w = q.shape[1], kvx.shape[1]
    ctx_blk0 = n_lat // ctx_len
    cur = lambda b, i: (b * nb + i, 0)
    prev = lambda b, i: (b * nb + jnp.maximum(i - 1, 0), 0)
    nxt = lambda b, i: (b * nb + jnp.minimum(i + 1, nb - 1), 0)
    cx = lambda b, i: (ctx_blk0 + b, 0)
    smem = pl.BlockSpec(memory_space=pltpu.SMEM)
    kvb = lambda f: pl.BlockSpec((ATT_BLOCK, kvx_w), f)
    att = pl.pallas_call(
        _attn_lat_kernel,
        grid=(n_batch, nb),
        in_specs=[smem, pl.BlockSpec((ATT_BLOCK, att_w), cur),
                  kvb(prev), kvb(cur), kvb(nxt), pl.BlockSpec((ctx_len, kvx_w), cx)],
        out_specs=pl.BlockSpec((ATT_BLOCK, att_w), cur),
        out_shape=jax.ShapeDtypeStruct((n_tok, att_w), F32),
        compiler_params=_cparams("arbitrary", "arbitrary"),
        name="attn_latent",
    )(sink, q, kvx, kvx, kvx, kvx)
    if not need_ctx:
        return att
    cxb = lambda b: (ctx_blk0 + b, 0)
    return pl.pallas_call(
        _attn_ctx_kernel,
        grid=(n_batch,),
        in_specs=[smem, pl.BlockSpec(memory_space=pl.ANY),
                  pl.BlockSpec((ctx_len, att_w), cxb), pl.BlockSpec((ctx_len, kvx_w), cxb)],
        out_specs=pl.BlockSpec((ctx_len, att_w), cxb),
        out_shape=jax.ShapeDtypeStruct((n_tok, att_w), F32),
        input_output_aliases={1: 0},
        compiler_params=_cparams("arbitrary"),
        name="attn_context",
    )(sink, att, q, kvx)


def _route(logits):
    lane = lax.broadcasted_iota(jnp.int32, logits.shape, 1).astype(F32)
    big = float(LANES)
    gl = jnp.where((lane >= N_EXPERTS) & (lane < N_EXPERTS + N_GROUPS), logits, NEG)
    gmax = jnp.max(gl, axis=-1, keepdims=True)
    gidx = jnp.min(jnp.where(gl == gmax, lane, big), axis=-1, keepdims=True) - N_EXPERTS
    p_sel = 1.0 / jnp.sum(jnp.exp(gl - gmax), axis=-1, keepdims=True)
    lo = gidx * EXPERTS_PER_GROUP
    el = jnp.where((lane >= lo) & (lane < lo + EXPERTS_PER_GROUP), logits, NEG)
    m1 = jnp.max(el, axis=-1, keepdims=True)
    i1 = jnp.min(jnp.where(el == m1, lane, big), axis=-1, keepdims=True)
    el2 = jnp.where(lane == i1, NEG, el)
    m2 = jnp.max(el2, axis=-1, keepdims=True)
    i2 = jnp.min(jnp.where(el2 == m2, lane, big), axis=-1, keepdims=True)
    t = jnp.exp(m2 - m1)
    w1 = p_sel / (1.0 + t)
    w2 = p_sel * t / (1.0 + t)
    return jnp.where(lane == 0, i1, jnp.where(lane == 1, i2, jnp.where(lane == 2, w1,
                     jnp.where(lane == 3, w2, 0.0))))


def _tail(n1, n2, gain, wo_ref, x, mod, nf_ref, wr_ref, br_ref, xo_ref, tok_ref, route_ref):
    half = n1.shape[1]
    y = (_bdot(_rms(n1, gain[:, :half]), wo_ref[0:half, :])
         + _bdot(_rms(n2, gain[:, half:]), wo_ref[half:, :]))
    xn = x + mod[2:3] * y
    xo_ref[...] = xn
    tok = _modulate(xn, nf_ref[...], mod[3:4], mod[4:5])
    tok_ref[...] = tok
    logits = jnp.dot(tok, wr_ref[...], precision=HIGHEST, preferred_element_type=F32) + br_ref[...]
    route_ref[...] = _route(logits)


def _seq_flags(j, n_lat_tiles, tiles_per_seq):
    is_ctx = j >= n_lat_tiles
    first = is_ctx | (j % tiles_per_seq == 0)
    last = is_ctx | (j % tiles_per_seq == tiles_per_seq - 1)
    return first, last


def _with_halo(prev_ref, tile, next_ref, first, last):
    p = jnp.where(first, 0.0, prev_ref[...])
    n = jnp.where(last, 0.0, next_ref[...])
    return jnp.concatenate([p, tile, n], axis=0)


def _shift_rows(e, k):
    return pltpu.roll(e, (-k) % e.shape[0], 0)


def _merge_even_kernel(att_ref, gb_ref, u_ref, up_ref, un_ref, xl_ref, xc_ref, mod_ref, cw_ref, cb_ref,
                       on_ref, wo_ref, nf_ref, wr_ref, br_ref,
                       xo_ref, tok_ref, route_ref, *, n_lat_tiles, tiles_per_seq):
    j = pl.program_id(0)
    first, last = _seq_flags(j, n_lat_tiles, tiles_per_seq)
    e = _with_halo(up_ref, u_ref[...], un_ref, first, last)
    cw = cw_ref[...]
    lo, hi = SUBLANES, SUBLANES + SEQ_TILE
    conv = (cw[0:1] * _shift_rows(e, -1)[lo:hi] + cw[1:2] * e[lo:hi]
            + cw[2:3] * _shift_rows(e, 1)[lo:hi] + cb_ref[...])
    conv = gb_ref[...] * conv
    x = _pick_tokens(xl_ref, xc_ref, n_lat_tiles)
    _tail(att_ref[...], conv, on_ref[...], wo_ref, x, mod_ref[0], nf_ref, wr_ref, br_ref,
          xo_ref, tok_ref, route_ref)


def _halo_specs(width, n_rows):
    r = SEQ_TILE // SUBLANES
    n_blk = n_rows // SUBLANES
    prev = pl.BlockSpec((SUBLANES, width), lambda j: (jnp.maximum(j * r - 1, 0), 0))
    nxt = pl.BlockSpec((SUBLANES, width), lambda j: (jnp.minimum((j + 1) * r, n_blk - 1), 0))
    return prev, nxt


def _tail_specs(d, n_out_tok):
    row = lambda j: (j, 0)
    out_specs = [pl.BlockSpec((SEQ_TILE, d), row), pl.BlockSpec((SEQ_TILE, d), row),
                 pl.BlockSpec((SEQ_TILE, LANES), row)]
    out_shape = [jax.ShapeDtypeStruct((n_out_tok, d), F32), jax.ShapeDtypeStruct((n_out_tok, d), F32),
                 jax.ShapeDtypeStruct((n_out_tok, LANES), F32)]
    return out_specs, out_shape


def _merge_even(att, gb, u, xparts, modl, cw, cb, on, wo_bf, nf, wr, br, dims, n_out_tok):
    n_tok = u.shape[0]
    d = xparts[0].shape[1]
    n_batch, seq = dims
    tps = seq // SEQ_TILE
    mod_map, _ = _tile_maps(n_batch * tps, tps, n_batch)
    xs, xspecs = _token_operands(xparts, n_batch * tps)
    sc_w = u.shape[1]
    row = lambda j: (j, 0)
    const = lambda j: (0, 0)
    prev, nxt = _halo_specs(sc_w, n_tok)
    out_specs, out_shape = _tail_specs(d, n_out_tok)
    return pl.pallas_call(
        functools.partial(_merge_even_kernel, n_lat_tiles=n_batch * tps, tiles_per_seq=tps),
        grid=(n_out_tok // SEQ_TILE,),
        in_specs=[pl.BlockSpec((SEQ_TILE, att.shape[1]), row),
                  pl.BlockSpec((SEQ_TILE, sc_w), row),
                  pl.BlockSpec((SEQ_TILE, sc_w), row), prev, nxt] + xspecs + [
                  pl.BlockSpec((1, SUBLANES, d), mod_map),
                  pl.BlockSpec(cw.shape, const), pl.BlockSpec((1, sc_w), const),
                  pl.BlockSpec((1, d), const), pl.BlockSpec(wo_bf.shape, const),
                  pl.BlockSpec((1, d), const), pl.BlockSpec(wr.shape, const),
                  pl.BlockSpec((1, LANES), const)],
        out_specs=out_specs, out_shape=out_shape,
        compiler_params=_cparams("arbitrary"),
        name="merge_even",
    )(att, gb, u, u, u, *xs, modl, cw, cb, on, wo_bf, nf, wr, br)


def _inproj_odd_kernel(xl_ref, xc_ref, mod_ref, g_ref, w_ref, ux_ref, gg_ref, up_ref, *, n_lat_tiles):
    mod = mod_ref[0]
    x = _pick_tokens(xl_ref, xc_ref, n_lat_tiles)
    h = _modulate(x, g_ref[...], mod[0:1], mod[1:2]).astype(BF16)
    w = ux_ref.shape[1]
    ux_ref[...] = jnp.dot(h, w_ref[:, 0:w], preferred_element_type=F32)
    ug = jnp.dot(h, w_ref[:, w:2 * w], preferred_element_type=F32)
    gg_ref[...] = jax.nn.gelu(ug, approximate=True)
    up_ref[...] = jnp.dot(h, w_ref[:, 2 * w:3 * w], preferred_element_type=F32)


def _inproj_odd(xparts, modl, g, w_bf, dims):
    n_tok = sum(p.shape[0] for p in xparts)
    d = xparts[0].shape[1]
    n_batch, seq = dims
    tps = seq // SEQ_TILE
    mod_map, _ = _tile_maps(n_batch * tps, tps, n_batch)
    xs, xspecs = _token_operands(xparts, n_batch * tps)
    w = w_bf.shape[1] // 3
    row = lambda j: (j, 0)
    const = lambda j: (0, 0)
    return pl.pallas_call(
        functools.partial(_inproj_odd_kernel, n_lat_tiles=n_batch * tps),
        grid=(n_tok // SEQ_TILE,),
        in_specs=xspecs + [
                  pl.BlockSpec((1, SUBLANES, d), mod_map),
                  pl.BlockSpec((1, d), const),
                  pl.BlockSpec(w_bf.shape, const)],
        out_specs=[pl.BlockSpec((SEQ_TILE, w), row)] * 3,
        out_shape=[jax.ShapeDtypeStruct((n_tok, w), F32)] * 3,
        compiler_params=_cparams("arbitrary"),
        name="inproj_odd",
    )(*xs, modl, g, w_bf)


def _lru_conv(e, cw, cb):
    lo, hi = SUBLANES, SUBLANES + SEQ_TILE
    return (cw[0:1] * _shift_rows(e, -2)[lo:hi] + cw[1:2] * _shift_rows(e, -1)[lo:hi]
            + cw[2:3] * e[lo:hi] + cw[3:4] * _shift_rows(e, 1)[lo:hi] + cb)


def _lru_coeffs(xl, d, wa_ref, wx_ref, ba_ref, bx_ref, lam_ref, a_ref, b_ref):
    xb = xl.astype(BF16)
    chunk = wa_ref.shape[-1]
    n_chunk = xl.shape[1] // chunk
    pre_a = jnp.concatenate([jnp.dot(xb[:, c * chunk:(c + 1) * chunk], wa_ref[d, c],
                                     preferred_element_type=F32) for c in range(n_chunk)], axis=-1)
    pre_x = jnp.concatenate([jnp.dot(xb[:, c * chunk:(c + 1) * chunk], wx_ref[d, c],
                                     preferred_element_type=F32) for c in range(n_chunk)], axis=-1)
    r = jax.nn.sigmoid(pre_a + ba_ref[d])
    gi = jax.nn.sigmoid(pre_x + bx_ref[d])
    log_a = (-LRU_C * jax.nn.softplus(-lam_ref[d])) * r
    a_ref[...] = jnp.exp(log_a)
    th = jnp.tanh(log_a)
    b_ref[...] = jnp.sqrt(-2.0 * th / (1.0 - th)) * gi * xl


def _scan_tile(a_ref, b_ref, h_ref, carry, reverse):
    n_grp = a_ref.shape[0] // SUBLANES
    row = lax.broadcasted_iota(jnp.int32, (SUBLANES, a_ref.shape[1]), 0)

    def body(g, carry):
        gi = (n_grp - 1 - g) if reverse else g
        r0 = pl.multiple_of(gi * SUBLANES, SUBLANES)
        a8 = a_ref[pl.ds(r0, SUBLANES), :]
        b8 = b_ref[pl.ds(r0, SUBLANES), :]
        for s in (1, 2, 4):
            if reverse:
                ok = row < SUBLANES - s
                sh = SUBLANES - s
            else:
                ok = row >= s
                sh = s
            a_sh = jnp.where(ok, pltpu.roll(a8, sh, 0), 1.0)
            b_sh = jnp.where(ok, pltpu.roll(b8, sh, 0), 0.0)
            b8 = a8 * b_sh + b8
            a8 = a8 * a_sh
        h8 = a8 * carry + b8
        if h_ref is not None:
            h_ref[pl.ds(r0, SUBLANES), :] = h8
        return h8[0:1] if reverse else h8[SUBLANES - 1:SUBLANES]

    return lax.fori_loop(0, n_grp, body, carry)


def _lru_kernel(uf_ref, ufp_ref, ufn_ref, ur_ref, urp_ref, urn_ref, uc_ref, cw_ref, cb_ref,
                wa_ref, wx_ref, ba_ref, bx_ref, lam_ref, hf_ref, hr_ref,
                cf_ref, cr_ref, a_ref, b_ref):
    i = pl.program_id(1)
    nt = pl.num_programs(1)
    cw, cb = cw_ref[...], cb_ref[...]
    coeffs = functools.partial(_lru_coeffs, wa_ref=wa_ref, wx_ref=wx_ref, ba_ref=ba_ref,
                               bx_ref=bx_ref, lam_ref=lam_ref, a_ref=a_ref, b_ref=b_ref)

    @pl.when(i == 0)
    def _():
        z = jnp.zeros((SUBLANES, uc_ref.shape[1]), F32)
        xc = _lru_conv(jnp.concatenate([z, uc_ref[...], z], axis=0), cw, cb)
        zero = jnp.zeros((1, uc_ref.shape[1]), F32)
        coeffs(xc, 0)
        cf_ref[...] = _scan_tile(a_ref, b_ref, None, zero, False)
        coeffs(xc, 1)
        cr_ref[...] = _scan_tile(a_ref, b_ref, None, zero, True)

    xf = _lru_conv(_with_halo(ufp_ref, uf_ref[...], ufn_ref, i == 0, i == nt - 1), cw, cb)
    coeffs(xf, 0)
    cf_ref[...] = _scan_tile(a_ref, b_ref, hf_ref, cf_ref[...], False)
    xr = _lru_conv(_with_halo(urp_ref, ur_ref[...], urn_ref, i == nt - 1, i == 0), cw, cb)
    coeffs(xr, 1)
    cr_ref[...] = _scan_tile(a_ref, b_ref, hr_ref, cr_ref[...], True)


def _lru(ux, cw, cb, wa_bd, wx_bd, ba, bx, lam, dims):
    n_tok, w = ux.shape
    n_batch, seq = dims
    nt = seq // SEQ_TILE
    n_lat = n_batch * seq
    r = SEQ_TILE // SUBLANES
    n_blk = n_tok // SUBLANES
    fwd = lambda b, i: b * nt + i
    rev = lambda b, i: b * nt + nt - 1 - i

    def specs(tile):
        return [pl.BlockSpec((SEQ_TILE, w), lambda b, i: (tile(b, i), 0)),
                pl.BlockSpec((SUBLANES, w), lambda b, i: (jnp.maximum(tile(b, i) * r - 1, 0), 0)),
                pl.BlockSpec((SUBLANES, w), lambda b, i: (jnp.minimum((tile(b, i) + 1) * r, n_blk - 1), 0))]

    const2 = lambda b, i: (0, 0)
    const3 = lambda b, i: (0, 0, 0)
    const4 = lambda b, i: (0, 0, 0, 0)
    return pl.pallas_call(
        _lru_kernel,
        grid=(n_batch, nt),
        in_specs=specs(fwd) + specs(rev) + [
            pl.BlockSpec((SEQ_TILE, w), lambda b, i: (n_lat // SEQ_TILE + b, 0)),
            pl.BlockSpec(cw.shape, const2), pl.BlockSpec((1, w), const2),
            pl.BlockSpec(wa_bd.shape, const4), pl.BlockSpec(wx_bd.shape, const4),
            pl.BlockSpec(ba.shape, const3), pl.BlockSpec(bx.shape, const3),
            pl.BlockSpec(lam.shape, const3)],
        out_specs=[pl.BlockSpec((SEQ_TILE, w), lambda b, i: (fwd(b, i), 0)),
                   pl.BlockSpec((SEQ_TILE, w), lambda b, i: (rev(b, i), 0))],
        out_shape=[jax.ShapeDtypeStruct((n_lat, w), F32)] * 2,
        scratch_shapes=[pltpu.VMEM((1, w), F32), pltpu.VMEM((1, w), F32),
                        pltpu.VMEM((SEQ_TILE, w), F32), pltpu.VMEM((SEQ_TILE, w), F32)],
        compiler_params=_cparams("arbitrary", "arbitrary"),
        name="rglru",
    )(ux, ux, ux, ux, ux, ux, ux, cw, cb, wa_bd, wx_bd, ba, bx, lam)


def _merge_odd_kernel(up_ref, upp_ref, upn_ref, hf_ref, hr_ref, gg_ref, x_ref, mod_ref,
                      pw_ref, ps_ref, on_ref, wo_ref, nf_ref, wr_ref, br_ref,
                      xo_ref, tok_ref, route_ref, *, tiles_per_seq, seq):
    j = pl.program_id(0)
    ts = j % tiles_per_seq
    e = _with_halo(upp_ref, up_ref[...], upn_ref, ts == 0, ts == tiles_per_seq - 1)
    gw = pw_ref.shape[-1]
    lo, hi = SUBLANES, SUBLANES + SEQ_TILE
    tpos = ts * SEQ_TILE + lax.broadcasted_iota(jnp.int32, (SEQ_TILE, 1), 0)
    parts = []
    for g, win in enumerate(POOL_WINDOWS):
        eg = e[:, g * gw:(g + 1) * gw]
        p = eg
        span = 1
        while span < win:
            p = p + _shift_rows(p, span)
            span *= 2
        half = win // 2
        s = _shift_rows(p, -half)[lo:hi]
        cnt = (jnp.minimum(tpos + half, seq) - jnp.maximum(tpos - half, 0)).astype(F32)
        dlt = s / cnt - eg[lo:hi]
        parts.append(_bdot(dlt, pw_ref[g]))
    pool = jnp.concatenate(parts, axis=-1) * ps_ref[...]
    lru = (hf_ref[...] + hr_ref[...]) * gg_ref[...]
    _tail(pool, lru, on_ref[...], wo_ref, x_ref[...], mod_ref[0], nf_ref, wr_ref, br_ref,
          xo_ref, tok_ref, route_ref)


def _merge_odd(up, hf, hr, gg, xf, modl, pw_bf, ps, on, wo_bf, nf, wr, br, dims):
    d = xf.shape[1]
    n_tok, w = up.shape
    n_batch, seq = dims
    tps = seq // SEQ_TILE
    n_lat = n_batch * seq
    mod_map, _ = _tile_maps(n_batch * tps, tps, n_batch)
    row = lambda j: (j, 0)
    const = lambda j: (0, 0)
    prev, nxt = _halo_specs(w, n_tok)
    out_specs, out_shape = _tail_specs(d, n_lat)
    return pl.pallas_call(
        functools.partial(_merge_odd_kernel, tiles_per_seq=tps, seq=seq),
        grid=(n_lat // SEQ_TILE,),
        in_specs=[pl.BlockSpec((SEQ_TILE, w), row), prev, nxt,
                  pl.BlockSpec((SEQ_TILE, w), row), pl.BlockSpec((SEQ_TILE, w), row),
                  pl.BlockSpec((SEQ_TILE, w), row),
                  pl.BlockSpec((SEQ_TILE, d), row),
                  pl.BlockSpec((1, SUBLANES, d), mod_map),
                  pl.BlockSpec(pw_bf.shape, lambda j: (0, 0, 0)), pl.BlockSpec((1, w), const),
                  pl.BlockSpec((1, d), const), pl.BlockSpec(wo_bf.shape, const),
                  pl.BlockSpec((1, d), const), pl.BlockSpec(wr.shape, const),
                  pl.BlockSpec((1, LANES), const)],
        out_specs=out_specs, out_shape=out_shape,
        compiler_params=_cparams("arbitrary"),
        name="merge_odd",
    )(up, up, up, hf, hr, gg, xf, modl, pw_bf, ps, on, wo_bf, nf, wr, br)


def _slots_kernel(route_ref, dest_ref, cnt_ref, tot_ref, run_ref, start_ref):
    phase = pl.program_id(0)
    i = pl.program_id(1)
    rt = route_ref[...]
    m = rt.shape[0]
    lane = lax.broadcasted_iota(jnp.int32, rt.shape, 1).astype(F32)
    oh1 = lane == rt[:, 0:1]
    oh2 = lane == rt[:, 1:2]
    s = jnp.where(oh1 | oh2, 1.0, 0.0)
    colsum = jnp.sum(s, axis=0, keepdims=True)

    @pl.when((phase == 0) & (i == 0))
    def _():
        tot_ref[...] = jnp.zeros_like(tot_ref)

    @pl.when(phase == 0)
    def _():
        tot_ref[...] += colsum

    @pl.when((phase == 1) & (i == 0))
    def _():
        tot = tot_ref[...]
        padded = jnp.floor((tot + (MOE_BLOCK - 1)) / MOE_BLOCK) * MOE_BLOCK
        r = lax.broadcasted_iota(jnp.int32, (LANES, LANES), 0)
        c = lax.broadcasted_iota(jnp.int32, (LANES, LANES), 1)
        upper = jnp.where(r < c, 1.0, 0.0)
        p8 = jnp.broadcast_to(padded, (SUBLANES, LANES))
        start = jnp.dot(p8, upper, precision=HIGHEST, preferred_element_type=F32)[0:1]
        start_ref[...] = start
        run_ref[...] = jnp.zeros_like(run_ref)
        sub = lax.broadcasted_iota(jnp.int32, cnt_ref.shape, 0)
        cnt_ref[...] = jnp.where(sub == 0, tot, jnp.where(sub == 1, start, 0.0))

    @pl.when(phase == 1)
    def _():
        r = lax.broadcasted_iota(jnp.int32, (m, m), 0)
        c = lax.broadcasted_iota(jnp.int32, (m, m), 1)
        lower = jnp.where(c < r, 1.0, 0.0).astype(BF16)
        prefix = jnp.dot(lower, s.astype(BF16), preferred_element_type=F32)
        base = prefix + run_ref[...] + start_ref[...]
        d1 = jnp.sum(jnp.where(oh1, base, 0.0), axis=-1, keepdims=True)
        d2 = jnp.sum(jnp.where(oh2, base, 0.0), axis=-1, keepdims=True)
        dest_ref[...] = jnp.where(lane == 0, d1, jnp.where(lane == 1, d2, 0.0)).astype(jnp.int32)
        run_ref[...] += colsum


def _slots(route):
    n_tok = route.shape[0]
    nt = n_tok // TOK_TILE
    return pl.pallas_call(
        _slots_kernel,
        grid=(2, nt),
        in_specs=[pl.BlockSpec((TOK_TILE, LANES), lambda p, i: (i, 0))],
        out_specs=[pl.BlockSpec((TOK_TILE, LANES), lambda p, i: (i * p, 0)),
                   pl.BlockSpec((SUBLANES, LANES), lambda p, i: (0, 0))],
        out_shape=[jax.ShapeDtypeStruct((n_tok, LANES), jnp.int32),
                   jax.ShapeDtypeStruct((SUBLANES, LANES), F32)],
        scratch_shapes=[pltpu.VMEM((1, LANES), F32)] * 3,
        compiler_params=_cparams("arbitrary", "arbitrary"),
        name="moe_slots",
    )(route)


def _for_each_row(n_rows, fn):
    def body(g, c):
        g8 = pl.multiple_of(g * SUBLANES, SUBLANES)
        for j in range(SUBLANES):
            fn(g8, j)
        return c
    lax.fori_loop(0, n_rows // SUBLANES, body, 0)


def _dispatch_kernel(dest_ref, pad_lo_ref, pad_hi_ref, tok_ref, xs_ref, zrow, sem, zsem):
    i = pl.program_id(0)
    base = i * TOK_TILE

    def zero_copy(r):
        return pltpu.make_async_copy(zrow.at[pl.ds(0, 1)], xs_ref.at[pl.ds(r, 1)], zsem)

    def for_each_pad_row(fn):
        def per_expert(e, c):
            return lax.fori_loop(pad_lo_ref[e], pad_hi_ref[e], fn, c)
        lax.fori_loop(0, N_EXPERTS, per_expert, 0)

    @pl.when(i == 0)
    def _():
        zrow[...] = jnp.zeros_like(zrow)

        def fill(r, c):
            zero_copy(r).start()
            return c
        for_each_pad_row(fill)

    def issue(g8, j):
        rows = tok_ref.at[pl.ds(g8, SUBLANES)]
        for k in range(TOP_K):
            d = dest_ref[(base + g8) * TOP_K + (j * TOP_K + k)]
            pltpu.make_async_copy(rows.at[pl.ds(j, 1)], xs_ref.at[pl.ds(d, 1)], sem).start(priority=k)

    _for_each_row(TOK_TILE, issue)
    for k in range(TOP_K):
        pltpu.make_async_copy(tok_ref, xs_ref.at[pl.ds(0, TOK_TILE)], sem).wait()

    @pl.when(i == pl.num_programs(0) - 1)
    def _():
        def drain(r, c):
            zero_copy(r).wait()
            return c
        for_each_pad_row(drain)


def _dispatch(dest_flat, pad_lo, pad_hi, tok, n_slots):
    n_tok, d = tok.shape
    return pl.pallas_call(
        _dispatch_kernel,
        grid_spec=pltpu.PrefetchScalarGridSpec(
            num_scalar_prefetch=3,
            grid=(n_tok // TOK_TILE,),
            in_specs=[pl.BlockSpec((TOK_TILE, d), lambda i, *_: (i, 0))],
            out_specs=pl.BlockSpec(memory_space=pl.ANY),
            scratch_shapes=[pltpu.VMEM((SUBLANES, d), F32),
                            pltpu.SemaphoreType.DMA(()), pltpu.SemaphoreType.DMA(())]),
        out_shape=jax.ShapeDtypeStruct((n_slots, d), F32),
        compiler_params=_cparams("arbitrary"),
        name="moe_dispatch",
    )(dest_flat, pad_lo, pad_hi, tok)


def _expert_kernel(be_ref, nu_ref, x_ref, w1_ref, w3_ref, w2_ref, y_ref, w1b, w3b, w2b):
    i = pl.program_id(0)

    @pl.when(i < nu_ref[0])
    def _():
        changed = (i == 0) | (be_ref[i] != be_ref[jnp.maximum(i - 1, 0)])

        @pl.when(changed)
        def _():
            w1b[...] = w1_ref[0, 0].astype(BF16)
            w3b[...] = w3_ref[0, 0].astype(BF16)
            w2b[...] = w2_ref[0, 0].astype(BF16)

        x = x_ref[...].astype(BF16)
        a = jnp.dot(x, w1b[...], preferred_element_type=F32)
        b = jnp.dot(x, w3b[...], preferred_element_type=F32)
        y_ref[...] = _bdot(_silu(a) * b, w2b[...])


def _experts(block_e, n_used, xs, w1, w3, w2, layer):
    n_slots, d = xs.shape
    de = w1.shape[-1]
    n_blocks = n_slots // MOE_BLOCK
    blk = lambda i, be, nu: (jnp.minimum(i, nu[0] - 1), 0)
    wmap = lambda i, be, nu: (layer, be[i], 0, 0)
    return pl.pallas_call(
        _expert_kernel,
        grid_spec=pltpu.PrefetchScalarGridSpec(
            num_scalar_prefetch=2,
            grid=(n_blocks,),
            in_specs=[pl.BlockSpec((MOE_BLOCK, d), blk),
                      pl.BlockSpec((1, 1, d, de), wmap), pl.BlockSpec((1, 1, d, de), wmap),
                      pl.BlockSpec((1, 1, de, d), wmap)],
            out_specs=pl.BlockSpec((MOE_BLOCK, d), blk),
            scratch_shapes=[pltpu.VMEM((d, de), BF16), pltpu.VMEM((d, de), BF16),
                            pltpu.VMEM((de, d), BF16)]),
        out_shape=jax.ShapeDtypeStruct((n_slots, d), F32),
        compiler_params=_cparams("arbitrary"),
        name="moe_experts",
    )(block_e, n_used, xs, w1, w3, w2)


def _combine_kernel(dest_ref, x_ref, route_ref, mod_ref, ys_ref, o_ref, ybuf, sem):
    base = pl.program_id(0) * TOK_TILE

    def issue(g8, j):
        for k in range(TOP_K):
            d = dest_ref[(base + g8) * TOP_K + (j * TOP_K + k)]
            rows = ybuf.at[k, pl.ds(g8, SUBLANES)]
            pltpu.make_async_copy(ys_ref.at[pl.ds(d, 1)], rows.at[pl.ds(j, 1)], sem).start(priority=k)

    _for_each_row(TOK_TILE, issue)
    for k in range(TOP_K):
        pltpu.make_async_copy(ys_ref.at[pl.ds(0, TOK_TILE)], ybuf.at[k], sem).wait()
    rt = route_ref[...]
    y = rt[:, 2:3] * ybuf[0] + rt[:, 3:4] * ybuf[1]
    o_ref[...] = x_ref[...] + mod_ref[0][5:6] * y


def _combine(dest_flat, xn, route, modl, ys, dims):
    n_tok, d = xn.shape
    n_batch, seq = dims
    tpb = seq // TOK_TILE
    return pl.pallas_call(
        _combine_kernel,
        grid_spec=pltpu.PrefetchScalarGridSpec(
            num_scalar_prefetch=1,
            grid=(n_tok // TOK_TILE,),
            in_specs=[pl.BlockSpec((TOK_TILE, d), lambda i, dest: (i, 0)),
                      pl.BlockSpec((TOK_TILE, LANES), lambda i, dest: (i, 0)),
                      pl.BlockSpec((1, SUBLANES, d), lambda i, dest: (jnp.minimum(i // tpb, n_batch), 0, 0)),
                      pl.BlockSpec(memory_space=pl.ANY)],
            out_specs=pl.BlockSpec((TOK_TILE, d), lambda i, dest: (i, 0)),
            scratch_shapes=[pltpu.VMEM((TOP_K, TOK_TILE, d), F32), pltpu.SemaphoreType.DMA(())]),
        out_shape=jax.ShapeDtypeStruct((n_tok, d), F32),
        compiler_params=_cparams("arbitrary"),
        name="moe_combine",
    )(dest_flat, xn, route, modl, ys)


def _moe(xn, tok, route, modl, w1, w3, w2, layer, dims):
    n_tok = tok.shape[0]
    n = n_tok * TOP_K
    n_blocks = (n + N_EXPERTS * (MOE_BLOCK - 1) + MOE_BLOCK - 1) // MOE_BLOCK
    dest, counts = _slots(route)
    dest_flat = dest[:, :TOP_K].reshape(-1)
    cnt = counts[0, :N_EXPERTS].astype(jnp.int32)
    start = counts[1, :N_EXPERTS].astype(jnp.int32)
    blocks_per_e = (cnt + MOE_BLOCK - 1) // MOE_BLOCK
    blk_end = (start + blocks_per_e * MOE_BLOCK) // MOE_BLOCK
    n_used = blk_end[-1:]
    blk = jnp.minimum(jnp.arange(n_blocks, dtype=jnp.int32), n_used[0] - 1)
    block_e = jnp.sum((blk[:, None] >= blk_end[None, :]).astype(jnp.int32), axis=1)
    block_e = jnp.minimum(block_e, N_EXPERTS - 1)
    xs = _dispatch(dest_flat, start + cnt, blk_end * MOE_BLOCK, tok, n_blocks * MOE_BLOCK)
    ys = _experts(block_e, n_used, xs, w1, w3, w2, layer)
    return _combine(dest_flat, xn, route, modl, ys, dims)


def _rope_tables(seq):
    rows = seq // GRID_W
    row = jnp.repeat(jnp.arange(rows), GRID_W).astype(F32)
    col = jnp.tile(jnp.arange(GRID_W), rows).astype(F32)
    inv_freq = ROPE_BASE ** (-jnp.arange(ROPE_PAIRS, dtype=F32) / ROPE_PAIRS)
    ar, ac = row[:, None] * inv_freq, col[:, None] * inv_freq
    cos = jnp.concatenate([jnp.cos(ar), jnp.cos(ar), jnp.cos(ac), jnp.cos(ac)], axis=-1)
    sin = jnp.concatenate([-jnp.sin(ar), jnp.sin(ar), -jnp.sin(ac), jnp.sin(ac)], axis=-1)
    reps = LANES // HEAD_DIM
    cos, sin = jnp.tile(cos, (1, reps)), jnp.tile(sin, (1, reps))
    cos = jnp.concatenate([cos, jnp.ones((SEQ_TILE, LANES), F32)], axis=0)
    sin = jnp.concatenate([sin, jnp.zeros((SEQ_TILE, LANES), F32)], axis=0)
    return cos, sin


def _head_mean_matrix():
    r = jnp.arange(LANES)
    return jnp.where((r[:, None] // HEAD_DIM) == (r[None, :] // HEAD_DIM), 1.0 / HEAD_DIM, 0.0).astype(F32)


def _block_diag_chunks(w, chunk):
    dirs, nblk, bw, _ = w.shape
    per = chunk // bw
    w = w.reshape(dirs, nblk // per, per, bw, bw)
    eye = jnp.eye(per, dtype=w.dtype)
    out = jnp.einsum("dcpij,pq->dcpiqj", w, eye)
    return out.reshape(dirs, nblk // per, chunk, chunk).astype(BF16)


def _router_matrix(gw, gb, ew, eb):
    d = gw.shape[0]
    pad = LANES - N_EXPERTS - N_GROUPS
    wr = jnp.concatenate([ew, gw, jnp.zeros((d, pad), F32)], axis=1)
    br = jnp.concatenate([eb, gb, jnp.zeros((pad,), F32)])[None, :]
    return wr, br


def kernel(x, c, ctx, c_ctx, ada_w, ada_b, norm_mix, norm_ffn, out_norm, w_out, w_in_ab, q_norm, k_norm, attn_sink, sconv_w, sconv_b, w_in_cd, pool_w, pool_scale, lru_conv_w, lru_conv_b, lru_wa, lru_ba, lru_wx, lru_bx, lru_lambda, router_gw, router_gb, router_ew, router_eb, exp_w1, exp_w3, exp_w2):
    n_batch, seq, d = x.shape
    ctx_len = ctx.shape[1]
    depth = ada_w.shape[0]
    assert ctx_len == SEQ_TILE and seq % TOK_TILE == 0 and seq % GRID_W == 0
    assert depth == 2, "context-side odd-layer outputs are not implemented"
    dims = (n_batch, seq)
    n_lat = n_batch * seq

    cc = jnp.concatenate([c, c_ctx[None, :], jnp.zeros((SUBLANES - n_batch - 1, d), F32)], axis=0)
    mods = _ada(cc, ada_w, ada_b)
    mods = mods[:, :n_batch + 1].reshape(depth, n_batch + 1, N_MOD, d)
    mods = jnp.pad(mods, ((0, 0), (0, 0), (0, SUBLANES - N_MOD), (0, 0)))

    xparts = (x.reshape(n_lat, d), ctx.reshape(n_batch * ctx_len, d))
    cos_t, sin_t = _rope_tables(seq)
    pm = _head_mean_matrix()

    for i in range(depth):
        need_ctx = i < depth - 1
        j = i // 2
        modl = mods[i]
        n_out = n_lat + n_batch * ctx_len if need_ctx else n_lat
        wr, br = _router_matrix(router_gw[i], router_gb[i], router_ew[i], router_eb[i])
        wo_bf = w_out[i].astype(BF16)
        nm, nf, on = norm_mix[i][None, :], norm_ffn[i][None, :], out_norm[i][None, :]
        if i % 2 == 0:
            reps = LANES // HEAD_DIM
            q, kvx, gb, u = _inproj_even(xparts, modl, nm, w_in_ab[j].astype(BF16),
                                          jnp.tile(q_norm[j], reps)[None, :], jnp.tile(k_norm[j], reps)[None, :],
                                          cos_t, sin_t, pm, dims)
            att = _attention(q, kvx, attn_sink[j], dims, need_ctx)
            xn, tok, route = _merge_even(att, gb, u, xparts, modl, sconv_w[j], sconv_b[j][None, :], on, wo_bf,
                                         nf, wr, br, dims, n_out)
        else:
            ux, gg, up = _inproj_odd(xparts, modl, nm, w_in_cd[j].astype(BF16), dims)
            chunk = 2 * LANES
            hf, hr = _lru(ux, lru_conv_w[j], lru_conv_b[j][None, :],
                          _block_diag_chunks(lru_wa[j], chunk), _block_diag_chunks(lru_wx[j], chunk),
                          lru_ba[j][:, None, :], lru_bx[j][:, None, :], lru_lambda[j][:, None, :], dims)
            xn, tok, route = _merge_odd(up, hf, hr, gg, xparts[0], modl, pool_w[j].astype(BF16),
                                        pool_scale[j][None, :], on, wo_bf, nf, wr, br, dims)
        xparts = (_moe(xn, tok, route, modl, exp_w1, exp_w3, exp_w2, i, dims),)
    return xparts[0][:n_lat].reshape(n_batch, seq, d)
```

```python
import functools

import jax
import jax.numpy as jnp
from jax import lax
from jax.experimental import pallas as pl
from jax.experimental.pallas import tpu as pltpu

F32 = jnp.float32
BF16 = jnp.bfloat16
HIGHEST = lax.Precision.HIGHEST
LOG2E = 1.4426950408889634

GRID_W = 64
EPS = 1e-6
HEAD_DIM = 64
N_Q_HEADS = 8
N_KV_HEADS = 2
Q_PER_KV = N_Q_HEADS // N_KV_HEADS
WINDOW = 128
ATT_BLOCK = 128
ROPE_PAIRS = HEAD_DIM // 4
ROPE_BASE = 10000.0
POOL_WINDOWS = (2, 4, 8, 16)
LRU_C = 8.0
N_GROUPS = 4
EXPERTS_PER_GROUP = 8
N_EXPERTS = N_GROUPS * EXPERTS_PER_GROUP
TOP_K = 2
MOE_BLOCK = 256
N_MOD = 6

SUBLANES = 8
LANES = 128
SEQ_TILE = 256
TOK_TILE = 512
NEG = -1e30
VMEM_LIMIT = 48 * 1024 * 1024


def _cparams(*sem):
    return pltpu.CompilerParams(dimension_semantics=sem, vmem_limit_bytes=VMEM_LIMIT)


def _rms(x, g):
    ms = jnp.mean(x * x, axis=-1, keepdims=True)
    return x * lax.rsqrt(ms + EPS) * g


def _modulate(x, g, shift, scale):
    return _rms(x, g) * (1.0 + scale) + shift


def _bdot(a, b):
    return jnp.dot(a.astype(BF16), b, preferred_element_type=F32)


def _silu(x):
    return x * jax.nn.sigmoid(x)


def _ada_kernel(c_ref, w_ref, b_ref, o_ref):
    s = _silu(c_ref[...])
    o_ref[0] = jnp.dot(s, w_ref[0], precision=HIGHEST, preferred_element_type=F32) + b_ref[0]


def _ada(cc, ada_w, ada_b):
    depth, d, n = ada_w.shape
    tn = 1536
    return pl.pallas_call(
        _ada_kernel,
        grid=(depth, n // tn),
        in_specs=[pl.BlockSpec((SUBLANES, d), lambda l, j: (0, 0)),
                  pl.BlockSpec((1, d, tn), lambda l, j: (l, 0, j)),
                  pl.BlockSpec((1, 1, tn), lambda l, j: (l, 0, j))],
        out_specs=pl.BlockSpec((1, SUBLANES, tn), lambda l, j: (l, 0, j)),
        out_shape=jax.ShapeDtypeStruct((depth, SUBLANES, n), F32),
        compiler_params=_cparams("arbitrary", "arbitrary"),
        name="ada",
    )(cc, ada_w, ada_b.reshape(depth, 1, n))


def _head_norm_rope(t, pm, g, cos, sin):
    ms = jnp.dot(t * t, pm, precision=HIGHEST, preferred_element_type=F32)
    t = t * lax.rsqrt(ms + EPS) * g
    lane = lax.broadcasted_iota(jnp.int32, t.shape, 1)
    first_half = (lane % (2 * ROPE_PAIRS)) < ROPE_PAIRS
    partner = jnp.where(first_half, pltpu.roll(t, LANES - ROPE_PAIRS, 1), pltpu.roll(t, ROPE_PAIRS, 1))
    return t * cos + partner * sin


def _token_operands(xparts, n_lat_tiles):
    d = xparts[0].shape[1]
    lat = pl.BlockSpec((SEQ_TILE, d), lambda j: (jnp.minimum(j, n_lat_tiles - 1), 0))
    if len(xparts) == 2:
        ctx = pl.BlockSpec((SEQ_TILE, d), lambda j: (jnp.maximum(j - n_lat_tiles, 0), 0))
        return list(xparts), [lat, ctx]
    ctx = pl.BlockSpec((SEQ_TILE, d), lambda j: (jnp.maximum(j, n_lat_tiles), 0))
    return [xparts[0], xparts[0]], [lat, ctx]


def _pick_tokens(xl_ref, xc_ref, n_lat_tiles):
    return jnp.where(pl.program_id(0) >= n_lat_tiles, xc_ref[...], xl_ref[...])


def _inproj_even_kernel(xl_ref, xc_ref, mod_ref, g_ref, w_ref, qg_ref, kg_ref, cos_ref, sin_ref, pm_ref,
                        q_ref, kvx_ref, gb_ref, u_ref, *, n_lat_tiles):
    mod = mod_ref[0]
    x = _pick_tokens(xl_ref, xc_ref, n_lat_tiles)
    h = _modulate(x, g_ref[...], mod[0:1], mod[1:2]).astype(BF16)
    cos, sin, pm = cos_ref[...], sin_ref[...], pm_ref[...]
    kv_w = N_KV_HEADS * HEAD_DIM
    att_w = N_Q_HEADS * HEAD_DIM
    sc_w = gb_ref.shape[1]
    kv = jnp.dot(h, w_ref[:, 0:2 * kv_w], preferred_element_type=F32)
    k = _head_norm_rope(kv[:, :kv_w], pm, kg_ref[...], cos, sin)
    v = kv[:, kv_w:]
    kvx_ref[...] = jnp.concatenate([k, pltpu.roll(k, HEAD_DIM, 1), v, pltpu.roll(v, HEAD_DIM, 1)],
                                   axis=-1).astype(BF16)
    q = jnp.dot(h, w_ref[:, 2 * kv_w:2 * kv_w + att_w], preferred_element_type=F32)
    qscale = HEAD_DIM ** -0.5 * LOG2E
    for c in range(att_w // LANES):
        qc = _head_norm_rope(q[:, c * LANES:(c + 1) * LANES], pm, qg_ref[...], cos, sin)
        q_ref[:, c * LANES:(c + 1) * LANES] = (qc * qscale).astype(BF16)
    o = 2 * kv_w + att_w
    gb_ref[...] = jnp.dot(h, w_ref[:, o:o + sc_w], preferred_element_type=F32)
    gc = jnp.dot(h, w_ref[:, o + sc_w:o + 2 * sc_w], preferred_element_type=F32)
    gx = jnp.dot(h, w_ref[:, o + 2 * sc_w:o + 3 * sc_w], preferred_element_type=F32)
    u_ref[...] = gc * gx


def _tile_maps(n_lat_tiles, tiles_per_seq, n_batch):
    def mod_map(j):
        return (jnp.minimum(j // tiles_per_seq, n_batch), 0, 0)

    def pos_map(j):
        return (jnp.where(j < n_lat_tiles, j % tiles_per_seq, tiles_per_seq), 0)

    return mod_map, pos_map


def _inproj_even(xparts, modl, g, w_bf, q_g, k_g, cos_t, sin_t, pm, dims):
    n_tok = sum(p.shape[0] for p in xparts)
    d = xparts[0].shape[1]
    n_batch, seq = dims
    tps = seq // SEQ_TILE
    mod_map, pos_map = _tile_maps(n_batch * tps, tps, n_batch)
    xs, xspecs = _token_operands(xparts, n_batch * tps)
    kv_w, att_w = N_KV_HEADS * HEAD_DIM, N_Q_HEADS * HEAD_DIM
    sc_w = (w_bf.shape[1] - 2 * kv_w - att_w) // 3
    row = lambda j: (j, 0)
    const = lambda j: (0, 0)
    return pl.pallas_call(
        functools.partial(_inproj_even_kernel, n_lat_tiles=n_batch * tps),
        grid=(n_tok // SEQ_TILE,),
        in_specs=xspecs + [
                  pl.BlockSpec((1, SUBLANES, d), mod_map),
                  pl.BlockSpec((1, d), const),
                  pl.BlockSpec(w_bf.shape, const),
                  pl.BlockSpec((1, LANES), const),
                  pl.BlockSpec((1, LANES), const),
                  pl.BlockSpec((SEQ_TILE, LANES), pos_map),
                  pl.BlockSpec((SEQ_TILE, LANES), pos_map),
                  pl.BlockSpec((LANES, LANES), const)],
        out_specs=[pl.BlockSpec((SEQ_TILE, att_w), row),
                   pl.BlockSpec((SEQ_TILE, 4 * kv_w), row),
                   pl.BlockSpec((SEQ_TILE, sc_w), row),
                   pl.BlockSpec((SEQ_TILE, sc_w), row)],
        out_shape=[jax.ShapeDtypeStruct((n_tok, att_w), BF16),
                   jax.ShapeDtypeStruct((n_tok, 4 * kv_w), BF16),
                   jax.ShapeDtypeStruct((n_tok, sc_w), F32),
                   jax.ShapeDtypeStruct((n_tok, sc_w), F32)],
        compiler_params=_cparams("arbitrary"),
        name="inproj_even",
    )(*xs, modl, g, w_bf, q_g, k_g, cos_t, sin_t, pm)


def _attn_heads(q, kvx, mask_fn, sink_ref, o_ref):
    m, n = q.shape[0], kvx.shape[0]
    k_nat, k_swp = kvx[:, 0:LANES], kvx[:, LANES:2 * LANES]
    v_nat, v_swp = kvx[:, 2 * LANES:3 * LANES], kvx[:, 3 * LANES:4 * LANES]
    lo_n = lax.broadcasted_iota(jnp.int32, (n, LANES), 1) < HEAD_DIM
    lo_m = lax.broadcasted_iota(jnp.int32, (2 * m, LANES), 1) < HEAD_DIM
    top = lax.broadcasted_iota(jnp.int32, (2 * m, 1), 0) < m
    zero, one = jnp.zeros_like(k_nat), jnp.ones_like(v_nat)
    outs = []
    for h in range(N_KV_HEADS):
        k_lo, k_hi = (k_nat, k_swp) if h == 0 else (k_swp, k_nat)
        v_lo, v_hi = (v_nat, v_swp) if h == 0 else (v_swp, v_nat)
        g0 = 2 * h
        q2 = jnp.concatenate([q[:, g0 * LANES:(g0 + 1) * LANES],
                              q[:, (g0 + 1) * LANES:(g0 + 2) * LANES]], axis=0)
        parts = []
        for par in range(2):
            kz = jnp.where(lo_n, k_lo, zero) if par == 0 else jnp.where(lo_n, zero, k_hi)
            vz = jnp.where(lo_n, v_lo, one) if par == 0 else jnp.where(lo_n, one, v_hi)
            s = lax.dot_general(q2, kz, (((1,), (1,)), ((), ())), preferred_element_type=F32)
            s = mask_fn(s)
            hq = Q_PER_KV * h + par
            snk = jnp.where(top, sink_ref[hq], sink_ref[hq + 2]) * LOG2E
            mx = jnp.maximum(jnp.max(s, axis=-1, keepdims=True), snk)
            p = jnp.exp2(s - mx)
            parts.append((jnp.dot(p.astype(BF16), vz, preferred_element_type=F32), jnp.exp2(snk - mx)))
        (o_lo, e_lo), (o_hi, e_hi) = parts
        num = jnp.where(lo_m, o_lo, o_hi)
        den = pltpu.roll(jnp.where(lo_m, o_hi, o_lo), HEAD_DIM, 1) + jnp.where(lo_m, e_lo, e_hi)
        o = num / den
        outs += [o[0:m], o[m:2 * m]]
    o_ref[...] = jnp.concatenate(outs, axis=-1)


def _attn_lat_kernel(sink_ref, q_ref, kp_ref, kc_ref, kn_ref, kx_ref, o_ref):
    i = pl.program_id(1)
    nb = pl.num_programs(1)
    kvx = jnp.concatenate([kp_ref[...], kc_ref[...], kn_ref[...], kx_ref[...]], axis=0)
    m = ATT_BLOCK
    tok = lax.broadcasted_iota(jnp.int32, (2 * m, m), 0) % m
    col = lax.broadcasted_iota(jnp.int32, (2 * m, m), 1)
    ok_prev = (col >= tok) & (i > 0)
    ok_next = (col <= tok) & (i < nb - 1)

    def mask_fn(s):
        return jnp.concatenate([jnp.where(ok_prev, s[:, 0:m], NEG), s[:, m:2 * m],
                                jnp.where(ok_next, s[:, 2 * m:3 * m], NEG), s[:, 3 * m:]], axis=1)

    _attn_heads(q_ref[...], kvx, mask_fn, sink_ref, o_ref)


def _attn_ctx_kernel(sink_ref, att_in_ref, q_ref, kx_ref, o_ref):
    del att_in_ref
    _attn_heads(q_ref[...], kx_ref[...], lambda s: s, sink_ref, o_ref)


def _attention(q, kvx, sink, dims, need_ctx):
    n_tok = q.shape[0]
    n_batch, seq = dims
    nb = seq // ATT_BLOCK
    n_lat = n_batch * seq
    ctx_len = (n_tok - n_lat) // n_batch
    att_w, kvx_w = q.shape[1], kvx.shape[1]
    ctx_blk0 = n_lat // ctx_len
    cur = lambda b, i: (b * nb + i, 0)
    prev = lambda b, i: (b * nb + jnp.maximum(i - 1, 0), 0)
    nxt = lambda b, i: (b * nb + jnp.minimum(i + 1, nb - 1), 0)
    cx = lambda b, i: (ctx_blk0 + b, 0)
    smem = pl.BlockSpec(memory_space=pltpu.SMEM)
    kvb = lambda f: pl.BlockSpec((ATT_BLOCK, kvx_w), f)
    att = pl.pallas_call(
        _attn_lat_kernel,
        grid=(n_batch, nb),
        in_specs=[smem, pl.BlockSpec((ATT_BLOCK, att_w), cur),
                  kvb(prev), kvb(cur), kvb(nxt), pl.BlockSpec((ctx_len, kvx_w), cx)],
        out_specs=pl.BlockSpec((ATT_BLOCK, att_w), cur),
        out_shape=jax.ShapeDtypeStruct((n_tok, att_w), F32),
        compiler_params=_cparams("arbitrary", "arbitrary"),
        name="attn_latent",
    )(sink, q, kvx, kvx, kvx, kvx)
    if not need_ctx:
        return att
    cxb = lambda b: (ctx_blk0 + b, 0)
    return pl.pallas_call(
        _attn_ctx_kernel,
        grid=(n_batch,),
        in_specs=[smem, pl.BlockSpec(memory_space=pl.ANY),
                  pl.BlockSpec((ctx_len, att_w), cxb), pl.BlockSpec((ctx_len, kvx_w), cxb)],
        out_specs=pl.BlockSpec((ctx_len, att_w), cxb),
        out_shape=jax.ShapeDtypeStruct((n_tok, att_w), F32),
        input_output_aliases={1: 0},
        compiler_params=_cparams("arbitrary"),
        name="attn_context",
    )(sink, att, q, kvx)


def _route(logits):
    lane = lax.broadcasted_iota(jnp.int32, logits.shape, 1).astype(F32)
    big = float(LANES)
    gl = jnp.where((lane >= N_EXPERTS) & (lane < N_EXPERTS + N_GROUPS), logits, NEG)
    gmax = jnp.max(gl, axis=-1, keepdims=True)
    gidx = jnp.min(jnp.where(gl == gmax, lane, big), axis=-1, keepdims=True) - N_EXPERTS
    p_sel = 1.0 / jnp.sum(jnp.exp(gl - gmax), axis=-1, keepdims=True)
    lo = gidx * EXPERTS_PER_GROUP
    el = jnp.where((lane >= lo) & (lane < lo + EXPERTS_PER_GROUP), logits, NEG)
    m1 = jnp.max(el, axis=-1, keepdims=True)
    i1 = jnp.min(jnp.where(el == m1, lane, big), axis=-1, keepdims=True)
    el2 = jnp.where(lane == i1, NEG, el)
    m2 = jnp.max(el2, axis=-1, keepdims=True)
    i2 = jnp.min(jnp.where(el2 == m2, lane, big), axis=-1, keepdims=True)
    t = jnp.exp(m2 - m1)
    w1 = p_sel / (1.0 + t)
    w2 = p_sel * t / (1.0 + t)
    return jnp.where(lane == 0, i1, jnp.where(lane == 1, i2, jnp.where(lane == 2, w1,
                     jnp.where(lane == 3, w2, 0.0))))


def _tail(n1, n2, gain, wo_ref, x, mod, nf_ref, wr_ref, br_ref, xo_ref, tok_ref, route_ref, cnt_ref):
    half = n1.shape[1]
    y = (_bdot(_rms(n1, gain[:, :half]), wo_ref[0:half, :])
         + _bdot(_rms(n2, gain[:, half:]), wo_ref[half:, :]))
    xn = x + mod[2:3] * y
    xo_ref[...] = xn
    tok = _modulate(xn, nf_ref[...], mod[3:4], mod[4:5])
    tok_ref[...] = tok
    t_hi = tok.astype(BF16)
    t_lo = (tok - t_hi.astype(F32)).astype(BF16)
    hh = jnp.dot(t_hi, wr_ref[...], preferred_element_type=F32)
    logits = (hh[:, :LANES] + hh[:, LANES:] + br_ref[...]
              + jnp.dot(t_lo, wr_ref[:, 0:LANES], preferred_element_type=F32))
    route = _route(logits)
    route_ref[...] = route
    lane = lax.broadcasted_iota(jnp.int32, route.shape, 1).astype(F32)
    hit = jnp.where((lane == route[:, 0:1]) | (lane == route[:, 1:2]), 1.0, 0.0)

    @pl.when(pl.program_id(0) == 0)
    def _():
        cnt_ref[...] = jnp.zeros_like(cnt_ref)

    cnt_ref[0:1, :] += jnp.sum(hit, axis=0, keepdims=True)


def _seq_flags(j, n_lat_tiles, tiles_per_seq):
    is_ctx = j >= n_lat_tiles
    first = is_ctx | (j % tiles_per_seq == 0)
    last = is_ctx | (j % tiles_per_seq == tiles_per_seq - 1)
    return first, last


def _with_halo(prev_ref, tile, next_ref, first, last):
    p = jnp.where(first, 0.0, prev_ref[...])
    n = jnp.where(last, 0.0, next_ref[...])
    return jnp.concatenate([p, tile, n], axis=0)


def _shift_rows(e, k):
    return pltpu.roll(e, (-k) % e.shape[0], 0)


def _merge_even_kernel(att_ref, gb_ref, u_ref, up_ref, un_ref, xl_ref, xc_ref, mod_ref, cw_ref, cb_ref,
                       on_ref, wo_ref, nf_ref, wr_ref, br_ref,
                       xo_ref, tok_ref, route_ref, cnt_ref, *, n_lat_tiles, tiles_per_seq):
    j = pl.program_id(0)
    first, last = _seq_flags(j, n_lat_tiles, tiles_per_seq)
    e = _with_halo(up_ref, u_ref[...], un_ref, first, last)
    cw = cw_ref[...]
    lo, hi = SUBLANES, SUBLANES + SEQ_TILE
    conv = (cw[0:1] * _shift_rows(e, -1)[lo:hi] + cw[1:2] * e[lo:hi]
            + cw[2:3] * _shift_rows(e, 1)[lo:hi] + cb_ref[...])
    conv = gb_ref[...] * conv
    x = _pick_tokens(xl_ref, xc_ref, n_lat_tiles)
    _tail(att_ref[...], conv, on_ref[...], wo_ref, x, mod_ref[0], nf_ref, wr_ref, br_ref,
          xo_ref, tok_ref, route_ref, cnt_ref)


def _halo_specs(width, n_rows):
    r = SEQ_TILE // SUBLANES
    n_blk = n_rows // SUBLANES
    prev = pl.BlockSpec((SUBLANES, width), lambda j: (jnp.maximum(j * r - 1, 0), 0))
    nxt = pl.BlockSpec((SUBLANES, width), lambda j: (jnp.minimum((j + 1) * r, n_blk - 1), 0))
    return prev, nxt


def _tail_specs(d, n_out_tok):
    row = lambda j: (j, 0)
    out_specs = [pl.BlockSpec((SEQ_TILE, d), row), pl.BlockSpec((SEQ_TILE, d), row),
                 pl.BlockSpec((SEQ_TILE, LANES), row), pl.BlockSpec((SUBLANES, LANES), lambda j: (0, 0))]
    out_shape = [jax.ShapeDtypeStruct((n_out_tok, d), F32), jax.ShapeDtypeStruct((n_out_tok, d), F32),
                 jax.ShapeDtypeStruct((n_out_tok, LANES), F32), jax.ShapeDtypeStruct((SUBLANES, LANES), F32)]
    return out_specs, out_shape


def _merge_even(att, gb, u, xparts, modl, cw, cb, on, wo_bf, nf, wr, br, dims, n_out_tok):
    n_tok = u.shape[0]
    d = xparts[0].shape[1]
    n_batch, seq = dims
    tps = seq // SEQ_TILE
    mod_map, _ = _tile_maps(n_batch * tps, tps, n_batch)
    xs, xspecs = _token_operands(xparts, n_batch * tps)
    sc_w = u.shape[1]
    row = lambda j: (j, 0)
    const = lambda j: (0, 0)
    prev, nxt = _halo_specs(sc_w, n_tok)
    out_specs, out_shape = _tail_specs(d, n_out_tok)
    return pl.pallas_call(
        functools.partial(_merge_even_kernel, n_lat_tiles=n_batch * tps, tiles_per_seq=tps),
        grid=(n_out_tok // SEQ_TILE,),
        in_specs=[pl.BlockSpec((SEQ_TILE, att.shape[1]), row),
                  pl.BlockSpec((SEQ_TILE, sc_w), row),
                  pl.BlockSpec((SEQ_TILE, sc_w), row), prev, nxt] + xspecs + [
                  pl.BlockSpec((1, SUBLANES, d), mod_map),
                  pl.BlockSpec(cw.shape, const), pl.BlockSpec((1, sc_w), const),
                  pl.BlockSpec((1, d), const), pl.BlockSpec(wo_bf.shape, const),
                  pl.BlockSpec((1, d), const), pl.BlockSpec(wr.shape, const),
                  pl.BlockSpec((1, LANES), const)],
        out_specs=out_specs, out_shape=out_shape,
        compiler_params=_cparams("arbitrary"),
        name="merge_even",
    )(att, gb, u, u, u, *xs, modl, cw, cb, on, wo_bf, nf, wr, br)


def _inproj_odd_kernel(xl_ref, xc_ref, mod_ref, g_ref, w_ref, ux_ref, gg_ref, up_ref, *, n_lat_tiles):
    mod = mod_ref[0]
    x = _pick_tokens(xl_ref, xc_ref, n_lat_tiles)
    h = _modulate(x, g_ref[...], mod[0:1], mod[1:2]).astype(BF16)
    w = ux_ref.shape[1]
    ux_ref[...] = jnp.dot(h, w_ref[:, 0:w], preferred_element_type=F32)
    ug = jnp.dot(h, w_ref[:, w:2 * w], preferred_element_type=F32)
    gg_ref[...] = jax.nn.gelu(ug, approximate=True)
    up_ref[...] = jnp.dot(h, w_ref[:, 2 * w:3 * w], preferred_element_type=F32)


def _inproj_odd(xparts, modl, g, w_bf, dims):
    n_tok = sum(p.shape[0] for p in xparts)
    d = xparts[0].shape[1]
    n_batch, seq = dims
    tps = seq // SEQ_TILE
    mod_map, _ = _tile_maps(n_batch * tps, tps, n_batch)
    xs, xspecs = _token_operands(xparts, n_batch * tps)
    w = w_bf.shape[1] // 3
    row = lambda j: (j, 0)
    const = lambda j: (0, 0)
    return pl.pallas_call(
        functools.partial(_inproj_odd_kernel, n_lat_tiles=n_batch * tps),
        grid=(n_tok // SEQ_TILE,),
        in_specs=xspecs + [
                  pl.BlockSpec((1, SUBLANES, d), mod_map),
                  pl.BlockSpec((1, d), const),
                  pl.BlockSpec(w_bf.shape, const)],
        out_specs=[pl.BlockSpec((SEQ_TILE, w), row)] * 3,
        out_shape=[jax.ShapeDtypeStruct((n_tok, w), F32)] * 3,
        compiler_params=_cparams("arbitrary"),
        name="inproj_odd",
    )(*xs, modl, g, w_bf)


def _lru_conv(e, cw, cb):
    lo, hi = SUBLANES, SUBLANES + SEQ_TILE
    return (cw[0:1] * _shift_rows(e, -2)[lo:hi] + cw[1:2] * _shift_rows(e, -1)[lo:hi]
            + cw[2:3] * e[lo:hi] + cw[3:4] * _shift_rows(e, 1)[lo:hi] + cb)


def _lru_coeffs(xl, d, wa_ref, wx_ref, ba_ref, bx_ref, lam_ref, a_ref, b_ref):
    xb = xl.astype(BF16)
    chunk = wa_ref.shape[-1]
    n_chunk = xl.shape[1] // chunk
    pre_a = jnp.concatenate([jnp.dot(xb[:, c * chunk:(c + 1) * chunk], wa_ref[d, c],
                                     preferred_element_type=F32) for c in range(n_chunk)], axis=-1)
    pre_x = jnp.concatenate([jnp.dot(xb[:, c * chunk:(c + 1) * chunk], wx_ref[d, c],
                                     preferred_element_type=F32) for c in range(n_chunk)], axis=-1)
    r = 0.5 + 0.5 * jnp.tanh(0.5 * (pre_a + ba_ref[d]))
    gi = 0.5 + 0.5 * jnp.tanh(0.5 * (pre_x + bx_ref[d]))
    log_a = (-LRU_C * jax.nn.softplus(-lam_ref[d])) * r
    a_ref[...] = jnp.exp(log_a)
    th = jnp.tanh(log_a)
    b_ref[...] = jnp.sqrt(-2.0 * th / (1.0 - th)) * gi * xl


def _scan_tile(a_ref, b_ref, h_ref, carry, reverse):
    n_grp = a_ref.shape[0] // SUBLANES
    row = lax.broadcasted_iota(jnp.int32, (SUBLANES, a_ref.shape[1]), 0)

    def body(g, carry):
        gi = (n_grp - 1 - g) if reverse else g
        r0 = pl.multiple_of(gi * SUBLANES, SUBLANES)
        a8 = a_ref[pl.ds(r0, SUBLANES), :]
        b8 = b_ref[pl.ds(r0, SUBLANES), :]
        for s in (1, 2, 4):
            if reverse:
                ok = row < SUBLANES - s
                sh = SUBLANES - s
            else:
                ok = row >= s
                sh = s
            a_sh = jnp.where(ok, pltpu.roll(a8, sh, 0), 1.0)
            b_sh = jnp.where(ok, pltpu.roll(b8, sh, 0), 0.0)
            b8 = a8 * b_sh + b8
            a8 = a8 * a_sh
        h8 = a8 * carry + b8
        if h_ref is not None:
            h_ref[pl.ds(r0, SUBLANES), :] = h8
        return h8[0:1] if reverse else h8[SUBLANES - 1:SUBLANES]

    return lax.fori_loop(0, n_grp, body, carry)


def _lru_kernel(uf_ref, ufp_ref, ufn_ref, ur_ref, urp_ref, urn_ref, uc_ref, cw_ref, cb_ref,
                wa_ref, wx_ref, ba_ref, bx_ref, lam_ref, hf_ref, hr_ref,
                cf_ref, cr_ref, a_ref, b_ref):
    i = pl.program_id(1)
    nt = pl.num_programs(1)
    cw, cb = cw_ref[...], cb_ref[...]
    coeffs = functools.partial(_lru_coeffs, wa_ref=wa_ref, wx_ref=wx_ref, ba_ref=ba_ref,
                               bx_ref=bx_ref, lam_ref=lam_ref, a_ref=a_ref, b_ref=b_ref)

    @pl.when(i == 0)
    def _():
        z = jnp.zeros((SUBLANES, uc_ref.shape[1]), F32)
        xc = _lru_conv(jnp.concatenate([z, uc_ref[...], z], axis=0), cw, cb)
        zero = jnp.zeros((1, uc_ref.shape[1]), F32)
        coeffs(xc, 0)
        cf_ref[...] = _scan_tile(a_ref, b_ref, None, zero, False)
        coeffs(xc, 1)
        cr_ref[...] = _scan_tile(a_ref, b_ref, None, zero, True)

    xf = _lru_conv(_with_halo(ufp_ref, uf_ref[...], ufn_ref, i == 0, i == nt - 1), cw, cb)
    coeffs(xf, 0)
    cf_ref[...] = _scan_tile(a_ref, b_ref, hf_ref, cf_ref[...], False)
    xr = _lru_conv(_with_halo(urp_ref, ur_ref[...], urn_ref, i == nt - 1, i == 0), cw, cb)
    coeffs(xr, 1)
    cr_ref[...] = _scan_tile(a_ref, b_ref, hr_ref, cr_ref[...], True)


def _lru(ux, cw, cb, wa_bd, wx_bd, ba, bx, lam, dims):
    n_tok, w = ux.shape
    n_batch, seq = dims
    nt = seq // SEQ_TILE
    n_lat = n_batch * seq
    r = SEQ_TILE // SUBLANES
    n_blk = n_tok // SUBLANES
    fwd = lambda b, i: b * nt + i
    rev = lambda b, i: b * nt + nt - 1 - i

    def specs(tile):
        return [pl.BlockSpec((SEQ_TILE, w), lambda b, i: (tile(b, i), 0)),
                pl.BlockSpec((SUBLANES, w), lambda b, i: (jnp.maximum(tile(b, i) * r - 1, 0), 0)),
                pl.BlockSpec((SUBLANES, w), lambda b, i: (jnp.minimum((tile(b, i) + 1) * r, n_blk - 1), 0))]

    const2 = lambda b, i: (0, 0)
    const3 = lambda b, i: (0, 0, 0)
    const4 = lambda b, i: (0, 0, 0, 0)
    return pl.pallas_call(
        _lru_kernel,
        grid=(n_batch, nt),
        in_specs=specs(fwd) + specs(rev) + [
            pl.BlockSpec((SEQ_TILE, w), lambda b, i: (n_lat // SEQ_TILE + b, 0)),
            pl.BlockSpec(cw.shape, const2), pl.BlockSpec((1, w), const2),
            pl.BlockSpec(wa_bd.shape, const4), pl.BlockSpec(wx_bd.shape, const4),
            pl.BlockSpec(ba.shape, const3), pl.BlockSpec(bx.shape, const3),
            pl.BlockSpec(lam.shape, const3)],
        out_specs=[pl.BlockSpec((SEQ_TILE, w), lambda b, i: (fwd(b, i), 0)),
                   pl.BlockSpec((SEQ_TILE, w), lambda b, i: (rev(b, i), 0))],
        out_shape=[jax.ShapeDtypeStruct((n_lat, w), F32)] * 2,
        scratch_shapes=[pltpu.VMEM((1, w), F32), pltpu.VMEM((1, w), F32),
                        pltpu.VMEM((SEQ_TILE, w), F32), pltpu.VMEM((SEQ_TILE, w), F32)],
        compiler_params=_cparams("arbitrary", "arbitrary"),
        name="rglru",
    )(ux, ux, ux, ux, ux, ux, ux, cw, cb, wa_bd, wx_bd, ba, bx, lam)


def _merge_odd_kernel(up_ref, upp_ref, upn_ref, hf_ref, hr_ref, gg_ref, x_ref, mod_ref,
                      pw_ref, ps_ref, on_ref, wo_ref, nf_ref, wr_ref, br_ref,
                      xo_ref, tok_ref, route_ref, cnt_ref, *, tiles_per_seq, seq):
    j = pl.program_id(0)
    ts = j % tiles_per_seq
    e = _with_halo(upp_ref, up_ref[...], upn_ref, ts == 0, ts == tiles_per_seq - 1)
    gw = pw_ref.shape[-1]
    lo, hi = SUBLANES, SUBLANES + SEQ_TILE
    tpos = ts * SEQ_TILE + lax.broadcasted_iota(jnp.int32, (SEQ_TILE, 1), 0)
    parts = []
    for g, win in enumerate(POOL_WINDOWS):
        eg = e[:, g * gw:(g + 1) * gw]
        p = eg
        span = 1
        while span < win:
            p = p + _shift_rows(p, span)
            span *= 2
        half = win // 2
        s = _shift_rows(p, -half)[lo:hi]
        cnt = (jnp.minimum(tpos + half, seq) - jnp.maximum(tpos - half, 0)).astype(F32)
        dlt = s / cnt - eg[lo:hi]
        parts.append(_bdot(dlt, pw_ref[g]))
    pool = jnp.concatenate(parts, axis=-1) * ps_ref[...]
    lru = (hf_ref[...] + hr_ref[...]) * gg_ref[...]
    _tail(pool, lru, on_ref[...], wo_ref, x_ref[...], mod_ref[0], nf_ref, wr_ref, br_ref,
          xo_ref, tok_ref, route_ref, cnt_ref)


def _merge_odd(up, hf, hr, gg, xf, modl, pw_bf, ps, on, wo_bf, nf, wr, br, dims):
    d = xf.shape[1]
    n_tok, w = up.shape
    n_batch, seq = dims
    tps = seq // SEQ_TILE
    n_lat = n_batch * seq
    mod_map, _ = _tile_maps(n_batch * tps, tps, n_batch)
    row = lambda j: (j, 0)
    const = lambda j: (0, 0)
    prev, nxt = _halo_specs(w, n_tok)
    out_specs, out_shape = _tail_specs(d, n_lat)
    return pl.pallas_call(
        functools.partial(_merge_odd_kernel, tiles_per_seq=tps, seq=seq),
        grid=(n_lat // SEQ_TILE,),
        in_specs=[pl.BlockSpec((SEQ_TILE, w), row), prev, nxt,
                  pl.BlockSpec((SEQ_TILE, w), row), pl.BlockSpec((SEQ_TILE, w), row),
                  pl.BlockSpec((SEQ_TILE, w), row),
                  pl.BlockSpec((SEQ_TILE, d), row),
                  pl.BlockSpec((1, SUBLANES, d), mod_map),
                  pl.BlockSpec(pw_bf.shape, lambda j: (0, 0, 0)), pl.BlockSpec((1, w), const),
                  pl.BlockSpec((1, d), const), pl.BlockSpec(wo_bf.shape, const),
                  pl.BlockSpec((1, d), const), pl.BlockSpec(wr.shape, const),
                  pl.BlockSpec((1, LANES), const)],
        out_specs=out_specs, out_shape=out_shape,
        compiler_params=_cparams("arbitrary"),
        name="merge_odd",
    )(up, up, up, hf, hr, gg, xf, modl, pw_bf, ps, on, wo_bf, nf, wr, br)


def _slots_kernel(route_ref, tot_ref, dest_ref, cnt_ref, run_ref, start_ref):
    i = pl.program_id(0)
    rt = route_ref[...]
    m = rt.shape[0]
    lane = lax.broadcasted_iota(jnp.int32, rt.shape, 1).astype(F32)
    oh1 = lane == rt[:, 0:1]
    oh2 = lane == rt[:, 1:2]
    s = jnp.where(oh1 | oh2, 1.0, 0.0)

    @pl.when(i == 0)
    def _():
        tot = tot_ref[0:1, :]
        padded = jnp.floor((tot + (MOE_BLOCK - 1)) / MOE_BLOCK) * MOE_BLOCK
        r = lax.broadcasted_iota(jnp.int32, (LANES, LANES), 0)
        c = lax.broadcasted_iota(jnp.int32, (LANES, LANES), 1)
        upper = jnp.where(r < c, 1.0, 0.0)
        p8 = jnp.broadcast_to(padded, (SUBLANES, LANES))
        start = jnp.dot(p8, upper, precision=HIGHEST, preferred_element_type=F32)[0:1]
        start_ref[...] = start
        run_ref[...] = jnp.zeros_like(run_ref)
        sub = lax.broadcasted_iota(jnp.int32, cnt_ref.shape, 0)
        cnt_ref[...] = jnp.where(sub == 0, tot, jnp.where(sub == 1, start, 0.0))

    r = lax.broadcasted_iota(jnp.int32, (m, m), 0)
    c = lax.broadcasted_iota(jnp.int32, (m, m), 1)
    lower = jnp.where(c < r, 1.0, 0.0).astype(BF16)
    prefix = jnp.dot(lower, s.astype(BF16), preferred_element_type=F32)
    base = prefix + run_ref[...] + start_ref[...]
    d1 = jnp.sum(jnp.where(oh1, base, 0.0), axis=-1, keepdims=True)
    d2 = jnp.sum(jnp.where(oh2, base, 0.0), axis=-1, keepdims=True)
    dest_ref[...] = jnp.where(lane == 0, d1, jnp.where(lane == 1, d2, 0.0)).astype(jnp.int32)
    run_ref[...] += jnp.sum(s, axis=0, keepdims=True)


def _slots(route, totals):
    n_tok = route.shape[0]
    nt = n_tok // TOK_TILE
    return pl.pallas_call(
        _slots_kernel,
        grid=(nt,),
        in_specs=[pl.BlockSpec((TOK_TILE, LANES), lambda i: (i, 0)),
                  pl.BlockSpec((SUBLANES, LANES), lambda i: (0, 0))],
        out_specs=[pl.BlockSpec((TOK_TILE, LANES), lambda i: (i, 0)),
                   pl.BlockSpec((SUBLANES, LANES), lambda i: (0, 0))],
        out_shape=[jax.ShapeDtypeStruct((n_tok, LANES), jnp.int32),
                   jax.ShapeDtypeStruct((SUBLANES, LANES), F32)],
        scratch_shapes=[pltpu.VMEM((1, LANES), F32)] * 2,
        compiler_params=_cparams("arbitrary"),
        name="moe_slots",
    )(route, totals)


def _for_each_row(n_rows, fn):
    def body(g, c):
        g8 = pl.multiple_of(g * SUBLANES, SUBLANES)
        for j in range(SUBLANES):
            fn(g8, j)
        return c
    lax.fori_loop(0, n_rows // SUBLANES, body, 0)


def _dispatch_kernel(dest_ref, pad_lo_ref, pad_hi_ref, tok_ref, xs_ref, zrow, sem, zsem):
    i = pl.program_id(0)
    base = i * TOK_TILE

    def zero_copy(r):
        return pltpu.make_async_copy(zrow.at[pl.ds(0, 1)], xs_ref.at[pl.ds(r, 1)], zsem)

    def for_each_pad_row(fn):
        def per_expert(e, c):
            return lax.fori_loop(pad_lo_ref[e], pad_hi_ref[e], fn, c)
        lax.fori_loop(0, N_EXPERTS, per_expert, 0)

    @pl.when(i == 0)
    def _():
        zrow[...] = jnp.zeros_like(zrow)

        def fill(r, c):
            zero_copy(r).start()
            return c
        for_each_pad_row(fill)

    def issue(g8, j):
        rows = tok_ref.at[pl.ds(g8, SUBLANES)]
        for k in range(TOP_K):
            d = dest_ref[(base + g8) * TOP_K + (j * TOP_K + k)]
            pltpu.make_async_copy(rows.at[pl.ds(j, 1)], xs_ref.at[pl.ds(d, 1)], sem).start(priority=k)

    _for_each_row(TOK_TILE, issue)
    for k in range(TOP_K):
        pltpu.make_async_copy(tok_ref, xs_ref.at[pl.ds(0, TOK_TILE)], sem).wait()

    @pl.when(i == pl.num_programs(0) - 1)
    def _():
        def drain(r, c):
            zero_copy(r).wait()
            return c
        for_each_pad_row(drain)


def _dispatch(dest_flat, pad_lo, pad_hi, tok, n_slots):
    n_tok, d = tok.shape
    return pl.pallas_call(
        _dispatch_kernel,
        grid_spec=pltpu.PrefetchScalarGridSpec(
            num_scalar_prefetch=3,
            grid=(n_tok // TOK_TILE,),
            in_specs=[pl.BlockSpec((TOK_TILE, d), lambda i, *_: (i, 0))],
            out_specs=pl.BlockSpec(memory_space=pl.ANY),
            scratch_shapes=[pltpu.VMEM((SUBLANES, d), F32),
                            pltpu.SemaphoreType.DMA(()), pltpu.SemaphoreType.DMA(())]),
        out_shape=jax.ShapeDtypeStruct((n_slots, d), F32),
        compiler_params=_cparams("arbitrary"),
        name="moe_dispatch",
    )(dest_flat, pad_lo, pad_hi, tok)


def _expert_kernel(be_ref, nu_ref, x_ref, w1_ref, w3_ref, w2_ref, y_ref, w1b, w3b, w2b):
    i = pl.program_id(0)

    @pl.when(i < nu_ref[0])
    def _():
        changed = (i == 0) | (be_ref[i] != be_ref[jnp.maximum(i - 1, 0)])

        @pl.when(changed)
        def _():
            w1b[...] = w1_ref[0, 0].astype(BF16)
            w3b[...] = w3_ref[0, 0].astype(BF16)
            w2b[...] = w2_ref[0, 0].astype(BF16)

        x = x_ref[...].astype(BF16)
        a = jnp.dot(x, w1b[...], preferred_element_type=F32)
        b = jnp.dot(x, w3b[...], preferred_element_type=F32)
        y_ref[...] = _bdot(_silu(a) * b, w2b[...])


def _experts(block_e, n_used, xs, w1, w3, w2, layer):
    n_slots, d = xs.shape
    de = w1.shape[-1]
    n_blocks = n_slots // MOE_BLOCK
    blk = lambda i, be, nu: (jnp.minimum(i, nu[0] - 1), 0)
    wmap = lambda i, be, nu: (layer, be[i], 0, 0)
    return pl.pallas_call(
        _expert_kernel,
        grid_spec=pltpu.PrefetchScalarGridSpec(
            num_scalar_prefetch=2,
            grid=(n_blocks,),
            in_specs=[pl.BlockSpec((MOE_BLOCK, d), blk),
                      pl.BlockSpec((1, 1, d, de), wmap), pl.BlockSpec((1, 1, d, de), wmap),
                      pl.BlockSpec((1, 1, de, d), wmap)],
            out_specs=pl.BlockSpec((MOE_BLOCK, d), blk),
            scratch_shapes=[pltpu.VMEM((d, de), BF16), pltpu.VMEM((d, de), BF16),
                            pltpu.VMEM((de, d), BF16)]),
        out_shape=jax.ShapeDtypeStruct((n_slots, d), F32),
        compiler_params=_cparams("arbitrary"),
        name="moe_experts",
    )(block_e, n_used, xs, w1, w3, w2)


def _combine_kernel(dest_ref, x_ref, route_ref, mod_ref, ys_ref, o_ref, ybuf, sem):
    base = pl.program_id(0) * TOK_TILE

    def issue(g8, j):
        for k in range(TOP_K):
            d = dest_ref[(base + g8) * TOP_K + (j * TOP_K + k)]
            rows = ybuf.at[k, pl.ds(g8, SUBLANES)]
            pltpu.make_async_copy(ys_ref.at[pl.ds(d, 1)], rows.at[pl.ds(j, 1)], sem).start(priority=k)

    _for_each_row(TOK_TILE, issue)
    for k in range(TOP_K):
        pltpu.make_async_copy(ys_ref.at[pl.ds(0, TOK_TILE)], ybuf.at[k], sem).wait()
    rt = route_ref[...]
    y = rt[:, 2:3] * ybuf[0] + rt[:, 3:4] * ybuf[1]
    o_ref[...] = x_ref[...] + mod_ref[0][5:6] * y


def _combine(dest_flat, xn, route, modl, ys, dims):
    n_tok, d = xn.shape
    n_batch, seq = dims
    tpb = seq // TOK_TILE
    return pl.pallas_call(
        _combine_kernel,
        grid_spec=pltpu.PrefetchScalarGridSpec(
            num_scalar_prefetch=1,
            grid=(n_tok // TOK_TILE,),
            in_specs=[pl.BlockSpec((TOK_TILE, d), lambda i, dest: (i, 0)),
                      pl.BlockSpec((TOK_TILE, LANES), lambda i, dest: (i, 0)),
                      pl.BlockSpec((1, SUBLANES, d), lambda i, dest: (jnp.minimum(i // tpb, n_batch), 0, 0)),
                      pl.BlockSpec(memory_space=pl.ANY)],
            out_specs=pl.BlockSpec((TOK_TILE, d), lambda i, dest: (i, 0)),
            scratch_shapes=[pltpu.VMEM((TOP_K, TOK_TILE, d), F32), pltpu.SemaphoreType.DMA(())]),
        out_shape=jax.ShapeDtypeStruct((n_tok, d), F32),
        compiler_params=_cparams("arbitrary"),
        name="moe_combine",
    )(dest_flat, xn, route, modl, ys)


def _moe(xn, tok, route, totals, modl, w1, w3, w2, layer, dims):
    n_tok = tok.shape[0]
    n = n_tok * TOP_K
    n_blocks = (n + N_EXPERTS * (MOE_BLOCK - 1) + MOE_BLOCK - 1) // MOE_BLOCK
    dest, counts = _slots(route, totals)
    dest_flat = dest[:, :TOP_K].reshape(-1)
    cnt = counts[0, :N_EXPERTS].astype(jnp.int32)
    start = counts[1, :N_EXPERTS].astype(jnp.int32)
    blocks_per_e = (cnt + MOE_BLOCK - 1) // MOE_BLOCK
    blk_end = (start + blocks_per_e * MOE_BLOCK) // MOE_BLOCK
    n_used = blk_end[-1:]
    blk = jnp.minimum(jnp.arange(n_blocks, dtype=jnp.int32), n_used[0] - 1)
    block_e = jnp.sum((blk[:, None] >= blk_end[None, :]).astype(jnp.int32), axis=1)
    block_e = jnp.minimum(block_e, N_EXPERTS - 1)
    xs = _dispatch(dest_flat, start + cnt, blk_end * MOE_BLOCK, tok, n_blocks * MOE_BLOCK)
    ys = _experts(block_e, n_used, xs, w1, w3, w2, layer)
    return _combine(dest_flat, xn, route, modl, ys, dims)


def _rope_tables(seq):
    rows = seq // GRID_W
    row = jnp.repeat(jnp.arange(rows), GRID_W).astype(F32)
    col = jnp.tile(jnp.arange(GRID_W), rows).astype(F32)
    inv_freq = ROPE_BASE ** (-jnp.arange(ROPE_PAIRS, dtype=F32) / ROPE_PAIRS)
    ar, ac = row[:, None] * inv_freq, col[:, None] * inv_freq
    cos = jnp.concatenate([jnp.cos(ar), jnp.cos(ar), jnp.cos(ac), jnp.cos(ac)], axis=-1)
    sin = jnp.concatenate([-jnp.sin(ar), jnp.sin(ar), -jnp.sin(ac), jnp.sin(ac)], axis=-1)
    reps = LANES // HEAD_DIM
    cos, sin = jnp.tile(cos, (1, reps)), jnp.tile(sin, (1, reps))
    cos = jnp.concatenate([cos, jnp.ones((SEQ_TILE, LANES), F32)], axis=0)
    sin = jnp.concatenate([sin, jnp.zeros((SEQ_TILE, LANES), F32)], axis=0)
    return cos, sin


def _head_mean_matrix():
    r = jnp.arange(LANES)
    return jnp.where((r[:, None] // HEAD_DIM) == (r[None, :] // HEAD_DIM), 1.0 / HEAD_DIM, 0.0).astype(F32)


def _block_diag_chunks(w, chunk):
    dirs, nblk, bw, _ = w.shape
    per = chunk // bw
    w = w.reshape(dirs, nblk // per, per, bw, bw)
    eye = jnp.eye(per, dtype=w.dtype)
    out = jnp.einsum("dcpij,pq->dcpiqj", w, eye)
    return out.reshape(dirs, nblk // per, chunk, chunk).astype(BF16)


def _router_matrix(gw, gb, ew, eb):
    d = gw.shape[0]
    pad = LANES - N_EXPERTS - N_GROUPS
    wr = jnp.concatenate([ew, gw, jnp.zeros((d, pad), F32)], axis=1)
    br = jnp.concatenate([eb, gb, jnp.zeros((pad,), F32)])[None, :]
    hi = wr.astype(BF16)
    lo = (wr - hi.astype(F32)).astype(BF16)
    return jnp.concatenate([hi, lo], axis=1), br


def kernel(x, c, ctx, c_ctx, ada_w, ada_b, norm_mix, norm_ffn, out_norm, w_out, w_in_ab, q_norm, k_norm, attn_sink, sconv_w, sconv_b, w_in_cd, pool_w, pool_scale, lru_conv_w, lru_conv_b, lru_wa, lru_ba, lru_wx, lru_bx, lru_lambda, router_gw, router_gb, router_ew, router_eb, exp_w1, exp_w3, exp_w2):
    n_batch, seq, d = x.shape
    ctx_len = ctx.shape[1]
    depth = ada_w.shape[0]
    assert ctx_len == SEQ_TILE and seq % TOK_TILE == 0 and seq % GRID_W == 0
    assert depth == 2, "context-side odd-layer outputs are not implemented"
    dims = (n_batch, seq)
    n_lat = n_batch * seq

    cc = jnp.concatenate([c, c_ctx[None, :], jnp.zeros((SUBLANES - n_batch - 1, d), F32)], axis=0)
    mods = _ada(cc, ada_w, ada_b)
    mods = mods[:, :n_batch + 1].reshape(depth, n_batch + 1, N_MOD, d)
    mods = jnp.pad(mods, ((0, 0), (0, 0), (0, SUBLANES - N_MOD), (0, 0)))

    xparts = (x.reshape(n_lat, d), ctx.reshape(n_batch * ctx_len, d))
    cos_t, sin_t = _rope_tables(seq)
    pm = _head_mean_matrix()

    for i in range(depth):
        need_ctx = i < depth - 1
        j = i // 2
        modl = mods[i]
        n_out = n_lat + n_batch * ctx_len if need_ctx else n_lat
        wr, br = _router_matrix(router_gw[i], router_gb[i], router_ew[i], router_eb[i])
        wo_bf = w_out[i].astype(BF16)
        nm, nf, on = norm_mix[i][None, :], norm_ffn[i][None, :], out_norm[i][None, :]
        if i % 2 == 0:
            reps = LANES // HEAD_DIM
            q, kvx, gb, u = _inproj_even(xparts, modl, nm, w_in_ab[j].astype(BF16),
                                          jnp.tile(q_norm[j], reps)[None, :], jnp.tile(k_norm[j], reps)[None, :],
                                          cos_t, sin_t, pm, dims)
            att = _attention(q, kvx, attn_sink[j], dims, need_ctx)
            xn, tok, route, totals = _merge_even(att, gb, u, xparts, modl, sconv_w[j], sconv_b[j][None, :], on, wo_bf,
                                         nf, wr, br, dims, n_out)
        else:
            ux, gg, up = _inproj_odd(xparts, modl, nm, w_in_cd[j].astype(BF16), dims)
            chunk = 2 * LANES
            hf, hr = _lru(ux, lru_conv_w[j], lru_conv_b[j][None, :],
                          _block_diag_chunks(lru_wa[j], chunk), _block_diag_chunks(lru_wx[j], chunk),
                          lru_ba[j][:, None, :], lru_bx[j][:, None, :], lru_lambda[j][:, None, :], dims)
            xn, tok, route, totals = _merge_odd(up, hf, hr, gg, xparts[0], modl, pool_w[j].astype(BF16),
                                        pool_scale[j][None, :], on, wo_bf, nf, wr, br, dims)
        xparts = (_moe(xn, tok, route, totals, modl, exp_w1, exp_w3, exp_w2, i, dims),)
    return xparts[0][:n_lat].reshape(n_batch, seq, d)
```

```python
import functools

import jax
import jax.numpy as jnp
from jax import lax
from jax.experimental import pallas as pl
from jax.experimental.pallas import tpu as pltpu

F32 = jnp.float32
BF16 = jnp.bfloat16
HIGHEST = lax.Precision.HIGHEST
LOG2E = 1.4426950408889634

GRID_W = 64
EPS = 1e-6
HEAD_DIM = 64
N_Q_HEADS = 8
N_KV_HEADS = 2
Q_PER_KV = N_Q_HEADS // N_KV_HEADS
WINDOW = 128
ATT_BLOCK = 128
ROPE_PAIRS = HEAD_DIM // 4
ROPE_BASE = 10000.0
POOL_WINDOWS = (2, 4, 8, 16)
LRU_C = 8.0
N_GROUPS = 4
EXPERTS_PER_GROUP = 8
N_EXPERTS = N_GROUPS * EXPERTS_PER_GROUP
TOP_K = 2
MOE_BLOCK = 512
ZERO_CHUNK = 64
N_MOD = 6

SUBLANES = 8
LANES = 128
SEQ_TILE = 256
TOK_TILE = 512
NEG = -1e30
VMEM_LIMIT = 48 * 1024 * 1024


def _cparams(*sem):
    return pltpu.CompilerParams(dimension_semantics=sem, vmem_limit_bytes=VMEM_LIMIT)


def _rms(x, g):
    ms = jnp.mean(x * x, axis=-1, keepdims=True)
    return x * lax.rsqrt(ms + EPS) * g


def _modulate(x, g, shift, scale):
    return _rms(x, g) * (1.0 + scale) + shift


def _bdot(a, b):
    return jnp.dot(a.astype(BF16), b, preferred_element_type=F32)


def _silu(x):
    return x * jax.nn.sigmoid(x)


def _ada_kernel(c_ref, w_ref, b_ref, o_ref):
    s = _silu(c_ref[...])
    o_ref[0] = jnp.dot(s, w_ref[0], precision=HIGHEST, preferred_element_type=F32) + b_ref[0]


def _ada(cc, ada_w, ada_b):
    depth, d, n = ada_w.shape
    tn = 1536
    return pl.pallas_call(
        _ada_kernel,
        grid=(depth, n // tn),
        in_specs=[pl.BlockSpec((SUBLANES, d), lambda l, j: (0, 0)),
                  pl.BlockSpec((1, d, tn), lambda l, j: (l, 0, j)),
                  pl.BlockSpec((1, 1, tn), lambda l, j: (l, 0, j))],
        out_specs=pl.BlockSpec((1, SUBLANES, tn), lambda l, j: (l, 0, j)),
        out_shape=jax.ShapeDtypeStruct((depth, SUBLANES, n), F32),
        compiler_params=_cparams("arbitrary", "arbitrary"),
        name="ada",
    )(cc, ada_w, ada_b.reshape(depth, 1, n))


def _head_norm_rope(t, pm, g, cos, sin):
    m = t.shape[0]
    t2 = t * t
    hi = t2.astype(BF16)
    lo = (t2 - hi.astype(F32)).astype(BF16)
    ms2 = jnp.dot(jnp.concatenate([hi, lo], axis=0), pm, preferred_element_type=F32)
    ms = ms2[0:m] + ms2[m:2 * m]
    t = t * lax.rsqrt(ms + EPS) * g
    lane = lax.broadcasted_iota(jnp.int32, t.shape, 1)
    first_half = (lane % (2 * ROPE_PAIRS)) < ROPE_PAIRS
    partner = jnp.where(first_half, pltpu.roll(t, LANES - ROPE_PAIRS, 1), pltpu.roll(t, ROPE_PAIRS, 1))
    return t * cos + partner * sin


def _token_operands(xparts, n_lat_tiles, tile):
    d = xparts[0].shape[1]
    lat = pl.BlockSpec((tile, d), lambda j: (jnp.minimum(j, n_lat_tiles - 1), 0))
    if len(xparts) == 2:
        ctx = pl.BlockSpec((tile, d), lambda j: (jnp.maximum(j - n_lat_tiles, 0), 0))
        return list(xparts), [lat, ctx]
    ctx = pl.BlockSpec((tile, d), lambda j: (jnp.maximum(j, n_lat_tiles), 0))
    return [xparts[0], xparts[0]], [lat, ctx]


def _pick_tokens(xl_ref, xc_ref, n_lat_tiles):
    return jnp.where(pl.program_id(0) >= n_lat_tiles, xc_ref[...], xl_ref[...])


def _inproj_even_kernel(xl_ref, xc_ref, mod_ref, g_ref, w_ref, qg_ref, kg_ref, cos_ref, sin_ref, pm_ref,
                        q_ref, kvx_ref, gb_ref, u_ref, *, n_lat_tiles):
    mod = mod_ref[0]
    x = _pick_tokens(xl_ref, xc_ref, n_lat_tiles)
    h = _modulate(x, g_ref[...], mod[0:1], mod[1:2]).astype(BF16)
    cos, sin, pm = cos_ref[...], sin_ref[...], pm_ref[...]
    kv_w = N_KV_HEADS * HEAD_DIM
    att_w = N_Q_HEADS * HEAD_DIM
    sc_w = gb_ref.shape[1]
    kv = jnp.dot(h, w_ref[:, 0:2 * kv_w], preferred_element_type=F32)
    k = _head_norm_rope(kv[:, :kv_w], pm, kg_ref[...], cos, sin)
    v = kv[:, kv_w:]
    kvx_ref[...] = jnp.concatenate([k, pltpu.roll(k, HEAD_DIM, 1), v, pltpu.roll(v, HEAD_DIM, 1)],
                                   axis=-1).astype(BF16)
    q = jnp.dot(h, w_ref[:, 2 * kv_w:2 * kv_w + att_w], preferred_element_type=F32)
    qscale = HEAD_DIM ** -0.5 * LOG2E
    for c in range(att_w // LANES):
        qc = _head_norm_rope(q[:, c * LANES:(c + 1) * LANES], pm, qg_ref[...], cos, sin)
        q_ref[:, c * LANES:(c + 1) * LANES] = (qc * qscale).astype(BF16)
    o = 2 * kv_w + att_w
    gb_ref[...] = jnp.dot(h, w_ref[:, o:o + sc_w], preferred_element_type=F32)
    gc = jnp.dot(h, w_ref[:, o + sc_w:o + 2 * sc_w], preferred_element_type=F32)
    gx = jnp.dot(h, w_ref[:, o + 2 * sc_w:o + 3 * sc_w], preferred_element_type=F32)
    u_ref[...] = gc * gx


def _tile_maps(n_lat_tiles, tiles_per_seq, n_batch):
    def mod_map(j):
        return (jnp.minimum(j // tiles_per_seq, n_batch), 0, 0)

    def pos_map(j):
        return (jnp.where(j < n_lat_tiles, j % tiles_per_seq, tiles_per_seq), 0)

    return mod_map, pos_map


def _inproj_even(xparts, modl, g, w_bf, q_g, k_g, cos_t, sin_t, pm, dims):
    n_tok = sum(p.shape[0] for p in xparts)
    d = xparts[0].shape[1]
    n_batch, seq = dims
    tile = TOK_TILE
    tps = seq // tile
    mod_map, pos_map = _tile_maps(n_batch * tps, tps, n_batch)
    xs, xspecs = _token_operands(xparts, n_batch * tps, tile)
    kv_w, att_w = N_KV_HEADS * HEAD_DIM, N_Q_HEADS * HEAD_DIM
    sc_w = (w_bf.shape[1] - 2 * kv_w - att_w) // 3
    row = lambda j: (j, 0)
    const = lambda j: (0, 0)
    return pl.pallas_call(
        functools.partial(_inproj_even_kernel, n_lat_tiles=n_batch * tps),
        grid=(n_tok // tile,),
        in_specs=xspecs + [
                  pl.BlockSpec((1, SUBLANES, d), mod_map),
                  pl.BlockSpec((1, d), const),
                  pl.BlockSpec(w_bf.shape, const),
                  pl.BlockSpec((1, LANES), const),
                  pl.BlockSpec((1, LANES), const),
                  pl.BlockSpec((tile, LANES), pos_map),
                  pl.BlockSpec((tile, LANES), pos_map),
                  pl.BlockSpec((LANES, LANES), const)],
        out_specs=[pl.BlockSpec((tile, att_w), row),
                   pl.BlockSpec((tile, 4 * kv_w), row),
                   pl.BlockSpec((tile, sc_w), row),
                   pl.BlockSpec((tile, sc_w), row)],
        out_shape=[jax.ShapeDtypeStruct((n_tok, att_w), BF16),
                   jax.ShapeDtypeStruct((n_tok, 4 * kv_w), BF16),
                   jax.ShapeDtypeStruct((n_tok, sc_w), F32),
                   jax.ShapeDtypeStruct((n_tok, sc_w), F32)],
        compiler_params=_cparams("arbitrary"),
        name="inproj_even",
    )(*xs, modl, g, w_bf, q_g, k_g, cos_t, sin_t, pm)


def _attn_heads(q, kvx, mask_fn, sink_ref, o_ref):
    m, n = q.shape[0], kvx.shape[0]
    k_nat, k_swp = kvx[:, 0:LANES], kvx[:, LANES:2 * LANES]
    v_nat, v_swp = kvx[:, 2 * LANES:3 * LANES], kvx[:, 3 * LANES:4 * LANES]
    lo_n = lax.broadcasted_iota(jnp.int32, (n, LANES), 1) < HEAD_DIM
    lo_m = lax.broadcasted_iota(jnp.int32, (2 * m, LANES), 1) < HEAD_DIM
    top = lax.broadcasted_iota(jnp.int32, (2 * m, 1), 0) < m
    zero, one = jnp.zeros_like(k_nat), jnp.ones_like(v_nat)
    outs = []
    for h in range(N_KV_HEADS):
        k_lo, k_hi = (k_nat, k_swp) if h == 0 else (k_swp, k_nat)
        v_lo, v_hi = (v_nat, v_swp) if h == 0 else (v_swp, v_nat)
        g0 = 2 * h
        q2 = jnp.concatenate([q[:, g0 * LANES:(g0 + 1) * LANES],
                              q[:, (g0 + 1) * LANES:(g0 + 2) * LANES]], axis=0)
        parts = []
        for par in range(2):
            kz = jnp.where(lo_n, k_lo, zero) if par == 0 else jnp.where(lo_n, zero, k_hi)
            vz = jnp.where(lo_n, v_lo, one) if par == 0 else jnp.where(lo_n, one, v_hi)
            s = lax.dot_general(q2, kz, (((1,), (1,)), ((), ())), preferred_element_type=F32)
            s = mask_fn(s)
            hq = Q_PER_KV * h + par
            snk = jnp.where(top, sink_ref[hq], sink_ref[hq + 2]) * LOG2E
            mx = jnp.maximum(jnp.max(s, axis=-1, keepdims=True), snk)
            p = jnp.exp2(s - mx)
            parts.append((jnp.dot(p.astype(BF16), vz, preferred_element_type=F32), jnp.exp2(snk - mx)))
        (o_lo, e_lo), (o_hi, e_hi) = parts
        num = jnp.where(lo_m, o_lo, o_hi)
        den = pltpu.roll(jnp.where(lo_m, o_hi, o_lo), HEAD_DIM, 1) + jnp.where(lo_m, e_lo, e_hi)
        o = num / den
        outs += [o[0:m], o[m:2 * m]]
    o_ref[...] = jnp.concatenate(outs, axis=-1)


def _attn_lat_kernel(sink_ref, q_ref, kp_ref, kc_ref, kn_ref, kx_ref, o_ref):
    i = pl.program_id(1)
    nb = pl.num_programs(1)
    kvx = jnp.concatenate([kp_ref[...], kc_ref[...], kn_ref[...], kx_ref[...]], axis=0)
    m = ATT_BLOCK
    tok = lax.broadcasted_iota(jnp.int32, (2 * m, m), 0) % m
    col = lax.broadcasted_iota(jnp.int32, (2 * m, m), 1)
    ok_prev = (col >= tok) & (i > 0)
    ok_next = (col <= tok) & (i < nb - 1)

    def mask_fn(s):
        return jnp.concatenate([jnp.where(ok_prev, s[:, 0:m], NEG), s[:, m:2 * m],
                                jnp.where(ok_next, s[:, 2 * m:3 * m], NEG), s[:, 3 * m:]], axis=1)

    _attn_heads(q_ref[...], kvx, mask_fn, sink_ref, o_ref)


def _attn_ctx_kernel(sink_ref, att_in_ref, q_ref, kx_ref, o_ref):
    del att_in_ref
    _attn_heads(q_ref[...], kx_ref[...], lambda s: s, sink_ref, o_ref)


def _attention(q, kvx, sink, dims, need_ctx):
    n_tok = q.shape[0]
    n_batch, seq = dims
    nb = seq // ATT_BLOCK
    n_lat = n_batch * seq
    ctx_len = (n_tok - n_lat) // n_batch
    att_w, kvx_w = q.shape[1], kvx.shape[1]
    ctx_blk0 = n_lat // ctx_len
    cur = lambda b, i: (b * nb + i, 0)
    prev = lambda b, i: (b * nb + jnp.maximum(i - 1, 0), 0)
    nxt = lambda b, i: (b * nb + jnp.minimum(i + 1, nb - 1), 0)
    cx = lambda b, i: (ctx_blk0 + b, 0)
    smem = pl.BlockSpec(memory_space=pltpu.SMEM)
    kvb = lambda f: pl.BlockSpec((ATT_BLOCK, kvx_w), f)
    att = pl.pallas_call(
        _attn_lat_kernel,
        grid=(n_batch, nb),
        in_specs=[smem, pl.BlockSpec((ATT_BLOCK, att_w), cur),
                  kvb(prev), kvb(cur), kvb(nxt), pl.BlockSpec((ctx_len, kvx_w), cx)],
        out_specs=pl.BlockSpec((ATT_BLOCK, att_w), cur),
        out_shape=jax.ShapeDtypeStruct((n_tok, att_w), F32),
        compiler_params=_cparams("arbitrary", "arbitrary"),
        name="attn_latent",
    )(sink, q, kvx, kvx, kvx, kvx)
    if not need_ctx:
        return att
    cxb = lambda b: (ctx_blk0 + b, 0)
    return pl.pallas_call(
        _attn_ctx_kernel,
        grid=(n_batch,),
        in_specs=[smem, pl.BlockSpec(memory_space=pl.ANY),
                  pl.BlockSpec((ctx_len, att_w), cxb), pl.BlockSpec((ctx_len, kvx_w), cxb)],
        out_specs=pl.BlockSpec((ctx_len, att_w), cxb),
        out_shape=jax.ShapeDtypeStruct((n_tok, att_w), F32),
        input_output_aliases={1: 0},
        compiler_params=_cparams("arbitrary"),
        name="attn_context",
    )(sink, att, q, kvx)


def _route(logits):
    lane = lax.broadcasted_iota(jnp.int32, logits.shape, 1).astype(F32)
    big = float(LANES)
    gl = jnp.where((lane >= N_EXPERTS) & (lane < N_EXPERTS + N_GROUPS), logits, NEG)
    gmax = jnp.max(gl, axis=-1, keepdims=True)
    gidx = jnp.min(jnp.where(gl == gmax, lane, big), axis=-1, keepdims=True) - N_EXPERTS
    p_sel = 1.0 / jnp.sum(jnp.exp(gl - gmax), axis=-1, keepdims=True)
    lo = gidx * EXPERTS_PER_GROUP
    el = jnp.where((lane >= lo) & (lane < lo + EXPERTS_PER_GROUP), logits, NEG)
    m1 = jnp.max(el, axis=-1, keepdims=True)
    i1 = jnp.min(jnp.where(el == m1, lane, big), axis=-1, keepdims=True)
    el2 = jnp.where(lane == i1, NEG, el)
    m2 = jnp.max(el2, axis=-1, keepdims=True)
    i2 = jnp.min(jnp.where(el2 == m2, lane, big), axis=-1, keepdims=True)
    t = jnp.exp(m2 - m1)
    w1 = p_sel / (1.0 + t)
    w2 = p_sel * t / (1.0 + t)
    return jnp.where(lane == 0, i1, jnp.where(lane == 1, i2, jnp.where(lane == 2, w1,
                     jnp.where(lane == 3, w2, 0.0))))


def _tail(n1, n2, gain, wo_ref, x, mod, nf_ref, wr_ref, br_ref, xo_ref, tok_ref, route_ref, cnt_ref):
    half = n1.shape[1]
    y = (_bdot(_rms(n1, gain[:, :half]), wo_ref[0:half, :])
         + _bdot(_rms(n2, gain[:, half:]), wo_ref[half:, :]))
    xn = x + mod[2:3] * y
    xo_ref[...] = xn
    tok = _modulate(xn, nf_ref[...], mod[3:4], mod[4:5])
    tok_ref[...] = tok
    t_hi = tok.astype(BF16)
    t_lo = (tok - t_hi.astype(F32)).astype(BF16)
    hh = jnp.dot(t_hi, wr_ref[...], preferred_element_type=F32)
    logits = (hh[:, :LANES] + hh[:, LANES:] + br_ref[...]
              + jnp.dot(t_lo, wr_ref[:, 0:LANES], preferred_element_type=F32))
    route = _route(logits)
    route_ref[...] = route
    lane = lax.broadcasted_iota(jnp.int32, route.shape, 1).astype(F32)
    hit = jnp.where((lane == route[:, 0:1]) | (lane == route[:, 1:2]), 1.0, 0.0)

    @pl.when(pl.program_id(0) == 0)
    def _():
        cnt_ref[...] = jnp.zeros_like(cnt_ref)

    cnt_ref[0:1, :] += jnp.sum(hit, axis=0, keepdims=True)


def _seq_flags(j, n_lat_tiles, tiles_per_seq):
    is_ctx = j >= n_lat_tiles
    first = is_ctx | (j % tiles_per_seq == 0)
    last = is_ctx | (j % tiles_per_seq == tiles_per_seq - 1)
    return first, last


def _with_halo(prev_ref, tile, next_ref, first, last):
    p = jnp.where(first, 0.0, prev_ref[...])
    n = jnp.where(last, 0.0, next_ref[...])
    return jnp.concatenate([p, tile, n], axis=0)


def _shift_rows(e, k):
    return pltpu.roll(e, (-k) % e.shape[0], 0)


def _merge_even_kernel(att_ref, gb_ref, u_ref, up_ref, un_ref, xl_ref, xc_ref, mod_ref, cw_ref, cb_ref,
                       on_ref, wo_ref, nf_ref, wr_ref, br_ref,
                       xo_ref, tok_ref, route_ref, cnt_ref, *, n_lat_tiles, tiles_per_seq):
    j = pl.program_id(0)
    first, last = _seq_flags(j, n_lat_tiles, tiles_per_seq)
    e = _with_halo(up_ref, u_ref[...], un_ref, first, last)
    cw = cw_ref[...]
    lo, hi = SUBLANES, SUBLANES + SEQ_TILE
    conv = (cw[0:1] * _shift_rows(e, -1)[lo:hi] + cw[1:2] * e[lo:hi]
            + cw[2:3] * _shift_rows(e, 1)[lo:hi] + cb_ref[...])
    conv = gb_ref[...] * conv
    x = _pick_tokens(xl_ref, xc_ref, n_lat_tiles)
    _tail(att_ref[...], conv, on_ref[...], wo_ref, x, mod_ref[0], nf_ref, wr_ref, br_ref,
          xo_ref, tok_ref, route_ref, cnt_ref)


def _halo_specs(width, n_rows):
    r = SEQ_TILE // SUBLANES
    n_blk = n_rows // SUBLANES
    prev = pl.BlockSpec((SUBLANES, width), lambda j: (jnp.maximum(j * r - 1, 0), 0))
    nxt = pl.BlockSpec((SUBLANES, width), lambda j: (jnp.minimum((j + 1) * r, n_blk - 1), 0))
    return prev, nxt


def _tail_specs(d, n_out_tok):
    row = lambda j: (j, 0)
    out_specs = [pl.BlockSpec((SEQ_TILE, d), row), pl.BlockSpec((SEQ_TILE, d), row),
                 pl.BlockSpec((SEQ_TILE, LANES), row), pl.BlockSpec((SUBLANES, LANES), lambda j: (0, 0))]
    out_shape = [jax.ShapeDtypeStruct((n_out_tok, d), F32), jax.ShapeDtypeStruct((n_out_tok, d), F32),
                 jax.ShapeDtypeStruct((n_out_tok, LANES), F32), jax.ShapeDtypeStruct((SUBLANES, LANES), F32)]
    return out_specs, out_shape


def _merge_even(att, gb, u, xparts, modl, cw, cb, on, wo_bf, nf, wr, br, dims, n_out_tok):
    n_tok = u.shape[0]
    d = xparts[0].shape[1]
    n_batch, seq = dims
    tps = seq // SEQ_TILE
    mod_map, _ = _tile_maps(n_batch * tps, tps, n_batch)
    xs, xspecs = _token_operands(xparts, n_batch * tps, SEQ_TILE)
    sc_w = u.shape[1]
    row = lambda j: (j, 0)
    const = lambda j: (0, 0)
    prev, nxt = _halo_specs(sc_w, n_tok)
    out_specs, out_shape = _tail_specs(d, n_out_tok)
    return pl.pallas_call(
        functools.partial(_merge_even_kernel, n_lat_tiles=n_batch * tps, tiles_per_seq=tps),
        grid=(n_out_tok // SEQ_TILE,),
        in_specs=[pl.BlockSpec((SEQ_TILE, att.shape[1]), row),
                  pl.BlockSpec((SEQ_TILE, sc_w), row),
                  pl.BlockSpec((SEQ_TILE, sc_w), row), prev, nxt] + xspecs + [
                  pl.BlockSpec((1, SUBLANES, d), mod_map),
                  pl.BlockSpec(cw.shape, const), pl.BlockSpec((1, sc_w), const),
                  pl.BlockSpec((1, d), const), pl.BlockSpec(wo_bf.shape, const),
                  pl.BlockSpec((1, d), const), pl.BlockSpec(wr.shape, const),
                  pl.BlockSpec((1, LANES), const)],
        out_specs=out_specs, out_shape=out_shape,
        compiler_params=_cparams("arbitrary"),
        name="merge_even",
    )(att, gb, u, u, u, *xs, modl, cw, cb, on, wo_bf, nf, wr, br)


def _inproj_odd_kernel(xl_ref, xc_ref, mod_ref, g_ref, w_ref, ux_ref, gg_ref, up_ref, *, n_lat_tiles):
    mod = mod_ref[0]
    x = _pick_tokens(xl_ref, xc_ref, n_lat_tiles)
    h = _modulate(x, g_ref[...], mod[0:1], mod[1:2]).astype(BF16)
    w = ux_ref.shape[1]
    ux_ref[...] = jnp.dot(h, w_ref[:, 0:w], preferred_element_type=F32)
    ug = jnp.dot(h, w_ref[:, w:2 * w], preferred_element_type=F32)
    gg_ref[...] = jax.nn.gelu(ug, approximate=True)
    up_ref[...] = jnp.dot(h, w_ref[:, 2 * w:3 * w], preferred_element_type=F32)


def _inproj_odd(xparts, modl, g, w_bf, dims):
    n_tok = sum(p.shape[0] for p in xparts)
    d = xparts[0].shape[1]
    n_batch, seq = dims
    tile = TOK_TILE
    tps = seq // tile
    mod_map, _ = _tile_maps(n_batch * tps, tps, n_batch)
    xs, xspecs = _token_operands(xparts, n_batch * tps, tile)
    w = w_bf.shape[1] // 3
    row = lambda j: (j, 0)
    const = lambda j: (0, 0)
    return pl.pallas_call(
        functools.partial(_inproj_odd_kernel, n_lat_tiles=n_batch * tps),
        grid=(n_tok // tile,),
        in_specs=xspecs + [
                  pl.BlockSpec((1, SUBLANES, d), mod_map),
                  pl.BlockSpec((1, d), const),
                  pl.BlockSpec(w_bf.shape, const)],
        out_specs=[pl.BlockSpec((tile, w), row)] * 3,
        out_shape=[jax.ShapeDtypeStruct((n_tok, w), F32)] * 3,
        compiler_params=_cparams("arbitrary"),
        name="inproj_odd",
    )(*xs, modl, g, w_bf)


def _lru_conv(e, cw, cb):
    lo, hi = SUBLANES, SUBLANES + SEQ_TILE
    return (cw[0:1] * _shift_rows(e, -2)[lo:hi] + cw[1:2] * _shift_rows(e, -1)[lo:hi]
            + cw[2:3] * e[lo:hi] + cw[3:4] * _shift_rows(e, 1)[lo:hi] + cb)


def _lru_coeffs(xl, d, wa_ref, wx_ref, ba_ref, bx_ref, lam_ref, a_ref, b_ref):
    xb = xl.astype(BF16)
    chunk = wa_ref.shape[-1]
    n_chunk = xl.shape[1] // chunk
    pre_a = jnp.concatenate([jnp.dot(xb[:, c * chunk:(c + 1) * chunk], wa_ref[d, c],
                                     preferred_element_type=F32) for c in range(n_chunk)], axis=-1)
    pre_x = jnp.concatenate([jnp.dot(xb[:, c * chunk:(c + 1) * chunk], wx_ref[d, c],
                                     preferred_element_type=F32) for c in range(n_chunk)], axis=-1)
    r = 0.5 + 0.5 * jnp.tanh(0.5 * (pre_a + ba_ref[d]))
    gi = 0.5 + 0.5 * jnp.tanh(0.5 * (pre_x + bx_ref[d]))
    log_a = (-LRU_C * jax.nn.softplus(-lam_ref[d])) * r
    a_ref[...] = jnp.exp(log_a)
    th = jnp.tanh(log_a)
    b_ref[...] = jnp.sqrt(-2.0 * th / (1.0 - th)) * gi * xl


def _scan_tile(a_ref, b_ref, h_ref, carry, reverse):
    n_grp = a_ref.shape[0] // SUBLANES
    row = lax.broadcasted_iota(jnp.int32, (SUBLANES, a_ref.shape[1]), 0)

    def body(g, carry):
        gi = (n_grp - 1 - g) if reverse else g
        r0 = pl.multiple_of(gi * SUBLANES, SUBLANES)
        a8 = a_ref[pl.ds(r0, SUBLANES), :]
        b8 = b_ref[pl.ds(r0, SUBLANES), :]
        for s in (1, 2, 4):
            if reverse:
                ok = row < SUBLANES - s
                sh = SUBLANES - s
            else:
                ok = row >= s
                sh = s
            a_sh = jnp.where(ok, pltpu.roll(a8, sh, 0), 1.0)
            b_sh = jnp.where(ok, pltpu.roll(b8, sh, 0), 0.0)
            b8 = a8 * b_sh + b8
            a8 = a8 * a_sh
        h8 = a8 * carry + b8
        if h_ref is not None:
            h_ref[pl.ds(r0, SUBLANES), :] = h8
        return h8[0:1] if reverse else h8[SUBLANES - 1:SUBLANES]

    return lax.fori_loop(0, n_grp, body, carry)


def _lru_kernel(uf_ref, ufp_ref, ufn_ref, ur_ref, urp_ref, urn_ref, uc_ref, cw_ref, cb_ref,
                wa_ref, wx_ref, ba_ref, bx_ref, lam_ref, hf_ref, hr_ref,
                cf_ref, cr_ref, a_ref, b_ref):
    i = pl.program_id(1)
    nt = pl.num_programs(1)
    cw, cb = cw_ref[...], cb_ref[...]
    coeffs = functools.partial(_lru_coeffs, wa_ref=wa_ref, wx_ref=wx_ref, ba_ref=ba_ref,
                               bx_ref=bx_ref, lam_ref=lam_ref, a_ref=a_ref, b_ref=b_ref)

    @pl.when(i == 0)
    def _():
        z = jnp.zeros((SUBLANES, uc_ref.shape[1]), F32)
        xc = _lru_conv(jnp.concatenate([z, uc_ref[...], z], axis=0), cw, cb)
        zero = jnp.zeros((1, uc_ref.shape[1]), F32)
        coeffs(xc, 0)
        cf_ref[...] = _scan_tile(a_ref, b_ref, None, zero, False)
        coeffs(xc, 1)
        cr_ref[...] = _scan_tile(a_ref, b_ref, None, zero, True)

    xf = _lru_conv(_with_halo(ufp_ref, uf_ref[...], ufn_ref, i == 0, i == nt - 1), cw, cb)
    coeffs(xf, 0)
    cf_ref[...] = _scan_tile(a_ref, b_ref, hf_ref, cf_ref[...], False)
    xr = _lru_conv(_with_halo(urp_ref, ur_ref[...], urn_ref, i == nt - 1, i == 0), cw, cb)
    coeffs(xr, 1)
    cr_ref[...] = _scan_tile(a_ref, b_ref, hr_ref, cr_ref[...], True)


def _lru(ux, cw, cb, wa_bd, wx_bd, ba, bx, lam, dims):
    n_tok, w = ux.shape
    n_batch, seq = dims
    nt = seq // SEQ_TILE
    n_lat = n_batch * seq
    r = SEQ_TILE // SUBLANES
    n_blk = n_tok // SUBLANES
    fwd = lambda b, i: b * nt + i
    rev = lambda b, i: b * nt + nt - 1 - i

    def specs(tile):
        return [pl.BlockSpec((SEQ_TILE, w), lambda b, i: (tile(b, i), 0)),
                pl.BlockSpec((SUBLANES, w), lambda b, i: (jnp.maximum(tile(b, i) * r - 1, 0), 0)),
                pl.BlockSpec((SUBLANES, w), lambda b, i: (jnp.minimum((tile(b, i) + 1) * r, n_blk - 1), 0))]

    const2 = lambda b, i: (0, 0)
    const3 = lambda b, i: (0, 0, 0)
    const4 = lambda b, i: (0, 0, 0, 0)
    return pl.pallas_call(
        _lru_kernel,
        grid=(n_batch, nt),
        in_specs=specs(fwd) + specs(rev) + [
            pl.BlockSpec((SEQ_TILE, w), lambda b, i: (n_lat // SEQ_TILE + b, 0)),
            pl.BlockSpec(cw.shape, const2), pl.BlockSpec((1, w), const2),
            pl.BlockSpec(wa_bd.shape, const4), pl.BlockSpec(wx_bd.shape, const4),
            pl.BlockSpec(ba.shape, const3), pl.BlockSpec(bx.shape, const3),
            pl.BlockSpec(lam.shape, const3)],
        out_specs=[pl.BlockSpec((SEQ_TILE, w), lambda b, i: (fwd(b, i), 0)),
                   pl.BlockSpec((SEQ_TILE, w), lambda b, i: (rev(b, i), 0))],
        out_shape=[jax.ShapeDtypeStruct((n_lat, w), F32)] * 2,
        scratch_shapes=[pltpu.VMEM((1, w), F32), pltpu.VMEM((1, w), F32),
                        pltpu.VMEM((SEQ_TILE, w), F32), pltpu.VMEM((SEQ_TILE, w), F32)],
        compiler_params=_cparams("arbitrary", "arbitrary"),
        name="rglru",
    )(ux, ux, ux, ux, ux, ux, ux, cw, cb, wa_bd, wx_bd, ba, bx, lam)


def _merge_odd_kernel(up_ref, upp_ref, upn_ref, hf_ref, hr_ref, gg_ref, x_ref, mod_ref,
                      pw_ref, ps_ref, on_ref, wo_ref, nf_ref, wr_ref, br_ref,
                      xo_ref, tok_ref, route_ref, cnt_ref, *, tiles_per_seq, seq):
    j = pl.program_id(0)
    ts = j % tiles_per_seq
    e = _with_halo(upp_ref, up_ref[...], upn_ref, ts == 0, ts == tiles_per_seq - 1)
    gw = pw_ref.shape[-1]
    lo, hi = SUBLANES, SUBLANES + SEQ_TILE
    tpos = ts * SEQ_TILE + lax.broadcasted_iota(jnp.int32, (SEQ_TILE, 1), 0)
    parts = []
    for g, win in enumerate(POOL_WINDOWS):
        eg = e[:, g * gw:(g + 1) * gw]
        p = eg
        span = 1
        while span < win:
            p = p + _shift_rows(p, span)
            span *= 2
        half = win // 2
        s = _shift_rows(p, -half)[lo:hi]
        cnt = (jnp.minimum(tpos + half, seq) - jnp.maximum(tpos - half, 0)).astype(F32)
        dlt = s / cnt - eg[lo:hi]
        parts.append(_bdot(dlt, pw_ref[g]))
    pool = jnp.concatenate(parts, axis=-1) * ps_ref[...]
    lru = (hf_ref[...] + hr_ref[...]) * gg_ref[...]
    _tail(pool, lru, on_ref[...], wo_ref, x_ref[...], mod_ref[0], nf_ref, wr_ref, br_ref,
          xo_ref, tok_ref, route_ref, cnt_ref)


def _merge_odd(up, hf, hr, gg, xf, modl, pw_bf, ps, on, wo_bf, nf, wr, br, dims):
    d = xf.shape[1]
    n_tok, w = up.shape
    n_batch, seq = dims
    tps = seq // SEQ_TILE
    n_lat = n_batch * seq
    mod_map, _ = _tile_maps(n_batch * tps, tps, n_batch)
    row = lambda j: (j, 0)
    const = lambda j: (0, 0)
    prev, nxt = _halo_specs(w, n_tok)
    out_specs, out_shape = _tail_specs(d, n_lat)
    return pl.pallas_call(
        functools.partial(_merge_odd_kernel, tiles_per_seq=tps, seq=seq),
        grid=(n_lat // SEQ_TILE,),
        in_specs=[pl.BlockSpec((SEQ_TILE, w), row), prev, nxt,
                  pl.BlockSpec((SEQ_TILE, w), row), pl.BlockSpec((SEQ_TILE, w), row),
                  pl.BlockSpec((SEQ_TILE, w), row),
                  pl.BlockSpec((SEQ_TILE, d), row),
                  pl.BlockSpec((1, SUBLANES, d), mod_map),
                  pl.BlockSpec(pw_bf.shape, lambda j: (0, 0, 0)), pl.BlockSpec((1, w), const),
                  pl.BlockSpec((1, d), const), pl.BlockSpec(wo_bf.shape, const),
                  pl.BlockSpec((1, d), const), pl.BlockSpec(wr.shape, const),
                  pl.BlockSpec((1, LANES), const)],
        out_specs=out_specs, out_shape=out_shape,
        compiler_params=_cparams("arbitrary"),
        name="merge_odd",
    )(up, up, up, hf, hr, gg, xf, modl, pw_bf, ps, on, wo_bf, nf, wr, br)


def _slots_kernel(route_ref, tot_ref, dest_ref, cnt_ref, run_ref, start_ref):
    i = pl.program_id(0)
    rt = route_ref[...]
    m = rt.shape[0]
    lane = lax.broadcasted_iota(jnp.int32, rt.shape, 1).astype(F32)
    oh1 = lane == rt[:, 0:1]
    oh2 = lane == rt[:, 1:2]
    s = jnp.where(oh1 | oh2, 1.0, 0.0)

    @pl.when(i == 0)
    def _():
        tot = tot_ref[0:1, :]
        padded = jnp.floor((tot + (MOE_BLOCK - 1)) / MOE_BLOCK) * MOE_BLOCK
        r = lax.broadcasted_iota(jnp.int32, (LANES, LANES), 0)
        c = lax.broadcasted_iota(jnp.int32, (LANES, LANES), 1)
        upper = jnp.where(r < c, 1.0, 0.0)
        p8 = jnp.broadcast_to(padded, (SUBLANES, LANES))
        start = jnp.dot(p8, upper, precision=HIGHEST, preferred_element_type=F32)[0:1]
        start_ref[...] = start
        run_ref[...] = jnp.zeros_like(run_ref)
        sub = lax.broadcasted_iota(jnp.int32, cnt_ref.shape, 0)
        cnt_ref[...] = jnp.where(sub == 0, tot, jnp.where(sub == 1, start, 0.0))

    r = lax.broadcasted_iota(jnp.int32, (m, m), 0)
    c = lax.broadcasted_iota(jnp.int32, (m, m), 1)
    lower = jnp.where(c < r, 1.0, 0.0).astype(BF16)
    prefix = jnp.dot(lower, s.astype(BF16), preferred_element_type=F32)
    base = prefix + run_ref[...] + start_ref[...]
    d1 = jnp.sum(jnp.where(oh1, base, 0.0), axis=-1, keepdims=True)
    d2 = jnp.sum(jnp.where(oh2, base, 0.0), axis=-1, keepdims=True)
    dest_ref[...] = jnp.where(lane == 0, d1, jnp.where(lane == 1, d2, 0.0)).astype(jnp.int32)
    run_ref[...] += jnp.sum(s, axis=0, keepdims=True)


def _slots(route, totals):
    n_tok = route.shape[0]
    nt = n_tok // TOK_TILE
    return pl.pallas_call(
        _slots_kernel,
        grid=(nt,),
        in_specs=[pl.BlockSpec((TOK_TILE, LANES), lambda i: (i, 0)),
                  pl.BlockSpec((SUBLANES, LANES), lambda i: (0, 0))],
        out_specs=[pl.BlockSpec((TOK_TILE, LANES), lambda i: (i, 0)),
                   pl.BlockSpec((SUBLANES, LANES), lambda i: (0, 0))],
        out_shape=[jax.ShapeDtypeStruct((n_tok, LANES), jnp.int32),
                   jax.ShapeDtypeStruct((SUBLANES, LANES), F32)],
        scratch_shapes=[pltpu.VMEM((1, LANES), F32)] * 2,
        compiler_params=_cparams("arbitrary"),
        name="moe_slots",
    )(route, totals)


def _for_each_row(n_rows, fn):
    def body(g, c):
        g8 = pl.multiple_of(g * SUBLANES, SUBLANES)
        for j in range(SUBLANES):
            fn(g8, j)
        return c
    lax.fori_loop(0, n_rows // SUBLANES, body, 0)


def _dispatch_kernel(dest_ref, pad_lo_ref, pad_hi_ref, tok_ref, xs_ref, zrow, sem, zsem):
    i = pl.program_id(0)
    base = i * TOK_TILE

    def zero_copy(r, n):
        return pltpu.make_async_copy(zrow.at[pl.ds(0, n)], xs_ref.at[pl.ds(r, n)], zsem)

    def for_each_pad_chunk(fn):
        def per_expert(e, c):
            lo, hi = pad_lo_ref[e], pad_hi_ref[e]
            a8 = (lo + SUBLANES - 1) // SUBLANES * SUBLANES
            a64 = (a8 + ZERO_CHUNK - 1) // ZERO_CHUNK * ZERO_CHUNK

            def rows(r, c):
                fn(zero_copy(r, 1))
                return c

            def eights(q, c):
                fn(zero_copy(pl.multiple_of(a8 + q * SUBLANES, SUBLANES), SUBLANES))
                return c

            def chunks(q, c):
                fn(zero_copy(pl.multiple_of(a64 + q * ZERO_CHUNK, ZERO_CHUNK), ZERO_CHUNK))
                return c

            lax.fori_loop(lo, a8, rows, 0)
            lax.fori_loop(0, (a64 - a8) // SUBLANES, eights, 0)
            lax.fori_loop(0, (hi - a64) // ZERO_CHUNK, chunks, 0)
            return c
        lax.fori_loop(0, N_EXPERTS, per_expert, 0)

    @pl.when(i == 0)
    def _():
        zrow[...] = jnp.zeros_like(zrow)
        for_each_pad_chunk(lambda cp: cp.start())

    def issue(g8, j):
        rows = tok_ref.at[pl.ds(g8, SUBLANES)]
        for k in range(TOP_K):
            d = dest_ref[(base + g8) * TOP_K + (j * TOP_K + k)]
            pltpu.make_async_copy(rows.at[pl.ds(j, 1)], xs_ref.at[pl.ds(d, 1)], sem).start(priority=k)

    _for_each_row(TOK_TILE, issue)
    for k in range(TOP_K):
        pltpu.make_async_copy(tok_ref, xs_ref.at[pl.ds(0, TOK_TILE)], sem).wait()

    @pl.when(i == pl.num_programs(0) - 1)
    def _():
        for_each_pad_chunk(lambda cp: cp.wait())


def _dispatch(dest_flat, pad_lo, pad_hi, tok, n_slots):
    n_tok, d = tok.shape
    return pl.pallas_call(
        _dispatch_kernel,
        grid_spec=pltpu.PrefetchScalarGridSpec(
            num_scalar_prefetch=3,
            grid=(n_tok // TOK_TILE,),
            in_specs=[pl.BlockSpec((TOK_TILE, d), lambda i, *_: (i, 0))],
            out_specs=pl.BlockSpec(memory_space=pl.ANY),
            scratch_shapes=[pltpu.VMEM((ZERO_CHUNK, d), F32),
                            pltpu.SemaphoreType.DMA(()), pltpu.SemaphoreType.DMA(())]),
        out_shape=jax.ShapeDtypeStruct((n_slots, d), F32),
        compiler_params=_cparams("arbitrary"),
        name="moe_dispatch",
    )(dest_flat, pad_lo, pad_hi, tok)


def _expert_kernel(be_ref, nu_ref, x_ref, w1_ref, w3_ref, w2_ref, y_ref, w1b, w3b, w2b):
    i = pl.program_id(0)

    @pl.when(i < nu_ref[0])
    def _():
        changed = (i == 0) | (be_ref[i] != be_ref[jnp.maximum(i - 1, 0)])

        @pl.when(changed)
        def _():
            w1b[...] = w1_ref[0, 0].astype(BF16)
            w3b[...] = w3_ref[0, 0].astype(BF16)
            w2b[...] = w2_ref[0, 0].astype(BF16)

        x = x_ref[...].astype(BF16)
        a = jnp.dot(x, w1b[...], preferred_element_type=F32)
        b = jnp.dot(x, w3b[...], preferred_element_type=F32)
        y_ref[...] = _bdot(_silu(a) * b, w2b[...])


def _experts(block_e, n_used, xs, w1, w3, w2, layer):
    n_slots, d = xs.shape
    de = w1.shape[-1]
    n_blocks = n_slots // MOE_BLOCK
    blk = lambda i, be, nu: (jnp.minimum(i, nu[0] - 1), 0)
    wmap = lambda i, be, nu: (layer, be[i], 0, 0)
    return pl.pallas_call(
        _expert_kernel,
        grid_spec=pltpu.PrefetchScalarGridSpec(
            num_scalar_prefetch=2,
            grid=(n_blocks,),
            in_specs=[pl.BlockSpec((MOE_BLOCK, d), blk),
                      pl.BlockSpec((1, 1, d, de), wmap), pl.BlockSpec((1, 1, d, de), wmap),
                      pl.BlockSpec((1, 1, de, d), wmap)],
            out_specs=pl.BlockSpec((MOE_BLOCK, d), blk),
            scratch_shapes=[pltpu.VMEM((d, de), BF16), pltpu.VMEM((d, de), BF16),
                            pltpu.VMEM((de, d), BF16)]),
        out_shape=jax.ShapeDtypeStruct((n_slots, d), F32),
        compiler_params=_cparams("arbitrary"),
        name="moe_experts",
    )(block_e, n_used, xs, w1, w3, w2)


def _combine_kernel(dest_ref, x_ref, route_ref, mod_ref, ys_ref, o_ref, ybuf, sem):
    base = pl.program_id(0) * TOK_TILE

    def issue(g8, j):
        for k in range(TOP_K):
            d = dest_ref[(base + g8) * TOP_K + (j * TOP_K + k)]
            rows = ybuf.at[k, pl.ds(g8, SUBLANES)]
            pltpu.make_async_copy(ys_ref.at[pl.ds(d, 1)], rows.at[pl.ds(j, 1)], sem).start(priority=k)

    _for_each_row(TOK_TILE, issue)
    for k in range(TOP_K):
        pltpu.make_async_copy(ys_ref.at[pl.ds(0, TOK_TILE)], ybuf.at[k], sem).wait()
    rt = route_ref[...]
    y = rt[:, 2:3] * ybuf[0] + rt[:, 3:4] * ybuf[1]
    o_ref[...] = x_ref[...] + mod_ref[0][5:6] * y


def _combine(dest_flat, xn, route, modl, ys, dims):
    n_tok, d = xn.shape
    n_batch, seq = dims
    tpb = seq // TOK_TILE
    return pl.pallas_call(
        _combine_kernel,
        grid_spec=pltpu.PrefetchScalarGridSpec(
            num_scalar_prefetch=1,
            grid=(n_tok // TOK_TILE,),
            in_specs=[pl.BlockSpec((TOK_TILE, d), lambda i, dest: (i, 0)),
                      pl.BlockSpec((TOK_TILE, LANES), lambda i, dest: (i, 0)),
                      pl.BlockSpec((1, SUBLANES, d), lambda i, dest: (jnp.minimum(i // tpb, n_batch), 0, 0)),
                      pl.BlockSpec(memory_space=pl.ANY)],
            out_specs=pl.BlockSpec((TOK_TILE, d), lambda i, dest: (i, 0)),
            scratch_shapes=[pltpu.VMEM((TOP_K, TOK_TILE, d), F32), pltpu.SemaphoreType.DMA(())]),
        out_shape=jax.ShapeDtypeStruct((n_tok, d), F32),
        compiler_params=_cparams("arbitrary"),
        name="moe_combine",
    )(dest_flat, xn, route, modl, ys)


def _moe(xn, tok, route, totals, modl, w1, w3, w2, layer, dims):
    n_tok = tok.shape[0]
    n = n_tok * TOP_K
    n_blocks = (n + N_EXPERTS * (MOE_BLOCK - 1) + MOE_BLOCK - 1) // MOE_BLOCK
    dest, counts = _slots(route, totals)
    dest_flat = dest[:, :TOP_K].reshape(-1)
    cnt = counts[0, :N_EXPERTS].astype(jnp.int32)
    start = counts[1, :N_EXPERTS].astype(jnp.int32)
    blocks_per_e = (cnt + MOE_BLOCK - 1) // MOE_BLOCK
    blk_end = (start + blocks_per_e * MOE_BLOCK) // MOE_BLOCK
    n_used = blk_end[-1:]
    blk = jnp.minimum(jnp.arange(n_blocks, dtype=jnp.int32), n_used[0] - 1)
    block_e = jnp.sum((blk[:, None] >= blk_end[None, :]).astype(jnp.int32), axis=1)
    block_e = jnp.minimum(block_e, N_EXPERTS - 1)
    xs = _dispatch(dest_flat, start + cnt, blk_end * MOE_BLOCK, tok, n_blocks * MOE_BLOCK)
    ys = _experts(block_e, n_used, xs, w1, w3, w2, layer)
    return _combine(dest_flat, xn, route, modl, ys, dims)


def _rope_tables(seq):
    rows = seq // GRID_W
    row = jnp.repeat(jnp.arange(rows), GRID_W).astype(F32)
    col = jnp.tile(jnp.arange(GRID_W), rows).astype(F32)
    inv_freq = ROPE_BASE ** (-jnp.arange(ROPE_PAIRS, dtype=F32) / ROPE_PAIRS)
    ar, ac = row[:, None] * inv_freq, col[:, None] * inv_freq
    cos = jnp.concatenate([jnp.cos(ar), jnp.cos(ar), jnp.cos(ac), jnp.cos(ac)], axis=-1)
    sin = jnp.concatenate([-jnp.sin(ar), jnp.sin(ar), -jnp.sin(ac), jnp.sin(ac)], axis=-1)
    reps = LANES // HEAD_DIM
    cos, sin = jnp.tile(cos, (1, reps)), jnp.tile(sin, (1, reps))
    cos = jnp.concatenate([cos, jnp.ones((TOK_TILE, LANES), F32)], axis=0)
    sin = jnp.concatenate([sin, jnp.zeros((TOK_TILE, LANES), F32)], axis=0)
    return cos, sin


def _head_mean_matrix():
    r = jnp.arange(LANES)
    return jnp.where((r[:, None] // HEAD_DIM) == (r[None, :] // HEAD_DIM), 1.0 / HEAD_DIM, 0.0).astype(BF16)


def _block_diag_chunks(w, chunk):
    dirs, nblk, bw, _ = w.shape
    per = chunk // bw
    w = w.reshape(dirs, nblk // per, per, bw, bw)
    eye = jnp.eye(per, dtype=w.dtype)
    out = jnp.einsum("dcpij,pq->dcpiqj", w, eye)
    return out.reshape(dirs, nblk // per, chunk, chunk).astype(BF16)


def _router_matrix(gw, gb, ew, eb):
    d = gw.shape[0]
    pad = LANES - N_EXPERTS - N_GROUPS
    wr = jnp.concatenate([ew, gw, jnp.zeros((d, pad), F32)], axis=1)
    br = jnp.concatenate([eb, gb, jnp.zeros((pad,), F32)])[None, :]
    hi = wr.astype(BF16)
    lo = (wr - hi.astype(F32)).astype(BF16)
    return jnp.concatenate([hi, lo], axis=1), br


def kernel(x, c, ctx, c_ctx, ada_w, ada_b, norm_mix, norm_ffn, out_norm, w_out, w_in_ab, q_norm, k_norm, attn_sink, sconv_w, sconv_b, w_in_cd, pool_w, pool_scale, lru_conv_w, lru_conv_b, lru_wa, lru_ba, lru_wx, lru_bx, lru_lambda, router_gw, router_gb, router_ew, router_eb, exp_w1, exp_w3, exp_w2):
    n_batch, seq, d = x.shape
    ctx_len = ctx.shape[1]
    depth = ada_w.shape[0]
    assert ctx_len == SEQ_TILE and seq % TOK_TILE == 0 and seq % GRID_W == 0
    assert depth == 2, "context-side odd-layer outputs are not implemented"
    dims = (n_batch, seq)
    n_lat = n_batch * seq

    cc = jnp.concatenate([c, c_ctx[None, :], jnp.zeros((SUBLANES - n_batch - 1, d), F32)], axis=0)
    mods = _ada(cc, ada_w, ada_b)
    mods = mods[:, :n_batch + 1].reshape(depth, n_batch + 1, N_MOD, d)
    mods = jnp.pad(mods, ((0, 0), (0, 0), (0, SUBLANES - N_MOD), (0, 0)))

    xparts = (x.reshape(n_lat, d), ctx.reshape(n_batch * ctx_len, d))
    cos_t, sin_t = _rope_tables(seq)
    pm = _head_mean_matrix()

    for i in range(depth):
        need_ctx = i < depth - 1
        j = i // 2
        modl = mods[i]
        n_out = n_lat + n_batch * ctx_len if need_ctx else n_lat
        wr, br = _router_matrix(router_gw[i], router_gb[i], router_ew[i], router_eb[i])
        wo_bf = w_out[i].astype(BF16)
        nm, nf, on = norm_mix[i][None, :], norm_ffn[i][None, :], out_norm[i][None, :]
        if i % 2 == 0:
            reps = LANES // HEAD_DIM
            q, kvx, gb, u = _inproj_even(xparts, modl, nm, w_in_ab[j].astype(BF16),
                                          jnp.tile(q_norm[j], reps)[None, :], jnp.tile(k_norm[j], reps)[None, :],
                                          cos_t, sin_t, pm, dims)
            att = _attention(q, kvx, attn_sink[j], dims, need_ctx)
            xn, tok, route, totals = _merge_even(att, gb, u, xparts, modl, sconv_w[j], sconv_b[j][None, :], on, wo_bf,
                                         nf, wr, br, dims, n_out)
        else:
            ux, gg, up = _inproj_odd(xparts, modl, nm, w_in_cd[j].astype(BF16), dims)
            chunk = 2 * LANES
            hf, hr = _lru(ux, lru_conv_w[j], lru_conv_b[j][None, :],
                          _block_diag_chunks(lru_wa[j], chunk), _block_diag_chunks(lru_wx[j], chunk),
                          lru_ba[j][:, None, :], lru_bx[j][:, None, :], lru_lambda[j][:, None, :], dims)
            xn, tok, route, totals = _merge_odd(up, hf, hr, gg, xparts[0], modl, pool_w[j].astype(BF16),
                                        pool_scale[j][None, :], on, wo_bf, nf, wr, br, dims)
        xparts = (_moe(xn, tok, route, totals, modl, exp_w1, exp_w3, exp_w2, i, dims),)
    return xparts[0][:n_lat].reshape(n_batch, seq, d)
```

```python
import functools

import jax
import jax.numpy as jnp
from jax import lax
from jax.experimental import pallas as pl
from jax.experimental.pallas import tpu as pltpu

F32 = jnp.float32
BF16 = jnp.bfloat16
HIGHEST = lax.Precision.HIGHEST
LOG2E = 1.4426950408889634

GRID_W = 64
EPS = 1e-6
HEAD_DIM = 64
N_Q_HEADS = 8
N_KV_HEADS = 2
Q_PER_KV = N_Q_HEADS // N_KV_HEADS
WINDOW = 128
ATT_BLOCK = 128
ROPE_PAIRS = HEAD_DIM // 4
ROPE_BASE = 10000.0
POOL_WINDOWS = (2, 4, 8, 16)
LRU_C = 8.0
N_GROUPS = 4
EXPERTS_PER_GROUP = 8
N_EXPERTS = N_GROUPS * EXPERTS_PER_GROUP
TOP_K = 2
MOE_BLOCK = 512
ZERO_CHUNK = 64
N_MOD = 6

SUBLANES = 8
LANES = 128
SEQ_TILE = 256
TOK_TILE = 512
MERGE_TILE = 512
NEG = -1e30
VMEM_LIMIT = 48 * 1024 * 1024


def _cparams(*sem):
    return pltpu.CompilerParams(dimension_semantics=sem, vmem_limit_bytes=VMEM_LIMIT)


def _rms(x, g):
    ms = jnp.mean(x * x, axis=-1, keepdims=True)
    return x * lax.rsqrt(ms + EPS) * g


def _modulate(x, g, shift, scale):
    return _rms(x, g) * (1.0 + scale) + shift


def _bdot(a, b):
    return jnp.dot(a.astype(BF16), b, preferred_element_type=F32)


def _silu(x):
    return x * jax.nn.sigmoid(x)


def _ada_kernel(c_ref, w_ref, b_ref, o_ref):
    s = _silu(c_ref[...])
    o_ref[0] = jnp.dot(s, w_ref[0], precision=HIGHEST, preferred_element_type=F32) + b_ref[0]


def _ada(cc, ada_w, ada_b):
    depth, d, n = ada_w.shape
    tn = 1536
    return pl.pallas_call(
        _ada_kernel,
        grid=(depth, n // tn),
        in_specs=[pl.BlockSpec((SUBLANES, d), lambda l, j: (0, 0)),
                  pl.BlockSpec((1, d, tn), lambda l, j: (l, 0, j)),
                  pl.BlockSpec((1, 1, tn), lambda l, j: (l, 0, j))],
        out_specs=pl.BlockSpec((1, SUBLANES, tn), lambda l, j: (l, 0, j)),
        out_shape=jax.ShapeDtypeStruct((depth, SUBLANES, n), F32),
        compiler_params=_cparams("arbitrary", "arbitrary"),
        name="ada",
    )(cc, ada_w, ada_b.reshape(depth, 1, n))


def _head_norm_rope(t, pm, g, cos, sin):
    m = t.shape[0]
    t2 = t * t
    hi = t2.astype(BF16)
    lo = (t2 - hi.astype(F32)).astype(BF16)
    ms2 = jnp.dot(jnp.concatenate([hi, lo], axis=0), pm, preferred_element_type=F32)
    ms = ms2[0:m] + ms2[m:2 * m]
    t = t * lax.rsqrt(ms + EPS) * g
    lane = lax.broadcasted_iota(jnp.int32, t.shape, 1)
    first_half = (lane % (2 * ROPE_PAIRS)) < ROPE_PAIRS
    partner = jnp.where(first_half, pltpu.roll(t, LANES - ROPE_PAIRS, 1), pltpu.roll(t, ROPE_PAIRS, 1))
    return t * cos + partner * sin


def _token_operands(xparts, n_lat_tiles, tile):
    d = xparts[0].shape[1]
    lat = pl.BlockSpec((tile, d), lambda j: (jnp.minimum(j, n_lat_tiles - 1), 0))
    if len(xparts) == 2:
        ctx = pl.BlockSpec((tile, d), lambda j: (jnp.maximum(j - n_lat_tiles, 0), 0))
        return list(xparts), [lat, ctx]
    ctx = pl.BlockSpec((tile, d), lambda j: (jnp.maximum(j, n_lat_tiles), 0))
    return [xparts[0], xparts[0]], [lat, ctx]


def _pick_tokens(xl_ref, xc_ref, n_lat_tiles):
    return jnp.where(pl.program_id(0) >= n_lat_tiles, xc_ref[...], xl_ref[...])


def _inproj_even_kernel(xl_ref, xc_ref, mod_ref, g_ref, w_ref, qg_ref, kg_ref, cos_ref, sin_ref, pm_ref,
                        q_ref, kvx_ref, gb_ref, u_ref, *, n_lat_tiles):
    mod = mod_ref[0]
    x = _pick_tokens(xl_ref, xc_ref, n_lat_tiles)
    h = _modulate(x, g_ref[...], mod[0:1], mod[1:2]).astype(BF16)
    cos, sin, pm = cos_ref[...], sin_ref[...], pm_ref[...]
    kv_w = N_KV_HEADS * HEAD_DIM
    att_w = N_Q_HEADS * HEAD_DIM
    sc_w = gb_ref.shape[1]
    kv = jnp.dot(h, w_ref[:, 0:2 * kv_w], preferred_element_type=F32)
    k = _head_norm_rope(kv[:, :kv_w], pm, kg_ref[...], cos, sin)
    v = kv[:, kv_w:]
    kvx_ref[...] = jnp.concatenate([k, pltpu.roll(k, HEAD_DIM, 1), v, pltpu.roll(v, HEAD_DIM, 1)],
                                   axis=-1).astype(BF16)
    q = jnp.dot(h, w_ref[:, 2 * kv_w:2 * kv_w + att_w], preferred_element_type=F32)
    qscale = HEAD_DIM ** -0.5 * LOG2E
    for c in range(att_w // LANES):
        qc = _head_norm_rope(q[:, c * LANES:(c + 1) * LANES], pm, qg_ref[...], cos, sin)
        q_ref[:, c * LANES:(c + 1) * LANES] = (qc * qscale).astype(BF16)
    o = 2 * kv_w + att_w
    gb_ref[...] = jnp.dot(h, w_ref[:, o:o + sc_w], preferred_element_type=F32)
    gc = jnp.dot(h, w_ref[:, o + sc_w:o + 2 * sc_w], preferred_element_type=F32)
    gx = jnp.dot(h, w_ref[:, o + 2 * sc_w:o + 3 * sc_w], preferred_element_type=F32)
    u_ref[...] = gc * gx


def _tile_maps(n_lat_tiles, tiles_per_seq, n_batch):
    def mod_map(j):
        return (jnp.minimum(j // tiles_per_seq, n_batch), 0, 0)

    def pos_map(j):
        return (jnp.where(j < n_lat_tiles, j % tiles_per_seq, tiles_per_seq), 0)

    return mod_map, pos_map


def _inproj_even(xparts, modl, g, w_bf, q_g, k_g, cos_t, sin_t, pm, dims):
    n_tok = sum(p.shape[0] for p in xparts)
    d = xparts[0].shape[1]
    n_batch, seq = dims
    tile = TOK_TILE
    tps = seq // tile
    mod_map, pos_map = _tile_maps(n_batch * tps, tps, n_batch)
    xs, xspecs = _token_operands(xparts, n_batch * tps, tile)
    kv_w, att_w = N_KV_HEADS * HEAD_DIM, N_Q_HEADS * HEAD_DIM
    sc_w = (w_bf.shape[1] - 2 * kv_w - att_w) // 3
    row = lambda j: (j, 0)
    const = lambda j: (0, 0)
    return pl.pallas_call(
        functools.partial(_inproj_even_kernel, n_lat_tiles=n_batch * tps),
        grid=(n_tok // tile,),
        in_specs=xspecs + [
                  pl.BlockSpec((1, SUBLANES, d), mod_map),
                  pl.BlockSpec((1, d), const),
                  pl.BlockSpec(w_bf.shape, const),
                  pl.BlockSpec((1, LANES), const),
                  pl.BlockSpec((1, LANES), const),
                  pl.BlockSpec((tile, LANES), pos_map),
                  pl.BlockSpec((tile, LANES), pos_map),
                  pl.BlockSpec((LANES, LANES), const)],
        out_specs=[pl.BlockSpec((tile, att_w), row),
                   pl.BlockSpec((tile, 4 * kv_w), row),
                   pl.BlockSpec((tile, sc_w), row),
                   pl.BlockSpec((tile, sc_w), row)],
        out_shape=[jax.ShapeDtypeStruct((n_tok, att_w), BF16),
                   jax.ShapeDtypeStruct((n_tok, 4 * kv_w), BF16),
                   jax.ShapeDtypeStruct((n_tok, sc_w), F32),
                   jax.ShapeDtypeStruct((n_tok, sc_w), F32)],
        compiler_params=_cparams("arbitrary"),
        name="inproj_even",
    )(*xs, modl, g, w_bf, q_g, k_g, cos_t, sin_t, pm)


def _attn_heads(q, kvx, mask_fn, sink_ref, o_ref):
    m, n = q.shape[0], kvx.shape[0]
    k_nat, k_swp = kvx[:, 0:LANES], kvx[:, LANES:2 * LANES]
    v_nat, v_swp = kvx[:, 2 * LANES:3 * LANES], kvx[:, 3 * LANES:4 * LANES]
    lo_n = lax.broadcasted_iota(jnp.int32, (n, LANES), 1) < HEAD_DIM
    lo_m = lax.broadcasted_iota(jnp.int32, (2 * m, LANES), 1) < HEAD_DIM
    top = lax.broadcasted_iota(jnp.int32, (2 * m, 1), 0) < m
    zero, one = jnp.zeros_like(k_nat), jnp.ones_like(v_nat)
    outs = []
    for h in range(N_KV_HEADS):
        k_lo, k_hi = (k_nat, k_swp) if h == 0 else (k_swp, k_nat)
        v_lo, v_hi = (v_nat, v_swp) if h == 0 else (v_swp, v_nat)
        g0 = 2 * h
        q2 = jnp.concatenate([q[:, g0 * LANES:(g0 + 1) * LANES],
                              q[:, (g0 + 1) * LANES:(g0 + 2) * LANES]], axis=0)
        parts = []
        for par in range(2):
            kz = jnp.where(lo_n, k_lo, zero) if par == 0 else jnp.where(lo_n, zero, k_hi)
            vz = jnp.where(lo_n, v_lo, one) if par == 0 else jnp.where(lo_n, one, v_hi)
            s = lax.dot_general(q2, kz, (((1,), (1,)), ((), ())), preferred_element_type=F32)
            s = mask_fn(s)
            hq = Q_PER_KV * h + par
            snk = jnp.where(top, sink_ref[hq], sink_ref[hq + 2]) * LOG2E
            mx = jnp.maximum(jnp.max(s, axis=-1, keepdims=True), snk)
            p = jnp.exp2(s - mx)
            parts.append((jnp.dot(p.astype(BF16), vz, preferred_element_type=F32), jnp.exp2(snk - mx)))
        (o_lo, e_lo), (o_hi, e_hi) = parts
        num = jnp.where(lo_m, o_lo, o_hi)
        den = pltpu.roll(jnp.where(lo_m, o_hi, o_lo), HEAD_DIM, 1) + jnp.where(lo_m, e_lo, e_hi)
        o = num / den
        outs += [o[0:m], o[m:2 * m]]
    o_ref[...] = jnp.concatenate(outs, axis=-1)


def _attn_lat_kernel(sink_ref, q_ref, kp_ref, kc_ref, kn_ref, kx_ref, o_ref):
    i = pl.program_id(1)
    nb = pl.num_programs(1)
    kvx = jnp.concatenate([kp_ref[...], kc_ref[...], kn_ref[...], kx_ref[...]], axis=0)
    m = ATT_BLOCK
    tok = lax.broadcasted_iota(jnp.int32, (2 * m, m), 0) % m
    col = lax.broadcasted_iota(jnp.int32, (2 * m, m), 1)
    ok_prev = (col >= tok) & (i > 0)
    ok_next = (col <= tok) & (i < nb - 1)

    def mask_fn(s):
        return jnp.concatenate([jnp.where(ok_prev, s[:, 0:m], NEG), s[:, m:2 * m],
                                jnp.where(ok_next, s[:, 2 * m:3 * m], NEG), s[:, 3 * m:]], axis=1)

    _attn_heads(q_ref[...], kvx, mask_fn, sink_ref, o_ref)


def _attn_ctx_kernel(sink_ref, att_in_ref, q_ref, kx_ref, o_ref):
    del att_in_ref
    _attn_heads(q_ref[...], kx_ref[...], lambda s: s, sink_ref, o_ref)


def _attention(q, kvx, sink, dims, need_ctx):
    n_tok = q.shape[0]
    n_batch, seq = dims
    nb = seq // ATT_BLOCK
    n_lat = n_batch * seq
    ctx_len = (n_tok - n_lat) // n_batch
    att_w, kvx_w = q.shape[1], kvx.shape[1]
    ctx_blk0 = n_lat // ctx_len
    cur = lambda b, i: (b * nb + i, 0)
    prev = lambda b, i: (b * nb + jnp.maximum(i - 1, 0), 0)
    nxt = lambda b, i: (b * nb + jnp.minimum(i + 1, nb - 1), 0)
    cx = lambda b, i: (ctx_blk0 + b, 0)
    smem = pl.BlockSpec(memory_space=pltpu.SMEM)
    kvb = lambda f: pl.BlockSpec((ATT_BLOCK, kvx_w), f)
    att = pl.pallas_call(
        _attn_lat_kernel,
        grid=(n_batch, nb),
        in_specs=[smem, pl.BlockSpec((ATT_BLOCK, att_w), cur),
                  kvb(prev), kvb(cur), kvb(nxt), pl.BlockSpec((ctx_len, kvx_w), cx)],
        out_specs=pl.BlockSpec((ATT_BLOCK, att_w), cur),
        out_shape=jax.ShapeDtypeStruct((n_tok, att_w), F32),
        compiler_params=_cparams("arbitrary", "arbitrary"),
        name="attn_latent",
    )(sink, q, kvx, kvx, kvx, kvx)
    if not need_ctx:
        return att
    cxb = lambda b: (ctx_blk0 + b, 0)
    return pl.pallas_call(
        _attn_ctx_kernel,
        grid=(n_batch,),
        in_specs=[smem, pl.BlockSpec(memory_space=pl.ANY),
                  pl.BlockSpec((ctx_len, att_w), cxb), pl.BlockSpec((ctx_len, kvx_w), cxb)],
        out_specs=pl.BlockSpec((ctx_len, att_w), cxb),
        out_shape=jax.ShapeDtypeStruct((n_tok, att_w), F32),
        input_output_aliases={1: 0},
        compiler_params=_cparams("arbitrary"),
        name="attn_context",
    )(sink, att, q, kvx)


def _route(logits):
    lane = lax.broadcasted_iota(jnp.int32, logits.shape, 1).astype(F32)
    big = float(LANES)
    gl = jnp.where((lane >= N_EXPERTS) & (lane < N_EXPERTS + N_GROUPS), logits, NEG)
    gmax = jnp.max(gl, axis=-1, keepdims=True)
    gidx = jnp.min(jnp.where(gl == gmax, lane, big), axis=-1, keepdims=True) - N_EXPERTS
    p_sel = 1.0 / jnp.sum(jnp.exp(gl - gmax), axis=-1, keepdims=True)
    lo = gidx * EXPERTS_PER_GROUP
    el = jnp.where((lane >= lo) & (lane < lo + EXPERTS_PER_GROUP), logits, NEG)
    m1 = jnp.max(el, axis=-1, keepdims=True)
    i1 = jnp.min(jnp.where(el == m1, lane, big), axis=-1, keepdims=True)
    el2 = jnp.where(lane == i1, NEG, el)
    m2 = jnp.max(el2, axis=-1, keepdims=True)
    i2 = jnp.min(jnp.where(el2 == m2, lane, big), axis=-1, keepdims=True)
    t = jnp.exp(m2 - m1)
    w1 = p_sel / (1.0 + t)
    w2 = p_sel * t / (1.0 + t)
    return jnp.where(lane == 0, i1, jnp.where(lane == 1, i2, jnp.where(lane == 2, w1,
                     jnp.where(lane == 3, w2, 0.0))))


def _tail(n1, n2, gain, wo_ref, x, mod, nf_ref, wr_ref, br_ref, xo_ref, tok_ref, route_ref, cnt_ref):
    half = n1.shape[1]
    y = (_bdot(_rms(n1, gain[:, :half]), wo_ref[0:half, :])
         + _bdot(_rms(n2, gain[:, half:]), wo_ref[half:, :]))
    xn = x + mod[2:3] * y
    xo_ref[...] = xn
    tok = _modulate(xn, nf_ref[...], mod[3:4], mod[4:5])
    tok_ref[...] = tok
    t_hi = tok.astype(BF16)
    t_lo = (tok - t_hi.astype(F32)).astype(BF16)
    hh = jnp.dot(t_hi, wr_ref[...], preferred_element_type=F32)
    logits = (hh[:, :LANES] + hh[:, LANES:] + br_ref[...]
              + jnp.dot(t_lo, wr_ref[:, 0:LANES], preferred_element_type=F32))
    route = _route(logits)
    route_ref[...] = route
    lane = lax.broadcasted_iota(jnp.int32, route.shape, 1).astype(F32)
    hit = jnp.where((lane == route[:, 0:1]) | (lane == route[:, 1:2]), 1.0, 0.0)

    @pl.when(pl.program_id(0) == 0)
    def _():
        cnt_ref[...] = jnp.zeros_like(cnt_ref)

    cnt_ref[0:1, :] += jnp.sum(hit, axis=0, keepdims=True)


def _seq_flags(j, n_lat_tiles, tiles_per_seq):
    is_ctx = j >= n_lat_tiles
    first = is_ctx | (j % tiles_per_seq == 0)
    last = is_ctx | (j % tiles_per_seq == tiles_per_seq - 1)
    return first, last


def _with_halo(prev_ref, tile, next_ref, first, last):
    p = jnp.where(first, 0.0, prev_ref[...])
    n = jnp.where(last, 0.0, next_ref[...])
    return jnp.concatenate([p, tile, n], axis=0)


def _shift_rows(e, k):
    return pltpu.roll(e, (-k) % e.shape[0], 0)


def _merge_even_kernel(att_ref, gb_ref, u_ref, up_ref, un_ref, xl_ref, xc_ref, mod_ref, cw_ref, cb_ref,
                       on_ref, wo_ref, nf_ref, wr_ref, br_ref,
                       xo_ref, tok_ref, route_ref, cnt_ref, *, n_lat_tiles, tiles_per_seq):
    j = pl.program_id(0)
    first, last = _seq_flags(j, n_lat_tiles, tiles_per_seq)
    e = _with_halo(up_ref, u_ref[...], un_ref, first, last)
    cw = cw_ref[...]
    lo, hi = SUBLANES, SUBLANES + MERGE_TILE
    pos = lax.broadcasted_iota(jnp.int32, (MERGE_TILE, 1), 0) % SEQ_TILE
    is_ctx = j >= n_lat_tiles
    before = jnp.where(is_ctx & (pos == 0), 0.0, _shift_rows(e, -1)[lo:hi])
    after = jnp.where(is_ctx & (pos == SEQ_TILE - 1), 0.0, _shift_rows(e, 1)[lo:hi])
    conv = cw[0:1] * before + cw[1:2] * e[lo:hi] + cw[2:3] * after + cb_ref[...]
    conv = gb_ref[...] * conv
    x = _pick_tokens(xl_ref, xc_ref, n_lat_tiles)
    _tail(att_ref[...], conv, on_ref[...], wo_ref, x, mod_ref[0], nf_ref, wr_ref, br_ref,
          xo_ref, tok_ref, route_ref, cnt_ref)


def _halo_specs(width, n_rows):
    r = MERGE_TILE // SUBLANES
    n_blk = n_rows // SUBLANES
    prev = pl.BlockSpec((SUBLANES, width), lambda j: (jnp.maximum(j * r - 1, 0), 0))
    nxt = pl.BlockSpec((SUBLANES, width), lambda j: (jnp.minimum((j + 1) * r, n_blk - 1), 0))
    return prev, nxt


def _tail_specs(d, n_out_tok):
    row = lambda j: (j, 0)
    out_specs = [pl.BlockSpec((MERGE_TILE, d), row), pl.BlockSpec((MERGE_TILE, d), row),
                 pl.BlockSpec((MERGE_TILE, LANES), row), pl.BlockSpec((SUBLANES, LANES), lambda j: (0, 0))]
    out_shape = [jax.ShapeDtypeStruct((n_out_tok, d), F32), jax.ShapeDtypeStruct((n_out_tok, d), F32),
                 jax.ShapeDtypeStruct((n_out_tok, LANES), F32), jax.ShapeDtypeStruct((SUBLANES, LANES), F32)]
    return out_specs, out_shape


def _merge_even(att, gb, u, xparts, modl, cw, cb, on, wo_bf, nf, wr, br, dims, n_out_tok):
    n_tok = u.shape[0]
    d = xparts[0].shape[1]
    n_batch, seq = dims
    tps = seq // MERGE_TILE
    mod_map, _ = _tile_maps(n_batch * tps, tps, n_batch)
    xs, xspecs = _token_operands(xparts, n_batch * tps, MERGE_TILE)
    sc_w = u.shape[1]
    row = lambda j: (j, 0)
    const = lambda j: (0, 0)
    prev, nxt = _halo_specs(sc_w, n_tok)
    out_specs, out_shape = _tail_specs(d, n_out_tok)
    return pl.pallas_call(
        functools.partial(_merge_even_kernel, n_lat_tiles=n_batch * tps, tiles_per_seq=tps),
        grid=(n_out_tok // MERGE_TILE,),
        in_specs=[pl.BlockSpec((MERGE_TILE, att.shape[1]), row),
                  pl.BlockSpec((MERGE_TILE, sc_w), row),
                  pl.BlockSpec((MERGE_TILE, sc_w), row), prev, nxt] + xspecs + [
                  pl.BlockSpec((1, SUBLANES, d), mod_map),
                  pl.BlockSpec(cw.shape, const), pl.BlockSpec((1, sc_w), const),
                  pl.BlockSpec((1, d), const), pl.BlockSpec(wo_bf.shape, const),
                  pl.BlockSpec((1, d), const), pl.BlockSpec(wr.shape, const),
                  pl.BlockSpec((1, LANES), const)],
        out_specs=out_specs, out_shape=out_shape,
        compiler_params=_cparams("arbitrary"),
        name="merge_even",
    )(att, gb, u, u, u, *xs, modl, cw, cb, on, wo_bf, nf, wr, br)


def _inproj_odd_kernel(xl_ref, xc_ref, mod_ref, g_ref, w_ref, ux_ref, gg_ref, up_ref, *, n_lat_tiles):
    mod = mod_ref[0]
    x = _pick_tokens(xl_ref, xc_ref, n_lat_tiles)
    h = _modulate(x, g_ref[...], mod[0:1], mod[1:2]).astype(BF16)
    w = ux_ref.shape[1]
    ux_ref[...] = jnp.dot(h, w_ref[:, 0:w], preferred_element_type=F32)
    ug = jnp.dot(h, w_ref[:, w:2 * w], preferred_element_type=F32)
    gg_ref[...] = jax.nn.gelu(ug, approximate=True)
    up_ref[...] = jnp.dot(h, w_ref[:, 2 * w:3 * w], preferred_element_type=F32)


def _inproj_odd(xparts, modl, g, w_bf, dims):
    n_tok = sum(p.shape[0] for p in xparts)
    d = xparts[0].shape[1]
    n_batch, seq = dims
    tile = TOK_TILE
    tps = seq // tile
    mod_map, _ = _tile_maps(n_batch * tps, tps, n_batch)
    xs, xspecs = _token_operands(xparts, n_batch * tps, tile)
    w = w_bf.shape[1] // 3
    row = lambda j: (j, 0)
    const = lambda j: (0, 0)
    return pl.pallas_call(
        functools.partial(_inproj_odd_kernel, n_lat_tiles=n_batch * tps),
        grid=(n_tok // tile,),
        in_specs=xspecs + [
                  pl.BlockSpec((1, SUBLANES, d), mod_map),
                  pl.BlockSpec((1, d), const),
                  pl.BlockSpec(w_bf.shape, const)],
        out_specs=[pl.BlockSpec((tile, w), row)] * 3,
        out_shape=[jax.ShapeDtypeStruct((n_tok, w), F32)] * 3,
        compiler_params=_cparams("arbitrary"),
        name="inproj_odd",
    )(*xs, modl, g, w_bf)


def _lru_conv(e, cw, cb):
    lo, hi = SUBLANES, SUBLANES + SEQ_TILE
    return (cw[0:1] * _shift_rows(e, -2)[lo:hi] + cw[1:2] * _shift_rows(e, -1)[lo:hi]
            + cw[2:3] * e[lo:hi] + cw[3:4] * _shift_rows(e, 1)[lo:hi] + cb)


def _lru_coeffs(xl, d, wa_ref, wx_ref, ba_ref, bx_ref, lam_ref, a_ref, b_ref):
    xb = xl.astype(BF16)
    chunk = wa_ref.shape[-1]
    n_chunk = xl.shape[1] // chunk
    pre_a = jnp.concatenate([jnp.dot(xb[:, c * chunk:(c + 1) * chunk], wa_ref[d, c],
                                     preferred_element_type=F32) for c in range(n_chunk)], axis=-1)
    pre_x = jnp.concatenate([jnp.dot(xb[:, c * chunk:(c + 1) * chunk], wx_ref[d, c],
                                     preferred_element_type=F32) for c in range(n_chunk)], axis=-1)
    tr = jnp.tanh(pre_a + ba_ref[d])
    tg = jnp.tanh(pre_x + bx_ref[d])
    c_half = (-0.5 * LRU_C) * jax.nn.softplus(-lam_ref[d])
    log_a = c_half * tr + c_half
    a_ref[...] = jnp.exp(log_a)
    th = jnp.tanh(log_a)
    b_ref[...] = jnp.sqrt(-0.5 * th / (1.0 - th)) * (1.0 + tg) * xl


def _scan_tile(a_ref, b_ref, h_ref, carry, reverse):
    n_grp = a_ref.shape[0] // SUBLANES
    row = lax.broadcasted_iota(jnp.int32, (SUBLANES, a_ref.shape[1]), 0)

    def body(g, carry):
        gi = (n_grp - 1 - g) if reverse else g
        r0 = pl.multiple_of(gi * SUBLANES, SUBLANES)
        a8 = a_ref[pl.ds(r0, SUBLANES), :]
        b8 = b_ref[pl.ds(r0, SUBLANES), :]
        for s in (1, 2, 4):
            if reverse:
                ok = row < SUBLANES - s
                sh = SUBLANES - s
            else:
                ok = row >= s
                sh = s
            a_sh = jnp.where(ok, pltpu.roll(a8, sh, 0), 1.0)
            b_sh = jnp.where(ok, pltpu.roll(b8, sh, 0), 0.0)
            b8 = a8 * b_sh + b8
            a8 = a8 * a_sh
        h8 = a8 * carry + b8
        if h_ref is not None:
            h_ref[pl.ds(r0, SUBLANES), :] = h8
        return h8[0:1] if reverse else h8[SUBLANES - 1:SUBLANES]

    return lax.fori_loop(0, n_grp, body, carry, unroll=4)


def _lru_kernel(uf_ref, ufp_ref, ufn_ref, ur_ref, urp_ref, urn_ref, uc_ref, cw_ref, cb_ref,
                wa_ref, wx_ref, ba_ref, bx_ref, lam_ref, hf_ref, hr_ref,
                cf_ref, cr_ref, a_ref, b_ref):
    i = pl.program_id(1)
    nt = pl.num_programs(1)
    cw, cb = cw_ref[...], cb_ref[...]
    coeffs = functools.partial(_lru_coeffs, wa_ref=wa_ref, wx_ref=wx_ref, ba_ref=ba_ref,
                               bx_ref=bx_ref, lam_ref=lam_ref, a_ref=a_ref, b_ref=b_ref)

    @pl.when(i == 0)
    def _():
        z = jnp.zeros((SUBLANES, uc_ref.shape[1]), F32)
        xc = _lru_conv(jnp.concatenate([z, uc_ref[...], z], axis=0), cw, cb)
        zero = jnp.zeros((1, uc_ref.shape[1]), F32)
        coeffs(xc, 0)
        cf_ref[...] = _scan_tile(a_ref, b_ref, None, zero, False)
        coeffs(xc, 1)
        cr_ref[...] = _scan_tile(a_ref, b_ref, None, zero, True)

    xf = _lru_conv(_with_halo(ufp_ref, uf_ref[...], ufn_ref, i == 0, i == nt - 1), cw, cb)
    coeffs(xf, 0)
    cf_ref[...] = _scan_tile(a_ref, b_ref, hf_ref, cf_ref[...], False)
    xr = _lru_conv(_with_halo(urp_ref, ur_ref[...], urn_ref, i == nt - 1, i == 0), cw, cb)
    coeffs(xr, 1)
    cr_ref[...] = _scan_tile(a_ref, b_ref, hr_ref, cr_ref[...], True)


def _lru(ux, cw, cb, wa_bd, wx_bd, ba, bx, lam, dims):
    n_tok, w = ux.shape
    n_batch, seq = dims
    nt = seq // SEQ_TILE
    n_lat = n_batch * seq
    r = SEQ_TILE // SUBLANES
    n_blk = n_tok // SUBLANES
    fwd = lambda b, i: b * nt + i
    rev = lambda b, i: b * nt + nt - 1 - i

    def specs(tile):
        return [pl.BlockSpec((SEQ_TILE, w), lambda b, i: (tile(b, i), 0)),
                pl.BlockSpec((SUBLANES, w), lambda b, i: (jnp.maximum(tile(b, i) * r - 1, 0), 0)),
                pl.BlockSpec((SUBLANES, w), lambda b, i: (jnp.minimum((tile(b, i) + 1) * r, n_blk - 1), 0))]

    const2 = lambda b, i: (0, 0)
    const3 = lambda b, i: (0, 0, 0)
    const4 = lambda b, i: (0, 0, 0, 0)
    return pl.pallas_call(
        _lru_kernel,
        grid=(n_batch, nt),
        in_specs=specs(fwd) + specs(rev) + [
            pl.BlockSpec((SEQ_TILE, w), lambda b, i: (n_lat // SEQ_TILE + b, 0)),
            pl.BlockSpec(cw.shape, const2), pl.BlockSpec((1, w), const2),
            pl.BlockSpec(wa_bd.shape, const4), pl.BlockSpec(wx_bd.shape, const4),
            pl.BlockSpec(ba.shape, const3), pl.BlockSpec(bx.shape, const3),
            pl.BlockSpec(lam.shape, const3)],
        out_specs=[pl.BlockSpec((SEQ_TILE, w), lambda b, i: (fwd(b, i), 0)),
                   pl.BlockSpec((SEQ_TILE, w), lambda b, i: (rev(b, i), 0))],
        out_shape=[jax.ShapeDtypeStruct((n_lat, w), F32)] * 2,
        scratch_shapes=[pltpu.VMEM((1, w), F32), pltpu.VMEM((1, w), F32),
                        pltpu.VMEM((SEQ_TILE, w), F32), pltpu.VMEM((SEQ_TILE, w), F32)],
        compiler_params=_cparams("arbitrary", "arbitrary"),
        name="rglru",
    )(ux, ux, ux, ux, ux, ux, ux, cw, cb, wa_bd, wx_bd, ba, bx, lam)


def _merge_odd_kernel(up_ref, upp_ref, upn_ref, hf_ref, hr_ref, gg_ref, x_ref, mod_ref,
                      pw_ref, ps_ref, on_ref, wo_ref, nf_ref, wr_ref, br_ref,
                      xo_ref, tok_ref, route_ref, cnt_ref, *, tiles_per_seq, seq):
    j = pl.program_id(0)
    ts = j % tiles_per_seq
    e = _with_halo(upp_ref, up_ref[...], upn_ref, ts == 0, ts == tiles_per_seq - 1)
    gw = pw_ref.shape[-1]
    lo, hi = SUBLANES, SUBLANES + MERGE_TILE
    tpos = ts * MERGE_TILE + lax.broadcasted_iota(jnp.int32, (MERGE_TILE, 1), 0)
    parts = []
    for g, win in enumerate(POOL_WINDOWS):
        eg = e[:, g * gw:(g + 1) * gw]
        p = eg
        span = 1
        while span < win:
            p = p + _shift_rows(p, span)
            span *= 2
        half = win // 2
        s = _shift_rows(p, -half)[lo:hi]
        cnt = (jnp.minimum(tpos + half, seq) - jnp.maximum(tpos - half, 0)).astype(F32)
        dlt = s / cnt - eg[lo:hi]
        parts.append(_bdot(dlt, pw_ref[g]))
    pool = jnp.concatenate(parts, axis=-1) * ps_ref[...]
    lru = (hf_ref[...] + hr_ref[...]) * gg_ref[...]
    _tail(pool, lru, on_ref[...], wo_ref, x_ref[...], mod_ref[0], nf_ref, wr_ref, br_ref,
          xo_ref, tok_ref, route_ref, cnt_ref)


def _merge_odd(up, hf, hr, gg, xf, modl, pw_bf, ps, on, wo_bf, nf, wr, br, dims):
    d = xf.shape[1]
    n_tok, w = up.shape
    n_batch, seq = dims
    tps = seq // MERGE_TILE
    n_lat = n_batch * seq
    mod_map, _ = _tile_maps(n_batch * tps, tps, n_batch)
    row = lambda j: (j, 0)
    const = lambda j: (0, 0)
    prev, nxt = _halo_specs(w, n_tok)
    out_specs, out_shape = _tail_specs(d, n_lat)
    return pl.pallas_call(
        functools.partial(_merge_odd_kernel, tiles_per_seq=tps, seq=seq),
        grid=(n_lat // MERGE_TILE,),
        in_specs=[pl.BlockSpec((MERGE_TILE, w), row), prev, nxt,
                  pl.BlockSpec((MERGE_TILE, w), row), pl.BlockSpec((MERGE_TILE, w), row),
                  pl.BlockSpec((MERGE_TILE, w), row),
                  pl.BlockSpec((MERGE_TILE, d), row),
                  pl.BlockSpec((1, SUBLANES, d), mod_map),
                  pl.BlockSpec(pw_bf.shape, lambda j: (0, 0, 0)), pl.BlockSpec((1, w), const),
                  pl.BlockSpec((1, d), const), pl.BlockSpec(wo_bf.shape, const),
                  pl.BlockSpec((1, d), const), pl.BlockSpec(wr.shape, const),
                  pl.BlockSpec((1, LANES), const)],
        out_specs=out_specs, out_shape=out_shape,
        compiler_params=_cparams("arbitrary"),
        name="merge_odd",
    )(up, up, up, hf, hr, gg, xf, modl, pw_bf, ps, on, wo_bf, nf, wr, br)


def _slots_kernel(route_ref, tot_ref, dest_ref, cnt_ref, run_ref, start_ref):
    i = pl.program_id(0)
    rt = route_ref[...]
    m = rt.shape[0]
    lane = lax.broadcasted_iota(jnp.int32, rt.shape, 1).astype(F32)
    oh1 = lane == rt[:, 0:1]
    oh2 = lane == rt[:, 1:2]
    s = jnp.where(oh1 | oh2, 1.0, 0.0)

    @pl.when(i == 0)
    def _():
        tot = tot_ref[0:1, :]
        padded = jnp.floor((tot + (MOE_BLOCK - 1)) / MOE_BLOCK) * MOE_BLOCK
        r = lax.broadcasted_iota(jnp.int32, (LANES, LANES), 0)
        c = lax.broadcasted_iota(jnp.int32, (LANES, LANES), 1)
        upper = jnp.where(r < c, 1.0, 0.0)
        p8 = jnp.broadcast_to(padded, (SUBLANES, LANES))
        start = jnp.dot(p8, upper, precision=HIGHEST, preferred_element_type=F32)[0:1]
        start_ref[...] = start
        run_ref[...] = jnp.zeros_like(run_ref)
        sub = lax.broadcasted_iota(jnp.int32, cnt_ref.shape, 0)
        cnt_ref[...] = jnp.where(sub == 0, tot, jnp.where(sub == 1, start, 0.0))

    r = lax.broadcasted_iota(jnp.int32, (m, m), 0)
    c = lax.broadcasted_iota(jnp.int32, (m, m), 1)
    lower = jnp.where(c < r, 1.0, 0.0).astype(BF16)
    prefix = jnp.dot(lower, s.astype(BF16), preferred_element_type=F32)
    base = prefix + run_ref[...] + start_ref[...]
    d1 = jnp.sum(jnp.where(oh1, base, 0.0), axis=-1, keepdims=True)
    d2 = jnp.sum(jnp.where(oh2, base, 0.0), axis=-1, keepdims=True)
    dest_ref[...] = jnp.where(lane == 0, d1, jnp.where(lane == 1, d2, 0.0)).astype(jnp.int32)
    run_ref[...] += jnp.sum(s, axis=0, keepdims=True)


def _slots(route, totals):
    n_tok = route.shape[0]
    nt = n_tok // TOK_TILE
    return pl.pallas_call(
        _slots_kernel,
        grid=(nt,),
        in_specs=[pl.BlockSpec((TOK_TILE, LANES), lambda i: (i, 0)),
                  pl.BlockSpec((SUBLANES, LANES), lambda i: (0, 0))],
        out_specs=[pl.BlockSpec((TOK_TILE, LANES), lambda i: (i, 0)),
                   pl.BlockSpec((SUBLANES, LANES), lambda i: (0, 0))],
        out_shape=[jax.ShapeDtypeStruct((n_tok, LANES), jnp.int32),
                   jax.ShapeDtypeStruct((SUBLANES, LANES), F32)],
        scratch_shapes=[pltpu.VMEM((1, LANES), F32)] * 2,
        compiler_params=_cparams("arbitrary"),
        name="moe_slots",
    )(route, totals)


def _for_each_row(n_rows, fn):
    def body(g, c):
        g8 = pl.multiple_of(g * SUBLANES, SUBLANES)
        for j in range(SUBLANES):
            fn(g8, j)
        return c
    lax.fori_loop(0, n_rows // SUBLANES, body, 0)


def _dispatch_kernel(dest_ref, pad_lo_ref, pad_hi_ref, tok_ref, xs_ref, zrow, sem, zsem):
    i = pl.program_id(0)
    base = i * TOK_TILE

    def zero_copy(r, n):
        return pltpu.make_async_copy(zrow.at[pl.ds(0, n)], xs_ref.at[pl.ds(r, n)], zsem)

    def for_each_pad_chunk(fn):
        def per_expert(e, c):
            lo, hi = pad_lo_ref[e], pad_hi_ref[e]
            a8 = (lo + SUBLANES - 1) // SUBLANES * SUBLANES
            a64 = (a8 + ZERO_CHUNK - 1) // ZERO_CHUNK * ZERO_CHUNK

            def rows(r, c):
                fn(zero_copy(r, 1))
                return c

            def eights(q, c):
                fn(zero_copy(pl.multiple_of(a8 + q * SUBLANES, SUBLANES), SUBLANES))
                return c

            def chunks(q, c):
                fn(zero_copy(pl.multiple_of(a64 + q * ZERO_CHUNK, ZERO_CHUNK), ZERO_CHUNK))
                return c

            lax.fori_loop(lo, a8, rows, 0)
            lax.fori_loop(0, (a64 - a8) // SUBLANES, eights, 0)
            lax.fori_loop(0, (hi - a64) // ZERO_CHUNK, chunks, 0)
            return c
        lax.fori_loop(0, N_EXPERTS, per_expert, 0)

    @pl.when(i == 0)
    def _():
        zrow[...] = jnp.zeros_like(zrow)
        for_each_pad_chunk(lambda cp: cp.start())

    def issue(g8, j):
        rows = tok_ref.at[pl.ds(g8, SUBLANES)]
        for k in range(TOP_K):
            d = dest_ref[(base + g8) * TOP_K + (j * TOP_K + k)]
            pltpu.make_async_copy(rows.at[pl.ds(j, 1)], xs_ref.at[pl.ds(d, 1)], sem).start(priority=k)

    _for_each_row(TOK_TILE, issue)
    for k in range(TOP_K):
        pltpu.make_async_copy(tok_ref, xs_ref.at[pl.ds(0, TOK_TILE)], sem).wait()

    @pl.when(i == pl.num_programs(0) - 1)
    def _():
        for_each_pad_chunk(lambda cp: cp.wait())


def _dispatch(dest_flat, pad_lo, pad_hi, tok, n_slots):
    n_tok, d = tok.shape
    return pl.pallas_call(
        _dispatch_kernel,
        grid_spec=pltpu.PrefetchScalarGridSpec(
            num_scalar_prefetch=3,
            grid=(n_tok // TOK_TILE,),
            in_specs=[pl.BlockSpec((TOK_TILE, d), lambda i, *_: (i, 0))],
            out_specs=pl.BlockSpec(memory_space=pl.ANY),
            scratch_shapes=[pltpu.VMEM((ZERO_CHUNK, d), F32),
                            pltpu.SemaphoreType.DMA(()), pltpu.SemaphoreType.DMA(())]),
        out_shape=jax.ShapeDtypeStruct((n_slots, d), F32),
        compiler_params=_cparams("arbitrary"),
        name="moe_dispatch",
    )(dest_flat, pad_lo, pad_hi, tok)


def _expert_kernel(be_ref, nu_ref, x_ref, w1_ref, w3_ref, w2_ref, y_ref, w1b, w3b, w2b):
    i = pl.program_id(0)

    @pl.when(i < nu_ref[0])
    def _():
        changed = (i == 0) | (be_ref[i] != be_ref[jnp.maximum(i - 1, 0)])

        @pl.when(changed)
        def _():
            w1b[...] = w1_ref[0, 0].astype(BF16)
            w3b[...] = w3_ref[0, 0].astype(BF16)
            w2b[...] = w2_ref[0, 0].astype(BF16)

        x = x_ref[...].astype(BF16)
        a = jnp.dot(x, w1b[...], preferred_element_type=F32)
        b = jnp.dot(x, w3b[...], preferred_element_type=F32)
        y_ref[...] = _bdot(_silu(a) * b, w2b[...])


def _experts(block_e, n_used, xs, w1, w3, w2, layer):
    n_slots, d = xs.shape
    de = w1.shape[-1]
    n_blocks = n_slots // MOE_BLOCK
    blk = lambda i, be, nu: (jnp.minimum(i, nu[0] - 1), 0)
    wmap = lambda i, be, nu: (layer, be[i], 0, 0)
    return pl.pallas_call(
        _expert_kernel,
        grid_spec=pltpu.PrefetchScalarGridSpec(
            num_scalar_prefetch=2,
            grid=(n_blocks,),
            in_specs=[pl.BlockSpec((MOE_BLOCK, d), blk),
                      pl.BlockSpec((1, 1, d, de), wmap), pl.BlockSpec((1, 1, d, de), wmap),
                      pl.BlockSpec((1, 1, de, d), wmap)],
            out_specs=pl.BlockSpec((MOE_BLOCK, d), blk),
            scratch_shapes=[pltpu.VMEM((d, de), BF16), pltpu.VMEM((d, de), BF16),
                            pltpu.VMEM((de, d), BF16)]),
        out_shape=jax.ShapeDtypeStruct((n_slots, d), F32),
        compiler_params=_cparams("arbitrary"),
        name="moe_experts",
    )(block_e, n_used, xs, w1, w3, w2)


def _combine_kernel(dest_ref, x_ref, route_ref, mod_ref, ys_ref, o_ref, ybuf, sem):
    base = pl.program_id(0) * TOK_TILE

    def issue(g8, j):
        for k in range(TOP_K):
            d = dest_ref[(base + g8) * TOP_K + (j * TOP_K + k)]
            rows = ybuf.at[k, pl.ds(g8, SUBLANES)]
            pltpu.make_async_copy(ys_ref.at[pl.ds(d, 1)], rows.at[pl.ds(j, 1)], sem).start(priority=k)

    _for_each_row(TOK_TILE, issue)
    for k in range(TOP_K):
        pltpu.make_async_copy(ys_ref.at[pl.ds(0, TOK_TILE)], ybuf.at[k], sem).wait()
    rt = route_ref[...]
    y = rt[:, 2:3] * ybuf[0] + rt[:, 3:4] * ybuf[1]
    o_ref[...] = x_ref[...] + mod_ref[0][5:6] * y


def _combine(dest_flat, xn, route, modl, ys, dims):
    n_tok, d = xn.shape
    n_batch, seq = dims
    tpb = seq // TOK_TILE
    return pl.pallas_call(
        _combine_kernel,
        grid_spec=pltpu.PrefetchScalarGridSpec(
            num_scalar_prefetch=1,
            grid=(n_tok // TOK_TILE,),
            in_specs=[pl.BlockSpec((TOK_TILE, d), lambda i, dest: (i, 0)),
                      pl.BlockSpec((TOK_TILE, LANES), lambda i, dest: (i, 0)),
                      pl.BlockSpec((1, SUBLANES, d), lambda i, dest: (jnp.minimum(i // tpb, n_batch), 0, 0)),
                      pl.BlockSpec(memory_space=pl.ANY)],
            out_specs=pl.BlockSpec((TOK_TILE, d), lambda i, dest: (i, 0)),
            scratch_shapes=[pltpu.VMEM((TOP_K, TOK_TILE, d), F32), pltpu.SemaphoreType.DMA(())]),
        out_shape=jax.ShapeDtypeStruct((n_tok, d), F32),
        compiler_params=_cparams("arbitrary"),
        name="moe_combine",
    )(dest_flat, xn, route, modl, ys)


def _moe(xn, tok, route, totals, modl, w1, w3, w2, layer, dims):
    n_tok = tok.shape[0]
    n = n_tok * TOP_K
    n_blocks = (n + N_EXPERTS * (MOE_BLOCK - 1) + MOE_BLOCK - 1) // MOE_BLOCK
    dest, counts = _slots(route, totals)
    dest_flat = dest[:, :TOP_K].reshape(-1)
    cnt = counts[0, :N_EXPERTS].astype(jnp.int32)
    start = counts[1, :N_EXPERTS].astype(jnp.int32)
    blocks_per_e = (cnt + MOE_BLOCK - 1) // MOE_BLOCK
    blk_end = (start + blocks_per_e * MOE_BLOCK) // MOE_BLOCK
    n_used = blk_end[-1:]
    blk = jnp.minimum(jnp.arange(n_blocks, dtype=jnp.int32), n_used[0] - 1)
    block_e = jnp.sum((blk[:, None] >= blk_end[None, :]).astype(jnp.int32), axis=1)
    block_e = jnp.minimum(block_e, N_EXPERTS - 1)
    xs = _dispatch(dest_flat, start + cnt, blk_end * MOE_BLOCK, tok, n_blocks * MOE_BLOCK)
    ys = _experts(block_e, n_used, xs, w1, w3, w2, layer)
    return _combine(dest_flat, xn, route, modl, ys, dims)


def _rope_tables(seq):
    rows = seq // GRID_W
    row = jnp.repeat(jnp.arange(rows), GRID_W).astype(F32)
    col = jnp.tile(jnp.arange(GRID_W), rows).astype(F32)
    inv_freq = ROPE_BASE ** (-jnp.arange(ROPE_PAIRS, dtype=F32) / ROPE_PAIRS)
    ar, ac = row[:, None] * inv_freq, col[:, None] * inv_freq
    cos = jnp.concatenate([jnp.cos(ar), jnp.cos(ar), jnp.cos(ac), jnp.cos(ac)], axis=-1)
    sin = jnp.concatenate([-jnp.sin(ar), jnp.sin(ar), -jnp.sin(ac), jnp.sin(ac)], axis=-1)
    reps = LANES // HEAD_DIM
    cos, sin = jnp.tile(cos, (1, reps)), jnp.tile(sin, (1, reps))
    cos = jnp.concatenate([cos, jnp.ones((TOK_TILE, LANES), F32)], axis=0)
    sin = jnp.concatenate([sin, jnp.zeros((TOK_TILE, LANES), F32)], axis=0)
    return cos, sin


def _head_mean_matrix():
    r = jnp.arange(LANES)
    return jnp.where((r[:, None] // HEAD_DIM) == (r[None, :] // HEAD_DIM), 1.0 / HEAD_DIM, 0.0).astype(BF16)


def _block_diag_chunks(w, chunk):
    dirs, nblk, bw, _ = w.shape
    per = chunk // bw
    w = w.reshape(dirs, nblk // per, per, bw, bw)
    eye = jnp.eye(per, dtype=w.dtype)
    out = jnp.einsum("dcpij,pq->dcpiqj", w, eye)
    return out.reshape(dirs, nblk // per, chunk, chunk).astype(BF16)


def _router_matrix(gw, gb, ew, eb):
    d = gw.shape[0]
    pad = LANES - N_EXPERTS - N_GROUPS
    wr = jnp.concatenate([ew, gw, jnp.zeros((d, pad), F32)], axis=1)
    br = jnp.concatenate([eb, gb, jnp.zeros((pad,), F32)])[None, :]
    hi = wr.astype(BF16)
    lo = (wr - hi.astype(F32)).astype(BF16)
    return jnp.concatenate([hi, lo], axis=1), br


def kernel(x, c, ctx, c_ctx, ada_w, ada_b, norm_mix, norm_ffn, out_norm, w_out, w_in_ab, q_norm, k_norm, attn_sink, sconv_w, sconv_b, w_in_cd, pool_w, pool_scale, lru_conv_w, lru_conv_b, lru_wa, lru_ba, lru_wx, lru_bx, lru_lambda, router_gw, router_gb, router_ew, router_eb, exp_w1, exp_w3, exp_w2):
    n_batch, seq, d = x.shape
    ctx_len = ctx.shape[1]
    depth = ada_w.shape[0]
    assert ctx_len == SEQ_TILE and seq % TOK_TILE == 0 and seq % GRID_W == 0
    assert seq % MERGE_TILE == 0 and (n_batch * ctx_len) % MERGE_TILE == 0 and MERGE_TILE % SEQ_TILE == 0
    assert depth == 2, "context-side odd-layer outputs are not implemented"
    dims = (n_batch, seq)
    n_lat = n_batch * seq

    cc = jnp.concatenate([c, c_ctx[None, :], jnp.zeros((SUBLANES - n_batch - 1, d), F32)], axis=0)
    mods = _ada(cc, ada_w, ada_b)
    mods = mods[:, :n_batch + 1].reshape(depth, n_batch + 1, N_MOD, d)
    mods = jnp.pad(mods, ((0, 0), (0, 0), (0, SUBLANES - N_MOD), (0, 0)))

    xparts = (x.reshape(n_lat, d), ctx.reshape(n_batch * ctx_len, d))
    cos_t, sin_t = _rope_tables(seq)
    pm = _head_mean_matrix()

    for i in range(depth):
        need_ctx = i < depth - 1
        j = i // 2
        modl = mods[i]
        n_out = n_lat + n_batch * ctx_len if need_ctx else n_lat
        wr, br = _router_matrix(router_gw[i], router_gb[i], router_ew[i], router_eb[i])
        wo_bf = w_out[i].astype(BF16)
        nm, nf, on = norm_mix[i][None, :], norm_ffn[i][None, :], out_norm[i][None, :]
        if i % 2 == 0:
            reps = LANES // HEAD_DIM
            q, kvx, gb, u = _inproj_even(xparts, modl, nm, w_in_ab[j].astype(BF16),
                                          jnp.tile(q_norm[j], reps)[None, :], jnp.tile(k_norm[j], reps)[None, :],
                                          cos_t, sin_t, pm, dims)
            att = _attention(q, kvx, attn_sink[j], dims, need_ctx)
            xn, tok, route, totals = _merge_even(att, gb, u, xparts, modl, sconv_w[j], sconv_b[j][None, :], on, wo_bf,
                                         nf, wr, br, dims, n_out)
        else:
            ux, gg, up = _inproj_odd(xparts, modl, nm, w_in_cd[j].astype(BF16), dims)
            chunk = 2 * LANES
            hf, hr = _lru(ux, lru_conv_w[j], lru_conv_b[j][None, :],
                          _block_diag_chunks(0.5 * lru_wa[j], chunk), _block_diag_chunks(0.5 * lru_wx[j], chunk),
                          0.5 * lru_ba[j][:, None, :], 0.5 * lru_bx[j][:, None, :],
                          lru_lambda[j][:, None, :], dims)
            xn, tok, route, totals = _merge_odd(up, hf, hr, gg, xparts[0], modl, pool_w[j].astype(BF16),
                                        pool_scale[j][None, :], on, wo_bf, nf, wr, br, dims)
        xparts = (_moe(xn, tok, route, totals, modl, exp_w1, exp_w3, exp_w2, i, dims),)
    return xparts[0][:n_lat].reshape(n_batch, seq, d)
```

```python
import functools

import jax
import jax.numpy as jnp
from jax import lax
from jax.experimental import pallas as pl
from jax.experimental.pallas import tpu as pltpu

F32 = jnp.float32
BF16 = jnp.bfloat16
HIGHEST = lax.Precision.HIGHEST
LOG2E = 1.4426950408889634

GRID_W = 64
EPS = 1e-6
HEAD_DIM = 64
N_Q_HEADS = 8
N_KV_HEADS = 2
Q_PER_KV = N_Q_HEADS // N_KV_HEADS
WINDOW = 128
ATT_BLOCK = 128
ROPE_PAIRS = HEAD_DIM // 4
ROPE_BASE = 10000.0
POOL_WINDOWS = (2, 4, 8, 16)
LRU_C = 8.0
N_GROUPS = 4
EXPERTS_PER_GROUP = 8
N_EXPERTS = N_GROUPS * EXPERTS_PER_GROUP
TOP_K = 2
MOE_BLOCK = 512
ZERO_CHUNK = 64
N_MOD = 6

SUBLANES = 8
LANES = 128
SEQ_TILE = 256
TOK_TILE = 512
MERGE_TILE = 512
ATT_TILE = 256
NEG = -1e30
VMEM_LIMIT = 48 * 1024 * 1024


def _cparams(*sem):
    return pltpu.CompilerParams(dimension_semantics=sem, vmem_limit_bytes=VMEM_LIMIT)


def _rms(x, g):
    ms = jnp.mean(x * x, axis=-1, keepdims=True)
    return x * lax.rsqrt(ms + EPS) * g


def _modulate(x, g, shift, scale):
    return _rms(x, g) * (1.0 + scale) + shift


def _bdot(a, b):
    return jnp.dot(a.astype(BF16), b, preferred_element_type=F32)


def _silu(x):
    return x * jax.nn.sigmoid(x)


def _ada_kernel(c_ref, w_ref, b_ref, o_ref):
    s = _silu(c_ref[...])
    o_ref[0] = jnp.dot(s, w_ref[0], precision=HIGHEST, preferred_element_type=F32) + b_ref[0]


def _ada(cc, ada_w, ada_b):
    depth, d, n = ada_w.shape
    tn = 1536
    return pl.pallas_call(
        _ada_kernel,
        grid=(depth, n // tn),
        in_specs=[pl.BlockSpec((SUBLANES, d), lambda l, j: (0, 0)),
                  pl.BlockSpec((1, d, tn), lambda l, j: (l, 0, j)),
                  pl.BlockSpec((1, 1, tn), lambda l, j: (l, 0, j))],
        out_specs=pl.BlockSpec((1, SUBLANES, tn), lambda l, j: (l, 0, j)),
        out_shape=jax.ShapeDtypeStruct((depth, SUBLANES, n), F32),
        compiler_params=_cparams("arbitrary", "arbitrary"),
        name="ada",
    )(cc, ada_w, ada_b.reshape(depth, 1, n))


def _head_norm_rope(t, pm, g, cos, sin):
    m = t.shape[0]
    t2 = t * t
    hi = t2.astype(BF16)
    lo = (t2 - hi.astype(F32)).astype(BF16)
    ms2 = jnp.dot(jnp.concatenate([hi, lo], axis=0), pm, preferred_element_type=F32)
    ms = ms2[0:m] + ms2[m:2 * m]
    t = t * lax.rsqrt(ms + EPS) * g
    lane = lax.broadcasted_iota(jnp.int32, t.shape, 1)
    first_half = (lane % (2 * ROPE_PAIRS)) < ROPE_PAIRS
    partner = jnp.where(first_half, pltpu.roll(t, LANES - ROPE_PAIRS, 1), pltpu.roll(t, ROPE_PAIRS, 1))
    return t * cos + partner * sin


def _token_operands(xparts, n_lat_tiles, tile):
    d = xparts[0].shape[1]
    lat = pl.BlockSpec((tile, d), lambda j: (jnp.minimum(j, n_lat_tiles - 1), 0))
    if len(xparts) == 2:
        ctx = pl.BlockSpec((tile, d), lambda j: (jnp.maximum(j - n_lat_tiles, 0), 0))
        return list(xparts), [lat, ctx]
    ctx = pl.BlockSpec((tile, d), lambda j: (jnp.maximum(j, n_lat_tiles), 0))
    return [xparts[0], xparts[0]], [lat, ctx]


def _pick_tokens(xl_ref, xc_ref, n_lat_tiles):
    return jnp.where(pl.program_id(0) >= n_lat_tiles, xc_ref[...], xl_ref[...])


def _inproj_even_kernel(xl_ref, xc_ref, mod_ref, g_ref, w_ref, qg_ref, kg_ref, cos_ref, sin_ref, pm_ref,
                        q_ref, kvx_ref, gb_ref, u_ref, *, n_lat_tiles):
    mod = mod_ref[0]
    x = _pick_tokens(xl_ref, xc_ref, n_lat_tiles)
    h = _modulate(x, g_ref[...], mod[0:1], mod[1:2]).astype(BF16)
    cos, sin, pm = cos_ref[...], sin_ref[...], pm_ref[...]
    kv_w = N_KV_HEADS * HEAD_DIM
    att_w = N_Q_HEADS * HEAD_DIM
    sc_w = gb_ref.shape[1]
    kv = jnp.dot(h, w_ref[:, 0:2 * kv_w], preferred_element_type=F32)
    k = _head_norm_rope(kv[:, :kv_w], pm, kg_ref[...], cos, sin)
    v = kv[:, kv_w:]
    kvx_ref[...] = jnp.concatenate([k, pltpu.roll(k, HEAD_DIM, 1), v, pltpu.roll(v, HEAD_DIM, 1)],
                                   axis=-1).astype(BF16)
    q = jnp.dot(h, w_ref[:, 2 * kv_w:2 * kv_w + att_w], preferred_element_type=F32)
    qscale = HEAD_DIM ** -0.5 * LOG2E
    for c in range(att_w // LANES):
        qc = _head_norm_rope(q[:, c * LANES:(c + 1) * LANES], pm, qg_ref[...], cos, sin)
        q_ref[:, c * LANES:(c + 1) * LANES] = (qc * qscale).astype(BF16)
    o = 2 * kv_w + att_w
    gb_ref[...] = jnp.dot(h, w_ref[:, o:o + sc_w], preferred_element_type=F32)
    gc = jnp.dot(h, w_ref[:, o + sc_w:o + 2 * sc_w], preferred_element_type=F32)
    gx = jnp.dot(h, w_ref[:, o + 2 * sc_w:o + 3 * sc_w], preferred_element_type=F32)
    u_ref[...] = gc * gx


def _tile_maps(n_lat_tiles, tiles_per_seq, n_batch):
    def mod_map(j):
        return (jnp.minimum(j // tiles_per_seq, n_batch), 0, 0)

    def pos_map(j):
        return (jnp.where(j < n_lat_tiles, j % tiles_per_seq, tiles_per_seq), 0)

    return mod_map, pos_map


def _inproj_even(xparts, modl, g, w_bf, q_g, k_g, cos_t, sin_t, pm, dims):
    n_tok = sum(p.shape[0] for p in xparts)
    d = xparts[0].shape[1]
    n_batch, seq = dims
    tile = TOK_TILE
    tps = seq // tile
    mod_map, pos_map = _tile_maps(n_batch * tps, tps, n_batch)
    xs, xspecs = _token_operands(xparts, n_batch * tps, tile)
    kv_w, att_w = N_KV_HEADS * HEAD_DIM, N_Q_HEADS * HEAD_DIM
    sc_w = (w_bf.shape[1] - 2 * kv_w - att_w) // 3
    row = lambda j: (j, 0)
    const = lambda j: (0, 0)
    return pl.pallas_call(
        functools.partial(_inproj_even_kernel, n_lat_tiles=n_batch * tps),
        grid=(n_tok // tile,),
        in_specs=xspecs + [
                  pl.BlockSpec((1, SUBLANES, d), mod_map),
                  pl.BlockSpec((1, d), const),
                  pl.BlockSpec(w_bf.shape, const),
                  pl.BlockSpec((1, LANES), const),
                  pl.BlockSpec((1, LANES), const),
                  pl.BlockSpec((tile, LANES), pos_map),
                  pl.BlockSpec((tile, LANES), pos_map),
                  pl.BlockSpec((LANES, LANES), const)],
        out_specs=[pl.BlockSpec((tile, att_w), row),
                   pl.BlockSpec((tile, 4 * kv_w), row),
                   pl.BlockSpec((tile, sc_w), row),
                   pl.BlockSpec((tile, sc_w), row)],
        out_shape=[jax.ShapeDtypeStruct((n_tok, att_w), BF16),
                   jax.ShapeDtypeStruct((n_tok, 4 * kv_w), BF16),
                   jax.ShapeDtypeStruct((n_tok, sc_w), F32),
                   jax.ShapeDtypeStruct((n_tok, sc_w), F32)],
        compiler_params=_cparams("arbitrary"),
        name="inproj_even",
    )(*xs, modl, g, w_bf, q_g, k_g, cos_t, sin_t, pm)


def _attn_heads(q, kvx, mask_fn, sink_ref, o_ref):
    m, n = q.shape[0], kvx.shape[0]
    k_nat, k_swp = kvx[:, 0:LANES], kvx[:, LANES:2 * LANES]
    v_nat, v_swp = kvx[:, 2 * LANES:3 * LANES], kvx[:, 3 * LANES:4 * LANES]
    lo_n = lax.broadcasted_iota(jnp.int32, (n, LANES), 1) < HEAD_DIM
    lo_m = lax.broadcasted_iota(jnp.int32, (2 * m, LANES), 1) < HEAD_DIM
    top = lax.broadcasted_iota(jnp.int32, (2 * m, 1), 0) < m
    zero = jnp.zeros_like(k_nat)
    r2 = lax.broadcasted_iota(jnp.int32, (2 * n, LANES), 0) < n
    l2 = lax.broadcasted_iota(jnp.int32, (2 * n, LANES), 1) < HEAD_DIM
    ones_both = jnp.where(r2 == l2, 1.0, 0.0).astype(BF16)
    outs = []
    for h in range(N_KV_HEADS):
        k_lo, k_hi = (k_nat, k_swp) if h == 0 else (k_swp, k_nat)
        v_lo, v_hi = (v_nat, v_swp) if h == 0 else (v_swp, v_nat)
        g0 = 2 * h
        q2 = jnp.concatenate([q[:, g0 * LANES:(g0 + 1) * LANES],
                              q[:, (g0 + 1) * LANES:(g0 + 2) * LANES]], axis=0)
        k_both = jnp.concatenate([jnp.where(lo_n, k_lo, zero), jnp.where(lo_n, zero, k_hi)], axis=0)
        v_both = jnp.concatenate([jnp.where(lo_n, v_lo, zero), jnp.where(lo_n, zero, v_hi)], axis=0)
        s = lax.dot_general(q2, k_both, (((1,), (1,)), ((), ())), preferred_element_type=F32)
        hq = Q_PER_KV * h
        halves = []
        for par in range(2):
            sp = mask_fn(s[:, par * n:(par + 1) * n])
            snk = jnp.where(top, sink_ref[hq + par], sink_ref[hq + 2 + par]) * LOG2E
            mx = jnp.maximum(jnp.max(sp, axis=-1, keepdims=True), snk)
            halves.append((jnp.exp2(sp - mx), jnp.exp2(snk - mx)))
        (p_lo, e_lo), (p_hi, e_hi) = halves
        p = jnp.concatenate([p_lo, p_hi], axis=1).astype(BF16)
        num = jnp.dot(p, v_both, preferred_element_type=F32)
        den = jnp.dot(p, ones_both, preferred_element_type=F32) + jnp.where(lo_m, e_lo, e_hi)
        o = num / den
        outs += [o[0:m], o[m:2 * m]]
    o_ref[...] = jnp.concatenate(outs, axis=-1)


def _attn_lat_kernel(sink_ref, q_ref, kp_ref, kc_ref, kn_ref, kx_ref, o_ref):
    i = pl.program_id(1)
    nb = pl.num_programs(1)
    kvx = jnp.concatenate([kp_ref[...], kc_ref[...], kn_ref[...], kx_ref[...]], axis=0)
    m, w = ATT_TILE, WINDOW
    tok = lax.broadcasted_iota(jnp.int32, (2 * m, m), 0) % m
    col = lax.broadcasted_iota(jnp.int32, (2 * m, m), 1)
    tok_w = lax.broadcasted_iota(jnp.int32, (2 * m, w), 0) % m
    col_w = lax.broadcasted_iota(jnp.int32, (2 * m, w), 1)
    ok_prev = (col_w >= tok_w) & (i > 0)
    ok_cur = (col - tok <= w) & (tok - col <= w)
    ok_next = (col_w + (m - w) <= tok_w) & (i < nb - 1)

    def mask_fn(s):
        return jnp.concatenate([jnp.where(ok_prev, s[:, 0:w], NEG), jnp.where(ok_cur, s[:, w:w + m], NEG),
                                jnp.where(ok_next, s[:, w + m:2 * w + m], NEG), s[:, 2 * w + m:]], axis=1)

    _attn_heads(q_ref[...], kvx, mask_fn, sink_ref, o_ref)


def _attn_ctx_kernel(sink_ref, att_in_ref, q_ref, kx_ref, o_ref):
    del att_in_ref
    _attn_heads(q_ref[...], kx_ref[...], lambda s: s, sink_ref, o_ref)


def _attention(q, kvx, sink, dims, need_ctx):
    n_tok = q.shape[0]
    n_batch, seq = dims
    nb = seq // ATT_TILE
    r = ATT_TILE // WINDOW
    nw = seq // WINDOW
    n_lat = n_batch * seq
    ctx_len = (n_tok - n_lat) // n_batch
    att_w, kvx_w = q.shape[1], kvx.shape[1]
    ctx_blk0 = n_lat // ctx_len
    cur = lambda b, i: (b * nb + i, 0)
    prev = lambda b, i: (b * nw + jnp.maximum(i * r - 1, 0), 0)
    nxt = lambda b, i: (b * nw + jnp.minimum((i + 1) * r, nw - 1), 0)
    cx = lambda b, i: (ctx_blk0 + b, 0)
    smem = pl.BlockSpec(memory_space=pltpu.SMEM)
    att = pl.pallas_call(
        _attn_lat_kernel,
        grid=(n_batch, nb),
        in_specs=[smem, pl.BlockSpec((ATT_TILE, att_w), cur),
                  pl.BlockSpec((WINDOW, kvx_w), prev), pl.BlockSpec((ATT_TILE, kvx_w), cur),
                  pl.BlockSpec((WINDOW, kvx_w), nxt), pl.BlockSpec((ctx_len, kvx_w), cx)],
        out_specs=pl.BlockSpec((ATT_TILE, att_w), cur),
        out_shape=jax.ShapeDtypeStruct((n_tok, att_w), F32),
        compiler_params=_cparams("arbitrary", "arbitrary"),
        name="attn_latent",
    )(sink, q, kvx, kvx, kvx, kvx)
    if not need_ctx:
        return att
    cxb = lambda b: (ctx_blk0 + b, 0)
    return pl.pallas_call(
        _attn_ctx_kernel,
        grid=(n_batch,),
        in_specs=[smem, pl.BlockSpec(memory_space=pl.ANY),
                  pl.BlockSpec((ctx_len, att_w), cxb), pl.BlockSpec((ctx_len, kvx_w), cxb)],
        out_specs=pl.BlockSpec((ctx_len, att_w), cxb),
        out_shape=jax.ShapeDtypeStruct((n_tok, att_w), F32),
        input_output_aliases={1: 0},
        compiler_params=_cparams("arbitrary"),
        name="attn_context",
    )(sink, att, q, kvx)


def _route(logits):
    lane = lax.broadcasted_iota(jnp.int32, logits.shape, 1).astype(F32)
    big = float(LANES)
    gl = jnp.where((lane >= N_EXPERTS) & (lane < N_EXPERTS + N_GROUPS), logits, NEG)
    gmax = jnp.max(gl, axis=-1, keepdims=True)
    gidx = jnp.min(jnp.where(gl == gmax, lane, big), axis=-1, keepdims=True) - N_EXPERTS
    p_sel = 1.0 / jnp.sum(jnp.exp(gl - gmax), axis=-1, keepdims=True)
    lo = gidx * EXPERTS_PER_GROUP
    el = jnp.where((lane >= lo) & (lane < lo + EXPERTS_PER_GROUP), logits, NEG)
    m1 = jnp.max(el, axis=-1, keepdims=True)
    i1 = jnp.min(jnp.where(el == m1, lane, big), axis=-1, keepdims=True)
    el2 = jnp.where(lane == i1, NEG, el)
    m2 = jnp.max(el2, axis=-1, keepdims=True)
    i2 = jnp.min(jnp.where(el2 == m2, lane, big), axis=-1, keepdims=True)
    t = jnp.exp(m2 - m1)
    w1 = p_sel / (1.0 + t)
    w2 = p_sel * t / (1.0 + t)
    return jnp.where(lane == 0, i1, jnp.where(lane == 1, i2, jnp.where(lane == 2, w1,
                     jnp.where(lane == 3, w2, 0.0))))


def _tail(n1, n2, gain, wo_ref, x, mod, nf_ref, wr_ref, br_ref, xo_ref, tok_ref, route_ref, cnt_ref):
    half = n1.shape[1]
    y = (_bdot(_rms(n1, gain[:, :half]), wo_ref[0:half, :])
         + _bdot(_rms(n2, gain[:, half:]), wo_ref[half:, :]))
    xn = x + mod[2:3] * y
    xo_ref[...] = xn
    tok = _modulate(xn, nf_ref[...], mod[3:4], mod[4:5])
    tok_ref[...] = tok
    t_hi = tok.astype(BF16)
    t_lo = (tok - t_hi.astype(F32)).astype(BF16)
    hh = jnp.dot(t_hi, wr_ref[...], preferred_element_type=F32)
    logits = (hh[:, :LANES] + hh[:, LANES:] + br_ref[...]
              + jnp.dot(t_lo, wr_ref[:, 0:LANES], preferred_element_type=F32))
    route = _route(logits)
    route_ref[...] = route
    lane = lax.broadcasted_iota(jnp.int32, route.shape, 1).astype(F32)
    hit = jnp.where((lane == route[:, 0:1]) | (lane == route[:, 1:2]), 1.0, 0.0)

    @pl.when(pl.program_id(0) == 0)
    def _():
        cnt_ref[...] = jnp.zeros_like(cnt_ref)

    cnt_ref[0:1, :] += jnp.sum(hit, axis=0, keepdims=True)


def _seq_flags(j, n_lat_tiles, tiles_per_seq):
    is_ctx = j >= n_lat_tiles
    first = is_ctx | (j % tiles_per_seq == 0)
    last = is_ctx | (j % tiles_per_seq == tiles_per_seq - 1)
    return first, last


def _with_halo(prev_ref, tile, next_ref, first, last):
    p = jnp.where(first, 0.0, prev_ref[...])
    n = jnp.where(last, 0.0, next_ref[...])
    return jnp.concatenate([p, tile, n], axis=0)


def _shift_rows(e, k):
    return pltpu.roll(e, (-k) % e.shape[0], 0)


def _merge_even_kernel(att_ref, gb_ref, u_ref, up_ref, un_ref, xl_ref, xc_ref, mod_ref, cw_ref, cb_ref,
                       on_ref, wo_ref, nf_ref, wr_ref, br_ref,
                       xo_ref, tok_ref, route_ref, cnt_ref, *, n_lat_tiles, tiles_per_seq):
    j = pl.program_id(0)
    first, last = _seq_flags(j, n_lat_tiles, tiles_per_seq)
    e = _with_halo(up_ref, u_ref[...], un_ref, first, last)
    cw = cw_ref[...]
    lo, hi = SUBLANES, SUBLANES + MERGE_TILE
    pos = lax.broadcasted_iota(jnp.int32, (MERGE_TILE, 1), 0) % SEQ_TILE
    is_ctx = j >= n_lat_tiles
    before = jnp.where(is_ctx & (pos == 0), 0.0, _shift_rows(e, -1)[lo:hi])
    after = jnp.where(is_ctx & (pos == SEQ_TILE - 1), 0.0, _shift_rows(e, 1)[lo:hi])
    conv = cw[0:1] * before + cw[1:2] * e[lo:hi] + cw[2:3] * after + cb_ref[...]
    conv = gb_ref[...] * conv
    x = _pick_tokens(xl_ref, xc_ref, n_lat_tiles)
    _tail(att_ref[...], conv, on_ref[...], wo_ref, x, mod_ref[0], nf_ref, wr_ref, br_ref,
          xo_ref, tok_ref, route_ref, cnt_ref)


def _halo_specs(width, n_rows):
    r = MERGE_TILE // SUBLANES
    n_blk = n_rows // SUBLANES
    prev = pl.BlockSpec((SUBLANES, width), lambda j: (jnp.maximum(j * r - 1, 0), 0))
    nxt = pl.BlockSpec((SUBLANES, width), lambda j: (jnp.minimum((j + 1) * r, n_blk - 1), 0))
    return prev, nxt


def _tail_specs(d, n_out_tok):
    row = lambda j: (j, 0)
    out_specs = [pl.BlockSpec((MERGE_TILE, d), row), pl.BlockSpec((MERGE_TILE, d), row),
                 pl.BlockSpec((MERGE_TILE, LANES), row), pl.BlockSpec((SUBLANES, LANES), lambda j: (0, 0))]
    out_shape = [jax.ShapeDtypeStruct((n_out_tok, d), F32), jax.ShapeDtypeStruct((n_out_tok, d), F32),
                 jax.ShapeDtypeStruct((n_out_tok, LANES), F32), jax.ShapeDtypeStruct((SUBLANES, LANES), F32)]
    return out_specs, out_shape


def _merge_even(att, gb, u, xparts, modl, cw, cb, on, wo_bf, nf, wr, br, dims, n_out_tok):
    n_tok = u.shape[0]
    d = xparts[0].shape[1]
    n_batch, seq = dims
    tps = seq // MERGE_TILE
    mod_map, _ = _tile_maps(n_batch * tps, tps, n_batch)
    xs, xspecs = _token_operands(xparts, n_batch * tps, MERGE_TILE)
    sc_w = u.shape[1]
    row = lambda j: (j, 0)
    const = lambda j: (0, 0)
    prev, nxt = _halo_specs(sc_w, n_tok)
    out_specs, out_shape = _tail_specs(d, n_out_tok)
    return pl.pallas_call(
        functools.partial(_merge_even_kernel, n_lat_tiles=n_batch * tps, tiles_per_seq=tps),
        grid=(n_out_tok // MERGE_TILE,),
        in_specs=[pl.BlockSpec((MERGE_TILE, att.shape[1]), row),
                  pl.BlockSpec((MERGE_TILE, sc_w), row),
                  pl.BlockSpec((MERGE_TILE, sc_w), row), prev, nxt] + xspecs + [
                  pl.BlockSpec((1, SUBLANES, d), mod_map),
                  pl.BlockSpec(cw.shape, const), pl.BlockSpec((1, sc_w), const),
                  pl.BlockSpec((1, d), const), pl.BlockSpec(wo_bf.shape, const),
                  pl.BlockSpec((1, d), const), pl.BlockSpec(wr.shape, const),
                  pl.BlockSpec((1, LANES), const)],
        out_specs=out_specs, out_shape=out_shape,
        compiler_params=_cparams("arbitrary"),
        name="merge_even",
    )(att, gb, u, u, u, *xs, modl, cw, cb, on, wo_bf, nf, wr, br)


def _inproj_odd_kernel(xl_ref, xc_ref, mod_ref, g_ref, w_ref, ux_ref, gg_ref, up_ref, *, n_lat_tiles):
    mod = mod_ref[0]
    x = _pick_tokens(xl_ref, xc_ref, n_lat_tiles)
    h = _modulate(x, g_ref[...], mod[0:1], mod[1:2]).astype(BF16)
    w = ux_ref.shape[1]
    ux_ref[...] = jnp.dot(h, w_ref[:, 0:w], preferred_element_type=F32)
    ug = jnp.dot(h, w_ref[:, w:2 * w], preferred_element_type=F32)
    gg_ref[...] = jax.nn.gelu(ug, approximate=True)
    up_ref[...] = jnp.dot(h, w_ref[:, 2 * w:3 * w], preferred_element_type=F32)


def _inproj_odd(xparts, modl, g, w_bf, dims):
    n_tok = sum(p.shape[0] for p in xparts)
    d = xparts[0].shape[1]
    n_batch, seq = dims
    tile = TOK_TILE
    tps = seq // tile
    mod_map, _ = _tile_maps(n_batch * tps, tps, n_batch)
    xs, xspecs = _token_operands(xparts, n_batch * tps, tile)
    w = w_bf.shape[1] // 3
    row = lambda j: (j, 0)
    const = lambda j: (0, 0)
    return pl.pallas_call(
        functools.partial(_inproj_odd_kernel, n_lat_tiles=n_batch * tps),
        grid=(n_tok // tile,),
        in_specs=xspecs + [
                  pl.BlockSpec((1, SUBLANES, d), mod_map),
                  pl.BlockSpec((1, d), const),
                  pl.BlockSpec(w_bf.shape, const)],
        out_specs=[pl.BlockSpec((tile, w), row)] * 3,
        out_shape=[jax.ShapeDtypeStruct((n_tok, w), F32)] * 3,
        compiler_params=_cparams("arbitrary"),
        name="inproj_odd",
    )(*xs, modl, g, w_bf)


def _lru_conv(e, cw, cb):
    lo, hi = SUBLANES, SUBLANES + SEQ_TILE
    return (cw[0:1] * _shift_rows(e, -2)[lo:hi] + cw[1:2] * _shift_rows(e, -1)[lo:hi]
            + cw[2:3] * e[lo:hi] + cw[3:4] * _shift_rows(e, 1)[lo:hi] + cb)


def _lru_coeffs(xl, d, wa_ref, wx_ref, ba_ref, bx_ref, lam_ref, a_ref, b_ref):
    xb = xl.astype(BF16)
    chunk = wa_ref.shape[-1]
    n_chunk = xl.shape[1] // chunk
    pre_a = jnp.concatenate([jnp.dot(xb[:, c * chunk:(c + 1) * chunk], wa_ref[d, c],
                                     preferred_element_type=F32) for c in range(n_chunk)], axis=-1)
    pre_x = jnp.concatenate([jnp.dot(xb[:, c * chunk:(c + 1) * chunk], wx_ref[d, c],
                                     preferred_element_type=F32) for c in range(n_chunk)], axis=-1)
    tr = jnp.tanh(pre_a + ba_ref[d])
    tg = jnp.tanh(pre_x + bx_ref[d])
    c_half = (-0.5 * LRU_C) * jax.nn.softplus(-lam_ref[d])
    log_a = c_half * tr + c_half
    a_ref[...] = jnp.exp(log_a)
    th = jnp.tanh(log_a)
    b_ref[...] = jnp.sqrt(-0.5 * th / (1.0 - th)) * (1.0 + tg) * xl


def _scan_tile(a_ref, b_ref, h_ref, carry, reverse):
    n_grp = a_ref.shape[0] // SUBLANES
    row = lax.broadcasted_iota(jnp.int32, (SUBLANES, a_ref.shape[1]), 0)

    def body(g, carry):
        gi = (n_grp - 1 - g) if reverse else g
        r0 = pl.multiple_of(gi * SUBLANES, SUBLANES)
        a8 = a_ref[pl.ds(r0, SUBLANES), :]
        b8 = b_ref[pl.ds(r0, SUBLANES), :]
        for s in (1, 2, 4):
            if reverse:
                ok = row < SUBLANES - s
                sh = SUBLANES - s
            else:
                ok = row >= s
                sh = s
            a_sh = jnp.where(ok, pltpu.roll(a8, sh, 0), 1.0)
            b_sh = jnp.where(ok, pltpu.roll(b8, sh, 0), 0.0)
            b8 = a8 * b_sh + b8
            a8 = a8 * a_sh
        h8 = a8 * carry + b8
        if h_ref is not None:
            h_ref[pl.ds(r0, SUBLANES), :] = h8
        return h8[0:1] if reverse else h8[SUBLANES - 1:SUBLANES]

    return lax.fori_loop(0, n_grp, body, carry, unroll=4)


def _lru_kernel(uf_ref, ufp_ref, ufn_ref, ur_ref, urp_ref, urn_ref, uc_ref, cw_ref, cb_ref,
                wa_ref, wx_ref, ba_ref, bx_ref, lam_ref, hf_ref, hr_ref,
                cf_ref, cr_ref, a_ref, b_ref):
    i = pl.program_id(1)
    nt = pl.num_programs(1)
    cw, cb = cw_ref[...], cb_ref[...]
    coeffs = functools.partial(_lru_coeffs, wa_ref=wa_ref, wx_ref=wx_ref, ba_ref=ba_ref,
                               bx_ref=bx_ref, lam_ref=lam_ref, a_ref=a_ref, b_ref=b_ref)

    @pl.when(i == 0)
    def _():
        z = jnp.zeros((SUBLANES, uc_ref.shape[1]), F32)
        xc = _lru_conv(jnp.concatenate([z, uc_ref[...], z], axis=0), cw, cb)
        zero = jnp.zeros((1, uc_ref.shape[1]), F32)
        coeffs(xc, 0)
        cf_ref[...] = _scan_tile(a_ref, b_ref, None, zero, False)
        coeffs(xc, 1)
        cr_ref[...] = _scan_tile(a_ref, b_ref, None, zero, True)

    xf = _lru_conv(_with_halo(ufp_ref, uf_ref[...], ufn_ref, i == 0, i == nt - 1), cw, cb)
    coeffs(xf, 0)
    cf_ref[...] = _scan_tile(a_ref, b_ref, hf_ref, cf_ref[...], False)
    xr = _lru_conv(_with_halo(urp_ref, ur_ref[...], urn_ref, i == nt - 1, i == 0), cw, cb)
    coeffs(xr, 1)
    cr_ref[...] = _scan_tile(a_ref, b_ref, hr_ref, cr_ref[...], True)


def _lru(ux, cw, cb, wa_bd, wx_bd, ba, bx, lam, dims):
    n_tok, w = ux.shape
    n_batch, seq = dims
    nt = seq // SEQ_TILE
    n_lat = n_batch * seq
    r = SEQ_TILE // SUBLANES
    n_blk = n_tok // SUBLANES
    fwd = lambda b, i: b * nt + i
    rev = lambda b, i: b * nt + nt - 1 - i

    def specs(tile):
        return [pl.BlockSpec((SEQ_TILE, w), lambda b, i: (tile(b, i), 0)),
                pl.BlockSpec((SUBLANES, w), lambda b, i: (jnp.maximum(tile(b, i) * r - 1, 0), 0)),
                pl.BlockSpec((SUBLANES, w), lambda b, i: (jnp.minimum((tile(b, i) + 1) * r, n_blk - 1), 0))]

    const2 = lambda b, i: (0, 0)
    const3 = lambda b, i: (0, 0, 0)
    const4 = lambda b, i: (0, 0, 0, 0)
    return pl.pallas_call(
        _lru_kernel,
        grid=(n_batch, nt),
        in_specs=specs(fwd) + specs(rev) + [
            pl.BlockSpec((SEQ_TILE, w), lambda b, i: (n_lat // SEQ_TILE + b, 0)),
            pl.BlockSpec(cw.shape, const2), pl.BlockSpec((1, w), const2),
            pl.BlockSpec(wa_bd.shape, const4), pl.BlockSpec(wx_bd.shape, const4),
            pl.BlockSpec(ba.shape, const3), pl.BlockSpec(bx.shape, const3),
            pl.BlockSpec(lam.shape, const3)],
        out_specs=[pl.BlockSpec((SEQ_TILE, w), lambda b, i: (fwd(b, i), 0)),
                   pl.BlockSpec((SEQ_TILE, w), lambda b, i: (rev(b, i), 0))],
        out_shape=[jax.ShapeDtypeStruct((n_lat, w), F32)] * 2,
        scratch_shapes=[pltpu.VMEM((1, w), F32), pltpu.VMEM((1, w), F32),
                        pltpu.VMEM((SEQ_TILE, w), F32), pltpu.VMEM((SEQ_TILE, w), F32)],
        compiler_params=_cparams("arbitrary", "arbitrary"),
        name="rglru",
    )(ux, ux, ux, ux, ux, ux, ux, cw, cb, wa_bd, wx_bd, ba, bx, lam)


def _merge_odd_kernel(up_ref, upp_ref, upn_ref, hf_ref, hr_ref, gg_ref, x_ref, mod_ref,
                      pw_ref, ps_ref, on_ref, wo_ref, nf_ref, wr_ref, br_ref,
                      xo_ref, tok_ref, route_ref, cnt_ref, *, tiles_per_seq, seq):
    j = pl.program_id(0)
    ts = j % tiles_per_seq
    e = _with_halo(upp_ref, up_ref[...], upn_ref, ts == 0, ts == tiles_per_seq - 1)
    gw = pw_ref.shape[-1]
    lo, hi = SUBLANES, SUBLANES + MERGE_TILE
    tpos = ts * MERGE_TILE + lax.broadcasted_iota(jnp.int32, (MERGE_TILE, 1), 0)
    parts = []
    for g, win in enumerate(POOL_WINDOWS):
        eg = e[:, g * gw:(g + 1) * gw]
        p = eg
        span = 1
        while span < win:
            p = p + _shift_rows(p, span)
            span *= 2
        half = win // 2
        s = _shift_rows(p, -half)[lo:hi]
        cnt = (jnp.minimum(tpos + half, seq) - jnp.maximum(tpos - half, 0)).astype(F32)
        dlt = s / cnt - eg[lo:hi]
        parts.append(_bdot(dlt, pw_ref[g]))
    pool = jnp.concatenate(parts, axis=-1) * ps_ref[...]
    lru = (hf_ref[...] + hr_ref[...]) * gg_ref[...]
    _tail(pool, lru, on_ref[...], wo_ref, x_ref[...], mod_ref[0], nf_ref, wr_ref, br_ref,
          xo_ref, tok_ref, route_ref, cnt_ref)


def _merge_odd(up, hf, hr, gg, xf, modl, pw_bf, ps, on, wo_bf, nf, wr, br, dims):
    d = xf.shape[1]
    n_tok, w = up.shape
    n_batch, seq = dims
    tps = seq // MERGE_TILE
    n_lat = n_batch * seq
    mod_map, _ = _tile_maps(n_batch * tps, tps, n_batch)
    row = lambda j: (j, 0)
    const = lambda j: (0, 0)
    prev, nxt = _halo_specs(w, n_tok)
    out_specs, out_shape = _tail_specs(d, n_lat)
    return pl.pallas_call(
        functools.partial(_merge_odd_kernel, tiles_per_seq=tps, seq=seq),
        grid=(n_lat // MERGE_TILE,),
        in_specs=[pl.BlockSpec((MERGE_TILE, w), row), prev, nxt,
                  pl.BlockSpec((MERGE_TILE, w), row), pl.BlockSpec((MERGE_TILE, w), row),
                  pl.BlockSpec((MERGE_TILE, w), row),
                  pl.BlockSpec((MERGE_TILE, d), row),
                  pl.BlockSpec((1, SUBLANES, d), mod_map),
                  pl.BlockSpec(pw_bf.shape, lambda j: (0, 0, 0)), pl.BlockSpec((1, w), const),
                  pl.BlockSpec((1, d), const), pl.BlockSpec(wo_bf.shape, const),
                  pl.BlockSpec((1, d), const), pl.BlockSpec(wr.shape, const),
                  pl.BlockSpec((1, LANES), const)],
        out_specs=out_specs, out_shape=out_shape,
        compiler_params=_cparams("arbitrary"),
        name="merge_odd",
    )(up, up, up, hf, hr, gg, xf, modl, pw_bf, ps, on, wo_bf, nf, wr, br)


def _slots_kernel(route_ref, tot_ref, dest_ref, cnt_ref, run_ref, start_ref):
    i = pl.program_id(0)
    rt = route_ref[...]
    m = rt.shape[0]
    lane = lax.broadcasted_iota(jnp.int32, rt.shape, 1).astype(F32)
    oh1 = lane == rt[:, 0:1]
    oh2 = lane == rt[:, 1:2]
    s = jnp.where(oh1 | oh2, 1.0, 0.0)

    @pl.when(i == 0)
    def _():
        tot = tot_ref[0:1, :]
        padded = jnp.floor((tot + (MOE_BLOCK - 1)) / MOE_BLOCK) * MOE_BLOCK
        r = lax.broadcasted_iota(jnp.int32, (LANES, LANES), 0)
        c = lax.broadcasted_iota(jnp.int32, (LANES, LANES), 1)
        upper = jnp.where(r < c, 1.0, 0.0)
        p8 = jnp.broadcast_to(padded, (SUBLANES, LANES))
        start = jnp.dot(p8, upper, precision=HIGHEST, preferred_element_type=F32)[0:1]
        start_ref[...] = start
        run_ref[...] = jnp.zeros_like(run_ref)
        sub = lax.broadcasted_iota(jnp.int32, cnt_ref.shape, 0)
        cnt_ref[...] = jnp.where(sub == 0, tot, jnp.where(sub == 1, start, 0.0))

    r = lax.broadcasted_iota(jnp.int32, (m, m), 0)
    c = lax.broadcasted_iota(jnp.int32, (m, m), 1)
    lower = jnp.where(c < r, 1.0, 0.0).astype(BF16)
    prefix = jnp.dot(lower, s.astype(BF16), preferred_element_type=F32)
    base = prefix + run_ref[...] + start_ref[...]
    d1 = jnp.sum(jnp.where(oh1, base, 0.0), axis=-1, keepdims=True)
    d2 = jnp.sum(jnp.where(oh2, base, 0.0), axis=-1, keepdims=True)
    dest_ref[...] = jnp.where(lane == 0, d1, jnp.where(lane == 1, d2, 0.0)).astype(jnp.int32)
    run_ref[...] += jnp.sum(s, axis=0, keepdims=True)


def _slots(route, totals):
    n_tok = route.shape[0]
    nt = n_tok // TOK_TILE
    return pl.pallas_call(
        _slots_kernel,
        grid=(nt,),
        in_specs=[pl.BlockSpec((TOK_TILE, LANES), lambda i: (i, 0)),
                  pl.BlockSpec((SUBLANES, LANES), lambda i: (0, 0))],
        out_specs=[pl.BlockSpec((TOK_TILE, LANES), lambda i: (i, 0)),
                   pl.BlockSpec((SUBLANES, LANES), lambda i: (0, 0))],
        out_shape=[jax.ShapeDtypeStruct((n_tok, LANES), jnp.int32),
                   jax.ShapeDtypeStruct((SUBLANES, LANES), F32)],
        scratch_shapes=[pltpu.VMEM((1, LANES), F32)] * 2,
        compiler_params=_cparams("arbitrary"),
        name="moe_slots",
    )(route, totals)


def _for_each_row(n_rows, fn):
    def body(g, c):
        g8 = pl.multiple_of(g * SUBLANES, SUBLANES)
        for j in range(SUBLANES):
            fn(g8, j)
        return c
    lax.fori_loop(0, n_rows // SUBLANES, body, 0)


def _dispatch_kernel(dest_ref, pad_lo_ref, pad_hi_ref, tok_ref, xs_ref, zrow, sem, zsem):
    i = pl.program_id(0)
    base = i * TOK_TILE

    def zero_copy(r, n):
        return pltpu.make_async_copy(zrow.at[pl.ds(0, n)], xs_ref.at[pl.ds(r, n)], zsem)

    def for_each_pad_chunk(fn):
        def per_expert(e, c):
            lo, hi = pad_lo_ref[e], pad_hi_ref[e]
            a8 = (lo + SUBLANES - 1) // SUBLANES * SUBLANES
            a64 = (a8 + ZERO_CHUNK - 1) // ZERO_CHUNK * ZERO_CHUNK

            def rows(r, c):
                fn(zero_copy(r, 1))
                return c

            def eights(q, c):
                fn(zero_copy(pl.multiple_of(a8 + q * SUBLANES, SUBLANES), SUBLANES))
                return c

            def chunks(q, c):
                fn(zero_copy(pl.multiple_of(a64 + q * ZERO_CHUNK, ZERO_CHUNK), ZERO_CHUNK))
                return c

            lax.fori_loop(lo, a8, rows, 0)
            lax.fori_loop(0, (a64 - a8) // SUBLANES, eights, 0)
            lax.fori_loop(0, (hi - a64) // ZERO_CHUNK, chunks, 0)
            return c
        lax.fori_loop(0, N_EXPERTS, per_expert, 0)

    @pl.when(i == 0)
    def _():
        zrow[...] = jnp.zeros_like(zrow)
        for_each_pad_chunk(lambda cp: cp.start())

    def issue(g8, j):
        rows = tok_ref.at[pl.ds(g8, SUBLANES)]
        for k in range(TOP_K):
            d = dest_ref[(base + g8) * TOP_K + (j * TOP_K + k)]
            pltpu.make_async_copy(rows.at[pl.ds(j, 1)], xs_ref.at[pl.ds(d, 1)], sem).start(priority=k)

    _for_each_row(TOK_TILE, issue)
    for k in range(TOP_K):
        pltpu.make_async_copy(tok_ref, xs_ref.at[pl.ds(0, TOK_TILE)], sem).wait()

    @pl.when(i == pl.num_programs(0) - 1)
    def _():
        for_each_pad_chunk(lambda cp: cp.wait())


def _dispatch(dest_flat, pad_lo, pad_hi, tok, n_slots):
    n_tok, d = tok.shape
    return pl.pallas_call(
        _dispatch_kernel,
        grid_spec=pltpu.PrefetchScalarGridSpec(
            num_scalar_prefetch=3,
            grid=(n_tok // TOK_TILE,),
            in_specs=[pl.BlockSpec((TOK_TILE, d), lambda i, *_: (i, 0))],
            out_specs=pl.BlockSpec(memory_space=pl.ANY),
            scratch_shapes=[pltpu.VMEM((ZERO_CHUNK, d), F32),
                            pltpu.SemaphoreType.DMA(()), pltpu.SemaphoreType.DMA(())]),
        out_shape=jax.ShapeDtypeStruct((n_slots, d), F32),
        compiler_params=_cparams("arbitrary"),
        name="moe_dispatch",
    )(dest_flat, pad_lo, pad_hi, tok)


def _expert_kernel(be_ref, nu_ref, x_ref, w1_ref, w3_ref, w2_ref, y_ref, w1b, w3b, w2b):
    i = pl.program_id(0)

    @pl.when(i < nu_ref[0])
    def _():
        changed = (i == 0) | (be_ref[i] != be_ref[jnp.maximum(i - 1, 0)])

        @pl.when(changed)
        def _():
            w1b[...] = w1_ref[0, 0].astype(BF16)
            w3b[...] = w3_ref[0, 0].astype(BF16)
            w2b[...] = w2_ref[0, 0].astype(BF16)

        x = x_ref[...].astype(BF16)
        a = jnp.dot(x, w1b[...], preferred_element_type=F32)
        b = jnp.dot(x, w3b[...], preferred_element_type=F32)
        y_ref[...] = _bdot(_silu(a) * b, w2b[...])


def _experts(block_e, n_used, xs, w1, w3, w2, layer):
    n_slots, d = xs.shape
    de = w1.shape[-1]
    n_blocks = n_slots // MOE_BLOCK
    blk = lambda i, be, nu: (jnp.minimum(i, nu[0] - 1), 0)
    wmap = lambda i, be, nu: (layer, be[i], 0, 0)
    return pl.pallas_call(
        _expert_kernel,
        grid_spec=pltpu.PrefetchScalarGridSpec(
            num_scalar_prefetch=2,
            grid=(n_blocks,),
            in_specs=[pl.BlockSpec((MOE_BLOCK, d), blk),
                      pl.BlockSpec((1, 1, d, de), wmap), pl.BlockSpec((1, 1, d, de), wmap),
                      pl.BlockSpec((1, 1, de, d), wmap)],
            out_specs=pl.BlockSpec((MOE_BLOCK, d), blk),
            scratch_shapes=[pltpu.VMEM((d, de), BF16), pltpu.VMEM((d, de), BF16),
                            pltpu.VMEM((de, d), BF16)]),
        out_shape=jax.ShapeDtypeStruct((n_slots, d), F32),
        compiler_params=_cparams("arbitrary"),
        name="moe_experts",
    )(block_e, n_used, xs, w1, w3, w2)


def _combine_kernel(dest_ref, x_ref, route_ref, mod_ref, ys_ref, o_ref, ybuf, sem):
    base = pl.program_id(0) * TOK_TILE

    def issue(g8, j):
        for k in range(TOP_K):
            d = dest_ref[(base + g8) * TOP_K + (j * TOP_K + k)]
            rows = ybuf.at[k, pl.ds(g8, SUBLANES)]
            pltpu.make_async_copy(ys_ref.at[pl.ds(d, 1)], rows.at[pl.ds(j, 1)], sem).start(priority=k)

    _for_each_row(TOK_TILE, issue)
    for k in range(TOP_K):
        pltpu.make_async_copy(ys_ref.at[pl.ds(0, TOK_TILE)], ybuf.at[k], sem).wait()
    rt = route_ref[...]
    y = rt[:, 2:3] * ybuf[0] + rt[:, 3:4] * ybuf[1]
    o_ref[...] = x_ref[...] + mod_ref[0][5:6] * y


def _combine(dest_flat, xn, route, modl, ys, dims):
    n_tok, d = xn.shape
    n_batch, seq = dims
    tpb = seq // TOK_TILE
    return pl.pallas_call(
        _combine_kernel,
        grid_spec=pltpu.PrefetchScalarGridSpec(
            num_scalar_prefetch=1,
            grid=(n_tok // TOK_TILE,),
            in_specs=[pl.BlockSpec((TOK_TILE, d), lambda i, dest: (i, 0)),
                      pl.BlockSpec((TOK_TILE, LANES), lambda i, dest: (i, 0)),
                      pl.BlockSpec((1, SUBLANES, d), lambda i, dest: (jnp.minimum(i // tpb, n_batch), 0, 0)),
                      pl.BlockSpec(memory_space=pl.ANY)],
            out_specs=pl.BlockSpec((TOK_TILE, d), lambda i, dest: (i, 0)),
            scratch_shapes=[pltpu.VMEM((TOP_K, TOK_TILE, d), F32), pltpu.SemaphoreType.DMA(())]),
        out_shape=jax.ShapeDtypeStruct((n_tok, d), F32),
        compiler_params=_cparams("arbitrary"),
        name="moe_combine",
    )(dest_flat, xn, route, modl, ys)


def _moe(xn, tok, route, totals, modl, w1, w3, w2, layer, dims):
    n_tok = tok.shape[0]
    n = n_tok * TOP_K
    n_blocks = (n + N_EXPERTS * (MOE_BLOCK - 1) + MOE_BLOCK - 1) // MOE_BLOCK
    dest, counts = _slots(route, totals)
    dest_flat = dest[:, :TOP_K].reshape(-1)
    cnt = counts[0, :N_EXPERTS].astype(jnp.int32)
    start = counts[1, :N_EXPERTS].astype(jnp.int32)
    blocks_per_e = (cnt + MOE_BLOCK - 1) // MOE_BLOCK
    blk_end = (start + blocks_per_e * MOE_BLOCK) // MOE_BLOCK
    n_used = blk_end[-1:]
    blk = jnp.minimum(jnp.arange(n_blocks, dtype=jnp.int32), n_used[0] - 1)
    block_e = jnp.sum((blk[:, None] >= blk_end[None, :]).astype(jnp.int32), axis=1)
    block_e = jnp.minimum(block_e, N_EXPERTS - 1)
    xs = _dispatch(dest_flat, start + cnt, blk_end * MOE_BLOCK, tok, n_blocks * MOE_BLOCK)
    ys = _experts(block_e, n_used, xs, w1, w3, w2, layer)
    return _combine(dest_flat, xn, route, modl, ys, dims)


def _rope_tables(seq):
    rows = seq // GRID_W
    row = jnp.repeat(jnp.arange(rows), GRID_W).astype(F32)
    col = jnp.tile(jnp.arange(GRID_W), rows).astype(F32)
    inv_freq = ROPE_BASE ** (-jnp.arange(ROPE_PAIRS, dtype=F32) / ROPE_PAIRS)
    ar, ac = row[:, None] * inv_freq, col[:, None] * inv_freq
    cos = jnp.concatenate([jnp.cos(ar), jnp.cos(ar), jnp.cos(ac), jnp.cos(ac)], axis=-1)
    sin = jnp.concatenate([-jnp.sin(ar), jnp.sin(ar), -jnp.sin(ac), jnp.sin(ac)], axis=-1)
    reps = LANES // HEAD_DIM
    cos, sin = jnp.tile(cos, (1, reps)), jnp.tile(sin, (1, reps))
    cos = jnp.concatenate([cos, jnp.ones((TOK_TILE, LANES), F32)], axis=0)
    sin = jnp.concatenate([sin, jnp.zeros((TOK_TILE, LANES), F32)], axis=0)
    return cos, sin


def _head_mean_matrix():
    r = jnp.arange(LANES)
    return jnp.where((r[:, None] // HEAD_DIM) == (r[None, :] // HEAD_DIM), 1.0 / HEAD_DIM, 0.0).astype(BF16)


def _block_diag_chunks(w, chunk):
    dirs, nblk, bw, _ = w.shape
    per = chunk // bw
    w = w.reshape(dirs, nblk // per, per, bw, bw)
    eye = jnp.eye(per, dtype=w.dtype)
    out = jnp.einsum("dcpij,pq->dcpiqj", w, eye)
    return out.reshape(dirs, nblk // per, chunk, chunk).astype(BF16)


def _router_matrix(gw, gb, ew, eb):
    d = gw.shape[0]
    pad = LANES - N_EXPERTS - N_GROUPS
    wr = jnp.concatenate([ew, gw, jnp.zeros((d, pad), F32)], axis=1)
    br = jnp.concatenate([eb, gb, jnp.zeros((pad,), F32)])[None, :]
    hi = wr.astype(BF16)
    lo = (wr - hi.astype(F32)).astype(BF16)
    return jnp.concatenate([hi, lo], axis=1), br


def kernel(x, c, ctx, c_ctx, ada_w, ada_b, norm_mix, norm_ffn, out_norm, w_out, w_in_ab, q_norm, k_norm, attn_sink, sconv_w, sconv_b, w_in_cd, pool_w, pool_scale, lru_conv_w, lru_conv_b, lru_wa, lru_ba, lru_wx, lru_bx, lru_lambda, router_gw, router_gb, router_ew, router_eb, exp_w1, exp_w3, exp_w2):
    n_batch, seq, d = x.shape
    ctx_len = ctx.shape[1]
    depth = ada_w.shape[0]
    assert ctx_len == SEQ_TILE and seq % TOK_TILE == 0 and seq % GRID_W == 0
    assert seq % MERGE_TILE == 0 and (n_batch * ctx_len) % MERGE_TILE == 0 and MERGE_TILE % SEQ_TILE == 0
    assert depth == 2, "context-side odd-layer outputs are not implemented"
    dims = (n_batch, seq)
    n_lat = n_batch * seq

    cc = jnp.concatenate([c, c_ctx[None, :], jnp.zeros((SUBLANES - n_batch - 1, d), F32)], axis=0)
    mods = _ada(cc, ada_w, ada_b)
    mods = mods[:, :n_batch + 1].reshape(depth, n_batch + 1, N_MOD, d)
    mods = jnp.pad(mods, ((0, 0), (0, 0), (0, SUBLANES - N_MOD), (0, 0)))

    xparts = (x.reshape(n_lat, d), ctx.reshape(n_batch * ctx_len, d))
    cos_t, sin_t = _rope_tables(seq)
    pm = _head_mean_matrix()

    for i in range(depth):
        need_ctx = i < depth - 1
        j = i // 2
        modl = mods[i]
        n_out = n_lat + n_batch * ctx_len if need_ctx else n_lat
        wr, br = _router_matrix(router_gw[i], router_gb[i], router_ew[i], router_eb[i])
        wo_bf = w_out[i].astype(BF16)
        nm, nf, on = norm_mix[i][None, :], norm_ffn[i][None, :], out_norm[i][None, :]
        if i % 2 == 0:
            reps = LANES // HEAD_DIM
            q, kvx, gb, u = _inproj_even(xparts, modl, nm, w_in_ab[j].astype(BF16),
                                          jnp.tile(q_norm[j], reps)[None, :], jnp.tile(k_norm[j], reps)[None, :],
                                          cos_t, sin_t, pm, dims)
            att = _attention(q, kvx, attn_sink[j], dims, need_ctx)
            xn, tok, route, totals = _merge_even(att, gb, u, xparts, modl, sconv_w[j], sconv_b[j][None, :], on, wo_bf,
                                         nf, wr, br, dims, n_out)
        else:
            ux, gg, up = _inproj_odd(xparts, modl, nm, w_in_cd[j].astype(BF16), dims)
            chunk = 2 * LANES
            hf, hr = _lru(ux, lru_conv_w[j], lru_conv_b[j][None, :],
                          _block_diag_chunks(0.5 * lru_wa[j], chunk), _block_diag_chunks(0.5 * lru_wx[j], chunk),
                          0.5 * lru_ba[j][:, None, :], 0.5 * lru_bx[j][:, None, :],
                          lru_lambda[j][:, None, :], dims)
            xn, tok, route, totals = _merge_odd(up, hf, hr, gg, xparts[0], modl, pool_w[j].astype(BF16),
                                        pool_scale[j][None, :], on, wo_bf, nf, wr, br, dims)
        xparts = (_moe(xn, tok, route, totals, modl, exp_w1, exp_w3, exp_w2, i, dims),)
    return xparts[0][:n_lat].reshape(n_batch, seq, d)
```

```python
import functools

import jax
import jax.numpy as jnp
from jax import lax
from jax.experimental import pallas as pl
from jax.experimental.pallas import tpu as pltpu

F32 = jnp.float32
BF16 = jnp.bfloat16
HIGHEST = lax.Precision.HIGHEST
LOG2E = 1.4426950408889634

GRID_W = 64
EPS = 1e-6
HEAD_DIM = 64
N_Q_HEADS = 8
N_KV_HEADS = 2
Q_PER_KV = N_Q_HEADS // N_KV_HEADS
WINDOW = 128
ATT_BLOCK = 128
ROPE_PAIRS = HEAD_DIM // 4
ROPE_BASE = 10000.0
POOL_WINDOWS = (2, 4, 8, 16)
LRU_C = 8.0
N_GROUPS = 4
EXPERTS_PER_GROUP = 8
N_EXPERTS = N_GROUPS * EXPERTS_PER_GROUP
TOP_K = 2
MOE_BLOCK = 512
ZERO_CHUNK = 64
N_MOD = 6

SUBLANES = 8
LANES = 128
SEQ_TILE = 256
TOK_TILE = 512
MERGE_TILE = 512
ATT_TILE = 256
NEG = -1e30
VMEM_LIMIT = 48 * 1024 * 1024


def _cparams(*sem):
    return pltpu.CompilerParams(dimension_semantics=sem, vmem_limit_bytes=VMEM_LIMIT)


def _rms(x, g):
    ms = jnp.mean(x * x, axis=-1, keepdims=True)
    return x * lax.rsqrt(ms + EPS) * g


def _modulate(x, g, shift, scale):
    return _rms(x, g) * (1.0 + scale) + shift


def _bdot(a, b):
    return jnp.dot(a.astype(BF16), b, preferred_element_type=F32)


def _silu(x):
    return x * jax.nn.sigmoid(x)


def _ada_kernel(c_ref, w_ref, b_ref, o_ref):
    s = _silu(c_ref[...])
    o_ref[0] = jnp.dot(s, w_ref[0], precision=HIGHEST, preferred_element_type=F32) + b_ref[0]


def _ada(cc, ada_w, ada_b):
    depth, d, n = ada_w.shape
    tn = 1536
    return pl.pallas_call(
        _ada_kernel,
        grid=(depth, n // tn),
        in_specs=[pl.BlockSpec((SUBLANES, d), lambda l, j: (0, 0)),
                  pl.BlockSpec((1, d, tn), lambda l, j: (l, 0, j)),
                  pl.BlockSpec((1, 1, tn), lambda l, j: (l, 0, j))],
        out_specs=pl.BlockSpec((1, SUBLANES, tn), lambda l, j: (l, 0, j)),
        out_shape=jax.ShapeDtypeStruct((depth, SUBLANES, n), F32),
        compiler_params=_cparams("arbitrary", "arbitrary"),
        name="ada",
    )(cc, ada_w, ada_b.reshape(depth, 1, n))


def _head_norm_rope(t, pm, g, cos, sin):
    m = t.shape[0]
    t2 = t * t
    hi = t2.astype(BF16)
    lo = (t2 - hi.astype(F32)).astype(BF16)
    ms2 = jnp.dot(jnp.concatenate([hi, lo], axis=0), pm, preferred_element_type=F32)
    ms = ms2[0:m] + ms2[m:2 * m]
    t = t * lax.rsqrt(ms + EPS) * g
    lane = lax.broadcasted_iota(jnp.int32, t.shape, 1)
    first_half = (lane % (2 * ROPE_PAIRS)) < ROPE_PAIRS
    partner = jnp.where(first_half, pltpu.roll(t, LANES - ROPE_PAIRS, 1), pltpu.roll(t, ROPE_PAIRS, 1))
    return t * cos + partner * sin


def _token_operands(xparts, n_lat_tiles, tile):
    d = xparts[0].shape[1]
    lat = pl.BlockSpec((tile, d), lambda j: (jnp.minimum(j, n_lat_tiles - 1), 0))
    if len(xparts) == 2:
        ctx = pl.BlockSpec((tile, d), lambda j: (jnp.maximum(j - n_lat_tiles, 0), 0))
        return list(xparts), [lat, ctx]
    ctx = pl.BlockSpec((tile, d), lambda j: (jnp.maximum(j, n_lat_tiles), 0))
    return [xparts[0], xparts[0]], [lat, ctx]


def _pick_tokens(xl_ref, xc_ref, n_lat_tiles):
    return jnp.where(pl.program_id(0) >= n_lat_tiles, xc_ref[...], xl_ref[...])


def _inproj_even_kernel(xl_ref, xc_ref, mod_ref, g_ref, w_ref, qg_ref, kg_ref, cos_ref, sin_ref, pm_ref,
                        q_ref, kvx_ref, gb_ref, u_ref, *, n_lat_tiles):
    mod = mod_ref[0]
    x = _pick_tokens(xl_ref, xc_ref, n_lat_tiles)
    h = _modulate(x, g_ref[...], mod[0:1], mod[1:2]).astype(BF16)
    cos, sin, pm = cos_ref[...], sin_ref[...], pm_ref[...]
    kv_w = N_KV_HEADS * HEAD_DIM
    att_w = N_Q_HEADS * HEAD_DIM
    sc_w = gb_ref.shape[1]
    kv = jnp.dot(h, w_ref[:, 0:2 * kv_w], preferred_element_type=F32)
    k = _head_norm_rope(kv[:, :kv_w], pm, kg_ref[...], cos, sin)
    v = kv[:, kv_w:]
    kvx_ref[...] = jnp.concatenate([k, pltpu.roll(k, HEAD_DIM, 1), v, pltpu.roll(v, HEAD_DIM, 1)],
                                   axis=-1).astype(BF16)
    q = jnp.dot(h, w_ref[:, 2 * kv_w:2 * kv_w + att_w], preferred_element_type=F32)
    qscale = HEAD_DIM ** -0.5 * LOG2E
    for c in range(att_w // LANES):
        qc = _head_norm_rope(q[:, c * LANES:(c + 1) * LANES], pm, qg_ref[...], cos, sin)
        q_ref[:, c * LANES:(c + 1) * LANES] = (qc * qscale).astype(BF16)
    o = 2 * kv_w + att_w
    gb_ref[...] = jnp.dot(h, w_ref[:, o:o + sc_w], preferred_element_type=F32)
    gc = jnp.dot(h, w_ref[:, o + sc_w:o + 2 * sc_w], preferred_element_type=F32)
    gx = jnp.dot(h, w_ref[:, o + 2 * sc_w:o + 3 * sc_w], preferred_element_type=F32)
    u_ref[...] = gc * gx


def _tile_maps(n_lat_tiles, tiles_per_seq, n_batch):
    def mod_map(j):
        return (jnp.minimum(j // tiles_per_seq, n_batch), 0, 0)

    def pos_map(j):
        return (jnp.where(j < n_lat_tiles, j % tiles_per_seq, tiles_per_seq), 0)

    return mod_map, pos_map


def _inproj_even(xparts, modl, g, w_bf, q_g, k_g, cos_t, sin_t, pm, dims):
    n_tok = sum(p.shape[0] for p in xparts)
    d = xparts[0].shape[1]
    n_batch, seq = dims
    tile = TOK_TILE
    tps = seq // tile
    mod_map, pos_map = _tile_maps(n_batch * tps, tps, n_batch)
    xs, xspecs = _token_operands(xparts, n_batch * tps, tile)
    kv_w, att_w = N_KV_HEADS * HEAD_DIM, N_Q_HEADS * HEAD_DIM
    sc_w = (w_bf.shape[1] - 2 * kv_w - att_w) // 3
    row = lambda j: (j, 0)
    const = lambda j: (0, 0)
    return pl.pallas_call(
        functools.partial(_inproj_even_kernel, n_lat_tiles=n_batch * tps),
        grid=(n_tok // tile,),
        in_specs=xspecs + [
                  pl.BlockSpec((1, SUBLANES, d), mod_map),
                  pl.BlockSpec((1, d), const),
                  pl.BlockSpec(w_bf.shape, const),
                  pl.BlockSpec((1, LANES), const),
                  pl.BlockSpec((1, LANES), const),
                  pl.BlockSpec((tile, LANES), pos_map),
                  pl.BlockSpec((tile, LANES), pos_map),
                  pl.BlockSpec((LANES, LANES), const)],
        out_specs=[pl.BlockSpec((tile, att_w), row),
                   pl.BlockSpec((tile, 4 * kv_w), row),
                   pl.BlockSpec((tile, sc_w), row),
                   pl.BlockSpec((tile, sc_w), row)],
        out_shape=[jax.ShapeDtypeStruct((n_tok, att_w), BF16),
                   jax.ShapeDtypeStruct((n_tok, 4 * kv_w), BF16),
                   jax.ShapeDtypeStruct((n_tok, sc_w), F32),
                   jax.ShapeDtypeStruct((n_tok, sc_w), F32)],
        compiler_params=_cparams("arbitrary"),
        name="inproj_even",
    )(*xs, modl, g, w_bf, q_g, k_g, cos_t, sin_t, pm)


def _attn_heads(q, kvx, mask_fn, sink_ref, o_ref):
    m, n = q.shape[0], kvx.shape[0]
    k_nat, k_swp = kvx[:, 0:LANES], kvx[:, LANES:2 * LANES]
    v_nat, v_swp = kvx[:, 2 * LANES:3 * LANES], kvx[:, 3 * LANES:4 * LANES]
    lo_n = lax.broadcasted_iota(jnp.int32, (n, LANES), 1) < HEAD_DIM
    lo_m = lax.broadcasted_iota(jnp.int32, (2 * m, LANES), 1) < HEAD_DIM
    top = lax.broadcasted_iota(jnp.int32, (2 * m, 1), 0) < m
    zero = jnp.zeros_like(k_nat)
    r2 = lax.broadcasted_iota(jnp.int32, (2 * n, LANES), 0) < n
    l2 = lax.broadcasted_iota(jnp.int32, (2 * n, LANES), 1) < HEAD_DIM
    ones_both = jnp.where(r2 == l2, 1.0, 0.0).astype(BF16)
    outs = []
    for h in range(N_KV_HEADS):
        k_lo, k_hi = (k_nat, k_swp) if h == 0 else (k_swp, k_nat)
        v_lo, v_hi = (v_nat, v_swp) if h == 0 else (v_swp, v_nat)
        g0 = 2 * h
        q2 = jnp.concatenate([q[:, g0 * LANES:(g0 + 1) * LANES],
                              q[:, (g0 + 1) * LANES:(g0 + 2) * LANES]], axis=0)
        k_both = jnp.concatenate([jnp.where(lo_n, k_lo, zero), jnp.where(lo_n, zero, k_hi)], axis=0)
        v_both = jnp.concatenate([jnp.where(lo_n, v_lo, zero), jnp.where(lo_n, zero, v_hi)], axis=0)
        s = lax.dot_general(q2, k_both, (((1,), (1,)), ((), ())), preferred_element_type=F32)
        hq = Q_PER_KV * h
        halves = []
        for par in range(2):
            sp = mask_fn(s[:, par * n:(par + 1) * n])
            snk = jnp.where(top, sink_ref[hq + par], sink_ref[hq + 2 + par]) * LOG2E
            mx = jnp.maximum(jnp.max(sp, axis=-1, keepdims=True), snk)
            halves.append((jnp.exp2(sp - mx), jnp.exp2(snk - mx)))
        (p_lo, e_lo), (p_hi, e_hi) = halves
        p = jnp.concatenate([p_lo, p_hi], axis=1).astype(BF16)
        num = jnp.dot(p, v_both, preferred_element_type=F32)
        den = jnp.dot(p, ones_both, preferred_element_type=F32) + jnp.where(lo_m, e_lo, e_hi)
        o = num / den
        outs += [o[0:m], o[m:2 * m]]
    o_ref[...] = jnp.concatenate(outs, axis=-1)


def _attn_lat_kernel(sink_ref, q_ref, kp_ref, kc_ref, kn_ref, kx_ref, o_ref):
    i = pl.program_id(1)
    nb = pl.num_programs(1)
    kvx = jnp.concatenate([kp_ref[...], kc_ref[...], kn_ref[...], kx_ref[...]], axis=0)
    m, w = ATT_TILE, WINDOW
    tok = lax.broadcasted_iota(jnp.int32, (2 * m, m), 0) % m
    col = lax.broadcasted_iota(jnp.int32, (2 * m, m), 1)
    tok_w = lax.broadcasted_iota(jnp.int32, (2 * m, w), 0) % m
    col_w = lax.broadcasted_iota(jnp.int32, (2 * m, w), 1)
    ok_prev = (col_w >= tok_w) & (i > 0)
    ok_cur = (col - tok <= w) & (tok - col <= w)
    ok_next = (col_w + (m - w) <= tok_w) & (i < nb - 1)

    def mask_fn(s):
        return jnp.concatenate([jnp.where(ok_prev, s[:, 0:w], NEG), jnp.where(ok_cur, s[:, w:w + m], NEG),
                                jnp.where(ok_next, s[:, w + m:2 * w + m], NEG), s[:, 2 * w + m:]], axis=1)

    _attn_heads(q_ref[...], kvx, mask_fn, sink_ref, o_ref)


def _attn_ctx_kernel(sink_ref, att_in_ref, q_ref, kx_ref, o_ref):
    del att_in_ref
    _attn_heads(q_ref[...], kx_ref[...], lambda s: s, sink_ref, o_ref)


def _attention(q, kvx, sink, dims, need_ctx):
    n_tok = q.shape[0]
    n_batch, seq = dims
    nb = seq // ATT_TILE
    r = ATT_TILE // WINDOW
    nw = seq // WINDOW
    n_lat = n_batch * seq
    ctx_len = (n_tok - n_lat) // n_batch
    att_w, kvx_w = q.shape[1], kvx.shape[1]
    ctx_blk0 = n_lat // ctx_len
    cur = lambda b, i: (b * nb + i, 0)
    prev = lambda b, i: (b * nw + jnp.maximum(i * r - 1, 0), 0)
    nxt = lambda b, i: (b * nw + jnp.minimum((i + 1) * r, nw - 1), 0)
    cx = lambda b, i: (ctx_blk0 + b, 0)
    smem = pl.BlockSpec(memory_space=pltpu.SMEM)
    att = pl.pallas_call(
        _attn_lat_kernel,
        grid=(n_batch, nb),
        in_specs=[smem, pl.BlockSpec((ATT_TILE, att_w), cur),
                  pl.BlockSpec((WINDOW, kvx_w), prev), pl.BlockSpec((ATT_TILE, kvx_w), cur),
                  pl.BlockSpec((WINDOW, kvx_w), nxt), pl.BlockSpec((ctx_len, kvx_w), cx)],
        out_specs=pl.BlockSpec((ATT_TILE, att_w), cur),
        out_shape=jax.ShapeDtypeStruct((n_tok, att_w), F32),
        compiler_params=_cparams("arbitrary", "arbitrary"),
        name="attn_latent",
    )(sink, q, kvx, kvx, kvx, kvx)
    if not need_ctx:
        return att
    cxb = lambda b: (ctx_blk0 + b, 0)
    return pl.pallas_call(
        _attn_ctx_kernel,
        grid=(n_batch,),
        in_specs=[smem, pl.BlockSpec(memory_space=pl.ANY),
                  pl.BlockSpec((ctx_len, att_w), cxb), pl.BlockSpec((ctx_len, kvx_w), cxb)],
        out_specs=pl.BlockSpec((ctx_len, att_w), cxb),
        out_shape=jax.ShapeDtypeStruct((n_tok, att_w), F32),
        input_output_aliases={1: 0},
        compiler_params=_cparams("arbitrary"),
        name="attn_context",
    )(sink, att, q, kvx)


def _route(logits):
    lane = lax.broadcasted_iota(jnp.int32, logits.shape, 1).astype(F32)
    big = float(LANES)
    gl = jnp.where((lane >= N_EXPERTS) & (lane < N_EXPERTS + N_GROUPS), logits, NEG)
    gmax = jnp.max(gl, axis=-1, keepdims=True)
    gidx = jnp.min(jnp.where(gl == gmax, lane, big), axis=-1, keepdims=True) - N_EXPERTS
    p_sel = 1.0 / jnp.sum(jnp.exp(gl - gmax), axis=-1, keepdims=True)
    lo = gidx * EXPERTS_PER_GROUP
    el = jnp.where((lane >= lo) & (lane < lo + EXPERTS_PER_GROUP), logits, NEG)
    m1 = jnp.max(el, axis=-1, keepdims=True)
    i1 = jnp.min(jnp.where(el == m1, lane, big), axis=-1, keepdims=True)
    el2 = jnp.where(lane == i1, NEG, el)
    m2 = jnp.max(el2, axis=-1, keepdims=True)
    i2 = jnp.min(jnp.where(el2 == m2, lane, big), axis=-1, keepdims=True)
    t = jnp.exp(m2 - m1)
    w1 = p_sel / (1.0 + t)
    w2 = p_sel * t / (1.0 + t)
    return jnp.where(lane == 0, i1, jnp.where(lane == 1, i2, jnp.where(lane == 2, w1,
                     jnp.where(lane == 3, w2, 0.0))))


def _tail(n1, n2, gain, wo_ref, x, mod, nf_ref, wr_ref, br_ref, xo_ref, tok_ref, route_ref, cnt_ref):
    half = n1.shape[1]
    y = (_bdot(_rms(n1, gain[:, :half]), wo_ref[0:half, :])
         + _bdot(_rms(n2, gain[:, half:]), wo_ref[half:, :]))
    xn = x + mod[2:3] * y
    xo_ref[...] = xn
    tok = _modulate(xn, nf_ref[...], mod[3:4], mod[4:5])
    tok_ref[...] = tok
    t_hi = tok.astype(BF16)
    t_lo = (tok - t_hi.astype(F32)).astype(BF16)
    hh = jnp.dot(t_hi, wr_ref[...], preferred_element_type=F32)
    logits = (hh[:, :LANES] + hh[:, LANES:] + br_ref[...]
              + jnp.dot(t_lo, wr_ref[:, 0:LANES], preferred_element_type=F32))
    route = _route(logits)
    route_ref[...] = route
    lane = lax.broadcasted_iota(jnp.int32, route.shape, 1).astype(F32)
    hit = jnp.where((lane == route[:, 0:1]) | (lane == route[:, 1:2]), 1.0, 0.0)

    @pl.when(pl.program_id(0) == 0)
    def _():
        cnt_ref[...] = jnp.zeros_like(cnt_ref)

    cnt_ref[0:1, :] += jnp.sum(hit, axis=0, keepdims=True)


def _seq_flags(j, n_lat_tiles, tiles_per_seq):
    is_ctx = j >= n_lat_tiles
    first = is_ctx | (j % tiles_per_seq == 0)
    last = is_ctx | (j % tiles_per_seq == tiles_per_seq - 1)
    return first, last


def _with_halo(prev_ref, tile, next_ref, first, last):
    p = jnp.where(first, 0.0, prev_ref[...])
    n = jnp.where(last, 0.0, next_ref[...])
    return jnp.concatenate([p, tile, n], axis=0)


def _shift_rows(e, k):
    return pltpu.roll(e, (-k) % e.shape[0], 0)


def _merge_even_kernel(att_ref, gb_ref, u_ref, up_ref, un_ref, xl_ref, xc_ref, mod_ref, cw_ref, cb_ref,
                       on_ref, wo_ref, nf_ref, wr_ref, br_ref,
                       xo_ref, tok_ref, route_ref, cnt_ref, *, n_lat_tiles, tiles_per_seq):
    j = pl.program_id(0)
    first, last = _seq_flags(j, n_lat_tiles, tiles_per_seq)
    e = _with_halo(up_ref, u_ref[...], un_ref, first, last)
    cw = cw_ref[...]
    lo, hi = SUBLANES, SUBLANES + MERGE_TILE
    pos = lax.broadcasted_iota(jnp.int32, (MERGE_TILE, 1), 0) % SEQ_TILE
    is_ctx = j >= n_lat_tiles
    before = jnp.where(is_ctx & (pos == 0), 0.0, _shift_rows(e, -1)[lo:hi])
    after = jnp.where(is_ctx & (pos == SEQ_TILE - 1), 0.0, _shift_rows(e, 1)[lo:hi])
    conv = cw[0:1] * before + cw[1:2] * e[lo:hi] + cw[2:3] * after + cb_ref[...]
    conv = gb_ref[...] * conv
    x = _pick_tokens(xl_ref, xc_ref, n_lat_tiles)
    _tail(att_ref[...], conv, on_ref[...], wo_ref, x, mod_ref[0], nf_ref, wr_ref, br_ref,
          xo_ref, tok_ref, route_ref, cnt_ref)


def _halo_specs(width, n_rows):
    r = MERGE_TILE // SUBLANES
    n_blk = n_rows // SUBLANES
    prev = pl.BlockSpec((SUBLANES, width), lambda j: (jnp.maximum(j * r - 1, 0), 0))
    nxt = pl.BlockSpec((SUBLANES, width), lambda j: (jnp.minimum((j + 1) * r, n_blk - 1), 0))
    return prev, nxt


def _tail_specs(d, n_out_tok):
    row = lambda j: (j, 0)
    out_specs = [pl.BlockSpec((MERGE_TILE, d), row), pl.BlockSpec((MERGE_TILE, d), row),
                 pl.BlockSpec((MERGE_TILE, LANES), row), pl.BlockSpec((SUBLANES, LANES), lambda j: (0, 0))]
    out_shape = [jax.ShapeDtypeStruct((n_out_tok, d), F32), jax.ShapeDtypeStruct((n_out_tok, d), F32),
                 jax.ShapeDtypeStruct((n_out_tok, LANES), F32), jax.ShapeDtypeStruct((SUBLANES, LANES), F32)]
    return out_specs, out_shape


def _merge_even(att, gb, u, xparts, modl, cw, cb, on, wo_bf, nf, wr, br, dims, n_out_tok):
    n_tok = u.shape[0]
    d = xparts[0].shape[1]
    n_batch, seq = dims
    tps = seq // MERGE_TILE
    mod_map, _ = _tile_maps(n_batch * tps, tps, n_batch)
    xs, xspecs = _token_operands(xparts, n_batch * tps, MERGE_TILE)
    sc_w = u.shape[1]
    row = lambda j: (j, 0)
    const = lambda j: (0, 0)
    prev, nxt = _halo_specs(sc_w, n_tok)
    out_specs, out_shape = _tail_specs(d, n_out_tok)
    return pl.pallas_call(
        functools.partial(_merge_even_kernel, n_lat_tiles=n_batch * tps, tiles_per_seq=tps),
        grid=(n_out_tok // MERGE_TILE,),
        in_specs=[pl.BlockSpec((MERGE_TILE, att.shape[1]), row),
                  pl.BlockSpec((MERGE_TILE, sc_w), row),
                  pl.BlockSpec((MERGE_TILE, sc_w), row), prev, nxt] + xspecs + [
                  pl.BlockSpec((1, SUBLANES, d), mod_map),
                  pl.BlockSpec(cw.shape, const), pl.BlockSpec((1, sc_w), const),
                  pl.BlockSpec((1, d), const), pl.BlockSpec(wo_bf.shape, const),
                  pl.BlockSpec((1, d), const), pl.BlockSpec(wr.shape, const),
                  pl.BlockSpec((1, LANES), const)],
        out_specs=out_specs, out_shape=out_shape,
        compiler_params=_cparams("arbitrary"),
        name="merge_even",
    )(att, gb, u, u, u, *xs, modl, cw, cb, on, wo_bf, nf, wr, br)


def _odd_projection(x, mod, g_ref, w_ref, ux_ref, gg_ref, up_ref):
    h = _modulate(x, g_ref[...], mod[0:1], mod[1:2]).astype(BF16)
    w = ux_ref.shape[1]
    ux_ref[...] = jnp.dot(h, w_ref[:, 0:w], preferred_element_type=F32)
    ug = jnp.dot(h, w_ref[:, w:2 * w], preferred_element_type=F32)
    gg_ref[...] = jax.nn.gelu(ug, approximate=True)
    up_ref[...] = jnp.dot(h, w_ref[:, 2 * w:3 * w], preferred_element_type=F32)


def _inproj_odd_kernel(xl_ref, xc_ref, mod_ref, g_ref, w_ref, ux_ref, gg_ref, up_ref, *, n_lat_tiles):
    x = _pick_tokens(xl_ref, xc_ref, n_lat_tiles)
    _odd_projection(x, mod_ref[0], g_ref, w_ref, ux_ref, gg_ref, up_ref)


def _inproj_odd(xparts, modl, g, w_bf, dims):
    n_tok = sum(p.shape[0] for p in xparts)
    d = xparts[0].shape[1]
    n_batch, seq = dims
    tile = TOK_TILE
    tps = seq // tile
    mod_map, _ = _tile_maps(n_batch * tps, tps, n_batch)
    xs, xspecs = _token_operands(xparts, n_batch * tps, tile)
    w = w_bf.shape[1] // 3
    row = lambda j: (j, 0)
    const = lambda j: (0, 0)
    return pl.pallas_call(
        functools.partial(_inproj_odd_kernel, n_lat_tiles=n_batch * tps),
        grid=(n_tok // tile,),
        in_specs=xspecs + [
                  pl.BlockSpec((1, SUBLANES, d), mod_map),
                  pl.BlockSpec((1, d), const),
                  pl.BlockSpec(w_bf.shape, const)],
        out_specs=[pl.BlockSpec((tile, w), row)] * 3,
        out_shape=[jax.ShapeDtypeStruct((n_tok, w), F32)] * 3,
        compiler_params=_cparams("arbitrary"),
        name="inproj_odd",
    )(*xs, modl, g, w_bf)


def _lru_conv(e, cw, cb):
    lo, hi = SUBLANES, SUBLANES + SEQ_TILE
    return (cw[0:1] * _shift_rows(e, -2)[lo:hi] + cw[1:2] * _shift_rows(e, -1)[lo:hi]
            + cw[2:3] * e[lo:hi] + cw[3:4] * _shift_rows(e, 1)[lo:hi] + cb)


def _lru_coeffs(xl, d, wa_ref, wx_ref, ba_ref, bx_ref, lam_ref, a_ref, b_ref):
    xb = xl.astype(BF16)
    chunk = wa_ref.shape[-1]
    n_chunk = xl.shape[1] // chunk
    pre_a = jnp.concatenate([jnp.dot(xb[:, c * chunk:(c + 1) * chunk], wa_ref[d, c],
                                     preferred_element_type=F32) for c in range(n_chunk)], axis=-1)
    pre_x = jnp.concatenate([jnp.dot(xb[:, c * chunk:(c + 1) * chunk], wx_ref[d, c],
                                     preferred_element_type=F32) for c in range(n_chunk)], axis=-1)
    tr = jnp.tanh(pre_a + ba_ref[d])
    tg = jnp.tanh(pre_x + bx_ref[d])
    c_half = (-0.5 * LRU_C) * jax.nn.softplus(-lam_ref[d])
    log_a = c_half * tr + c_half
    a_ref[...] = jnp.exp(log_a)
    th = jnp.tanh(log_a)
    b_ref[...] = jnp.sqrt(-0.5 * th / (1.0 - th)) * (1.0 + tg) * xl


def _scan_tile(a_ref, b_ref, h_ref, carry, reverse):
    n_grp = a_ref.shape[0] // SUBLANES
    row = lax.broadcasted_iota(jnp.int32, (SUBLANES, a_ref.shape[1]), 0)

    def body(g, carry):
        gi = (n_grp - 1 - g) if reverse else g
        r0 = pl.multiple_of(gi * SUBLANES, SUBLANES)
        a8 = a_ref[pl.ds(r0, SUBLANES), :]
        b8 = b_ref[pl.ds(r0, SUBLANES), :]
        for s in (1, 2, 4):
            if reverse:
                ok = row < SUBLANES - s
                sh = SUBLANES - s
            else:
                ok = row >= s
                sh = s
            a_sh = jnp.where(ok, pltpu.roll(a8, sh, 0), 1.0)
            b_sh = jnp.where(ok, pltpu.roll(b8, sh, 0), 0.0)
            b8 = a8 * b_sh + b8
            a8 = a8 * a_sh
        h8 = a8 * carry + b8
        if h_ref is not None:
            h_ref[pl.ds(r0, SUBLANES), :] = h8
        return h8[0:1] if reverse else h8[SUBLANES - 1:SUBLANES]

    return lax.fori_loop(0, n_grp, body, carry, unroll=4)


def _lru_kernel(uf_ref, ufp_ref, ufn_ref, ur_ref, urp_ref, urn_ref, uc_ref, cw_ref, cb_ref,
                wa_ref, wx_ref, ba_ref, bx_ref, lam_ref, hf_ref, hr_ref,
                cf_ref, cr_ref, a_ref, b_ref):
    i = pl.program_id(1)
    nt = pl.num_programs(1)
    cw, cb = cw_ref[...], cb_ref[...]
    coeffs = functools.partial(_lru_coeffs, wa_ref=wa_ref, wx_ref=wx_ref, ba_ref=ba_ref,
                               bx_ref=bx_ref, lam_ref=lam_ref, a_ref=a_ref, b_ref=b_ref)

    @pl.when(i == 0)
    def _():
        z = jnp.zeros((SUBLANES, uc_ref.shape[1]), F32)
        xc = _lru_conv(jnp.concatenate([z, uc_ref[...], z], axis=0), cw, cb)
        zero = jnp.zeros((1, uc_ref.shape[1]), F32)
        coeffs(xc, 0)
        cf_ref[...] = _scan_tile(a_ref, b_ref, None, zero, False)
        coeffs(xc, 1)
        cr_ref[...] = _scan_tile(a_ref, b_ref, None, zero, True)

    xf = _lru_conv(_with_halo(ufp_ref, uf_ref[...], ufn_ref, i == 0, i == nt - 1), cw, cb)
    coeffs(xf, 0)
    cf_ref[...] = _scan_tile(a_ref, b_ref, hf_ref, cf_ref[...], False)
    xr = _lru_conv(_with_halo(urp_ref, ur_ref[...], urn_ref, i == nt - 1, i == 0), cw, cb)
    coeffs(xr, 1)
    cr_ref[...] = _scan_tile(a_ref, b_ref, hr_ref, cr_ref[...], True)


def _lru(ux, cw, cb, wa_bd, wx_bd, ba, bx, lam, dims):
    n_tok, w = ux.shape
    n_batch, seq = dims
    nt = seq // SEQ_TILE
    n_lat = n_batch * seq
    r = SEQ_TILE // SUBLANES
    n_blk = n_tok // SUBLANES
    fwd = lambda b, i: b * nt + i
    rev = lambda b, i: b * nt + nt - 1 - i

    def specs(tile):
        return [pl.BlockSpec((SEQ_TILE, w), lambda b, i: (tile(b, i), 0)),
                pl.BlockSpec((SUBLANES, w), lambda b, i: (jnp.maximum(tile(b, i) * r - 1, 0), 0)),
                pl.BlockSpec((SUBLANES, w), lambda b, i: (jnp.minimum((tile(b, i) + 1) * r, n_blk - 1), 0))]

    const2 = lambda b, i: (0, 0)
    const3 = lambda b, i: (0, 0, 0)
    const4 = lambda b, i: (0, 0, 0, 0)
    return pl.pallas_call(
        _lru_kernel,
        grid=(n_batch, nt),
        in_specs=specs(fwd) + specs(rev) + [
            pl.BlockSpec((SEQ_TILE, w), lambda b, i: (n_lat // SEQ_TILE + b, 0)),
            pl.BlockSpec(cw.shape, const2), pl.BlockSpec((1, w), const2),
            pl.BlockSpec(wa_bd.shape, const4), pl.BlockSpec(wx_bd.shape, const4),
            pl.BlockSpec(ba.shape, const3), pl.BlockSpec(bx.shape, const3),
            pl.BlockSpec(lam.shape, const3)],
        out_specs=[pl.BlockSpec((SEQ_TILE, w), lambda b, i: (fwd(b, i), 0)),
                   pl.BlockSpec((SEQ_TILE, w), lambda b, i: (rev(b, i), 0))],
        out_shape=[jax.ShapeDtypeStruct((n_lat, w), F32)] * 2,
        scratch_shapes=[pltpu.VMEM((1, w), F32), pltpu.VMEM((1, w), F32),
                        pltpu.VMEM((SEQ_TILE, w), F32), pltpu.VMEM((SEQ_TILE, w), F32)],
        compiler_params=_cparams("arbitrary", "arbitrary"),
        name="rglru",
    )(ux, ux, ux, ux, ux, ux, ux, cw, cb, wa_bd, wx_bd, ba, bx, lam)


def _merge_odd_kernel(up_ref, upp_ref, upn_ref, hf_ref, hr_ref, gg_ref, x_ref, mod_ref,
                      pw_ref, ps_ref, on_ref, wo_ref, nf_ref, wr_ref, br_ref,
                      xo_ref, tok_ref, route_ref, cnt_ref, *, tiles_per_seq, seq):
    j = pl.program_id(0)
    ts = j % tiles_per_seq
    e = _with_halo(upp_ref, up_ref[...], upn_ref, ts == 0, ts == tiles_per_seq - 1)
    gw = pw_ref.shape[-1]
    lo, hi = SUBLANES, SUBLANES + MERGE_TILE
    tpos = ts * MERGE_TILE + lax.broadcasted_iota(jnp.int32, (MERGE_TILE, 1), 0)
    parts = []
    for g, win in enumerate(POOL_WINDOWS):
        eg = e[:, g * gw:(g + 1) * gw]
        p = eg
        span = 1
        while span < win:
            p = p + _shift_rows(p, span)
            span *= 2
        half = win // 2
        s = _shift_rows(p, -half)[lo:hi]
        cnt = (jnp.minimum(tpos + half, seq) - jnp.maximum(tpos - half, 0)).astype(F32)
        dlt = s / cnt - eg[lo:hi]
        parts.append(_bdot(dlt, pw_ref[g]))
    pool = jnp.concatenate(parts, axis=-1) * ps_ref[...]
    lru = (hf_ref[...] + hr_ref[...]) * gg_ref[...]
    _tail(pool, lru, on_ref[...], wo_ref, x_ref[...], mod_ref[0], nf_ref, wr_ref, br_ref,
          xo_ref, tok_ref, route_ref, cnt_ref)


def _merge_odd(up, hf, hr, gg, xf, modl, pw_bf, ps, on, wo_bf, nf, wr, br, dims):
    d = xf.shape[1]
    n_tok, w = up.shape
    n_batch, seq = dims
    tps = seq // MERGE_TILE
    n_lat = n_batch * seq
    mod_map, _ = _tile_maps(n_batch * tps, tps, n_batch)
    row = lambda j: (j, 0)
    const = lambda j: (0, 0)
    prev, nxt = _halo_specs(w, n_tok)
    out_specs, out_shape = _tail_specs(d, n_lat)
    return pl.pallas_call(
        functools.partial(_merge_odd_kernel, tiles_per_seq=tps, seq=seq),
        grid=(n_lat // MERGE_TILE,),
        in_specs=[pl.BlockSpec((MERGE_TILE, w), row), prev, nxt,
                  pl.BlockSpec((MERGE_TILE, w), row), pl.BlockSpec((MERGE_TILE, w), row),
                  pl.BlockSpec((MERGE_TILE, w), row),
                  pl.BlockSpec((MERGE_TILE, d), row),
                  pl.BlockSpec((1, SUBLANES, d), mod_map),
                  pl.BlockSpec(pw_bf.shape, lambda j: (0, 0, 0)), pl.BlockSpec((1, w), const),
                  pl.BlockSpec((1, d), const), pl.BlockSpec(wo_bf.shape, const),
                  pl.BlockSpec((1, d), const), pl.BlockSpec(wr.shape, const),
                  pl.BlockSpec((1, LANES), const)],
        out_specs=out_specs, out_shape=out_shape,
        compiler_params=_cparams("arbitrary"),
        name="merge_odd",
    )(up, up, up, hf, hr, gg, xf, modl, pw_bf, ps, on, wo_bf, nf, wr, br)


def _slots_kernel(route_ref, tot_ref, dest_ref, cnt_ref, run_ref, start_ref):
    i = pl.program_id(0)
    rt = route_ref[...]
    m = rt.shape[0]
    lane = lax.broadcasted_iota(jnp.int32, rt.shape, 1).astype(F32)
    oh1 = lane == rt[:, 0:1]
    oh2 = lane == rt[:, 1:2]
    s = jnp.where(oh1 | oh2, 1.0, 0.0)

    @pl.when(i == 0)
    def _():
        tot = tot_ref[0:1, :]
        padded = jnp.floor((tot + (MOE_BLOCK - 1)) / MOE_BLOCK) * MOE_BLOCK
        r = lax.broadcasted_iota(jnp.int32, (LANES, LANES), 0)
        c = lax.broadcasted_iota(jnp.int32, (LANES, LANES), 1)
        upper = jnp.where(r < c, 1.0, 0.0)
        p8 = jnp.broadcast_to(padded, (SUBLANES, LANES))
        start = jnp.dot(p8, upper, precision=HIGHEST, preferred_element_type=F32)[0:1]
        start_ref[...] = start
        run_ref[...] = jnp.zeros_like(run_ref)
        sub = lax.broadcasted_iota(jnp.int32, cnt_ref.shape, 0)
        cnt_ref[...] = jnp.where(sub == 0, tot, jnp.where(sub == 1, start, 0.0))

    r = lax.broadcasted_iota(jnp.int32, (m, m), 0)
    c = lax.broadcasted_iota(jnp.int32, (m, m), 1)
    lower = jnp.where(c < r, 1.0, 0.0).astype(BF16)
    prefix = jnp.dot(lower, s.astype(BF16), preferred_element_type=F32)
    base = prefix + run_ref[...] + start_ref[...]
    d1 = jnp.sum(jnp.where(oh1, base, 0.0), axis=-1, keepdims=True)
    d2 = jnp.sum(jnp.where(oh2, base, 0.0), axis=-1, keepdims=True)
    dest_ref[...] = jnp.where(lane == 0, d1, jnp.where(lane == 1, d2, 0.0)).astype(jnp.int32)
    run_ref[...] += jnp.sum(s, axis=0, keepdims=True)


def _slots(route, totals):
    n_tok = route.shape[0]
    nt = n_tok // TOK_TILE
    return pl.pallas_call(
        _slots_kernel,
        grid=(nt,),
        in_specs=[pl.BlockSpec((TOK_TILE, LANES), lambda i: (i, 0)),
                  pl.BlockSpec((SUBLANES, LANES), lambda i: (0, 0))],
        out_specs=[pl.BlockSpec((TOK_TILE, LANES), lambda i: (i, 0)),
                   pl.BlockSpec((SUBLANES, LANES), lambda i: (0, 0))],
        out_shape=[jax.ShapeDtypeStruct((n_tok, LANES), jnp.int32),
                   jax.ShapeDtypeStruct((SUBLANES, LANES), F32)],
        scratch_shapes=[pltpu.VMEM((1, LANES), F32)] * 2,
        compiler_params=_cparams("arbitrary"),
        name="moe_slots",
    )(route, totals)


def _for_each_row(n_rows, fn):
    def body(g, c):
        g8 = pl.multiple_of(g * SUBLANES, SUBLANES)
        for j in range(SUBLANES):
            fn(g8, j)
        return c
    lax.fori_loop(0, n_rows // SUBLANES, body, 0)


def _dispatch_kernel(dest_ref, pad_lo_ref, pad_hi_ref, tok_ref, xs_ref, zrow, sem, zsem):
    i = pl.program_id(0)
    base = i * TOK_TILE

    def zero_copy(r, n):
        return pltpu.make_async_copy(zrow.at[pl.ds(0, n)], xs_ref.at[pl.ds(r, n)], zsem)

    def for_each_pad_chunk(fn):
        def per_expert(e, c):
            lo, hi = pad_lo_ref[e], pad_hi_ref[e]
            a8 = (lo + SUBLANES - 1) // SUBLANES * SUBLANES
            a64 = (a8 + ZERO_CHUNK - 1) // ZERO_CHUNK * ZERO_CHUNK

            def rows(r, c):
                fn(zero_copy(r, 1))
                return c

            def eights(q, c):
                fn(zero_copy(pl.multiple_of(a8 + q * SUBLANES, SUBLANES), SUBLANES))
                return c

            def chunks(q, c):
                fn(zero_copy(pl.multiple_of(a64 + q * ZERO_CHUNK, ZERO_CHUNK), ZERO_CHUNK))
                return c

            lax.fori_loop(lo, a8, rows, 0)
            lax.fori_loop(0, (a64 - a8) // SUBLANES, eights, 0)
            lax.fori_loop(0, (hi - a64) // ZERO_CHUNK, chunks, 0)
            return c
        lax.fori_loop(0, N_EXPERTS, per_expert, 0)

    @pl.when(i == 0)
    def _():
        zrow[...] = jnp.zeros_like(zrow)
        for_each_pad_chunk(lambda cp: cp.start())

    def issue(g8, j):
        rows = tok_ref.at[pl.ds(g8, SUBLANES)]
        for k in range(TOP_K):
            d = dest_ref[(base + g8) * TOP_K + (j * TOP_K + k)]
            pltpu.make_async_copy(rows.at[pl.ds(j, 1)], xs_ref.at[pl.ds(d, 1)], sem).start(priority=k)

    _for_each_row(TOK_TILE, issue)
    for k in range(TOP_K):
        pltpu.make_async_copy(tok_ref, xs_ref.at[pl.ds(0, TOK_TILE)], sem).wait()

    @pl.when(i == pl.num_programs(0) - 1)
    def _():
        for_each_pad_chunk(lambda cp: cp.wait())


def _dispatch(dest_flat, pad_lo, pad_hi, tok, n_slots):
    n_tok, d = tok.shape
    return pl.pallas_call(
        _dispatch_kernel,
        grid_spec=pltpu.PrefetchScalarGridSpec(
            num_scalar_prefetch=3,
            grid=(n_tok // TOK_TILE,),
            in_specs=[pl.BlockSpec((TOK_TILE, d), lambda i, *_: (i, 0))],
            out_specs=pl.BlockSpec(memory_space=pl.ANY),
            scratch_shapes=[pltpu.VMEM((ZERO_CHUNK, d), F32),
                            pltpu.SemaphoreType.DMA(()), pltpu.SemaphoreType.DMA(())]),
        out_shape=jax.ShapeDtypeStruct((n_slots, d), F32),
        compiler_params=_cparams("arbitrary"),
        name="moe_dispatch",
    )(dest_flat, pad_lo, pad_hi, tok)


def _expert_kernel(be_ref, nu_ref, x_ref, w1_ref, w3_ref, w2_ref, y_ref, w1b, w3b, w2b):
    i = pl.program_id(0)

    @pl.when(i < nu_ref[0])
    def _():
        changed = (i == 0) | (be_ref[i] != be_ref[jnp.maximum(i - 1, 0)])

        @pl.when(changed)
        def _():
            w1b[...] = w1_ref[0, 0].astype(BF16)
            w3b[...] = w3_ref[0, 0].astype(BF16)
            w2b[...] = w2_ref[0, 0].astype(BF16)

        x = x_ref[...].astype(BF16)
        a = jnp.dot(x, w1b[...], preferred_element_type=F32)
        b = jnp.dot(x, w3b[...], preferred_element_type=F32)
        y_ref[...] = _bdot(_silu(a) * b, w2b[...])


def _experts(block_e, n_used, xs, w1, w3, w2, layer):
    n_slots, d = xs.shape
    de = w1.shape[-1]
    n_blocks = n_slots // MOE_BLOCK
    blk = lambda i, be, nu: (jnp.minimum(i, nu[0] - 1), 0)
    wmap = lambda i, be, nu: (layer, be[i], 0, 0)
    return pl.pallas_call(
        _expert_kernel,
        grid_spec=pltpu.PrefetchScalarGridSpec(
            num_scalar_prefetch=2,
            grid=(n_blocks,),
            in_specs=[pl.BlockSpec((MOE_BLOCK, d), blk),
                      pl.BlockSpec((1, 1, d, de), wmap), pl.BlockSpec((1, 1, d, de), wmap),
                      pl.BlockSpec((1, 1, de, d), wmap)],
            out_specs=pl.BlockSpec((MOE_BLOCK, d), blk),
            scratch_shapes=[pltpu.VMEM((d, de), BF16), pltpu.VMEM((d, de), BF16),
                            pltpu.VMEM((de, d), BF16)]),
        out_shape=jax.ShapeDtypeStruct((n_slots, d), F32),
        compiler_params=_cparams("arbitrary"),
        name="moe_experts",
    )(block_e, n_used, xs, w1, w3, w2)


def _combine_kernel(dest_ref, x_ref, route_ref, mod_ref, ys_ref, *rest, fuse_next):
    if fuse_next:
        nmod_ref, g_ref, w_ref, o_ref, ux_ref, gg_ref, up_ref, ybuf, sem = rest
    else:
        o_ref, ybuf, sem = rest
    i = pl.program_id(0)
    n = pl.num_programs(0)

    def gather(tile, slot):
        base = tile * TOK_TILE

        def issue(g8, j):
            for k in range(TOP_K):
                d = dest_ref[(base + g8) * TOP_K + (j * TOP_K + k)]
                rows = ybuf.at[slot, k, pl.ds(g8, SUBLANES)]
                pltpu.make_async_copy(ys_ref.at[pl.ds(d, 1)], rows.at[pl.ds(j, 1)],
                                      sem.at[slot]).start(priority=k)

        _for_each_row(TOK_TILE, issue)

    @pl.when(i == 0)
    def _():
        gather(0, 0)

    @pl.when(i + 1 < n)
    def _():
        gather(i + 1, (i + 1) % 2)

    slot = i % 2
    for k in range(TOP_K):
        pltpu.make_async_copy(ys_ref.at[pl.ds(0, TOK_TILE)], ybuf.at[slot, k], sem.at[slot]).wait()
    rt = route_ref[...]
    y = rt[:, 2:3] * ybuf[slot, 0] + rt[:, 3:4] * ybuf[slot, 1]
    x1 = x_ref[...] + mod_ref[0][5:6] * y
    o_ref[...] = x1
    if fuse_next:
        _odd_projection(x1, nmod_ref[0], g_ref, w_ref, ux_ref, gg_ref, up_ref)


def _combine(dest_flat, xn, route, modl, ys, dims, next_odd=None):
    n_tok, d = xn.shape
    n_batch, seq = dims
    tpb = seq // TOK_TILE
    row = lambda i, dest: (i, 0)
    const = lambda i, dest: (0, 0)
    mod_spec = pl.BlockSpec((1, SUBLANES, d), lambda i, dest: (jnp.minimum(i // tpb, n_batch), 0, 0))
    in_specs = [pl.BlockSpec((TOK_TILE, d), row), pl.BlockSpec((TOK_TILE, LANES), row), mod_spec,
                pl.BlockSpec(memory_space=pl.ANY)]
    out_specs = [pl.BlockSpec((TOK_TILE, d), row)]
    out_shape = [jax.ShapeDtypeStruct((n_tok, d), F32)]
    args = [dest_flat, xn, route, modl, ys]
    if next_odd is not None:
        nmod, g, w_bf = next_odd
        w = w_bf.shape[1] // 3
        in_specs += [mod_spec, pl.BlockSpec((1, d), const), pl.BlockSpec(w_bf.shape, const)]
        out_specs += [pl.BlockSpec((TOK_TILE, w), row)] * 3
        out_shape += [jax.ShapeDtypeStruct((n_tok, w), F32)] * 3
        args += [nmod, g, w_bf]
    return pl.pallas_call(
        functools.partial(_combine_kernel, fuse_next=next_odd is not None),
        grid_spec=pltpu.PrefetchScalarGridSpec(
            num_scalar_prefetch=1,
            grid=(n_tok // TOK_TILE,),
            in_specs=in_specs, out_specs=out_specs,
            scratch_shapes=[pltpu.VMEM((2, TOP_K, TOK_TILE, d), F32), pltpu.SemaphoreType.DMA((2,))]),
        out_shape=out_shape,
        compiler_params=_cparams("arbitrary"),
        name="moe_combine",
    )(*args)


def _moe(xn, tok, route, totals, modl, w1, w3, w2, layer, dims, next_odd):
    n_tok = tok.shape[0]
    n = n_tok * TOP_K
    n_blocks = (n + N_EXPERTS * (MOE_BLOCK - 1) + MOE_BLOCK - 1) // MOE_BLOCK
    dest, counts = _slots(route, totals)
    dest_flat = dest[:, :TOP_K].reshape(-1)
    cnt = counts[0, :N_EXPERTS].astype(jnp.int32)
    start = counts[1, :N_EXPERTS].astype(jnp.int32)
    blocks_per_e = (cnt + MOE_BLOCK - 1) // MOE_BLOCK
    blk_end = (start + blocks_per_e * MOE_BLOCK) // MOE_BLOCK
    n_used = blk_end[-1:]
    blk = jnp.minimum(jnp.arange(n_blocks, dtype=jnp.int32), n_used[0] - 1)
    block_e = jnp.sum((blk[:, None] >= blk_end[None, :]).astype(jnp.int32), axis=1)
    block_e = jnp.minimum(block_e, N_EXPERTS - 1)
    xs = _dispatch(dest_flat, start + cnt, blk_end * MOE_BLOCK, tok, n_blocks * MOE_BLOCK)
    ys = _experts(block_e, n_used, xs, w1, w3, w2, layer)
    return _combine(dest_flat, xn, route, modl, ys, dims, next_odd)


def _rope_tables(seq):
    rows = seq // GRID_W
    row = jnp.repeat(jnp.arange(rows), GRID_W).astype(F32)
    col = jnp.tile(jnp.arange(GRID_W), rows).astype(F32)
    inv_freq = ROPE_BASE ** (-jnp.arange(ROPE_PAIRS, dtype=F32) / ROPE_PAIRS)
    ar, ac = row[:, None] * inv_freq, col[:, None] * inv_freq
    cos = jnp.concatenate([jnp.cos(ar), jnp.cos(ar), jnp.cos(ac), jnp.cos(ac)], axis=-1)
    sin = jnp.concatenate([-jnp.sin(ar), jnp.sin(ar), -jnp.sin(ac), jnp.sin(ac)], axis=-1)
    reps = LANES // HEAD_DIM
    cos, sin = jnp.tile(cos, (1, reps)), jnp.tile(sin, (1, reps))
    cos = jnp.concatenate([cos, jnp.ones((TOK_TILE, LANES), F32)], axis=0)
    sin = jnp.concatenate([sin, jnp.zeros((TOK_TILE, LANES), F32)], axis=0)
    return cos, sin


def _head_mean_matrix():
    r = jnp.arange(LANES)
    return jnp.where((r[:, None] // HEAD_DIM) == (r[None, :] // HEAD_DIM), 1.0 / HEAD_DIM, 0.0).astype(BF16)


def _block_diag_chunks(w, chunk):
    dirs, nblk, bw, _ = w.shape
    per = chunk // bw
    w = w.reshape(dirs, nblk // per, per, bw, bw)
    eye = jnp.eye(per, dtype=w.dtype)
    out = jnp.einsum("dcpij,pq->dcpiqj", w, eye)
    return out.reshape(dirs, nblk // per, chunk, chunk).astype(BF16)


def _router_matrix(gw, gb, ew, eb):
    d = gw.shape[0]
    pad = LANES - N_EXPERTS - N_GROUPS
    wr = jnp.concatenate([ew, gw, jnp.zeros((d, pad), F32)], axis=1)
    br = jnp.concatenate([eb, gb, jnp.zeros((pad,), F32)])[None, :]
    hi = wr.astype(BF16)
    lo = (wr - hi.astype(F32)).astype(BF16)
    return jnp.concatenate([hi, lo], axis=1), br


def kernel(x, c, ctx, c_ctx, ada_w, ada_b, norm_mix, norm_ffn, out_norm, w_out, w_in_ab, q_norm, k_norm, attn_sink, sconv_w, sconv_b, w_in_cd, pool_w, pool_scale, lru_conv_w, lru_conv_b, lru_wa, lru_ba, lru_wx, lru_bx, lru_lambda, router_gw, router_gb, router_ew, router_eb, exp_w1, exp_w3, exp_w2):
    n_batch, seq, d = x.shape
    ctx_len = ctx.shape[1]
    depth = ada_w.shape[0]
    assert ctx_len == SEQ_TILE and seq % TOK_TILE == 0 and seq % GRID_W == 0
    assert seq % MERGE_TILE == 0 and (n_batch * ctx_len) % MERGE_TILE == 0 and MERGE_TILE % SEQ_TILE == 0
    assert depth == 2, "context-side odd-layer outputs are not implemented"
    dims = (n_batch, seq)
    n_lat = n_batch * seq

    cc = jnp.concatenate([c, c_ctx[None, :], jnp.zeros((SUBLANES - n_batch - 1, d), F32)], axis=0)
    mods = _ada(cc, ada_w, ada_b)
    mods = mods[:, :n_batch + 1].reshape(depth, n_batch + 1, N_MOD, d)
    mods = jnp.pad(mods, ((0, 0), (0, 0), (0, SUBLANES - N_MOD), (0, 0)))

    xparts = (x.reshape(n_lat, d), ctx.reshape(n_batch * ctx_len, d))
    cos_t, sin_t = _rope_tables(seq)
    pm = _head_mean_matrix()

    projected = None
    for i in range(depth):
        need_ctx = i < depth - 1
        j = i // 2
        modl = mods[i]
        n_out = n_lat + n_batch * ctx_len if need_ctx else n_lat
        wr, br = _router_matrix(router_gw[i], router_gb[i], router_ew[i], router_eb[i])
        wo_bf = w_out[i].astype(BF16)
        nm, nf, on = norm_mix[i][None, :], norm_ffn[i][None, :], out_norm[i][None, :]
        if i % 2 == 0:
            reps = LANES // HEAD_DIM
            q, kvx, gb, u = _inproj_even(xparts, modl, nm, w_in_ab[j].astype(BF16),
                                          jnp.tile(q_norm[j], reps)[None, :], jnp.tile(k_norm[j], reps)[None, :],
                                          cos_t, sin_t, pm, dims)
            att = _attention(q, kvx, attn_sink[j], dims, need_ctx)
            xn, tok, route, totals = _merge_even(att, gb, u, xparts, modl, sconv_w[j], sconv_b[j][None, :], on, wo_bf,
                                         nf, wr, br, dims, n_out)
        else:
            if projected is None:
                projected = _inproj_odd(xparts, modl, nm, w_in_cd[j].astype(BF16), dims)
            ux, gg, up = projected
            chunk = 2 * LANES
            hf, hr = _lru(ux, lru_conv_w[j], lru_conv_b[j][None, :],
                          _block_diag_chunks(0.5 * lru_wa[j], chunk), _block_diag_chunks(0.5 * lru_wx[j], chunk),
                          0.5 * lru_ba[j][:, None, :], 0.5 * lru_bx[j][:, None, :],
                          lru_lambda[j][:, None, :], dims)
            xn, tok, route, totals = _merge_odd(up, hf, hr, gg, xparts[0], modl, pool_w[j].astype(BF16),
                                        pool_scale[j][None, :], on, wo_bf, nf, wr, br, dims)
        next_odd = None
        if need_ctx and (i + 1) % 2 == 1:
            next_odd = (mods[i + 1], norm_mix[i + 1][None, :], w_in_cd[(i + 1) // 2].astype(BF16))
        outs = _moe(xn, tok, route, totals, modl, exp_w1, exp_w3, exp_w2, i, dims, next_odd)
        xparts = (outs[0],)
        projected = tuple(outs[1:]) if next_odd is not None else None
    return xparts[0][:n_lat].reshape(n_batch, seq, d)
```

```python
import functools

import jax
import jax.numpy as jnp
from jax import lax
from jax.experimental import pallas as pl
from jax.experimental.pallas import tpu as pltpu

F32 = jnp.float32
BF16 = jnp.bfloat16
HIGHEST = lax.Precision.HIGHEST
LOG2E = 1.4426950408889634

GRID_W = 64
EPS = 1e-6
HEAD_DIM = 64
N_Q_HEADS = 8
N_KV_HEADS = 2
Q_PER_KV = N_Q_HEADS // N_KV_HEADS
WINDOW = 128
ATT_BLOCK = 128
ROPE_PAIRS = HEAD_DIM // 4
ROPE_BASE = 10000.0
POOL_WINDOWS = (2, 4, 8, 16)
LRU_C = 8.0
N_GROUPS = 4
EXPERTS_PER_GROUP = 8
N_EXPERTS = N_GROUPS * EXPERTS_PER_GROUP
TOP_K = 2
MOE_BLOCK = 512
ZERO_CHUNK = 64
N_MOD = 6

SUBLANES = 8
LANES = 128
SEQ_TILE = 256
TOK_TILE = 512
MERGE_TILE = 512
ATT_TILE = 256
NEG = -1e30
VMEM_LIMIT = 48 * 1024 * 1024


def _cparams(*sem):
    return pltpu.CompilerParams(dimension_semantics=sem, vmem_limit_bytes=VMEM_LIMIT)


def _rms(x, g):
    ms = jnp.mean(x * x, axis=-1, keepdims=True)
    return x * lax.rsqrt(ms + EPS) * g


def _modulate(x, g, shift, scale):
    return _rms(x, g) * (1.0 + scale) + shift


def _bdot(a, b):
    return jnp.dot(a.astype(BF16), b, preferred_element_type=F32)


def _silu(x):
    return x * jax.nn.sigmoid(x)


def _pack_bf16_pair(a, b):
    ha = lax.bitcast_convert_type(a.astype(BF16).astype(F32), jnp.uint32)
    hb = lax.bitcast_convert_type(b.astype(BF16).astype(F32), jnp.uint32)
    return ha | (hb >> 16)


def _unpack_bf16_pair(w):
    a = lax.bitcast_convert_type(w & jnp.uint32(0xFFFF0000), F32)
    b = lax.bitcast_convert_type(w << 16, F32)
    return a, b


def _ada_kernel(c_ref, w_ref, b_ref, o_ref):
    s = _silu(c_ref[...])
    o_ref[0] = jnp.dot(s, w_ref[0], precision=HIGHEST, preferred_element_type=F32) + b_ref[0]


def _ada(cc, ada_w, ada_b):
    depth, d, n = ada_w.shape
    tn = 1536
    return pl.pallas_call(
        _ada_kernel,
        grid=(depth, n // tn),
        in_specs=[pl.BlockSpec((SUBLANES, d), lambda l, j: (0, 0)),
                  pl.BlockSpec((1, d, tn), lambda l, j: (l, 0, j)),
                  pl.BlockSpec((1, 1, tn), lambda l, j: (l, 0, j))],
        out_specs=pl.BlockSpec((1, SUBLANES, tn), lambda l, j: (l, 0, j)),
        out_shape=jax.ShapeDtypeStruct((depth, SUBLANES, n), F32),
        compiler_params=_cparams("arbitrary", "arbitrary"),
        name="ada",
    )(cc, ada_w, ada_b.reshape(depth, 1, n))


def _head_norm_rope(t, pm, g, cos, sin):
    m = t.shape[0]
    t2 = t * t
    hi = t2.astype(BF16)
    lo = (t2 - hi.astype(F32)).astype(BF16)
    ms2 = jnp.dot(jnp.concatenate([hi, lo], axis=0), pm, preferred_element_type=F32)
    ms = ms2[0:m] + ms2[m:2 * m]
    t = t * lax.rsqrt(ms + EPS) * g
    lane = lax.broadcasted_iota(jnp.int32, t.shape, 1)
    first_half = (lane % (2 * ROPE_PAIRS)) < ROPE_PAIRS
    partner = jnp.where(first_half, pltpu.roll(t, LANES - ROPE_PAIRS, 1), pltpu.roll(t, ROPE_PAIRS, 1))
    return t * cos + partner * sin


def _token_operands(xparts, n_lat_tiles, tile):
    d = xparts[0].shape[1]
    lat = pl.BlockSpec((tile, d), lambda j: (jnp.minimum(j, n_lat_tiles - 1), 0))
    if len(xparts) == 2:
        ctx = pl.BlockSpec((tile, d), lambda j: (jnp.maximum(j - n_lat_tiles, 0), 0))
        return list(xparts), [lat, ctx]
    ctx = pl.BlockSpec((tile, d), lambda j: (jnp.maximum(j, n_lat_tiles), 0))
    return [xparts[0], xparts[0]], [lat, ctx]


def _pick_tokens(xl_ref, xc_ref, n_lat_tiles):
    return jnp.where(pl.program_id(0) >= n_lat_tiles, xc_ref[...], xl_ref[...])


def _inproj_even_kernel(xl_ref, xc_ref, mod_ref, g_ref, w_ref, qg_ref, kg_ref, cos_ref, sin_ref, pm_ref,
                        q_ref, kvx_ref, gb_ref, u_ref, *, n_lat_tiles):
    mod = mod_ref[0]
    x = _pick_tokens(xl_ref, xc_ref, n_lat_tiles)
    h = _modulate(x, g_ref[...], mod[0:1], mod[1:2]).astype(BF16)
    cos, sin, pm = cos_ref[...], sin_ref[...], pm_ref[...]
    kv_w = N_KV_HEADS * HEAD_DIM
    att_w = N_Q_HEADS * HEAD_DIM
    sc_w = gb_ref.shape[1]
    kv = jnp.dot(h, w_ref[:, 0:2 * kv_w], preferred_element_type=F32)
    k = _head_norm_rope(kv[:, :kv_w], pm, kg_ref[...], cos, sin)
    v = kv[:, kv_w:]
    kvx_ref[...] = jnp.concatenate([k, pltpu.roll(k, HEAD_DIM, 1), v, pltpu.roll(v, HEAD_DIM, 1)],
                                   axis=-1).astype(BF16)
    q = jnp.dot(h, w_ref[:, 2 * kv_w:2 * kv_w + att_w], preferred_element_type=F32)
    qscale = HEAD_DIM ** -0.5 * LOG2E
    for c in range(att_w // LANES):
        qc = _head_norm_rope(q[:, c * LANES:(c + 1) * LANES], pm, qg_ref[...], cos, sin)
        q_ref[:, c * LANES:(c + 1) * LANES] = (qc * qscale).astype(BF16)
    o = 2 * kv_w + att_w
    gb_ref[...] = jnp.dot(h, w_ref[:, o:o + sc_w], preferred_element_type=F32)
    gc = jnp.dot(h, w_ref[:, o + sc_w:o + 2 * sc_w], preferred_element_type=F32)
    gx = jnp.dot(h, w_ref[:, o + 2 * sc_w:o + 3 * sc_w], preferred_element_type=F32)
    u_ref[...] = gc * gx


def _tile_maps(n_lat_tiles, tiles_per_seq, n_batch):
    def mod_map(j):
        return (jnp.minimum(j // tiles_per_seq, n_batch), 0, 0)

    def pos_map(j):
        return (jnp.where(j < n_lat_tiles, j % tiles_per_seq, tiles_per_seq), 0)

    return mod_map, pos_map


def _inproj_even(xparts, modl, g, w_bf, q_g, k_g, cos_t, sin_t, pm, dims):
    n_tok = sum(p.shape[0] for p in xparts)
    d = xparts[0].shape[1]
    n_batch, seq = dims
    tile = TOK_TILE
    tps = seq // tile
    mod_map, pos_map = _tile_maps(n_batch * tps, tps, n_batch)
    xs, xspecs = _token_operands(xparts, n_batch * tps, tile)
    kv_w, att_w = N_KV_HEADS * HEAD_DIM, N_Q_HEADS * HEAD_DIM
    sc_w = (w_bf.shape[1] - 2 * kv_w - att_w) // 3
    row = lambda j: (j, 0)
    const = lambda j: (0, 0)
    return pl.pallas_call(
        functools.partial(_inproj_even_kernel, n_lat_tiles=n_batch * tps),
        grid=(n_tok // tile,),
        in_specs=xspecs + [
                  pl.BlockSpec((1, SUBLANES, d), mod_map),
                  pl.BlockSpec((1, d), const),
                  pl.BlockSpec(w_bf.shape, const),
                  pl.BlockSpec((1, LANES), const),
                  pl.BlockSpec((1, LANES), const),
                  pl.BlockSpec((tile, LANES), pos_map),
                  pl.BlockSpec((tile, LANES), pos_map),
                  pl.BlockSpec((LANES, LANES), const)],
        out_specs=[pl.BlockSpec((tile, att_w), row),
                   pl.BlockSpec((tile, 4 * kv_w), row),
                   pl.BlockSpec((tile, sc_w), row),
                   pl.BlockSpec((tile, sc_w), row)],
        out_shape=[jax.ShapeDtypeStruct((n_tok, att_w), BF16),
                   jax.ShapeDtypeStruct((n_tok, 4 * kv_w), BF16),
                   jax.ShapeDtypeStruct((n_tok, sc_w), F32),
                   jax.ShapeDtypeStruct((n_tok, sc_w), F32)],
        compiler_params=_cparams("arbitrary"),
        name="inproj_even",
    )(*xs, modl, g, w_bf, q_g, k_g, cos_t, sin_t, pm)


def _attn_heads(q, kvx, mask_fn, sink_ref, o_ref):
    m, n = q.shape[0], kvx.shape[0]
    k_nat, k_swp = kvx[:, 0:LANES], kvx[:, LANES:2 * LANES]
    v_nat, v_swp = kvx[:, 2 * LANES:3 * LANES], kvx[:, 3 * LANES:4 * LANES]
    lo_n = lax.broadcasted_iota(jnp.int32, (n, LANES), 1) < HEAD_DIM
    lo_m = lax.broadcasted_iota(jnp.int32, (2 * m, LANES), 1) < HEAD_DIM
    top = lax.broadcasted_iota(jnp.int32, (2 * m, 1), 0) < m
    zero = jnp.zeros_like(k_nat)
    r2 = lax.broadcasted_iota(jnp.int32, (2 * n, LANES), 0) < n
    l2 = lax.broadcasted_iota(jnp.int32, (2 * n, LANES), 1) < HEAD_DIM
    ones_both = jnp.where(r2 == l2, 1.0, 0.0).astype(BF16)
    outs = []
    for h in range(N_KV_HEADS):
        k_lo, k_hi = (k_nat, k_swp) if h == 0 else (k_swp, k_nat)
        v_lo, v_hi = (v_nat, v_swp) if h == 0 else (v_swp, v_nat)
        g0 = 2 * h
        q2 = jnp.concatenate([q[:, g0 * LANES:(g0 + 1) * LANES],
                              q[:, (g0 + 1) * LANES:(g0 + 2) * LANES]], axis=0)
        k_both = jnp.concatenate([jnp.where(lo_n, k_lo, zero), jnp.where(lo_n, zero, k_hi)], axis=0)
        v_both = jnp.concatenate([jnp.where(lo_n, v_lo, zero), jnp.where(lo_n, zero, v_hi)], axis=0)
        s = lax.dot_general(q2, k_both, (((1,), (1,)), ((), ())), preferred_element_type=F32)
        hq = Q_PER_KV * h
        halves = []
        for par in range(2):
            sp = mask_fn(s[:, par * n:(par + 1) * n])
            snk = jnp.where(top, sink_ref[hq + par], sink_ref[hq + 2 + par]) * LOG2E
            mx = jnp.maximum(jnp.max(sp, axis=-1, keepdims=True), snk)
            halves.append((jnp.exp2(sp - mx), jnp.exp2(snk - mx)))
        (p_lo, e_lo), (p_hi, e_hi) = halves
        p = jnp.concatenate([p_lo, p_hi], axis=1).astype(BF16)
        num = jnp.dot(p, v_both, preferred_element_type=F32)
        den = jnp.dot(p, ones_both, preferred_element_type=F32) + jnp.where(lo_m, e_lo, e_hi)
        o = num / den
        outs += [o[0:m], o[m:2 * m]]
    o_ref[...] = jnp.concatenate(outs, axis=-1)


def _attn_lat_kernel(sink_ref, q_ref, kp_ref, kc_ref, kn_ref, kx_ref, o_ref):
    i = pl.program_id(1)
    nb = pl.num_programs(1)
    kvx = jnp.concatenate([kp_ref[...], kc_ref[...], kn_ref[...], kx_ref[...]], axis=0)
    m, w = ATT_TILE, WINDOW
    tok = lax.broadcasted_iota(jnp.int32, (2 * m, m), 0) % m
    col = lax.broadcasted_iota(jnp.int32, (2 * m, m), 1)
    tok_w = lax.broadcasted_iota(jnp.int32, (2 * m, w), 0) % m
    col_w = lax.broadcasted_iota(jnp.int32, (2 * m, w), 1)
    ok_prev = (col_w >= tok_w) & (i > 0)
    ok_cur = (col - tok <= w) & (tok - col <= w)
    ok_next = (col_w + (m - w) <= tok_w) & (i < nb - 1)

    def mask_fn(s):
        return jnp.concatenate([jnp.where(ok_prev, s[:, 0:w], NEG), jnp.where(ok_cur, s[:, w:w + m], NEG),
                                jnp.where(ok_next, s[:, w + m:2 * w + m], NEG), s[:, 2 * w + m:]], axis=1)

    _attn_heads(q_ref[...], kvx, mask_fn, sink_ref, o_ref)


def _attn_ctx_kernel(sink_ref, att_in_ref, q_ref, kx_ref, o_ref):
    del att_in_ref
    _attn_heads(q_ref[...], kx_ref[...], lambda s: s, sink_ref, o_ref)


def _attention(q, kvx, sink, dims, need_ctx):
    n_tok = q.shape[0]
    n_batch, seq = dims
    nb = seq // ATT_TILE
    r = ATT_TILE // WINDOW
    nw = seq // WINDOW
    n_lat = n_batch * seq
    ctx_len = (n_tok - n_lat) // n_batch
    att_w, kvx_w = q.shape[1], kvx.shape[1]
    ctx_blk0 = n_lat // ctx_len
    cur = lambda b, i: (b * nb + i, 0)
    prev = lambda b, i: (b * nw + jnp.maximum(i * r - 1, 0), 0)
    nxt = lambda b, i: (b * nw + jnp.minimum((i + 1) * r, nw - 1), 0)
    cx = lambda b, i: (ctx_blk0 + b, 0)
    smem = pl.BlockSpec(memory_space=pltpu.SMEM)
    att = pl.pallas_call(
        _attn_lat_kernel,
        grid=(n_batch, nb),
        in_specs=[smem, pl.BlockSpec((ATT_TILE, att_w), cur),
                  pl.BlockSpec((WINDOW, kvx_w), prev), pl.BlockSpec((ATT_TILE, kvx_w), cur),
                  pl.BlockSpec((WINDOW, kvx_w), nxt), pl.BlockSpec((ctx_len, kvx_w), cx)],
        out_specs=pl.BlockSpec((ATT_TILE, att_w), cur),
        out_shape=jax.ShapeDtypeStruct((n_tok, att_w), F32),
        compiler_params=_cparams("arbitrary", "arbitrary"),
        name="attn_latent",
    )(sink, q, kvx, kvx, kvx, kvx)
    if not need_ctx:
        return att
    cxb = lambda b: (ctx_blk0 + b, 0)
    return pl.pallas_call(
        _attn_ctx_kernel,
        grid=(n_batch,),
        in_specs=[smem, pl.BlockSpec(memory_space=pl.ANY),
                  pl.BlockSpec((ctx_len, att_w), cxb), pl.BlockSpec((ctx_len, kvx_w), cxb)],
        out_specs=pl.BlockSpec((ctx_len, att_w), cxb),
        out_shape=jax.ShapeDtypeStruct((n_tok, att_w), F32),
        input_output_aliases={1: 0},
        compiler_params=_cparams("arbitrary"),
        name="attn_context",
    )(sink, att, q, kvx)


def _route(logits):
    lane = lax.broadcasted_iota(jnp.int32, logits.shape, 1).astype(F32)
    big = float(LANES)
    gl = jnp.where((lane >= N_EXPERTS) & (lane < N_EXPERTS + N_GROUPS), logits, NEG)
    gmax = jnp.max(gl, axis=-1, keepdims=True)
    gidx = jnp.min(jnp.where(gl == gmax, lane, big), axis=-1, keepdims=True) - N_EXPERTS
    p_sel = 1.0 / jnp.sum(jnp.exp(gl - gmax), axis=-1, keepdims=True)
    lo = gidx * EXPERTS_PER_GROUP
    el = jnp.where((lane >= lo) & (lane < lo + EXPERTS_PER_GROUP), logits, NEG)
    m1 = jnp.max(el, axis=-1, keepdims=True)
    i1 = jnp.min(jnp.where(el == m1, lane, big), axis=-1, keepdims=True)
    el2 = jnp.where(lane == i1, NEG, el)
    m2 = jnp.max(el2, axis=-1, keepdims=True)
    i2 = jnp.min(jnp.where(el2 == m2, lane, big), axis=-1, keepdims=True)
    t = jnp.exp(m2 - m1)
    w1 = p_sel / (1.0 + t)
    w2 = p_sel * t / (1.0 + t)
    return jnp.where(lane == 0, i1, jnp.where(lane == 1, i2, jnp.where(lane == 2, w1,
                     jnp.where(lane == 3, w2, 0.0))))


def _tail(n1, n2, gain, wo_ref, x, mod, nf_ref, wr_ref, br_ref, xo_ref, tok_ref, route_ref, cnt_ref):
    half = n1.shape[1]
    y = (_bdot(_rms(n1, gain[:, :half]), wo_ref[0:half, :])
         + _bdot(_rms(n2, gain[:, half:]), wo_ref[half:, :]))
    xn = x + mod[2:3] * y
    xo_ref[...] = xn
    tok = _modulate(xn, nf_ref[...], mod[3:4], mod[4:5])
    half_d = tok.shape[1] // 2
    tok_ref[...] = _pack_bf16_pair(tok[:, :half_d], tok[:, half_d:])
    t_hi = tok.astype(BF16)
    t_lo = (tok - t_hi.astype(F32)).astype(BF16)
    hh = jnp.dot(t_hi, wr_ref[...], preferred_element_type=F32)
    logits = (hh[:, :LANES] + hh[:, LANES:] + br_ref[...]
              + jnp.dot(t_lo, wr_ref[:, 0:LANES], preferred_element_type=F32))
    route = _route(logits)
    route_ref[...] = route
    lane = lax.broadcasted_iota(jnp.int32, route.shape, 1).astype(F32)
    hit = jnp.where((lane == route[:, 0:1]) | (lane == route[:, 1:2]), 1.0, 0.0)

    @pl.when(pl.program_id(0) == 0)
    def _():
        cnt_ref[...] = jnp.zeros_like(cnt_ref)

    cnt_ref[0:1, :] += jnp.sum(hit, axis=0, keepdims=True)


def _seq_flags(j, n_lat_tiles, tiles_per_seq):
    is_ctx = j >= n_lat_tiles
    first = is_ctx | (j % tiles_per_seq == 0)
    last = is_ctx | (j % tiles_per_seq == tiles_per_seq - 1)
    return first, last


def _with_halo(prev_ref, tile, next_ref, first, last):
    p = jnp.where(first, 0.0, prev_ref[...])
    n = jnp.where(last, 0.0, next_ref[...])
    return jnp.concatenate([p, tile, n], axis=0)


def _shift_rows(e, k):
    return pltpu.roll(e, (-k) % e.shape[0], 0)


def _merge_even_kernel(att_ref, gb_ref, u_ref, up_ref, un_ref, xl_ref, xc_ref, mod_ref, cw_ref, cb_ref,
                       on_ref, wo_ref, nf_ref, wr_ref, br_ref,
                       xo_ref, tok_ref, route_ref, cnt_ref, *, n_lat_tiles, tiles_per_seq):
    j = pl.program_id(0)
    first, last = _seq_flags(j, n_lat_tiles, tiles_per_seq)
    e = _with_halo(up_ref, u_ref[...], un_ref, first, last)
    cw = cw_ref[...]
    lo, hi = SUBLANES, SUBLANES + MERGE_TILE
    pos = lax.broadcasted_iota(jnp.int32, (MERGE_TILE, 1), 0) % SEQ_TILE
    is_ctx = j >= n_lat_tiles
    before = jnp.where(is_ctx & (pos == 0), 0.0, _shift_rows(e, -1)[lo:hi])
    after = jnp.where(is_ctx & (pos == SEQ_TILE - 1), 0.0, _shift_rows(e, 1)[lo:hi])
    conv = cw[0:1] * before + cw[1:2] * e[lo:hi] + cw[2:3] * after + cb_ref[...]
    conv = gb_ref[...] * conv
    x = _pick_tokens(xl_ref, xc_ref, n_lat_tiles)
    _tail(att_ref[...], conv, on_ref[...], wo_ref, x, mod_ref[0], nf_ref, wr_ref, br_ref,
          xo_ref, tok_ref, route_ref, cnt_ref)


def _halo_specs(width, n_rows):
    r = MERGE_TILE // SUBLANES
    n_blk = n_rows // SUBLANES
    prev = pl.BlockSpec((SUBLANES, width), lambda j: (jnp.maximum(j * r - 1, 0), 0))
    nxt = pl.BlockSpec((SUBLANES, width), lambda j: (jnp.minimum((j + 1) * r, n_blk - 1), 0))
    return prev, nxt


def _tail_specs(d, n_out_tok):
    row = lambda j: (j, 0)
    out_specs = [pl.BlockSpec((MERGE_TILE, d), row), pl.BlockSpec((MERGE_TILE, d // 2), row),
                 pl.BlockSpec((MERGE_TILE, LANES), row), pl.BlockSpec((SUBLANES, LANES), lambda j: (0, 0))]
    out_shape = [jax.ShapeDtypeStruct((n_out_tok, d), F32), jax.ShapeDtypeStruct((n_out_tok, d // 2), jnp.uint32),
                 jax.ShapeDtypeStruct((n_out_tok, LANES), F32), jax.ShapeDtypeStruct((SUBLANES, LANES), F32)]
    return out_specs, out_shape


def _merge_even(att, gb, u, xparts, modl, cw, cb, on, wo_bf, nf, wr, br, dims, n_out_tok):
    n_tok = u.shape[0]
    d = xparts[0].shape[1]
    n_batch, seq = dims
    tps = seq // MERGE_TILE
    mod_map, _ = _tile_maps(n_batch * tps, tps, n_batch)
    xs, xspecs = _token_operands(xparts, n_batch * tps, MERGE_TILE)
    sc_w = u.shape[1]
    row = lambda j: (j, 0)
    const = lambda j: (0, 0)
    prev, nxt = _halo_specs(sc_w, n_tok)
    out_specs, out_shape = _tail_specs(d, n_out_tok)
    return pl.pallas_call(
        functools.partial(_merge_even_kernel, n_lat_tiles=n_batch * tps, tiles_per_seq=tps),
        grid=(n_out_tok // MERGE_TILE,),
        in_specs=[pl.BlockSpec((MERGE_TILE, att.shape[1]), row),
                  pl.BlockSpec((MERGE_TILE, sc_w), row),
                  pl.BlockSpec((MERGE_TILE, sc_w), row), prev, nxt] + xspecs + [
                  pl.BlockSpec((1, SUBLANES, d), mod_map),
                  pl.BlockSpec(cw.shape, const), pl.BlockSpec((1, sc_w), const),
                  pl.BlockSpec((1, d), const), pl.BlockSpec(wo_bf.shape, const),
                  pl.BlockSpec((1, d), const), pl.BlockSpec(wr.shape, const),
                  pl.BlockSpec((1, LANES), const)],
        out_specs=out_specs, out_shape=out_shape,
        compiler_params=_cparams("arbitrary"),
        name="merge_even",
    )(att, gb, u, u, u, *xs, modl, cw, cb, on, wo_bf, nf, wr, br)


def _odd_projection(x, mod, g_ref, w_ref, ux_ref, gg_ref, up_ref):
    h = _modulate(x, g_ref[...], mod[0:1], mod[1:2]).astype(BF16)
    w = ux_ref.shape[1]
    ux_ref[...] = jnp.dot(h, w_ref[:, 0:w], preferred_element_type=F32)
    ug = jnp.dot(h, w_ref[:, w:2 * w], preferred_element_type=F32)
    gg_ref[...] = jax.nn.gelu(ug, approximate=True)
    up_ref[...] = jnp.dot(h, w_ref[:, 2 * w:3 * w], preferred_element_type=F32)


def _inproj_odd_kernel(xl_ref, xc_ref, mod_ref, g_ref, w_ref, ux_ref, gg_ref, up_ref, *, n_lat_tiles):
    x = _pick_tokens(xl_ref, xc_ref, n_lat_tiles)
    _odd_projection(x, mod_ref[0], g_ref, w_ref, ux_ref, gg_ref, up_ref)


def _inproj_odd(xparts, modl, g, w_bf, dims):
    n_tok = sum(p.shape[0] for p in xparts)
    d = xparts[0].shape[1]
    n_batch, seq = dims
    tile = TOK_TILE
    tps = seq // tile
    mod_map, _ = _tile_maps(n_batch * tps, tps, n_batch)
    xs, xspecs = _token_operands(xparts, n_batch * tps, tile)
    w = w_bf.shape[1] // 3
    row = lambda j: (j, 0)
    const = lambda j: (0, 0)
    return pl.pallas_call(
        functools.partial(_inproj_odd_kernel, n_lat_tiles=n_batch * tps),
        grid=(n_tok // tile,),
        in_specs=xspecs + [
                  pl.BlockSpec((1, SUBLANES, d), mod_map),
                  pl.BlockSpec((1, d), const),
                  pl.BlockSpec(w_bf.shape, const)],
        out_specs=[pl.BlockSpec((tile, w), row)] * 3,
        out_shape=[jax.ShapeDtypeStruct((n_tok, w), F32)] * 3,
        compiler_params=_cparams("arbitrary"),
        name="inproj_odd",
    )(*xs, modl, g, w_bf)


def _lru_conv(e, cw, cb):
    lo, hi = SUBLANES, SUBLANES + SEQ_TILE
    return (cw[0:1] * _shift_rows(e, -2)[lo:hi] + cw[1:2] * _shift_rows(e, -1)[lo:hi]
            + cw[2:3] * e[lo:hi] + cw[3:4] * _shift_rows(e, 1)[lo:hi] + cb)


def _lru_coeffs(xl, d, wa_ref, wx_ref, ba_ref, bx_ref, lam_ref, a_ref, b_ref):
    xb = xl.astype(BF16)
    chunk = wa_ref.shape[-1]
    n_chunk = xl.shape[1] // chunk
    pre_a = jnp.concatenate([jnp.dot(xb[:, c * chunk:(c + 1) * chunk], wa_ref[d, c],
                                     preferred_element_type=F32) for c in range(n_chunk)], axis=-1)
    pre_x = jnp.concatenate([jnp.dot(xb[:, c * chunk:(c + 1) * chunk], wx_ref[d, c],
                                     preferred_element_type=F32) for c in range(n_chunk)], axis=-1)
    tr = jnp.tanh(pre_a + ba_ref[d])
    tg = jnp.tanh(pre_x + bx_ref[d])
    c_half = (-0.5 * LRU_C) * jax.nn.softplus(-lam_ref[d])
    log_a = c_half * tr + c_half
    a_ref[...] = jnp.exp(log_a)
    th = jnp.tanh(log_a)
    b_ref[...] = jnp.sqrt(-0.5 * th / (1.0 - th)) * (1.0 + tg) * xl


def _scan_tile(a_ref, b_ref, h_ref, carry, reverse):
    n_grp = a_ref.shape[0] // SUBLANES
    row = lax.broadcasted_iota(jnp.int32, (SUBLANES, a_ref.shape[1]), 0)

    def body(g, carry):
        gi = (n_grp - 1 - g) if reverse else g
        r0 = pl.multiple_of(gi * SUBLANES, SUBLANES)
        a8 = a_ref[pl.ds(r0, SUBLANES), :]
        b8 = b_ref[pl.ds(r0, SUBLANES), :]
        for s in (1, 2, 4):
            if reverse:
                ok = row < SUBLANES - s
                sh = SUBLANES - s
            else:
                ok = row >= s
                sh = s
            a_sh = jnp.where(ok, pltpu.roll(a8, sh, 0), 1.0)
            b_sh = jnp.where(ok, pltpu.roll(b8, sh, 0), 0.0)
            b8 = a8 * b_sh + b8
            a8 = a8 * a_sh
        h8 = a8 * carry + b8
        if h_ref is not None:
            h_ref[pl.ds(r0, SUBLANES), :] = h8
        return h8[0:1] if reverse else h8[SUBLANES - 1:SUBLANES]

    return lax.fori_loop(0, n_grp, body, carry, unroll=4)


def _lru_kernel(uf_ref, ufp_ref, ufn_ref, ur_ref, urp_ref, urn_ref, uc_ref, cw_ref, cb_ref,
                wa_ref, wx_ref, ba_ref, bx_ref, lam_ref, hf_ref, hr_ref,
                cf_ref, cr_ref, a_ref, b_ref):
    i = pl.program_id(1)
    nt = pl.num_programs(1)
    cw, cb = cw_ref[...], cb_ref[...]
    coeffs = functools.partial(_lru_coeffs, wa_ref=wa_ref, wx_ref=wx_ref, ba_ref=ba_ref,
                               bx_ref=bx_ref, lam_ref=lam_ref, a_ref=a_ref, b_ref=b_ref)

    @pl.when(i == 0)
    def _():
        z = jnp.zeros((SUBLANES, uc_ref.shape[1]), F32)
        xc = _lru_conv(jnp.concatenate([z, uc_ref[...], z], axis=0), cw, cb)
        zero = jnp.zeros((1, uc_ref.shape[1]), F32)
        coeffs(xc, 0)
        cf_ref[...] = _scan_tile(a_ref, b_ref, None, zero, False)
        coeffs(xc, 1)
        cr_ref[...] = _scan_tile(a_ref, b_ref, None, zero, True)

    xf = _lru_conv(_with_halo(ufp_ref, uf_ref[...], ufn_ref, i == 0, i == nt - 1), cw, cb)
    coeffs(xf, 0)
    cf_ref[...] = _scan_tile(a_ref, b_ref, hf_ref, cf_ref[...], False)
    xr = _lru_conv(_with_halo(urp_ref, ur_ref[...], urn_ref, i == nt - 1, i == 0), cw, cb)
    coeffs(xr, 1)
    cr_ref[...] = _scan_tile(a_ref, b_ref, hr_ref, cr_ref[...], True)


def _lru(ux, cw, cb, wa_bd, wx_bd, ba, bx, lam, dims):
    n_tok, w = ux.shape
    n_batch, seq = dims
    nt = seq // SEQ_TILE
    n_lat = n_batch * seq
    r = SEQ_TILE // SUBLANES
    n_blk = n_tok // SUBLANES
    fwd = lambda b, i: b * nt + i
    rev = lambda b, i: b * nt + nt - 1 - i

    def specs(tile):
        return [pl.BlockSpec((SEQ_TILE, w), lambda b, i: (tile(b, i), 0)),
                pl.BlockSpec((SUBLANES, w), lambda b, i: (jnp.maximum(tile(b, i) * r - 1, 0), 0)),
                pl.BlockSpec((SUBLANES, w), lambda b, i: (jnp.minimum((tile(b, i) + 1) * r, n_blk - 1), 0))]

    const2 = lambda b, i: (0, 0)
    const3 = lambda b, i: (0, 0, 0)
    const4 = lambda b, i: (0, 0, 0, 0)
    return pl.pallas_call(
        _lru_kernel,
        grid=(n_batch, nt),
        in_specs=specs(fwd) + specs(rev) + [
            pl.BlockSpec((SEQ_TILE, w), lambda b, i: (n_lat // SEQ_TILE + b, 0)),
            pl.BlockSpec(cw.shape, const2), pl.BlockSpec((1, w), const2),
            pl.BlockSpec(wa_bd.shape, const4), pl.BlockSpec(wx_bd.shape, const4),
            pl.BlockSpec(ba.shape, const3), pl.BlockSpec(bx.shape, const3),
            pl.BlockSpec(lam.shape, const3)],
        out_specs=[pl.BlockSpec((SEQ_TILE, w), lambda b, i: (fwd(b, i), 0)),
                   pl.BlockSpec((SEQ_TILE, w), lambda b, i: (rev(b, i), 0))],
        out_shape=[jax.ShapeDtypeStruct((n_lat, w), F32)] * 2,
        scratch_shapes=[pltpu.VMEM((1, w), F32), pltpu.VMEM((1, w), F32),
                        pltpu.VMEM((SEQ_TILE, w), F32), pltpu.VMEM((SEQ_TILE, w), F32)],
        compiler_params=_cparams("arbitrary", "arbitrary"),
        name="rglru",
    )(ux, ux, ux, ux, ux, ux, ux, cw, cb, wa_bd, wx_bd, ba, bx, lam)


def _merge_odd_kernel(up_ref, upp_ref, upn_ref, hf_ref, hr_ref, gg_ref, x_ref, mod_ref,
                      pw_ref, ps_ref, on_ref, wo_ref, nf_ref, wr_ref, br_ref,
                      xo_ref, tok_ref, route_ref, cnt_ref, *, tiles_per_seq, seq):
    j = pl.program_id(0)
    ts = j % tiles_per_seq
    e = _with_halo(upp_ref, up_ref[...], upn_ref, ts == 0, ts == tiles_per_seq - 1)
    gw = pw_ref.shape[-1]
    lo, hi = SUBLANES, SUBLANES + MERGE_TILE
    tpos = ts * MERGE_TILE + lax.broadcasted_iota(jnp.int32, (MERGE_TILE, 1), 0)
    parts = []
    for g, win in enumerate(POOL_WINDOWS):
        eg = e[:, g * gw:(g + 1) * gw]
        p = eg
        span = 1
        while span < win:
            p = p + _shift_rows(p, span)
            span *= 2
        half = win // 2
        s = _shift_rows(p, -half)[lo:hi]
        cnt = (jnp.minimum(tpos + half, seq) - jnp.maximum(tpos - half, 0)).astype(F32)
        dlt = s / cnt - eg[lo:hi]
        parts.append(_bdot(dlt, pw_ref[g]))
    pool = jnp.concatenate(parts, axis=-1) * ps_ref[...]
    lru = (hf_ref[...] + hr_ref[...]) * gg_ref[...]
    _tail(pool, lru, on_ref[...], wo_ref, x_ref[...], mod_ref[0], nf_ref, wr_ref, br_ref,
          xo_ref, tok_ref, route_ref, cnt_ref)


def _merge_odd(up, hf, hr, gg, xf, modl, pw_bf, ps, on, wo_bf, nf, wr, br, dims):
    d = xf.shape[1]
    n_tok, w = up.shape
    n_batch, seq = dims
    tps = seq // MERGE_TILE
    n_lat = n_batch * seq
    mod_map, _ = _tile_maps(n_batch * tps, tps, n_batch)
    row = lambda j: (j, 0)
    const = lambda j: (0, 0)
    prev, nxt = _halo_specs(w, n_tok)
    out_specs, out_shape = _tail_specs(d, n_lat)
    return pl.pallas_call(
        functools.partial(_merge_odd_kernel, tiles_per_seq=tps, seq=seq),
        grid=(n_lat // MERGE_TILE,),
        in_specs=[pl.BlockSpec((MERGE_TILE, w), row), prev, nxt,
                  pl.BlockSpec((MERGE_TILE, w), row), pl.BlockSpec((MERGE_TILE, w), row),
                  pl.BlockSpec((MERGE_TILE, w), row),
                  pl.BlockSpec((MERGE_TILE, d), row),
                  pl.BlockSpec((1, SUBLANES, d), mod_map),
                  pl.BlockSpec(pw_bf.shape, lambda j: (0, 0, 0)), pl.BlockSpec((1, w), const),
                  pl.BlockSpec((1, d), const), pl.BlockSpec(wo_bf.shape, const),
                  pl.BlockSpec((1, d), const), pl.BlockSpec(wr.shape, const),
                  pl.BlockSpec((1, LANES), const)],
        out_specs=out_specs, out_shape=out_shape,
        compiler_params=_cparams("arbitrary"),
        name="merge_odd",
    )(up, up, up, hf, hr, gg, xf, modl, pw_bf, ps, on, wo_bf, nf, wr, br)


def _slots_kernel(route_ref, tot_ref, dest_ref, cnt_ref, run_ref, start_ref):
    i = pl.program_id(0)
    rt = route_ref[...]
    m = rt.shape[0]
    lane = lax.broadcasted_iota(jnp.int32, rt.shape, 1).astype(F32)
    oh1 = lane == rt[:, 0:1]
    oh2 = lane == rt[:, 1:2]
    s = jnp.where(oh1 | oh2, 1.0, 0.0)

    @pl.when(i == 0)
    def _():
        tot = tot_ref[0:1, :]
        padded = jnp.floor((tot + (MOE_BLOCK - 1)) / MOE_BLOCK) * MOE_BLOCK
        r = lax.broadcasted_iota(jnp.int32, (LANES, LANES), 0)
        c = lax.broadcasted_iota(jnp.int32, (LANES, LANES), 1)
        upper = jnp.where(r < c, 1.0, 0.0)
        p8 = jnp.broadcast_to(padded, (SUBLANES, LANES))
        start = jnp.dot(p8, upper, precision=HIGHEST, preferred_element_type=F32)[0:1]
        start_ref[...] = start
        run_ref[...] = jnp.zeros_like(run_ref)
        sub = lax.broadcasted_iota(jnp.int32, cnt_ref.shape, 0)
        cnt_ref[...] = jnp.where(sub == 0, tot, jnp.where(sub == 1, start, 0.0))

    r = lax.broadcasted_iota(jnp.int32, (m, m), 0)
    c = lax.broadcasted_iota(jnp.int32, (m, m), 1)
    lower = jnp.where(c < r, 1.0, 0.0).astype(BF16)
    prefix = jnp.dot(lower, s.astype(BF16), preferred_element_type=F32)
    base = prefix + run_ref[...] + start_ref[...]
    d1 = jnp.sum(jnp.where(oh1, base, 0.0), axis=-1, keepdims=True)
    d2 = jnp.sum(jnp.where(oh2, base, 0.0), axis=-1, keepdims=True)
    dest_ref[...] = jnp.where(lane == 0, d1, jnp.where(lane == 1, d2, 0.0)).astype(jnp.int32)
    run_ref[...] += jnp.sum(s, axis=0, keepdims=True)


def _slots(route, totals):
    n_tok = route.shape[0]
    nt = n_tok // TOK_TILE
    return pl.pallas_call(
        _slots_kernel,
        grid=(nt,),
        in_specs=[pl.BlockSpec((TOK_TILE, LANES), lambda i: (i, 0)),
                  pl.BlockSpec((SUBLANES, LANES), lambda i: (0, 0))],
        out_specs=[pl.BlockSpec((TOK_TILE, LANES), lambda i: (i, 0)),
                   pl.BlockSpec((SUBLANES, LANES), lambda i: (0, 0))],
        out_shape=[jax.ShapeDtypeStruct((n_tok, LANES), jnp.int32),
                   jax.ShapeDtypeStruct((SUBLANES, LANES), F32)],
        scratch_shapes=[pltpu.VMEM((1, LANES), F32)] * 2,
        compiler_params=_cparams("arbitrary"),
        name="moe_slots",
    )(route, totals)


def _for_each_row(n_rows, fn):
    def body(g, c):
        g8 = pl.multiple_of(g * SUBLANES, SUBLANES)
        for j in range(SUBLANES):
            fn(g8, j)
        return c
    lax.fori_loop(0, n_rows // SUBLANES, body, 0)


def _dispatch_kernel(dest_ref, pad_lo_ref, pad_hi_ref, tok_ref, xs_ref, zrow, sem, zsem):
    i = pl.program_id(0)
    base = i * TOK_TILE

    def zero_copy(r, n):
        return pltpu.make_async_copy(zrow.at[pl.ds(0, n)], xs_ref.at[pl.ds(r, n)], zsem)

    def for_each_pad_chunk(fn):
        def per_expert(e, c):
            lo, hi = pad_lo_ref[e], pad_hi_ref[e]
            a8 = (lo + SUBLANES - 1) // SUBLANES * SUBLANES
            a64 = (a8 + ZERO_CHUNK - 1) // ZERO_CHUNK * ZERO_CHUNK

            def rows(r, c):
                fn(zero_copy(r, 1))
                return c

            def eights(q, c):
                fn(zero_copy(pl.multiple_of(a8 + q * SUBLANES, SUBLANES), SUBLANES))
                return c

            def chunks(q, c):
                fn(zero_copy(pl.multiple_of(a64 + q * ZERO_CHUNK, ZERO_CHUNK), ZERO_CHUNK))
                return c

            lax.fori_loop(lo, a8, rows, 0)
            lax.fori_loop(0, (a64 - a8) // SUBLANES, eights, 0)
            lax.fori_loop(0, (hi - a64) // ZERO_CHUNK, chunks, 0)
            return c
        lax.fori_loop(0, N_EXPERTS, per_expert, 0)

    @pl.when(i == 0)
    def _():
        zrow[...] = jnp.zeros_like(zrow)
        for_each_pad_chunk(lambda cp: cp.start())

    def issue(g8, j):
        rows = tok_ref.at[pl.ds(g8, SUBLANES)]
        for k in range(TOP_K):
            d = dest_ref[(base + g8) * TOP_K + (j * TOP_K + k)]
            pltpu.make_async_copy(rows.at[pl.ds(j, 1)], xs_ref.at[pl.ds(d, 1)], sem).start(priority=k)

    _for_each_row(TOK_TILE, issue)
    for k in range(TOP_K):
        pltpu.make_async_copy(tok_ref, xs_ref.at[pl.ds(0, TOK_TILE)], sem).wait()

    @pl.when(i == pl.num_programs(0) - 1)
    def _():
        for_each_pad_chunk(lambda cp: cp.wait())


def _dispatch(dest_flat, pad_lo, pad_hi, tok, n_slots):
    n_tok, d = tok.shape
    return pl.pallas_call(
        _dispatch_kernel,
        grid_spec=pltpu.PrefetchScalarGridSpec(
            num_scalar_prefetch=3,
            grid=(n_tok // TOK_TILE,),
            in_specs=[pl.BlockSpec((TOK_TILE, d), lambda i, *_: (i, 0))],
            out_specs=pl.BlockSpec(memory_space=pl.ANY),
            scratch_shapes=[pltpu.VMEM((ZERO_CHUNK, d), tok.dtype),
                            pltpu.SemaphoreType.DMA(()), pltpu.SemaphoreType.DMA(())]),
        out_shape=jax.ShapeDtypeStruct((n_slots, d), tok.dtype),
        compiler_params=_cparams("arbitrary"),
        name="moe_dispatch",
    )(dest_flat, pad_lo, pad_hi, tok)


def _expert_kernel(be_ref, nu_ref, x_ref, w1_ref, w3_ref, w2_ref, y_ref, w1b, w3b, w2b):
    i = pl.program_id(0)

    @pl.when(i < nu_ref[0])
    def _():
        changed = (i == 0) | (be_ref[i] != be_ref[jnp.maximum(i - 1, 0)])

        @pl.when(changed)
        def _():
            w1b[...] = w1_ref[0, 0].astype(BF16)
            w3b[...] = w3_ref[0, 0].astype(BF16)
            w2b[...] = w2_ref[0, 0].astype(BF16)

        x_a, x_b = _unpack_bf16_pair(x_ref[...])
        x_a, x_b = x_a.astype(BF16), x_b.astype(BF16)
        half = x_ref.shape[1]
        a = (jnp.dot(x_a, w1b[0:half, :], preferred_element_type=F32)
             + jnp.dot(x_b, w1b[half:, :], preferred_element_type=F32))
        b = (jnp.dot(x_a, w3b[0:half, :], preferred_element_type=F32)
             + jnp.dot(x_b, w3b[half:, :], preferred_element_type=F32))
        y = _bdot(_silu(a) * b, w2b[...])
        y_ref[...] = _pack_bf16_pair(y[:, :half], y[:, half:])


def _experts(block_e, n_used, xs, w1, w3, w2, layer):
    n_slots, half_d = xs.shape
    d, de = w1.shape[-2], w1.shape[-1]
    n_blocks = n_slots // MOE_BLOCK
    blk = lambda i, be, nu: (jnp.minimum(i, nu[0] - 1), 0)
    wmap = lambda i, be, nu: (layer, be[i], 0, 0)
    return pl.pallas_call(
        _expert_kernel,
        grid_spec=pltpu.PrefetchScalarGridSpec(
            num_scalar_prefetch=2,
            grid=(n_blocks,),
            in_specs=[pl.BlockSpec((MOE_BLOCK, half_d), blk),
                      pl.BlockSpec((1, 1, d, de), wmap), pl.BlockSpec((1, 1, d, de), wmap),
                      pl.BlockSpec((1, 1, de, d), wmap)],
            out_specs=pl.BlockSpec((MOE_BLOCK, half_d), blk),
            scratch_shapes=[pltpu.VMEM((d, de), BF16), pltpu.VMEM((d, de), BF16),
                            pltpu.VMEM((de, d), BF16)]),
        out_shape=jax.ShapeDtypeStruct((n_slots, half_d), jnp.uint32),
        compiler_params=_cparams("arbitrary"),
        name="moe_experts",
    )(block_e, n_used, xs, w1, w3, w2)


def _combine_kernel(dest_ref, x_ref, route_ref, mod_ref, ys_ref, *rest, fuse_next):
    if fuse_next:
        nmod_ref, g_ref, w_ref, o_ref, ux_ref, gg_ref, up_ref, ybuf, sem = rest
    else:
        o_ref, ybuf, sem = rest
    i = pl.program_id(0)
    n = pl.num_programs(0)

    def gather(tile, slot):
        base = tile * TOK_TILE

        def issue(g8, j):
            for k in range(TOP_K):
                d = dest_ref[(base + g8) * TOP_K + (j * TOP_K + k)]
                rows = ybuf.at[slot, k, pl.ds(g8, SUBLANES)]
                pltpu.make_async_copy(ys_ref.at[pl.ds(d, 1)], rows.at[pl.ds(j, 1)],
                                      sem.at[slot]).start(priority=k)

        _for_each_row(TOK_TILE, issue)

    @pl.when(i == 0)
    def _():
        gather(0, 0)

    @pl.when(i + 1 < n)
    def _():
        gather(i + 1, (i + 1) % 2)

    slot = i % 2
    for k in range(TOP_K):
        pltpu.make_async_copy(ys_ref.at[pl.ds(0, TOK_TILE)], ybuf.at[slot, k], sem.at[slot]).wait()
    rt = route_ref[...]
    w1, w2 = rt[:, 2:3], rt[:, 3:4]
    y1a, y1b = _unpack_bf16_pair(ybuf[slot, 0])
    y2a, y2b = _unpack_bf16_pair(ybuf[slot, 1])
    y = jnp.concatenate([w1 * y1a + w2 * y2a, w1 * y1b + w2 * y2b], axis=1)
    x1 = x_ref[...] + mod_ref[0][5:6] * y
    o_ref[...] = x1
    if fuse_next:
        _odd_projection(x1, nmod_ref[0], g_ref, w_ref, ux_ref, gg_ref, up_ref)


def _combine(dest_flat, xn, route, modl, ys, dims, next_odd=None):
    n_tok, d = xn.shape
    n_batch, seq = dims
    tpb = seq // TOK_TILE
    row = lambda i, dest: (i, 0)
    const = lambda i, dest: (0, 0)
    mod_spec = pl.BlockSpec((1, SUBLANES, d), lambda i, dest: (jnp.minimum(i // tpb, n_batch), 0, 0))
    in_specs = [pl.BlockSpec((TOK_TILE, d), row), pl.BlockSpec((TOK_TILE, LANES), row), mod_spec,
                pl.BlockSpec(memory_space=pl.ANY)]
    out_specs = [pl.BlockSpec((TOK_TILE, d), row)]
    out_shape = [jax.ShapeDtypeStruct((n_tok, d), F32)]
    args = [dest_flat, xn, route, modl, ys]
    if next_odd is not None:
        nmod, g, w_bf = next_odd
        w = w_bf.shape[1] // 3
        in_specs += [mod_spec, pl.BlockSpec((1, d), const), pl.BlockSpec(w_bf.shape, const)]
        out_specs += [pl.BlockSpec((TOK_TILE, w), row)] * 3
        out_shape += [jax.ShapeDtypeStruct((n_tok, w), F32)] * 3
        args += [nmod, g, w_bf]
    return pl.pallas_call(
        functools.partial(_combine_kernel, fuse_next=next_odd is not None),
        grid_spec=pltpu.PrefetchScalarGridSpec(
            num_scalar_prefetch=1,
            grid=(n_tok // TOK_TILE,),
            in_specs=in_specs, out_specs=out_specs,
            scratch_shapes=[pltpu.VMEM((2, TOP_K, TOK_TILE, ys.shape[1]), ys.dtype),
                            pltpu.SemaphoreType.DMA((2,))]),
        out_shape=out_shape,
        compiler_params=_cparams("arbitrary"),
        name="moe_combine",
    )(*args)


def _moe(xn, tok, route, totals, modl, w1, w3, w2, layer, dims, next_odd):
    n_tok = tok.shape[0]
    n = n_tok * TOP_K
    n_blocks = (n + N_EXPERTS * (MOE_BLOCK - 1) + MOE_BLOCK - 1) // MOE_BLOCK
    dest, counts = _slots(route, totals)
    dest_flat = dest[:, :TOP_K].reshape(-1)
    cnt = counts[0, :N_EXPERTS].astype(jnp.int32)
    start = counts[1, :N_EXPERTS].astype(jnp.int32)
    blocks_per_e = (cnt + MOE_BLOCK - 1) // MOE_BLOCK
    blk_end = (start + blocks_per_e * MOE_BLOCK) // MOE_BLOCK
    n_used = blk_end[-1:]
    blk = jnp.minimum(jnp.arange(n_blocks, dtype=jnp.int32), n_used[0] - 1)
    block_e = jnp.sum((blk[:, None] >= blk_end[None, :]).astype(jnp.int32), axis=1)
    block_e = jnp.minimum(block_e, N_EXPERTS - 1)
    xs = _dispatch(dest_flat, start + cnt, blk_end * MOE_BLOCK, tok, n_blocks * MOE_BLOCK)
    ys = _experts(block_e, n_used, xs, w1, w3, w2, layer)
    return _combine(dest_flat, xn, route, modl, ys, dims, next_odd)


def _rope_tables(seq):
    rows = seq // GRID_W
    row = jnp.repeat(jnp.arange(rows), GRID_W).astype(F32)
    col = jnp.tile(jnp.arange(GRID_W), rows).astype(F32)
    inv_freq = ROPE_BASE ** (-jnp.arange(ROPE_PAIRS, dtype=F32) / ROPE_PAIRS)
    ar, ac = row[:, None] * inv_freq, col[:, None] * inv_freq
    cos = jnp.concatenate([jnp.cos(ar), jnp.cos(ar), jnp.cos(ac), jnp.cos(ac)], axis=-1)
    sin = jnp.concatenate([-jnp.sin(ar), jnp.sin(ar), -jnp.sin(ac), jnp.sin(ac)], axis=-1)
    reps = LANES // HEAD_DIM
    cos, sin = jnp.tile(cos, (1, reps)), jnp.tile(sin, (1, reps))
    cos = jnp.concatenate([cos, jnp.ones((TOK_TILE, LANES), F32)], axis=0)
    sin = jnp.concatenate([sin, jnp.zeros((TOK_TILE, LANES), F32)], axis=0)
    return cos, sin


def _head_mean_matrix():
    r = jnp.arange(LANES)
    return jnp.where((r[:, None] // HEAD_DIM) == (r[None, :] // HEAD_DIM), 1.0 / HEAD_DIM, 0.0).astype(BF16)


def _block_diag_chunks(w, chunk):
    dirs, nblk, bw, _ = w.shape
    per = chunk // bw
    w = w.reshape(dirs, nblk // per, per, bw, bw)
    eye = jnp.eye(per, dtype=w.dtype)
    out = jnp.einsum("dcpij,pq->dcpiqj", w, eye)
    return out.reshape(dirs, nblk // per, chunk, chunk).astype(BF16)


def _router_matrix(gw, gb, ew, eb):
    d = gw.shape[0]
    pad = LANES - N_EXPERTS - N_GROUPS
    wr = jnp.concatenate([ew, gw, jnp.zeros((d, pad), F32)], axis=1)
    br = jnp.concatenate([eb, gb, jnp.zeros((pad,), F32)])[None, :]
    hi = wr.astype(BF16)
    lo = (wr - hi.astype(F32)).astype(BF16)
    return jnp.concatenate([hi, lo], axis=1), br


def kernel(x, c, ctx, c_ctx, ada_w, ada_b, norm_mix, norm_ffn, out_norm, w_out, w_in_ab, q_norm, k_norm, attn_sink, sconv_w, sconv_b, w_in_cd, pool_w, pool_scale, lru_conv_w, lru_conv_b, lru_wa, lru_ba, lru_wx, lru_bx, lru_lambda, router_gw, router_gb, router_ew, router_eb, exp_w1, exp_w3, exp_w2):
    n_batch, seq, d = x.shape
    ctx_len = ctx.shape[1]
    depth = ada_w.shape[0]
    assert ctx_len == SEQ_TILE and seq % TOK_TILE == 0 and seq % GRID_W == 0
    assert seq % MERGE_TILE == 0 and (n_batch * ctx_len) % MERGE_TILE == 0 and MERGE_TILE % SEQ_TILE == 0
    assert depth == 2, "context-side odd-layer outputs are not implemented"
    dims = (n_batch, seq)
    n_lat = n_batch * seq

    cc = jnp.concatenate([c, c_ctx[None, :], jnp.zeros((SUBLANES - n_batch - 1, d), F32)], axis=0)
    mods = _ada(cc, ada_w, ada_b)
    mods = mods[:, :n_batch + 1].reshape(depth, n_batch + 1, N_MOD, d)
    mods = jnp.pad(mods, ((0, 0), (0, 0), (0, SUBLANES - N_MOD), (0, 0)))

    xparts = (x.reshape(n_lat, d), ctx.reshape(n_batch * ctx_len, d))
    cos_t, sin_t = _rope_tables(seq)
    pm = _head_mean_matrix()

    projected = None
    for i in range(depth):
        need_ctx = i < depth - 1
        j = i // 2
        modl = mods[i]
        n_out = n_lat + n_batch * ctx_len if need_ctx else n_lat
        wr, br = _router_matrix(router_gw[i], router_gb[i], router_ew[i], router_eb[i])
        wo_bf = w_out[i].astype(BF16)
        nm, nf, on = norm_mix[i][None, :], norm_ffn[i][None, :], out_norm[i][None, :]
        if i % 2 == 0:
            reps = LANES // HEAD_DIM
            q, kvx, gb, u = _inproj_even(xparts, modl, nm, w_in_ab[j].astype(BF16),
                                          jnp.tile(q_norm[j], reps)[None, :], jnp.tile(k_norm[j], reps)[None, :],
                                          cos_t, sin_t, pm, dims)
            att = _attention(q, kvx, attn_sink[j], dims, need_ctx)
            xn, tok, route, totals = _merge_even(att, gb, u, xparts, modl, sconv_w[j], sconv_b[j][None, :], on, wo_bf,
                                         nf, wr, br, dims, n_out)
        else:
            if projected is None:
                projected = _inproj_odd(xparts, modl, nm, w_in_cd[j].astype(BF16), dims)
            ux, gg, up = projected
            chunk = 2 * LANES
            hf, hr = _lru(ux, lru_conv_w[j], lru_conv_b[j][None, :],
                          _block_diag_chunks(0.5 * lru_wa[j], chunk), _block_diag_chunks(0.5 * lru_wx[j], chunk),
                          0.5 * lru_ba[j][:, None, :], 0.5 * lru_bx[j][:, None, :],
                          lru_lambda[j][:, None, :], dims)
            xn, tok, route, totals = _merge_odd(up, hf, hr, gg, xparts[0], modl, pool_w[j].astype(BF16),
                                        pool_scale[j][None, :], on, wo_bf, nf, wr, br, dims)
        next_odd = None
        if need_ctx and (i + 1) % 2 == 1:
            next_odd = (mods[i + 1], norm_mix[i + 1][None, :], w_in_cd[(i + 1) // 2].astype(BF16))
        outs = _moe(xn, tok, route, totals, modl, exp_w1, exp_w3, exp_w2, i, dims, next_odd)
        xparts = (outs[0],)
        projected = tuple(outs[1:]) if next_odd is not None else None
    return xparts[0][:n_lat].reshape(n_batch, seq, d)
```

```python
import functools

import jax
import jax.numpy as jnp
from jax import lax
from jax.experimental import pallas as pl
from jax.experimental.pallas import tpu as pltpu

F32 = jnp.float32
BF16 = jnp.bfloat16
HIGHEST = lax.Precision.HIGHEST
LOG2E = 1.4426950408889634

GRID_W = 64
EPS = 1e-6
HEAD_DIM = 64
N_Q_HEADS = 8
N_KV_HEADS = 2
Q_PER_KV = N_Q_HEADS // N_KV_HEADS
WINDOW = 128
ATT_BLOCK = 128
ROPE_PAIRS = HEAD_DIM // 4
ROPE_BASE = 10000.0
POOL_WINDOWS = (2, 4, 8, 16)
LRU_C = 8.0
N_GROUPS = 4
EXPERTS_PER_GROUP = 8
N_EXPERTS = N_GROUPS * EXPERTS_PER_GROUP
TOP_K = 2
MOE_BLOCK = 512
ZERO_CHUNK = 64
N_MOD = 6

SUBLANES = 8
LANES = 128
SEQ_TILE = 256
TOK_TILE = 512
MERGE_TILE = 512
ATT_TILE = 256
NEG = -1e30
VMEM_LIMIT = 48 * 1024 * 1024


def _cparams(*sem):
    return pltpu.CompilerParams(dimension_semantics=sem, vmem_limit_bytes=VMEM_LIMIT)


def _rms(x, g):
    ms = jnp.mean(x * x, axis=-1, keepdims=True)
    return x * lax.rsqrt(ms + EPS) * g


def _modulate(x, g, shift, scale):
    return _rms(x, g) * (1.0 + scale) + shift


def _bdot(a, b):
    return jnp.dot(a.astype(BF16), b, preferred_element_type=F32)


def _silu(x):
    return x * jax.nn.sigmoid(x)


def _pack_bf16_pair(a, b):
    ha = lax.bitcast_convert_type(a.astype(BF16).astype(F32), jnp.uint32)
    hb = lax.bitcast_convert_type(b.astype(BF16).astype(F32), jnp.uint32)
    return ha | (hb >> 16)


def _unpack_bf16_pair(w):
    a = lax.bitcast_convert_type(w & jnp.uint32(0xFFFF0000), F32)
    b = lax.bitcast_convert_type(w << 16, F32)
    return a, b


def _ada_kernel(c_ref, w_ref, b_ref, o_ref):
    s = _silu(c_ref[...])
    o_ref[0] = jnp.dot(s, w_ref[0], precision=HIGHEST, preferred_element_type=F32) + b_ref[0]


def _ada(cc, ada_w, ada_b):
    depth, d, n = ada_w.shape
    tn = 1536
    return pl.pallas_call(
        _ada_kernel,
        grid=(depth, n // tn),
        in_specs=[pl.BlockSpec((SUBLANES, d), lambda l, j: (0, 0)),
                  pl.BlockSpec((1, d, tn), lambda l, j: (l, 0, j)),
                  pl.BlockSpec((1, 1, tn), lambda l, j: (l, 0, j))],
        out_specs=pl.BlockSpec((1, SUBLANES, tn), lambda l, j: (l, 0, j)),
        out_shape=jax.ShapeDtypeStruct((depth, SUBLANES, n), F32),
        compiler_params=_cparams("arbitrary", "arbitrary"),
        name="ada",
    )(cc, ada_w, ada_b.reshape(depth, 1, n))


def _head_norm_rope(t, pm, g, cos, sin):
    m = t.shape[0]
    t2 = t * t
    hi = t2.astype(BF16)
    lo = (t2 - hi.astype(F32)).astype(BF16)
    ms2 = jnp.dot(jnp.concatenate([hi, lo], axis=0), pm, preferred_element_type=F32)
    ms = ms2[0:m] + ms2[m:2 * m]
    t = t * lax.rsqrt(ms + EPS) * g
    lane = lax.broadcasted_iota(jnp.int32, t.shape, 1)
    first_half = (lane % (2 * ROPE_PAIRS)) < ROPE_PAIRS
    partner = jnp.where(first_half, pltpu.roll(t, LANES - ROPE_PAIRS, 1), pltpu.roll(t, ROPE_PAIRS, 1))
    return t * cos + partner * sin


def _token_operands(xparts, n_lat_tiles, tile):
    d = xparts[0].shape[1]
    lat = pl.BlockSpec((tile, d), lambda j: (jnp.minimum(j, n_lat_tiles - 1), 0))
    if len(xparts) == 2:
        ctx = pl.BlockSpec((tile, d), lambda j: (jnp.maximum(j - n_lat_tiles, 0), 0))
        return list(xparts), [lat, ctx]
    ctx = pl.BlockSpec((tile, d), lambda j: (jnp.maximum(j, n_lat_tiles), 0))
    return [xparts[0], xparts[0]], [lat, ctx]


def _pick_tokens(xl_ref, xc_ref, n_lat_tiles):
    return jnp.where(pl.program_id(0) >= n_lat_tiles, xc_ref[...], xl_ref[...])


def _inproj_even_kernel(xl_ref, xc_ref, mod_ref, g_ref, w_ref, qg_ref, kg_ref, cos_ref, sin_ref, pm_ref,
                        q_ref, kvx_ref, gb_ref, u_ref, *, n_lat_tiles):
    mod = mod_ref[0]
    x = _pick_tokens(xl_ref, xc_ref, n_lat_tiles)
    h = _modulate(x, g_ref[...], mod[0:1], mod[1:2]).astype(BF16)
    cos, sin, pm = cos_ref[...], sin_ref[...], pm_ref[...]
    kv_w = N_KV_HEADS * HEAD_DIM
    att_w = N_Q_HEADS * HEAD_DIM
    sc_w = gb_ref.shape[1]
    kv = jnp.dot(h, w_ref[:, 0:2 * kv_w], preferred_element_type=F32)
    k = _head_norm_rope(kv[:, :kv_w], pm, kg_ref[...], cos, sin)
    v = kv[:, kv_w:]
    kvx_ref[...] = jnp.concatenate([k, pltpu.roll(k, HEAD_DIM, 1), v, pltpu.roll(v, HEAD_DIM, 1)],
                                   axis=-1).astype(BF16)
    q = jnp.dot(h, w_ref[:, 2 * kv_w:2 * kv_w + att_w], preferred_element_type=F32)
    qscale = HEAD_DIM ** -0.5 * LOG2E
    for c in range(att_w // LANES):
        qc = _head_norm_rope(q[:, c * LANES:(c + 1) * LANES], pm, qg_ref[...], cos, sin)
        q_ref[:, c * LANES:(c + 1) * LANES] = (qc * qscale).astype(BF16)
    o = 2 * kv_w + att_w
    gb_ref[...] = jnp.dot(h, w_ref[:, o:o + sc_w], preferred_element_type=F32)
    gc = jnp.dot(h, w_ref[:, o + sc_w:o + 2 * sc_w], preferred_element_type=F32)
    gx = jnp.dot(h, w_ref[:, o + 2 * sc_w:o + 3 * sc_w], preferred_element_type=F32)
    u_ref[...] = gc * gx


def _tile_maps(n_lat_tiles, tiles_per_seq, n_batch):
    def mod_map(j):
        return (jnp.minimum(j // tiles_per_seq, n_batch), 0, 0)

    def pos_map(j):
        return (jnp.where(j < n_lat_tiles, j % tiles_per_seq, tiles_per_seq), 0)

    return mod_map, pos_map


def _inproj_even(xparts, modl, g, w_bf, q_g, k_g, cos_t, sin_t, pm, dims):
    n_tok = sum(p.shape[0] for p in xparts)
    d = xparts[0].shape[1]
    n_batch, seq = dims
    tile = TOK_TILE
    tps = seq // tile
    mod_map, pos_map = _tile_maps(n_batch * tps, tps, n_batch)
    xs, xspecs = _token_operands(xparts, n_batch * tps, tile)
    kv_w, att_w = N_KV_HEADS * HEAD_DIM, N_Q_HEADS * HEAD_DIM
    sc_w = (w_bf.shape[1] - 2 * kv_w - att_w) // 3
    row = lambda j: (j, 0)
    const = lambda j: (0, 0)
    return pl.pallas_call(
        functools.partial(_inproj_even_kernel, n_lat_tiles=n_batch * tps),
        grid=(n_tok // tile,),
        in_specs=xspecs + [
                  pl.BlockSpec((1, SUBLANES, d), mod_map),
                  pl.BlockSpec((1, d), const),
                  pl.BlockSpec(w_bf.shape, const),
                  pl.BlockSpec((1, LANES), const),
                  pl.BlockSpec((1, LANES), const),
                  pl.BlockSpec((tile, LANES), pos_map),
                  pl.BlockSpec((tile, LANES), pos_map),
                  pl.BlockSpec((LANES, LANES), const)],
        out_specs=[pl.BlockSpec((tile, att_w), row),
                   pl.BlockSpec((tile, 4 * kv_w), row),
                   pl.BlockSpec((tile, sc_w), row),
                   pl.BlockSpec((tile, sc_w), row)],
        out_shape=[jax.ShapeDtypeStruct((n_tok, att_w), BF16),
                   jax.ShapeDtypeStruct((n_tok, 4 * kv_w), BF16),
                   jax.ShapeDtypeStruct((n_tok, sc_w), F32),
                   jax.ShapeDtypeStruct((n_tok, sc_w), F32)],
        compiler_params=_cparams("arbitrary"),
        name="inproj_even",
    )(*xs, modl, g, w_bf, q_g, k_g, cos_t, sin_t, pm)


def _attn_heads(q, kvx, mask_fn, sink_ref, o_ref):
    m, n = q.shape[0], kvx.shape[0]
    k_nat, k_swp = kvx[:, 0:LANES], kvx[:, LANES:2 * LANES]
    v_nat, v_swp = kvx[:, 2 * LANES:3 * LANES], kvx[:, 3 * LANES:4 * LANES]
    lo_n = lax.broadcasted_iota(jnp.int32, (n, LANES), 1) < HEAD_DIM
    lo_m = lax.broadcasted_iota(jnp.int32, (2 * m, LANES), 1) < HEAD_DIM
    top = lax.broadcasted_iota(jnp.int32, (2 * m, 1), 0) < m
    zero = jnp.zeros_like(k_nat)
    r2 = lax.broadcasted_iota(jnp.int32, (2 * n, LANES), 0) < n
    l2 = lax.broadcasted_iota(jnp.int32, (2 * n, LANES), 1) < HEAD_DIM
    ones_both = jnp.where(r2 == l2, 1.0, 0.0).astype(BF16)
    outs = []
    for h in range(N_KV_HEADS):
        k_lo, k_hi = (k_nat, k_swp) if h == 0 else (k_swp, k_nat)
        v_lo, v_hi = (v_nat, v_swp) if h == 0 else (v_swp, v_nat)
        g0 = 2 * h
        q2 = jnp.concatenate([q[:, g0 * LANES:(g0 + 1) * LANES],
                              q[:, (g0 + 1) * LANES:(g0 + 2) * LANES]], axis=0)
        k_both = jnp.concatenate([jnp.where(lo_n, k_lo, zero), jnp.where(lo_n, zero, k_hi)], axis=0)
        v_both = jnp.concatenate([jnp.where(lo_n, v_lo, zero), jnp.where(lo_n, zero, v_hi)], axis=0)
        s = lax.dot_general(q2, k_both, (((1,), (1,)), ((), ())), preferred_element_type=F32)
        hq = Q_PER_KV * h
        halves = []
        for par in range(2):
            sp = mask_fn(s[:, par * n:(par + 1) * n])
            snk = jnp.where(top, sink_ref[hq + par], sink_ref[hq + 2 + par]) * LOG2E
            mx = jnp.maximum(jnp.max(sp, axis=-1, keepdims=True), snk)
            halves.append((jnp.exp2(sp - mx), jnp.exp2(snk - mx)))
        (p_lo, e_lo), (p_hi, e_hi) = halves
        p = jnp.concatenate([p_lo, p_hi], axis=1).astype(BF16)
        num = jnp.dot(p, v_both, preferred_element_type=F32)
        den = jnp.dot(p, ones_both, preferred_element_type=F32) + jnp.where(lo_m, e_lo, e_hi)
        o = num / den
        outs += [o[0:m], o[m:2 * m]]
    o_ref[...] = jnp.concatenate(outs, axis=-1)


def _attn_lat_kernel(sink_ref, q_ref, kp_ref, kc_ref, kn_ref, kx_ref, o_ref):
    i = pl.program_id(1)
    nb = pl.num_programs(1)
    kvx = jnp.concatenate([kp_ref[...], kc_ref[...], kn_ref[...], kx_ref[...]], axis=0)
    m, w = ATT_TILE, WINDOW
    tok = lax.broadcasted_iota(jnp.int32, (2 * m, m), 0) % m
    col = lax.broadcasted_iota(jnp.int32, (2 * m, m), 1)
    tok_w = lax.broadcasted_iota(jnp.int32, (2 * m, w), 0) % m
    col_w = lax.broadcasted_iota(jnp.int32, (2 * m, w), 1)
    ok_prev = (col_w >= tok_w) & (i > 0)
    ok_cur = (col - tok <= w) & (tok - col <= w)
    ok_next = (col_w + (m - w) <= tok_w) & (i < nb - 1)

    def mask_fn(s):
        return jnp.concatenate([jnp.where(ok_prev, s[:, 0:w], NEG), jnp.where(ok_cur, s[:, w:w + m], NEG),
                                jnp.where(ok_next, s[:, w + m:2 * w + m], NEG), s[:, 2 * w + m:]], axis=1)

    _attn_heads(q_ref[...], kvx, mask_fn, sink_ref, o_ref)


def _attn_ctx_kernel(sink_ref, att_in_ref, q_ref, kx_ref, o_ref):
    del att_in_ref
    _attn_heads(q_ref[...], kx_ref[...], lambda s: s, sink_ref, o_ref)


def _attention(q, kvx, sink, dims, need_ctx):
    n_tok = q.shape[0]
    n_batch, seq = dims
    nb = seq // ATT_TILE
    r = ATT_TILE // WINDOW
    nw = seq // WINDOW
    n_lat = n_batch * seq
    ctx_len = (n_tok - n_lat) // n_batch
    att_w, kvx_w = q.shape[1], kvx.shape[1]
    ctx_blk0 = n_lat // ctx_len
    cur = lambda b, i: (b * nb + i, 0)
    prev = lambda b, i: (b * nw + jnp.maximum(i * r - 1, 0), 0)
    nxt = lambda b, i: (b * nw + jnp.minimum((i + 1) * r, nw - 1), 0)
    cx = lambda b, i: (ctx_blk0 + b, 0)
    smem = pl.BlockSpec(memory_space=pltpu.SMEM)
    att = pl.pallas_call(
        _attn_lat_kernel,
        grid=(n_batch, nb),
        in_specs=[smem, pl.BlockSpec((ATT_TILE, att_w), cur),
                  pl.BlockSpec((WINDOW, kvx_w), prev), pl.BlockSpec((ATT_TILE, kvx_w), cur),
                  pl.BlockSpec((WINDOW, kvx_w), nxt), pl.BlockSpec((ctx_len, kvx_w), cx)],
        out_specs=pl.BlockSpec((ATT_TILE, att_w), cur),
        out_shape=jax.ShapeDtypeStruct((n_tok, att_w), F32),
        compiler_params=_cparams("arbitrary", "arbitrary"),
        name="attn_latent",
    )(sink, q, kvx, kvx, kvx, kvx)
    if not need_ctx:
        return att
    cxb = lambda b: (ctx_blk0 + b, 0)
    return pl.pallas_call(
        _attn_ctx_kernel,
        grid=(n_batch,),
        in_specs=[smem, pl.BlockSpec(memory_space=pl.ANY),
                  pl.BlockSpec((ctx_len, att_w), cxb), pl.BlockSpec((ctx_len, kvx_w), cxb)],
        out_specs=pl.BlockSpec((ctx_len, att_w), cxb),
        out_shape=jax.ShapeDtypeStruct((n_tok, att_w), F32),
        input_output_aliases={1: 0},
        compiler_params=_cparams("arbitrary"),
        name="attn_context",
    )(sink, att, q, kvx)


def _route(logits):
    lane = lax.broadcasted_iota(jnp.int32, logits.shape, 1).astype(F32)
    big = float(LANES)
    gl = jnp.where((lane >= N_EXPERTS) & (lane < N_EXPERTS + N_GROUPS), logits, NEG)
    gmax = jnp.max(gl, axis=-1, keepdims=True)
    gidx = jnp.min(jnp.where(gl == gmax, lane, big), axis=-1, keepdims=True) - N_EXPERTS
    p_sel = 1.0 / jnp.sum(jnp.exp(gl - gmax), axis=-1, keepdims=True)
    lo = gidx * EXPERTS_PER_GROUP
    el = jnp.where((lane >= lo) & (lane < lo + EXPERTS_PER_GROUP), logits, NEG)
    m1 = jnp.max(el, axis=-1, keepdims=True)
    i1 = jnp.min(jnp.where(el == m1, lane, big), axis=-1, keepdims=True)
    el2 = jnp.where(lane == i1, NEG, el)
    m2 = jnp.max(el2, axis=-1, keepdims=True)
    i2 = jnp.min(jnp.where(el2 == m2, lane, big), axis=-1, keepdims=True)
    t = jnp.exp(m2 - m1)
    w1 = p_sel / (1.0 + t)
    w2 = p_sel * t / (1.0 + t)
    return jnp.where(lane == 0, i1, jnp.where(lane == 1, i2, jnp.where(lane == 2, w1,
                     jnp.where(lane == 3, w2, 0.0))))


def _tail(n1, n2, gain, wo_ref, x, mod, nf_ref, wr_ref, br_ref, xo_ref, tok_ref, route_ref, cnt_ref):
    half = n1.shape[1]
    y = (_bdot(_rms(n1, gain[:, :half]), wo_ref[0:half, :])
         + _bdot(_rms(n2, gain[:, half:]), wo_ref[half:, :]))
    xn = x + mod[2:3] * y
    xo_ref[...] = xn
    tok = _modulate(xn, nf_ref[...], mod[3:4], mod[4:5])
    half_d = tok.shape[1] // 2
    tok_ref[...] = _pack_bf16_pair(tok[:, :half_d], tok[:, half_d:])
    t_hi = tok.astype(BF16)
    t_lo = (tok - t_hi.astype(F32)).astype(BF16)
    hh = jnp.dot(t_hi, wr_ref[...], preferred_element_type=F32)
    logits = (hh[:, :LANES] + hh[:, LANES:] + br_ref[...]
              + jnp.dot(t_lo, wr_ref[:, 0:LANES], preferred_element_type=F32))
    route = _route(logits)
    route_ref[...] = route
    lane = lax.broadcasted_iota(jnp.int32, route.shape, 1).astype(F32)
    hit = jnp.where((lane == route[:, 0:1]) | (lane == route[:, 1:2]), 1.0, 0.0)

    @pl.when(pl.program_id(0) == 0)
    def _():
        cnt_ref[...] = jnp.zeros_like(cnt_ref)

    cnt_ref[0:1, :] += jnp.sum(hit, axis=0, keepdims=True)


def _seq_flags(j, n_lat_tiles, tiles_per_seq):
    is_ctx = j >= n_lat_tiles
    first = is_ctx | (j % tiles_per_seq == 0)
    last = is_ctx | (j % tiles_per_seq == tiles_per_seq - 1)
    return first, last


def _with_halo(prev_ref, tile, next_ref, first, last):
    p = jnp.where(first, 0.0, prev_ref[...])
    n = jnp.where(last, 0.0, next_ref[...])
    return jnp.concatenate([p, tile, n], axis=0)


def _shift_rows(e, k):
    return pltpu.roll(e, (-k) % e.shape[0], 0)


def _merge_even_kernel(att_ref, gb_ref, u_ref, up_ref, un_ref, xl_ref, xc_ref, mod_ref, cw_ref, cb_ref,
                       on_ref, wo_ref, nf_ref, wr_ref, br_ref,
                       xo_ref, tok_ref, route_ref, cnt_ref, *, n_lat_tiles, tiles_per_seq):
    j = pl.program_id(0)
    first, last = _seq_flags(j, n_lat_tiles, tiles_per_seq)
    e = _with_halo(up_ref, u_ref[...], un_ref, first, last)
    cw = cw_ref[...]
    lo, hi = SUBLANES, SUBLANES + MERGE_TILE
    pos = lax.broadcasted_iota(jnp.int32, (MERGE_TILE, 1), 0) % SEQ_TILE
    is_ctx = j >= n_lat_tiles
    before = jnp.where(is_ctx & (pos == 0), 0.0, _shift_rows(e, -1)[lo:hi])
    after = jnp.where(is_ctx & (pos == SEQ_TILE - 1), 0.0, _shift_rows(e, 1)[lo:hi])
    conv = cw[0:1] * before + cw[1:2] * e[lo:hi] + cw[2:3] * after + cb_ref[...]
    conv = gb_ref[...] * conv
    x = _pick_tokens(xl_ref, xc_ref, n_lat_tiles)
    _tail(att_ref[...], conv, on_ref[...], wo_ref, x, mod_ref[0], nf_ref, wr_ref, br_ref,
          xo_ref, tok_ref, route_ref, cnt_ref)


def _halo_specs(width, n_rows):
    r = MERGE_TILE // SUBLANES
    n_blk = n_rows // SUBLANES
    prev = pl.BlockSpec((SUBLANES, width), lambda j: (jnp.maximum(j * r - 1, 0), 0))
    nxt = pl.BlockSpec((SUBLANES, width), lambda j: (jnp.minimum((j + 1) * r, n_blk - 1), 0))
    return prev, nxt


def _tail_specs(d, n_out_tok):
    row = lambda j: (j, 0)
    out_specs = [pl.BlockSpec((MERGE_TILE, d), row), pl.BlockSpec((MERGE_TILE, d // 2), row),
                 pl.BlockSpec((MERGE_TILE, LANES), row), pl.BlockSpec((SUBLANES, LANES), lambda j: (0, 0))]
    out_shape = [jax.ShapeDtypeStruct((n_out_tok, d), F32), jax.ShapeDtypeStruct((n_out_tok, d // 2), jnp.uint32),
                 jax.ShapeDtypeStruct((n_out_tok, LANES), F32), jax.ShapeDtypeStruct((SUBLANES, LANES), F32)]
    return out_specs, out_shape


def _merge_even(att, gb, u, xparts, modl, cw, cb, on, wo_bf, nf, wr, br, dims, n_out_tok):
    n_tok = u.shape[0]
    d = xparts[0].shape[1]
    n_batch, seq = dims
    tps = seq // MERGE_TILE
    mod_map, _ = _tile_maps(n_batch * tps, tps, n_batch)
    xs, xspecs = _token_operands(xparts, n_batch * tps, MERGE_TILE)
    sc_w = u.shape[1]
    row = lambda j: (j, 0)
    const = lambda j: (0, 0)
    prev, nxt = _halo_specs(sc_w, n_tok)
    out_specs, out_shape = _tail_specs(d, n_out_tok)
    return pl.pallas_call(
        functools.partial(_merge_even_kernel, n_lat_tiles=n_batch * tps, tiles_per_seq=tps),
        grid=(n_out_tok // MERGE_TILE,),
        in_specs=[pl.BlockSpec((MERGE_TILE, att.shape[1]), row),
                  pl.BlockSpec((MERGE_TILE, sc_w), row),
                  pl.BlockSpec((MERGE_TILE, sc_w), row), prev, nxt] + xspecs + [
                  pl.BlockSpec((1, SUBLANES, d), mod_map),
                  pl.BlockSpec(cw.shape, const), pl.BlockSpec((1, sc_w), const),
                  pl.BlockSpec((1, d), const), pl.BlockSpec(wo_bf.shape, const),
                  pl.BlockSpec((1, d), const), pl.BlockSpec(wr.shape, const),
                  pl.BlockSpec((1, LANES), const)],
        out_specs=out_specs, out_shape=out_shape,
        compiler_params=_cparams("arbitrary"),
        name="merge_even",
    )(att, gb, u, u, u, *xs, modl, cw, cb, on, wo_bf, nf, wr, br)


def _odd_projection(x, mod, g_ref, w_ref, ux_ref, gg_ref, up_ref):
    h = _modulate(x, g_ref[...], mod[0:1], mod[1:2]).astype(BF16)
    w = ux_ref.shape[1]
    ux_ref[...] = jnp.dot(h, w_ref[:, 0:w], preferred_element_type=F32)
    ug = jnp.dot(h, w_ref[:, w:2 * w], preferred_element_type=F32)
    gg_ref[...] = jax.nn.gelu(ug, approximate=True)
    up_ref[...] = jnp.dot(h, w_ref[:, 2 * w:3 * w], preferred_element_type=F32)


def _inproj_odd_kernel(xl_ref, xc_ref, mod_ref, g_ref, w_ref, ux_ref, gg_ref, up_ref, *, n_lat_tiles):
    x = _pick_tokens(xl_ref, xc_ref, n_lat_tiles)
    _odd_projection(x, mod_ref[0], g_ref, w_ref, ux_ref, gg_ref, up_ref)


def _inproj_odd(xparts, modl, g, w_bf, dims):
    n_tok = sum(p.shape[0] for p in xparts)
    d = xparts[0].shape[1]
    n_batch, seq = dims
    tile = TOK_TILE
    tps = seq // tile
    mod_map, _ = _tile_maps(n_batch * tps, tps, n_batch)
    xs, xspecs = _token_operands(xparts, n_batch * tps, tile)
    w = w_bf.shape[1] // 3
    row = lambda j: (j, 0)
    const = lambda j: (0, 0)
    return pl.pallas_call(
        functools.partial(_inproj_odd_kernel, n_lat_tiles=n_batch * tps),
        grid=(n_tok // tile,),
        in_specs=xspecs + [
                  pl.BlockSpec((1, SUBLANES, d), mod_map),
                  pl.BlockSpec((1, d), const),
                  pl.BlockSpec(w_bf.shape, const)],
        out_specs=[pl.BlockSpec((tile, w), row)] * 3,
        out_shape=[jax.ShapeDtypeStruct((n_tok, w), F32)] * 3,
        compiler_params=_cparams("arbitrary"),
        name="inproj_odd",
    )(*xs, modl, g, w_bf)


def _lru_conv(e, cw, cb):
    lo, hi = SUBLANES, SUBLANES + SEQ_TILE
    return (cw[0:1] * _shift_rows(e, -2)[lo:hi] + cw[1:2] * _shift_rows(e, -1)[lo:hi]
            + cw[2:3] * e[lo:hi] + cw[3:4] * _shift_rows(e, 1)[lo:hi] + cb)


def _lru_coeffs(xl, d, wa_ref, wx_ref, ba_ref, bx_ref, lam_ref, a_ref, b_ref):
    xb = xl.astype(BF16)
    chunk = wa_ref.shape[-1]
    n_chunk = xl.shape[1] // chunk
    pre_a = jnp.concatenate([jnp.dot(xb[:, c * chunk:(c + 1) * chunk], wa_ref[d, c],
                                     preferred_element_type=F32) for c in range(n_chunk)], axis=-1)
    pre_x = jnp.concatenate([jnp.dot(xb[:, c * chunk:(c + 1) * chunk], wx_ref[d, c],
                                     preferred_element_type=F32) for c in range(n_chunk)], axis=-1)
    tr = jnp.tanh(pre_a + ba_ref[d])
    tg = jnp.tanh(pre_x + bx_ref[d])
    c_half = (-0.5 * LRU_C) * jax.nn.softplus(-lam_ref[d])
    log_a = c_half * tr + c_half
    a_ref[...] = jnp.exp(log_a)
    th = jnp.tanh(log_a)
    b_ref[...] = jnp.sqrt(-0.5 * th / (1.0 - th)) * (1.0 + tg) * xl


def _scan_tile(a_ref, b_ref, h_ref, carry, reverse):
    n_grp = a_ref.shape[0] // SUBLANES
    row = lax.broadcasted_iota(jnp.int32, (SUBLANES, a_ref.shape[1]), 0)

    def body(g, carry):
        gi = (n_grp - 1 - g) if reverse else g
        r0 = pl.multiple_of(gi * SUBLANES, SUBLANES)
        a8 = a_ref[pl.ds(r0, SUBLANES), :]
        b8 = b_ref[pl.ds(r0, SUBLANES), :]
        for s in (1, 2, 4):
            if reverse:
                ok = row < SUBLANES - s
                sh = SUBLANES - s
            else:
                ok = row >= s
                sh = s
            a_sh = jnp.where(ok, pltpu.roll(a8, sh, 0), 1.0)
            b_sh = jnp.where(ok, pltpu.roll(b8, sh, 0), 0.0)
            b8 = a8 * b_sh + b8
            a8 = a8 * a_sh
        h8 = a8 * carry + b8
        if h_ref is not None:
            h_ref[pl.ds(r0, SUBLANES), :] = h8
        return h8[0:1] if reverse else h8[SUBLANES - 1:SUBLANES]

    return lax.fori_loop(0, n_grp, body, carry, unroll=4)


def _lru_kernel(uf_ref, ufp_ref, ufn_ref, ur_ref, urp_ref, urn_ref, uc_ref, cw_ref, cb_ref,
                wa_ref, wx_ref, ba_ref, bx_ref, lam_ref, hf_ref, hr_ref,
                cf_ref, cr_ref, a_ref, b_ref):
    i = pl.program_id(1)
    nt = pl.num_programs(1)
    cw, cb = cw_ref[...], cb_ref[...]
    coeffs = functools.partial(_lru_coeffs, wa_ref=wa_ref, wx_ref=wx_ref, ba_ref=ba_ref,
                               bx_ref=bx_ref, lam_ref=lam_ref, a_ref=a_ref, b_ref=b_ref)

    @pl.when(i == 0)
    def _():
        z = jnp.zeros((SUBLANES, uc_ref.shape[1]), F32)
        xc = _lru_conv(jnp.concatenate([z, uc_ref[...], z], axis=0), cw, cb)
        zero = jnp.zeros((1, uc_ref.shape[1]), F32)
        coeffs(xc, 0)
        cf_ref[...] = _scan_tile(a_ref, b_ref, None, zero, False)
        coeffs(xc, 1)
        cr_ref[...] = _scan_tile(a_ref, b_ref, None, zero, True)

    xf = _lru_conv(_with_halo(ufp_ref, uf_ref[...], ufn_ref, i == 0, i == nt - 1), cw, cb)
    coeffs(xf, 0)
    cf_ref[...] = _scan_tile(a_ref, b_ref, hf_ref, cf_ref[...], False)
    xr = _lru_conv(_with_halo(urp_ref, ur_ref[...], urn_ref, i == nt - 1, i == 0), cw, cb)
    coeffs(xr, 1)
    cr_ref[...] = _scan_tile(a_ref, b_ref, hr_ref, cr_ref[...], True)


def _lru(ux, cw, cb, wa_bd, wx_bd, ba, bx, lam, dims):
    n_tok, w = ux.shape
    n_batch, seq = dims
    nt = seq // SEQ_TILE
    n_lat = n_batch * seq
    r = SEQ_TILE // SUBLANES
    n_blk = n_tok // SUBLANES
    fwd = lambda b, i: b * nt + i
    rev = lambda b, i: b * nt + nt - 1 - i

    def specs(tile):
        return [pl.BlockSpec((SEQ_TILE, w), lambda b, i: (tile(b, i), 0)),
                pl.BlockSpec((SUBLANES, w), lambda b, i: (jnp.maximum(tile(b, i) * r - 1, 0), 0)),
                pl.BlockSpec((SUBLANES, w), lambda b, i: (jnp.minimum((tile(b, i) + 1) * r, n_blk - 1), 0))]

    const2 = lambda b, i: (0, 0)
    const3 = lambda b, i: (0, 0, 0)
    const4 = lambda b, i: (0, 0, 0, 0)
    return pl.pallas_call(
        _lru_kernel,
        grid=(n_batch, nt),
        in_specs=specs(fwd) + specs(rev) + [
            pl.BlockSpec((SEQ_TILE, w), lambda b, i: (n_lat // SEQ_TILE + b, 0)),
            pl.BlockSpec(cw.shape, const2), pl.BlockSpec((1, w), const2),
            pl.BlockSpec(wa_bd.shape, const4), pl.BlockSpec(wx_bd.shape, const4),
            pl.BlockSpec(ba.shape, const3), pl.BlockSpec(bx.shape, const3),
            pl.BlockSpec(lam.shape, const3)],
        out_specs=[pl.BlockSpec((SEQ_TILE, w), lambda b, i: (fwd(b, i), 0)),
                   pl.BlockSpec((SEQ_TILE, w), lambda b, i: (rev(b, i), 0))],
        out_shape=[jax.ShapeDtypeStruct((n_lat, w), F32)] * 2,
        scratch_shapes=[pltpu.VMEM((1, w), F32), pltpu.VMEM((1, w), F32),
                        pltpu.VMEM((SEQ_TILE, w), F32), pltpu.VMEM((SEQ_TILE, w), F32)],
        compiler_params=_cparams("arbitrary", "arbitrary"),
        name="rglru",
    )(ux, ux, ux, ux, ux, ux, ux, cw, cb, wa_bd, wx_bd, ba, bx, lam)


def _merge_odd_kernel(up_ref, upp_ref, upn_ref, hf_ref, hr_ref, gg_ref, x_ref, mod_ref,
                      pw_ref, ps_ref, on_ref, wo_ref, nf_ref, wr_ref, br_ref,
                      xo_ref, tok_ref, route_ref, cnt_ref, *, tiles_per_seq, seq):
    j = pl.program_id(0)
    ts = j % tiles_per_seq
    e = _with_halo(upp_ref, up_ref[...], upn_ref, ts == 0, ts == tiles_per_seq - 1)
    gw = pw_ref.shape[-1]
    lo, hi = SUBLANES, SUBLANES + MERGE_TILE
    tpos = ts * MERGE_TILE + lax.broadcasted_iota(jnp.int32, (MERGE_TILE, 1), 0)
    parts = []
    for g, win in enumerate(POOL_WINDOWS):
        eg = e[:, g * gw:(g + 1) * gw]
        p = eg
        span = 1
        while span < win:
            p = p + _shift_rows(p, span)
            span *= 2
        half = win // 2
        s = _shift_rows(p, -half)[lo:hi]
        cnt = (jnp.minimum(tpos + half, seq) - jnp.maximum(tpos - half, 0)).astype(F32)
        dlt = s / cnt - eg[lo:hi]
        parts.append(_bdot(dlt, pw_ref[g]))
    pool = jnp.concatenate(parts, axis=-1) * ps_ref[...]
    lru = (hf_ref[...] + hr_ref[...]) * gg_ref[...]
    _tail(pool, lru, on_ref[...], wo_ref, x_ref[...], mod_ref[0], nf_ref, wr_ref, br_ref,
          xo_ref, tok_ref, route_ref, cnt_ref)


def _merge_odd(up, hf, hr, gg, xf, modl, pw_bf, ps, on, wo_bf, nf, wr, br, dims):
    d = xf.shape[1]
    n_tok, w = up.shape
    n_batch, seq = dims
    tps = seq // MERGE_TILE
    n_lat = n_batch * seq
    mod_map, _ = _tile_maps(n_batch * tps, tps, n_batch)
    row = lambda j: (j, 0)
    const = lambda j: (0, 0)
    prev, nxt = _halo_specs(w, n_tok)
    out_specs, out_shape = _tail_specs(d, n_lat)
    return pl.pallas_call(
        functools.partial(_merge_odd_kernel, tiles_per_seq=tps, seq=seq),
        grid=(n_lat // MERGE_TILE,),
        in_specs=[pl.BlockSpec((MERGE_TILE, w), row), prev, nxt,
                  pl.BlockSpec((MERGE_TILE, w), row), pl.BlockSpec((MERGE_TILE, w), row),
                  pl.BlockSpec((MERGE_TILE, w), row),
                  pl.BlockSpec((MERGE_TILE, d), row),
                  pl.BlockSpec((1, SUBLANES, d), mod_map),
                  pl.BlockSpec(pw_bf.shape, lambda j: (0, 0, 0)), pl.BlockSpec((1, w), const),
                  pl.BlockSpec((1, d), const), pl.BlockSpec(wo_bf.shape, const),
                  pl.BlockSpec((1, d), const), pl.BlockSpec(wr.shape, const),
                  pl.BlockSpec((1, LANES), const)],
        out_specs=out_specs, out_shape=out_shape,
        compiler_params=_cparams("arbitrary"),
        name="merge_odd",
    )(up, up, up, hf, hr, gg, xf, modl, pw_bf, ps, on, wo_bf, nf, wr, br)


def _slots_kernel(route_ref, tot_ref, dest_ref, cnt_ref, run_ref, start_ref):
    i = pl.program_id(0)
    rt = route_ref[...]
    m = rt.shape[0]
    lane = lax.broadcasted_iota(jnp.int32, rt.shape, 1).astype(F32)
    oh1 = lane == rt[:, 0:1]
    oh2 = lane == rt[:, 1:2]
    s = jnp.where(oh1 | oh2, 1.0, 0.0)

    @pl.when(i == 0)
    def _():
        tot = tot_ref[0:1, :]
        padded = jnp.floor((tot + (MOE_BLOCK - 1)) / MOE_BLOCK) * MOE_BLOCK
        r = lax.broadcasted_iota(jnp.int32, (LANES, LANES), 0)
        c = lax.broadcasted_iota(jnp.int32, (LANES, LANES), 1)
        upper = jnp.where(r < c, 1.0, 0.0)
        p8 = jnp.broadcast_to(padded, (SUBLANES, LANES))
        start = jnp.dot(p8, upper, precision=HIGHEST, preferred_element_type=F32)[0:1]
        start_ref[...] = start
        run_ref[...] = jnp.zeros_like(run_ref)
        sub = lax.broadcasted_iota(jnp.int32, cnt_ref.shape, 0)
        cnt_ref[...] = jnp.where(sub == 0, tot, jnp.where(sub == 1, start, 0.0))

    r = lax.broadcasted_iota(jnp.int32, (m, m), 0)
    c = lax.broadcasted_iota(jnp.int32, (m, m), 1)
    lower = jnp.where(c < r, 1.0, 0.0).astype(BF16)
    prefix = jnp.dot(lower, s.astype(BF16), preferred_element_type=F32)
    base = prefix + run_ref[...] + start_ref[...]
    d1 = jnp.sum(jnp.where(oh1, base, 0.0), axis=-1, keepdims=True)
    d2 = jnp.sum(jnp.where(oh2, base, 0.0), axis=-1, keepdims=True)
    dest_ref[...] = jnp.where(lane == 0, d1, jnp.where(lane == 1, d2, 0.0)).astype(jnp.int32)
    run_ref[...] += jnp.sum(s, axis=0, keepdims=True)


def _slots(route, totals):
    n_tok = route.shape[0]
    nt = n_tok // TOK_TILE
    return pl.pallas_call(
        _slots_kernel,
        grid=(nt,),
        in_specs=[pl.BlockSpec((TOK_TILE, LANES), lambda i: (i, 0)),
                  pl.BlockSpec((SUBLANES, LANES), lambda i: (0, 0))],
        out_specs=[pl.BlockSpec((TOK_TILE, LANES), lambda i: (i, 0)),
                   pl.BlockSpec((SUBLANES, LANES), lambda i: (0, 0))],
        out_shape=[jax.ShapeDtypeStruct((n_tok, LANES), jnp.int32),
                   jax.ShapeDtypeStruct((SUBLANES, LANES), F32)],
        scratch_shapes=[pltpu.VMEM((1, LANES), F32)] * 2,
        compiler_params=_cparams("arbitrary"),
        name="moe_slots",
    )(route, totals)


def _for_each_row(n_rows, fn):
    def body(g, c):
        g8 = pl.multiple_of(g * SUBLANES, SUBLANES)
        for j in range(SUBLANES):
            fn(g8, j)
        return c
    lax.fori_loop(0, n_rows // SUBLANES, body, 0)


def _dispatch_kernel(dest_ref, pad_lo_ref, pad_hi_ref, tok_ref, xs_ref, zrow, sem, zsem):
    i = pl.program_id(0)
    base = i * TOK_TILE

    def zero_copy(r, n):
        return pltpu.make_async_copy(zrow.at[pl.ds(0, n)], xs_ref.at[pl.ds(r, n)], zsem)

    def for_each_pad_chunk(fn):
        def per_expert(e, c):
            lo, hi = pad_lo_ref[e], pad_hi_ref[e]
            a8 = (lo + SUBLANES - 1) // SUBLANES * SUBLANES
            a64 = (a8 + ZERO_CHUNK - 1) // ZERO_CHUNK * ZERO_CHUNK

            def rows(r, c):
                fn(zero_copy(r, 1))
                return c

            def eights(q, c):
                fn(zero_copy(pl.multiple_of(a8 + q * SUBLANES, SUBLANES), SUBLANES))
                return c

            def chunks(q, c):
                fn(zero_copy(pl.multiple_of(a64 + q * ZERO_CHUNK, ZERO_CHUNK), ZERO_CHUNK))
                return c

            lax.fori_loop(lo, a8, rows, 0)
            lax.fori_loop(0, (a64 - a8) // SUBLANES, eights, 0)
            lax.fori_loop(0, (hi - a64) // ZERO_CHUNK, chunks, 0)
            return c
        lax.fori_loop(0, N_EXPERTS, per_expert, 0)

    @pl.when(i == 0)
    def _():
        zrow[...] = jnp.zeros_like(zrow)
        for_each_pad_chunk(lambda cp: cp.start())

    def issue(g8, j):
        rows = tok_ref.at[pl.ds(g8, SUBLANES)]
        for k in range(TOP_K):
            d = dest_ref[(base + g8) * TOP_K + (j * TOP_K + k)]
            pltpu.make_async_copy(rows.at[pl.ds(j, 1)], xs_ref.at[pl.ds(d, 1)], sem).start(priority=k)

    _for_each_row(TOK_TILE, issue)
    for k in range(TOP_K):
        pltpu.make_async_copy(tok_ref, xs_ref.at[pl.ds(0, TOK_TILE)], sem).wait()

    @pl.when(i == pl.num_programs(0) - 1)
    def _():
        for_each_pad_chunk(lambda cp: cp.wait())


def _dispatch(dest_flat, pad_lo, pad_hi, tok, n_slots):
    n_tok, d = tok.shape
    return pl.pallas_call(
        _dispatch_kernel,
        grid_spec=pltpu.PrefetchScalarGridSpec(
            num_scalar_prefetch=3,
            grid=(n_tok // TOK_TILE,),
            in_specs=[pl.BlockSpec((TOK_TILE, d), lambda i, *_: (i, 0))],
            out_specs=pl.BlockSpec(memory_space=pl.ANY),
            scratch_shapes=[pltpu.VMEM((ZERO_CHUNK, d), tok.dtype),
                            pltpu.SemaphoreType.DMA(()), pltpu.SemaphoreType.DMA(())]),
        out_shape=jax.ShapeDtypeStruct((n_slots, d), tok.dtype),
        compiler_params=_cparams("arbitrary"),
        name="moe_dispatch",
    )(dest_flat, pad_lo, pad_hi, tok)


def _expert_kernel(be_ref, nu_ref, x_ref, w1_ref, w3_ref, w2_ref, y_ref, w1b, w3b, w2b):
    i = pl.program_id(0)

    @pl.when(i < nu_ref[0])
    def _():
        changed = (i == 0) | (be_ref[i] != be_ref[jnp.maximum(i - 1, 0)])

        @pl.when(changed)
        def _():
            w1b[...] = w1_ref[0, 0].astype(BF16)
            w3b[...] = w3_ref[0, 0].astype(BF16)
            w2b[...] = w2_ref[0, 0].astype(BF16)

        x_a, x_b = _unpack_bf16_pair(x_ref[...])
        x_a, x_b = x_a.astype(BF16), x_b.astype(BF16)
        half = x_ref.shape[1]
        a = (jnp.dot(x_a, w1b[0:half, :], preferred_element_type=F32)
             + jnp.dot(x_b, w1b[half:, :], preferred_element_type=F32))
        b = (jnp.dot(x_a, w3b[0:half, :], preferred_element_type=F32)
             + jnp.dot(x_b, w3b[half:, :], preferred_element_type=F32))
        y = _bdot(_silu(a) * b, w2b[...])
        y_ref[...] = _pack_bf16_pair(y[:, :half], y[:, half:])


def _experts(block_e, n_used, xs, w1, w3, w2, layer):
    n_slots, half_d = xs.shape
    d, de = w1.shape[-2], w1.shape[-1]
    n_blocks = n_slots // MOE_BLOCK
    blk = lambda i, be, nu: (jnp.minimum(i, nu[0] - 1), 0)
    wmap = lambda i, be, nu: (layer, be[i], 0, 0)
    return pl.pallas_call(
        _expert_kernel,
        grid_spec=pltpu.PrefetchScalarGridSpec(
            num_scalar_prefetch=2,
            grid=(n_blocks,),
            in_specs=[pl.BlockSpec((MOE_BLOCK, half_d), blk),
                      pl.BlockSpec((1, 1, d, de), wmap), pl.BlockSpec((1, 1, d, de), wmap),
                      pl.BlockSpec((1, 1, de, d), wmap)],
            out_specs=pl.BlockSpec((MOE_BLOCK, half_d), blk),
            scratch_shapes=[pltpu.VMEM((d, de), BF16), pltpu.VMEM((d, de), BF16),
                            pltpu.VMEM((de, d), BF16)]),
        out_shape=jax.ShapeDtypeStruct((n_slots, half_d), jnp.uint32),
        compiler_params=_cparams("arbitrary"),
        name="moe_experts",
    )(block_e, n_used, xs, w1, w3, w2)


def _combine_kernel(dest_ref, x_ref, route_ref, mod_ref, ys_ref, *rest, fuse_next):
    if fuse_next:
        nmod_ref, g_ref, w_ref, o_ref, ux_ref, gg_ref, up_ref, yb0, yb1, sem = rest
    else:
        o_ref, yb0, yb1, sem = rest
    i = pl.program_id(0)
    n = pl.num_programs(0)
    bufs = ((yb0, sem.at[0]), (yb1, sem.at[1]))

    def issue(tile, half):
        ybuf, s = bufs[half]
        base = tile * (TOK_TILE * TOP_K)
        for r in range(TOK_TILE):
            for k in range(TOP_K):
                d = dest_ref[base + (r * TOP_K + k)]
                pltpu.make_async_copy(ys_ref.at[pl.ds(d, 1)], ybuf.at[k, pl.ds(r, 1)], s).start(priority=k)

    def wait(half):
        ybuf, s = bufs[half]
        for k in range(TOP_K):
            pltpu.make_async_copy(ys_ref.at[pl.ds(0, TOK_TILE)], ybuf.at[k], s).wait()

    def compute(half):
        ybuf, _ = bufs[half]
        rows = pl.ds(half * TOK_TILE, TOK_TILE)
        rt = route_ref[rows, :]
        w1, w2 = rt[:, 2:3], rt[:, 3:4]
        y1a, y1b = _unpack_bf16_pair(ybuf[0])
        y2a, y2b = _unpack_bf16_pair(ybuf[1])
        y = jnp.concatenate([w1 * y1a + w2 * y2a, w1 * y1b + w2 * y2b], axis=1)
        x1 = x_ref[rows, :] + mod_ref[0][5:6] * y
        o_ref[rows, :] = x1
        if fuse_next:
            _odd_projection(x1, nmod_ref[0], g_ref, w_ref, ux_ref.at[rows], gg_ref.at[rows], up_ref.at[rows])

    @pl.when(i == 0)
    def _():
        issue(0, 0)

    wait(0)
    issue(2 * i + 1, 1)
    compute(0)
    wait(1)
    issue(jnp.minimum(2 * i + 2, 2 * n - 1), 0)
    compute(1)

    @pl.when(i == n - 1)
    def _():
        wait(0)


def _combine(dest_flat, xn, route, modl, ys, dims, next_odd=None):
    n_tok, d = xn.shape
    n_batch, seq = dims
    blk = 2 * TOK_TILE
    assert seq % blk == 0 and n_tok % blk == 0
    tpb = seq // blk
    row = lambda i, dest: (i, 0)
    const = lambda i, dest: (0, 0)
    mod_spec = pl.BlockSpec((1, SUBLANES, d), lambda i, dest: (jnp.minimum(i // tpb, n_batch), 0, 0))
    in_specs = [pl.BlockSpec((blk, d), row), pl.BlockSpec((blk, LANES), row), mod_spec,
                pl.BlockSpec(memory_space=pl.ANY)]
    out_specs = [pl.BlockSpec((blk, d), row)]
    out_shape = [jax.ShapeDtypeStruct((n_tok, d), F32)]
    args = [dest_flat, xn, route, modl, ys]
    if next_odd is not None:
        nmod, g, w_bf = next_odd
        w = w_bf.shape[1] // 3
        in_specs += [mod_spec, pl.BlockSpec((1, d), const), pl.BlockSpec(w_bf.shape, const)]
        out_specs += [pl.BlockSpec((blk, w), row)] * 3
        out_shape += [jax.ShapeDtypeStruct((n_tok, w), F32)] * 3
        args += [nmod, g, w_bf]
    ybuf = pltpu.VMEM((TOP_K, TOK_TILE, ys.shape[1]), ys.dtype)
    return pl.pallas_call(
        functools.partial(_combine_kernel, fuse_next=next_odd is not None),
        grid_spec=pltpu.PrefetchScalarGridSpec(
            num_scalar_prefetch=1,
            grid=(n_tok // blk,),
            in_specs=in_specs, out_specs=out_specs,
            scratch_shapes=[ybuf, ybuf, pltpu.SemaphoreType.DMA((2,))]),
        out_shape=out_shape,
        compiler_params=_cparams("arbitrary"),
        name="moe_combine",
    )(*args)


def _moe(xn, tok, route, totals, modl, w1, w3, w2, layer, dims, next_odd):
    n_tok = tok.shape[0]
    n = n_tok * TOP_K
    n_blocks = (n + N_EXPERTS * (MOE_BLOCK - 1) + MOE_BLOCK - 1) // MOE_BLOCK
    dest, counts = _slots(route, totals)
    dest_flat = dest[:, :TOP_K].reshape(-1)
    cnt = counts[0, :N_EXPERTS].astype(jnp.int32)
    start = counts[1, :N_EXPERTS].astype(jnp.int32)
    blocks_per_e = (cnt + MOE_BLOCK - 1) // MOE_BLOCK
    blk_end = (start + blocks_per_e * MOE_BLOCK) // MOE_BLOCK
    n_used = blk_end[-1:]
    blk = jnp.minimum(jnp.arange(n_blocks, dtype=jnp.int32), n_used[0] - 1)
    block_e = jnp.sum((blk[:, None] >= blk_end[None, :]).astype(jnp.int32), axis=1)
    block_e = jnp.minimum(block_e, N_EXPERTS - 1)
    xs = _dispatch(dest_flat, start + cnt, blk_end * MOE_BLOCK, tok, n_blocks * MOE_BLOCK)
    ys = _experts(block_e, n_used, xs, w1, w3, w2, layer)
    return _combine(dest_flat, xn, route, modl, ys, dims, next_odd)


def _rope_tables(seq):
    rows = seq // GRID_W
    row = jnp.repeat(jnp.arange(rows), GRID_W).astype(F32)
    col = jnp.tile(jnp.arange(GRID_W), rows).astype(F32)
    inv_freq = ROPE_BASE ** (-jnp.arange(ROPE_PAIRS, dtype=F32) / ROPE_PAIRS)
    ar, ac = row[:, None] * inv_freq, col[:, None] * inv_freq
    cos = jnp.concatenate([jnp.cos(ar), jnp.cos(ar), jnp.cos(ac), jnp.cos(ac)], axis=-1)
    sin = jnp.concatenate([-jnp.sin(ar), jnp.sin(ar), -jnp.sin(ac), jnp.sin(ac)], axis=-1)
    reps = LANES // HEAD_DIM
    cos, sin = jnp.tile(cos, (1, reps)), jnp.tile(sin, (1, reps))
    cos = jnp.concatenate([cos, jnp.ones((TOK_TILE, LANES), F32)], axis=0)
    sin = jnp.concatenate([sin, jnp.zeros((TOK_TILE, LANES), F32)], axis=0)
    return cos, sin


def _head_mean_matrix():
    r = jnp.arange(LANES)
    return jnp.where((r[:, None] // HEAD_DIM) == (r[None, :] // HEAD_DIM), 1.0 / HEAD_DIM, 0.0).astype(BF16)


def _block_diag_chunks(w, chunk):
    dirs, nblk, bw, _ = w.shape
    per = chunk // bw
    w = w.reshape(dirs, nblk // per, per, bw, bw)
    eye = jnp.eye(per, dtype=w.dtype)
    out = jnp.einsum("dcpij,pq->dcpiqj", w, eye)
    return out.reshape(dirs, nblk // per, chunk, chunk).astype(BF16)


def _router_matrix(gw, gb, ew, eb):
    d = gw.shape[0]
    pad = LANES - N_EXPERTS - N_GROUPS
    wr = jnp.concatenate([ew, gw, jnp.zeros((d, pad), F32)], axis=1)
    br = jnp.concatenate([eb, gb, jnp.zeros((pad,), F32)])[None, :]
    hi = wr.astype(BF16)
    lo = (wr - hi.astype(F32)).astype(BF16)
    return jnp.concatenate([hi, lo], axis=1), br


def kernel(x, c, ctx, c_ctx, ada_w, ada_b, norm_mix, norm_ffn, out_norm, w_out, w_in_ab, q_norm, k_norm, attn_sink, sconv_w, sconv_b, w_in_cd, pool_w, pool_scale, lru_conv_w, lru_conv_b, lru_wa, lru_ba, lru_wx, lru_bx, lru_lambda, router_gw, router_gb, router_ew, router_eb, exp_w1, exp_w3, exp_w2):
    n_batch, seq, d = x.shape
    ctx_len = ctx.shape[1]
    depth = ada_w.shape[0]
    assert ctx_len == SEQ_TILE and seq % TOK_TILE == 0 and seq % GRID_W == 0
    assert seq % MERGE_TILE == 0 and (n_batch * ctx_len) % MERGE_TILE == 0 and MERGE_TILE % SEQ_TILE == 0
    assert depth == 2, "context-side odd-layer outputs are not implemented"
    dims = (n_batch, seq)
    n_lat = n_batch * seq

    cc = jnp.concatenate([c, c_ctx[None, :], jnp.zeros((SUBLANES - n_batch - 1, d), F32)], axis=0)
    mods = _ada(cc, ada_w, ada_b)
    mods = mods[:, :n_batch + 1].reshape(depth, n_batch + 1, N_MOD, d)
    mods = jnp.pad(mods, ((0, 0), (0, 0), (0, SUBLANES - N_MOD), (0, 0)))

    xparts = (x.reshape(n_lat, d), ctx.reshape(n_batch * ctx_len, d))
    cos_t, sin_t = _rope_tables(seq)
    pm = _head_mean_matrix()

    projected = None
    for i in range(depth):
        need_ctx = i < depth - 1
        j = i // 2
        modl = mods[i]
        n_out = n_lat + n_batch * ctx_len if need_ctx else n_lat
        wr, br = _router_matrix(router_gw[i], router_gb[i], router_ew[i], router_eb[i])
        wo_bf = w_out[i].astype(BF16)
        nm, nf, on = norm_mix[i][None, :], norm_ffn[i][None, :], out_norm[i][None, :]
        if i % 2 == 0:
            reps = LANES // HEAD_DIM
            q, kvx, gb, u = _inproj_even(xparts, modl, nm, w_in_ab[j].astype(BF16),
                                          jnp.tile(q_norm[j], reps)[None, :], jnp.tile(k_norm[j], reps)[None, :],
                                          cos_t, sin_t, pm, dims)
            att = _attention(q, kvx, attn_sink[j], dims, need_ctx)
            xn, tok, route, totals = _merge_even(att, gb, u, xparts, modl, sconv_w[j], sconv_b[j][None, :], on, wo_bf,
                                         nf, wr, br, dims, n_out)
        else:
            if projected is None:
                projected = _inproj_odd(xparts, modl, nm, w_in_cd[j].astype(BF16), dims)
            ux, gg, up = projected
            chunk = 2 * LANES
            hf, hr = _lru(ux, lru_conv_w[j], lru_conv_b[j][None, :],
                          _block_diag_chunks(0.5 * lru_wa[j], chunk), _block_diag_chunks(0.5 * lru_wx[j], chunk),
                          0.5 * lru_ba[j][:, None, :], 0.5 * lru_bx[j][:, None, :],
                          lru_lambda[j][:, None, :], dims)
            xn, tok, route, totals = _merge_odd(up, hf, hr, gg, xparts[0], modl, pool_w[j].astype(BF16),
                                        pool_scale[j][None, :], on, wo_bf, nf, wr, br, dims)
        next_odd = None
        if need_ctx and (i + 1) % 2 == 1:
            next_odd = (mods[i + 1], norm_mix[i + 1][None, :], w_in_cd[(i + 1) // 2].astype(BF16))
        outs = _moe(xn, tok, route, totals, modl, exp_w1, exp_w3, exp_w2, i, dims, next_odd)
        xparts = (outs[0],)
        projected = tuple(outs[1:]) if next_odd is not None else None
    return xparts[0][:n_lat].reshape(n_batch, seq, d)
```

```python
import functools

import jax
import jax.numpy as jnp
from jax import lax
from jax.experimental import pallas as pl
from jax.experimental.pallas import tpu as pltpu

F32 = jnp.float32
BF16 = jnp.bfloat16
HIGHEST = lax.Precision.HIGHEST
LOG2E = 1.4426950408889634

GRID_W = 64
EPS = 1e-6
HEAD_DIM = 64
N_Q_HEADS = 8
N_KV_HEADS = 2
Q_PER_KV = N_Q_HEADS // N_KV_HEADS
WINDOW = 128
ATT_BLOCK = 128
ROPE_PAIRS = HEAD_DIM // 4
ROPE_BASE = 10000.0
POOL_WINDOWS = (2, 4, 8, 16)
LRU_C = 8.0
N_GROUPS = 4
EXPERTS_PER_GROUP = 8
N_EXPERTS = N_GROUPS * EXPERTS_PER_GROUP
TOP_K = 2
MOE_BLOCK = 512
ZERO_CHUNK = 64
N_MOD = 6

SUBLANES = 8
LANES = 128
SEQ_TILE = 256
TOK_TILE = 512
MERGE_TILE = 512
ATT_TILE = 256
LRU_TILE = 512
NEG = -1e30
VMEM_LIMIT = 48 * 1024 * 1024


def _cparams(*sem):
    return pltpu.CompilerParams(dimension_semantics=sem, vmem_limit_bytes=VMEM_LIMIT)


def _rms(x, g):
    ms = jnp.mean(x * x, axis=-1, keepdims=True)
    return x * lax.rsqrt(ms + EPS) * g


def _modulate(x, g, shift, scale):
    return _rms(x, g) * (1.0 + scale) + shift


def _bdot(a, b):
    return jnp.dot(a.astype(BF16), b, preferred_element_type=F32)


def _silu(x):
    return x * jax.nn.sigmoid(x)


def _pack_bf16_pair(a, b):
    ha = lax.bitcast_convert_type(a.astype(BF16).astype(F32), jnp.uint32)
    hb = lax.bitcast_convert_type(b.astype(BF16).astype(F32), jnp.uint32)
    return ha | (hb >> 16)


def _unpack_bf16_pair(w):
    a = lax.bitcast_convert_type(w & jnp.uint32(0xFFFF0000), F32)
    b = lax.bitcast_convert_type(w << 16, F32)
    return a, b


def _ada_kernel(c_ref, w_ref, b_ref, o_ref):
    s = _silu(c_ref[...])
    o_ref[0] = jnp.dot(s, w_ref[0], precision=HIGHEST, preferred_element_type=F32) + b_ref[0]


def _ada(cc, ada_w, ada_b):
    depth, d, n = ada_w.shape
    tn = n // 2
    return pl.pallas_call(
        _ada_kernel,
        grid=(depth, n // tn),
        in_specs=[pl.BlockSpec((SUBLANES, d), lambda l, j: (0, 0)),
                  pl.BlockSpec((1, d, tn), lambda l, j: (l, 0, j)),
                  pl.BlockSpec((1, 1, tn), lambda l, j: (l, 0, j))],
        out_specs=pl.BlockSpec((1, SUBLANES, tn), lambda l, j: (l, 0, j)),
        out_shape=jax.ShapeDtypeStruct((depth, SUBLANES, n), F32),
        compiler_params=_cparams("arbitrary", "arbitrary"),
        name="ada",
    )(cc, ada_w, ada_b.reshape(depth, 1, n))


def _head_norm_rope(t, pm, g, cos, sin):
    m = t.shape[0]
    t2 = t * t
    hi = t2.astype(BF16)
    lo = (t2 - hi.astype(F32)).astype(BF16)
    ms2 = jnp.dot(jnp.concatenate([hi, lo], axis=0), pm, preferred_element_type=F32)
    ms = ms2[0:m] + ms2[m:2 * m]
    t = t * lax.rsqrt(ms + EPS) * g
    lane = lax.broadcasted_iota(jnp.int32, t.shape, 1)
    first_half = (lane % (2 * ROPE_PAIRS)) < ROPE_PAIRS
    partner = jnp.where(first_half, pltpu.roll(t, LANES - ROPE_PAIRS, 1), pltpu.roll(t, ROPE_PAIRS, 1))
    return t * cos + partner * sin


def _token_operands(xparts, n_lat_tiles, tile):
    d = xparts[0].shape[1]
    lat = pl.BlockSpec((tile, d), lambda j: (jnp.minimum(j, n_lat_tiles - 1), 0))
    if len(xparts) == 2:
        ctx = pl.BlockSpec((tile, d), lambda j: (jnp.maximum(j - n_lat_tiles, 0), 0))
        return list(xparts), [lat, ctx]
    ctx = pl.BlockSpec((tile, d), lambda j: (jnp.maximum(j, n_lat_tiles), 0))
    return [xparts[0], xparts[0]], [lat, ctx]


def _pick_tokens(xl_ref, xc_ref, n_lat_tiles):
    return jnp.where(pl.program_id(0) >= n_lat_tiles, xc_ref[...], xl_ref[...])


def _inproj_even_kernel(xl_ref, xc_ref, mod_ref, g_ref, w_ref, qg_ref, kg_ref, cos_ref, sin_ref, pm_ref,
                        q_ref, kvx_ref, gb_ref, u_ref, *, n_lat_tiles):
    mod = mod_ref[0]
    x = _pick_tokens(xl_ref, xc_ref, n_lat_tiles)
    h = _modulate(x, g_ref[...], mod[0:1], mod[1:2]).astype(BF16)
    cos, sin, pm = cos_ref[...], sin_ref[...], pm_ref[...]
    kv_w = N_KV_HEADS * HEAD_DIM
    att_w = N_Q_HEADS * HEAD_DIM
    sc_w = gb_ref.shape[1]
    kv = jnp.dot(h, w_ref[:, 0:2 * kv_w], preferred_element_type=F32)
    k = _head_norm_rope(kv[:, :kv_w], pm, kg_ref[...], cos, sin)
    v = kv[:, kv_w:]
    kvx_ref[...] = jnp.concatenate([k, pltpu.roll(k, HEAD_DIM, 1), v, pltpu.roll(v, HEAD_DIM, 1)],
                                   axis=-1).astype(BF16)
    q = jnp.dot(h, w_ref[:, 2 * kv_w:2 * kv_w + att_w], preferred_element_type=F32)
    qscale = HEAD_DIM ** -0.5 * LOG2E
    for c in range(att_w // LANES):
        qc = _head_norm_rope(q[:, c * LANES:(c + 1) * LANES], pm, qg_ref[...], cos, sin)
        q_ref[:, c * LANES:(c + 1) * LANES] = (qc * qscale).astype(BF16)
    o = 2 * kv_w + att_w
    gb_ref[...] = jnp.dot(h, w_ref[:, o:o + sc_w], preferred_element_type=F32)
    gc = jnp.dot(h, w_ref[:, o + sc_w:o + 2 * sc_w], preferred_element_type=F32)
    gx = jnp.dot(h, w_ref[:, o + 2 * sc_w:o + 3 * sc_w], preferred_element_type=F32)
    u_ref[...] = gc * gx


def _tile_maps(n_lat_tiles, tiles_per_seq, n_batch):
    def mod_map(j):
        return (jnp.minimum(j // tiles_per_seq, n_batch), 0, 0)

    def pos_map(j):
        return (jnp.where(j < n_lat_tiles, j % tiles_per_seq, tiles_per_seq), 0)

    return mod_map, pos_map


def _inproj_even(xparts, modl, g, w_bf, q_g, k_g, cos_t, sin_t, pm, dims):
    n_tok = sum(p.shape[0] for p in xparts)
    d = xparts[0].shape[1]
    n_batch, seq = dims
    tile = TOK_TILE
    tps = seq // tile
    mod_map, pos_map = _tile_maps(n_batch * tps, tps, n_batch)
    xs, xspecs = _token_operands(xparts, n_batch * tps, tile)
    kv_w, att_w = N_KV_HEADS * HEAD_DIM, N_Q_HEADS * HEAD_DIM
    sc_w = (w_bf.shape[1] - 2 * kv_w - att_w) // 3
    row = lambda j: (j, 0)
    const = lambda j: (0, 0)
    return pl.pallas_call(
        functools.partial(_inproj_even_kernel, n_lat_tiles=n_batch * tps),
        grid=(n_tok // tile,),
        in_specs=xspecs + [
                  pl.BlockSpec((1, SUBLANES, d), mod_map),
                  pl.BlockSpec((1, d), const),
                  pl.BlockSpec(w_bf.shape, const),
                  pl.BlockSpec((1, LANES), const),
                  pl.BlockSpec((1, LANES), const),
                  pl.BlockSpec((tile, LANES), pos_map),
                  pl.BlockSpec((tile, LANES), pos_map),
                  pl.BlockSpec((LANES, LANES), const)],
        out_specs=[pl.BlockSpec((tile, att_w), row),
                   pl.BlockSpec((tile, 4 * kv_w), row),
                   pl.BlockSpec((tile, sc_w), row),
                   pl.BlockSpec((tile, sc_w), row)],
        out_shape=[jax.ShapeDtypeStruct((n_tok, att_w), BF16),
                   jax.ShapeDtypeStruct((n_tok, 4 * kv_w), BF16),
                   jax.ShapeDtypeStruct((n_tok, sc_w), F32),
                   jax.ShapeDtypeStruct((n_tok, sc_w), F32)],
        compiler_params=_cparams("arbitrary"),
        name="inproj_even",
    )(*xs, modl, g, w_bf, q_g, k_g, cos_t, sin_t, pm)


def _attn_heads(q, kvx, mask_fn, sink_ref, o_ref):
    m, n = q.shape[0], kvx.shape[0]
    k_nat, k_swp = kvx[:, 0:LANES], kvx[:, LANES:2 * LANES]
    v_nat, v_swp = kvx[:, 2 * LANES:3 * LANES], kvx[:, 3 * LANES:4 * LANES]
    lo_n = lax.broadcasted_iota(jnp.int32, (n, LANES), 1) < HEAD_DIM
    lo_m = lax.broadcasted_iota(jnp.int32, (2 * m, LANES), 1) < HEAD_DIM
    top = lax.broadcasted_iota(jnp.int32, (2 * m, 1), 0) < m
    zero = jnp.zeros_like(k_nat)
    r2 = lax.broadcasted_iota(jnp.int32, (2 * n, LANES), 0) < n
    l2 = lax.broadcasted_iota(jnp.int32, (2 * n, LANES), 1) < HEAD_DIM
    ones_both = jnp.where(r2 == l2, 1.0, 0.0).astype(BF16)
    outs = []
    for h in range(N_KV_HEADS):
        k_lo, k_hi = (k_nat, k_swp) if h == 0 else (k_swp, k_nat)
        v_lo, v_hi = (v_nat, v_swp) if h == 0 else (v_swp, v_nat)
        g0 = 2 * h
        q2 = jnp.concatenate([q[:, g0 * LANES:(g0 + 1) * LANES],
                              q[:, (g0 + 1) * LANES:(g0 + 2) * LANES]], axis=0)
        k_both = jnp.concatenate([jnp.where(lo_n, k_lo, zero), jnp.where(lo_n, zero, k_hi)], axis=0)
        v_both = jnp.concatenate([jnp.where(lo_n, v_lo, zero), jnp.where(lo_n, zero, v_hi)], axis=0)
        s = lax.dot_general(q2, k_both, (((1,), (1,)), ((), ())), preferred_element_type=F32)
        hq = Q_PER_KV * h
        halves = []
        for par in range(2):
            sp = mask_fn(s[:, par * n:(par + 1) * n])
            snk = jnp.where(top, sink_ref[hq + par], sink_ref[hq + 2 + par]) * LOG2E
            mx = jnp.maximum(jnp.max(sp, axis=-1, keepdims=True), snk)
            halves.append((jnp.exp2(sp - mx), jnp.exp2(snk - mx)))
        (p_lo, e_lo), (p_hi, e_hi) = halves
        p = jnp.concatenate([p_lo, p_hi], axis=1).astype(BF16)
        nd = jnp.dot(p, jnp.concatenate([v_both, ones_both], axis=1), preferred_element_type=F32)
        o = nd[:, :LANES] / (nd[:, LANES:] + jnp.where(lo_m, e_lo, e_hi))
        outs += [o[0:m], o[m:2 * m]]
    o_ref[...] = jnp.concatenate(outs, axis=-1)


def _attn_lat_kernel(sink_ref, q_ref, kp_ref, kc_ref, kn_ref, kx_ref, o_ref):
    i = pl.program_id(1)
    nb = pl.num_programs(1)
    kvx = jnp.concatenate([kp_ref[...], kc_ref[...], kn_ref[...], kx_ref[...]], axis=0)
    m, w = ATT_TILE, WINDOW
    tok = lax.broadcasted_iota(jnp.int32, (2 * m, m), 0) % m
    col = lax.broadcasted_iota(jnp.int32, (2 * m, m), 1)
    tok_w = lax.broadcasted_iota(jnp.int32, (2 * m, w), 0) % m
    col_w = lax.broadcasted_iota(jnp.int32, (2 * m, w), 1)
    ok_prev = (col_w >= tok_w) & (i > 0)
    ok_cur = (col - tok <= w) & (tok - col <= w)
    ok_next = (col_w + (m - w) <= tok_w) & (i < nb - 1)

    def mask_fn(s):
        return jnp.concatenate([jnp.where(ok_prev, s[:, 0:w], NEG), jnp.where(ok_cur, s[:, w:w + m], NEG),
                                jnp.where(ok_next, s[:, w + m:2 * w + m], NEG), s[:, 2 * w + m:]], axis=1)

    _attn_heads(q_ref[...], kvx, mask_fn, sink_ref, o_ref)


def _attn_ctx_kernel(sink_ref, att_in_ref, q_ref, kx_ref, o_ref):
    del att_in_ref
    _attn_heads(q_ref[...], kx_ref[...], lambda s: s, sink_ref, o_ref)


def _attention(q, kvx, sink, dims, need_ctx):
    n_tok = q.shape[0]
    n_batch, seq = dims
    nb = seq // ATT_TILE
    r = ATT_TILE // WINDOW
    nw = seq // WINDOW
    n_lat = n_batch * seq
    ctx_len = (n_tok - n_lat) // n_batch
    att_w, kvx_w = q.shape[1], kvx.shape[1]
    ctx_blk0 = n_lat // ctx_len
    cur = lambda b, i: (b * nb + i, 0)
    prev = lambda b, i: (b * nw + jnp.maximum(i * r - 1, 0), 0)
    nxt = lambda b, i: (b * nw + jnp.minimum((i + 1) * r, nw - 1), 0)
    cx = lambda b, i: (ctx_blk0 + b, 0)
    smem = pl.BlockSpec(memory_space=pltpu.SMEM)
    att = pl.pallas_call(
        _attn_lat_kernel,
        grid=(n_batch, nb),
        in_specs=[smem, pl.BlockSpec((ATT_TILE, att_w), cur),
                  pl.BlockSpec((WINDOW, kvx_w), prev), pl.BlockSpec((ATT_TILE, kvx_w), cur),
                  pl.BlockSpec((WINDOW, kvx_w), nxt), pl.BlockSpec((ctx_len, kvx_w), cx)],
        out_specs=pl.BlockSpec((ATT_TILE, att_w), cur),
        out_shape=jax.ShapeDtypeStruct((n_tok, att_w), F32),
        compiler_params=_cparams("arbitrary", "arbitrary"),
        name="attn_latent",
    )(sink, q, kvx, kvx, kvx, kvx)
    if not need_ctx:
        return att
    cxb = lambda b: (ctx_blk0 + b, 0)
    return pl.pallas_call(
        _attn_ctx_kernel,
        grid=(n_batch,),
        in_specs=[smem, pl.BlockSpec(memory_space=pl.ANY),
                  pl.BlockSpec((ctx_len, att_w), cxb), pl.BlockSpec((ctx_len, kvx_w), cxb)],
        out_specs=pl.BlockSpec((ctx_len, att_w), cxb),
        out_shape=jax.ShapeDtypeStruct((n_tok, att_w), F32),
        input_output_aliases={1: 0},
        compiler_params=_cparams("arbitrary"),
        name="attn_context",
    )(sink, att, q, kvx)


def _route(logits):
    lane = lax.broadcasted_iota(jnp.int32, logits.shape, 1).astype(F32)
    big = float(LANES)
    gl = jnp.where((lane >= N_EXPERTS) & (lane < N_EXPERTS + N_GROUPS), logits, NEG)
    gmax = jnp.max(gl, axis=-1, keepdims=True)
    gidx = jnp.min(jnp.where(gl == gmax, lane, big), axis=-1, keepdims=True) - N_EXPERTS
    p_sel = 1.0 / jnp.sum(jnp.exp(gl - gmax), axis=-1, keepdims=True)
    lo = gidx * EXPERTS_PER_GROUP
    el = jnp.where((lane >= lo) & (lane < lo + EXPERTS_PER_GROUP), logits, NEG)
    m1 = jnp.max(el, axis=-1, keepdims=True)
    i1 = jnp.min(jnp.where(el == m1, lane, big), axis=-1, keepdims=True)
    el2 = jnp.where(lane == i1, NEG, el)
    m2 = jnp.max(el2, axis=-1, keepdims=True)
    i2 = jnp.min(jnp.where(el2 == m2, lane, big), axis=-1, keepdims=True)
    t = jnp.exp(m2 - m1)
    w1 = p_sel / (1.0 + t)
    w2 = p_sel * t / (1.0 + t)
    return jnp.where(lane == 0, i1, jnp.where(lane == 1, i2, jnp.where(lane == 2, w1,
                     jnp.where(lane == 3, w2, 0.0))))


def _tail(n1, n2, gain, wo_ref, x, mod, nf_ref, wr_ref, br_ref, xo_ref, tok_ref, route_ref, cnt_ref):
    half = n1.shape[1]
    y = (_bdot(_rms(n1, gain[:, :half]), wo_ref[0:half, :])
         + _bdot(_rms(n2, gain[:, half:]), wo_ref[half:, :]))
    xn = x + mod[2:3] * y
    xo_ref[...] = xn
    tok = _modulate(xn, nf_ref[...], mod[3:4], mod[4:5])
    half_d = tok.shape[1] // 2
    tok_ref[...] = _pack_bf16_pair(tok[:, :half_d], tok[:, half_d:])
    t_hi = tok.astype(BF16)
    t_lo = (tok - t_hi.astype(F32)).astype(BF16)
    hh = jnp.dot(t_hi, wr_ref[...], preferred_element_type=F32)
    logits = (hh[:, :LANES] + hh[:, LANES:] + br_ref[...]
              + jnp.dot(t_lo, wr_ref[:, 0:LANES], preferred_element_type=F32))
    route = _route(logits)
    route_ref[...] = route
    lane = lax.broadcasted_iota(jnp.int32, route.shape, 1).astype(F32)
    hit = jnp.where((lane == route[:, 0:1]) | (lane == route[:, 1:2]), 1.0, 0.0)

    @pl.when(pl.program_id(0) == 0)
    def _():
        cnt_ref[...] = jnp.zeros_like(cnt_ref)

    cnt_ref[0:1, :] += jnp.sum(hit, axis=0, keepdims=True)


def _seq_flags(j, n_lat_tiles, tiles_per_seq):
    is_ctx = j >= n_lat_tiles
    first = is_ctx | (j % tiles_per_seq == 0)
    last = is_ctx | (j % tiles_per_seq == tiles_per_seq - 1)
    return first, last


def _with_halo(prev_ref, tile, next_ref, first, last):
    p = jnp.where(first, 0.0, prev_ref[...])
    n = jnp.where(last, 0.0, next_ref[...])
    return jnp.concatenate([p, tile, n], axis=0)


def _shift_rows(e, k):
    return pltpu.roll(e, (-k) % e.shape[0], 0)


def _merge_even_kernel(att_ref, gb_ref, u_ref, up_ref, un_ref, xl_ref, xc_ref, mod_ref, cw_ref, cb_ref,
                       on_ref, wo_ref, nf_ref, wr_ref, br_ref,
                       xo_ref, tok_ref, route_ref, cnt_ref, *, n_lat_tiles, tiles_per_seq):
    j = pl.program_id(0)
    first, last = _seq_flags(j, n_lat_tiles, tiles_per_seq)
    e = _with_halo(up_ref, u_ref[...], un_ref, first, last)
    cw = cw_ref[...]
    lo, hi = SUBLANES, SUBLANES + MERGE_TILE
    pos = lax.broadcasted_iota(jnp.int32, (MERGE_TILE, 1), 0) % SEQ_TILE
    is_ctx = j >= n_lat_tiles
    before = jnp.where(is_ctx & (pos == 0), 0.0, _shift_rows(e, -1)[lo:hi])
    after = jnp.where(is_ctx & (pos == SEQ_TILE - 1), 0.0, _shift_rows(e, 1)[lo:hi])
    conv = cw[0:1] * before + cw[1:2] * e[lo:hi] + cw[2:3] * after + cb_ref[...]
    conv = gb_ref[...] * conv
    x = _pick_tokens(xl_ref, xc_ref, n_lat_tiles)
    _tail(att_ref[...], conv, on_ref[...], wo_ref, x, mod_ref[0], nf_ref, wr_ref, br_ref,
          xo_ref, tok_ref, route_ref, cnt_ref)


def _halo_specs(width, n_rows):
    r = MERGE_TILE // SUBLANES
    n_blk = n_rows // SUBLANES
    prev = pl.BlockSpec((SUBLANES, width), lambda j: (jnp.maximum(j * r - 1, 0), 0))
    nxt = pl.BlockSpec((SUBLANES, width), lambda j: (jnp.minimum((j + 1) * r, n_blk - 1), 0))
    return prev, nxt


def _tail_specs(d, n_out_tok):
    row = lambda j: (j, 0)
    out_specs = [pl.BlockSpec((MERGE_TILE, d), row), pl.BlockSpec((MERGE_TILE, d // 2), row),
                 pl.BlockSpec((MERGE_TILE, LANES), row), pl.BlockSpec((SUBLANES, LANES), lambda j: (0, 0))]
    out_shape = [jax.ShapeDtypeStruct((n_out_tok, d), F32), jax.ShapeDtypeStruct((n_out_tok, d // 2), jnp.uint32),
                 jax.ShapeDtypeStruct((n_out_tok, LANES), F32), jax.ShapeDtypeStruct((SUBLANES, LANES), F32)]
    return out_specs, out_shape


def _merge_even(att, gb, u, xparts, modl, cw, cb, on, wo_bf, nf, wr, br, dims, n_out_tok):
    n_tok = u.shape[0]
    d = xparts[0].shape[1]
    n_batch, seq = dims
    tps = seq // MERGE_TILE
    mod_map, _ = _tile_maps(n_batch * tps, tps, n_batch)
    xs, xspecs = _token_operands(xparts, n_batch * tps, MERGE_TILE)
    sc_w = u.shape[1]
    row = lambda j: (j, 0)
    const = lambda j: (0, 0)
    prev, nxt = _halo_specs(sc_w, n_tok)
    out_specs, out_shape = _tail_specs(d, n_out_tok)
    return pl.pallas_call(
        functools.partial(_merge_even_kernel, n_lat_tiles=n_batch * tps, tiles_per_seq=tps),
        grid=(n_out_tok // MERGE_TILE,),
        in_specs=[pl.BlockSpec((MERGE_TILE, att.shape[1]), row),
                  pl.BlockSpec((MERGE_TILE, sc_w), row),
                  pl.BlockSpec((MERGE_TILE, sc_w), row), prev, nxt] + xspecs + [
                  pl.BlockSpec((1, SUBLANES, d), mod_map),
                  pl.BlockSpec(cw.shape, const), pl.BlockSpec((1, sc_w), const),
                  pl.BlockSpec((1, d), const), pl.BlockSpec(wo_bf.shape, const),
                  pl.BlockSpec((1, d), const), pl.BlockSpec(wr.shape, const),
                  pl.BlockSpec((1, LANES), const)],
        out_specs=out_specs, out_shape=out_shape,
        compiler_params=_cparams("arbitrary"),
        name="merge_even",
    )(att, gb, u, u, u, *xs, modl, cw, cb, on, wo_bf, nf, wr, br)


def _odd_projection(x, mod, g_ref, w_ref, ux_ref, gg_ref, up_ref):
    h = _modulate(x, g_ref[...], mod[0:1], mod[1:2]).astype(BF16)
    w = ux_ref.shape[1]
    ux_ref[...] = jnp.dot(h, w_ref[:, 0:w], preferred_element_type=F32)
    ug = jnp.dot(h, w_ref[:, w:2 * w], preferred_element_type=F32)
    gg_ref[...] = jax.nn.gelu(ug, approximate=True)
    up_ref[...] = jnp.dot(h, w_ref[:, 2 * w:3 * w], preferred_element_type=F32)


def _inproj_odd_kernel(xl_ref, xc_ref, mod_ref, g_ref, w_ref, ux_ref, gg_ref, up_ref, *, n_lat_tiles):
    x = _pick_tokens(xl_ref, xc_ref, n_lat_tiles)
    _odd_projection(x, mod_ref[0], g_ref, w_ref, ux_ref, gg_ref, up_ref)


def _inproj_odd(xparts, modl, g, w_bf, dims):
    n_tok = sum(p.shape[0] for p in xparts)
    d = xparts[0].shape[1]
    n_batch, seq = dims
    tile = TOK_TILE
    tps = seq // tile
    mod_map, _ = _tile_maps(n_batch * tps, tps, n_batch)
    xs, xspecs = _token_operands(xparts, n_batch * tps, tile)
    w = w_bf.shape[1] // 3
    row = lambda j: (j, 0)
    const = lambda j: (0, 0)
    return pl.pallas_call(
        functools.partial(_inproj_odd_kernel, n_lat_tiles=n_batch * tps),
        grid=(n_tok // tile,),
        in_specs=xspecs + [
                  pl.BlockSpec((1, SUBLANES, d), mod_map),
                  pl.BlockSpec((1, d), const),
                  pl.BlockSpec(w_bf.shape, const)],
        out_specs=[pl.BlockSpec((tile, w), row)] * 3,
        out_shape=[jax.ShapeDtypeStruct((n_tok, w), F32)] * 3,
        compiler_params=_cparams("arbitrary"),
        name="inproj_odd",
    )(*xs, modl, g, w_bf)


def _lru_conv(e, cw, cb):
    lo, hi = SUBLANES, e.shape[0] - SUBLANES
    return (cw[0:1] * _shift_rows(e, -2)[lo:hi] + cw[1:2] * _shift_rows(e, -1)[lo:hi]
            + cw[2:3] * e[lo:hi] + cw[3:4] * _shift_rows(e, 1)[lo:hi] + cb)


def _lru_coeffs(xl, d, wa_ref, wx_ref, ba_ref, bx_ref, lam_ref, a_ref, b_ref):
    xb = xl.astype(BF16)
    chunk = wa_ref.shape[-1]
    n_chunk = xl.shape[1] // chunk
    pre_a = jnp.concatenate([jnp.dot(xb[:, c * chunk:(c + 1) * chunk], wa_ref[d, c],
                                     preferred_element_type=F32) for c in range(n_chunk)], axis=-1)
    pre_x = jnp.concatenate([jnp.dot(xb[:, c * chunk:(c + 1) * chunk], wx_ref[d, c],
                                     preferred_element_type=F32) for c in range(n_chunk)], axis=-1)
    tr = jnp.tanh(pre_a + ba_ref[d])
    tg = jnp.tanh(pre_x + bx_ref[d])
    c_half = (-0.5 * LRU_C) * jax.nn.softplus(-lam_ref[d])
    log_a = c_half * tr + c_half
    a_ref[...] = jnp.exp(log_a)
    th = jnp.tanh(log_a)
    b_ref[...] = jnp.sqrt(-0.5 * th / (1.0 - th)) * (1.0 + tg) * xl


def _scan_tile(a_ref, b_ref, h_ref, carry, reverse):
    n_grp = a_ref.shape[0] // SUBLANES
    row = lax.broadcasted_iota(jnp.int32, (SUBLANES, a_ref.shape[1]), 0)

    def body(g, carry):
        gi = (n_grp - 1 - g) if reverse else g
        r0 = pl.multiple_of(gi * SUBLANES, SUBLANES)
        a8 = a_ref[pl.ds(r0, SUBLANES), :]
        b8 = b_ref[pl.ds(r0, SUBLANES), :]
        for s in (1, 2, 4):
            if reverse:
                ok = row < SUBLANES - s
                sh = SUBLANES - s
            else:
                ok = row >= s
                sh = s
            a_sh = jnp.where(ok, pltpu.roll(a8, sh, 0), 1.0)
            b_sh = jnp.where(ok, pltpu.roll(b8, sh, 0), 0.0)
            b8 = a8 * b_sh + b8
            a8 = a8 * a_sh
        h8 = a8 * carry + b8
        if h_ref is not None:
            h_ref[pl.ds(r0, SUBLANES), :] = h8
        return h8[0:1] if reverse else h8[SUBLANES - 1:SUBLANES]

    return lax.fori_loop(0, n_grp, body, carry, unroll=4)


def _lru_kernel(uf_ref, ufp_ref, ufn_ref, ur_ref, urp_ref, urn_ref, uc_ref, cw_ref, cb_ref,
                wa_ref, wx_ref, ba_ref, bx_ref, lam_ref, hf_ref, hr_ref,
                cf_ref, cr_ref, a_ref, b_ref):
    i = pl.program_id(1)
    nt = pl.num_programs(1)
    cw, cb = cw_ref[...], cb_ref[...]
    coeffs = functools.partial(_lru_coeffs, wa_ref=wa_ref, wx_ref=wx_ref, ba_ref=ba_ref,
                               bx_ref=bx_ref, lam_ref=lam_ref, a_ref=a_ref, b_ref=b_ref)

    @pl.when(i == 0)
    def _():
        n_ctx = uc_ref.shape[0]
        ac_ref, bc_ref = a_ref.at[pl.ds(0, n_ctx)], b_ref.at[pl.ds(0, n_ctx)]
        z = jnp.zeros((SUBLANES, uc_ref.shape[1]), F32)
        xc = _lru_conv(jnp.concatenate([z, uc_ref[...], z], axis=0), cw, cb)
        zero = jnp.zeros((1, uc_ref.shape[1]), F32)
        coeffs(xc, 0, a_ref=ac_ref, b_ref=bc_ref)
        cf_ref[...] = _scan_tile(ac_ref, bc_ref, None, zero, False)
        coeffs(xc, 1, a_ref=ac_ref, b_ref=bc_ref)
        cr_ref[...] = _scan_tile(ac_ref, bc_ref, None, zero, True)

    xf = _lru_conv(_with_halo(ufp_ref, uf_ref[...], ufn_ref, i == 0, i == nt - 1), cw, cb)
    coeffs(xf, 0)
    cf_ref[...] = _scan_tile(a_ref, b_ref, hf_ref, cf_ref[...], False)
    xr = _lru_conv(_with_halo(urp_ref, ur_ref[...], urn_ref, i == nt - 1, i == 0), cw, cb)
    coeffs(xr, 1)
    cr_ref[...] = _scan_tile(a_ref, b_ref, hr_ref, cr_ref[...], True)


def _lru(ux, cw, cb, wa_bd, wx_bd, ba, bx, lam, dims):
    n_tok, w = ux.shape
    n_batch, seq = dims
    nt = seq // LRU_TILE
    n_lat = n_batch * seq
    r = LRU_TILE // SUBLANES
    n_blk = n_tok // SUBLANES
    fwd = lambda b, i: b * nt + i
    rev = lambda b, i: b * nt + nt - 1 - i

    def specs(tile):
        return [pl.BlockSpec((LRU_TILE, w), lambda b, i: (tile(b, i), 0)),
                pl.BlockSpec((SUBLANES, w), lambda b, i: (jnp.maximum(tile(b, i) * r - 1, 0), 0)),
                pl.BlockSpec((SUBLANES, w), lambda b, i: (jnp.minimum((tile(b, i) + 1) * r, n_blk - 1), 0))]

    const2 = lambda b, i: (0, 0)
    const3 = lambda b, i: (0, 0, 0)
    const4 = lambda b, i: (0, 0, 0, 0)
    return pl.pallas_call(
        _lru_kernel,
        grid=(n_batch, nt),
        in_specs=specs(fwd) + specs(rev) + [
            pl.BlockSpec((SEQ_TILE, w), lambda b, i: (n_lat // SEQ_TILE + b, 0)),
            pl.BlockSpec(cw.shape, const2), pl.BlockSpec((1, w), const2),
            pl.BlockSpec(wa_bd.shape, const4), pl.BlockSpec(wx_bd.shape, const4),
            pl.BlockSpec(ba.shape, const3), pl.BlockSpec(bx.shape, const3),
            pl.BlockSpec(lam.shape, const3)],
        out_specs=[pl.BlockSpec((LRU_TILE, w), lambda b, i: (fwd(b, i), 0)),
                   pl.BlockSpec((LRU_TILE, w), lambda b, i: (rev(b, i), 0))],
        out_shape=[jax.ShapeDtypeStruct((n_lat, w), F32)] * 2,
        scratch_shapes=[pltpu.VMEM((1, w), F32), pltpu.VMEM((1, w), F32),
                        pltpu.VMEM((LRU_TILE, w), F32), pltpu.VMEM((LRU_TILE, w), F32)],
        compiler_params=_cparams("arbitrary", "arbitrary"),
        name="rglru",
    )(ux, ux, ux, ux, ux, ux, ux, cw, cb, wa_bd, wx_bd, ba, bx, lam)


def _merge_odd_kernel(up_ref, upp_ref, upn_ref, hf_ref, hr_ref, gg_ref, x_ref, mod_ref,
                      pw_ref, ps_ref, on_ref, wo_ref, nf_ref, wr_ref, br_ref,
                      xo_ref, tok_ref, route_ref, cnt_ref, *, tiles_per_seq, seq):
    j = pl.program_id(0)
    ts = j % tiles_per_seq
    e = _with_halo(upp_ref, up_ref[...], upn_ref, ts == 0, ts == tiles_per_seq - 1)
    gw = pw_ref.shape[-1]
    lo, hi = SUBLANES, SUBLANES + MERGE_TILE
    tpos = ts * MERGE_TILE + lax.broadcasted_iota(jnp.int32, (MERGE_TILE, 1), 0)
    parts = []
    for g, win in enumerate(POOL_WINDOWS):
        eg = e[:, g * gw:(g + 1) * gw]
        p = eg
        span = 1
        while span < win:
            p = p + _shift_rows(p, span)
            span *= 2
        half = win // 2
        s = _shift_rows(p, -half)[lo:hi]
        cnt = (jnp.minimum(tpos + half, seq) - jnp.maximum(tpos - half, 0)).astype(F32)
        dlt = s / cnt - eg[lo:hi]
        parts.append(_bdot(dlt, pw_ref[g]))
    pool = jnp.concatenate(parts, axis=-1) * ps_ref[...]
    lru = (hf_ref[...] + hr_ref[...]) * gg_ref[...]
    _tail(pool, lru, on_ref[...], wo_ref, x_ref[...], mod_ref[0], nf_ref, wr_ref, br_ref,
          xo_ref, tok_ref, route_ref, cnt_ref)


def _merge_odd(up, hf, hr, gg, xf, modl, pw_bf, ps, on, wo_bf, nf, wr, br, dims):
    d = xf.shape[1]
    n_tok, w = up.shape
    n_batch, seq = dims
    tps = seq // MERGE_TILE
    n_lat = n_batch * seq
    mod_map, _ = _tile_maps(n_batch * tps, tps, n_batch)
    row = lambda j: (j, 0)
    const = lambda j: (0, 0)
    prev, nxt = _halo_specs(w, n_tok)
    out_specs, out_shape = _tail_specs(d, n_lat)
    return pl.pallas_call(
        functools.partial(_merge_odd_kernel, tiles_per_seq=tps, seq=seq),
        grid=(n_lat // MERGE_TILE,),
        in_specs=[pl.BlockSpec((MERGE_TILE, w), row), prev, nxt,
                  pl.BlockSpec((MERGE_TILE, w), row), pl.BlockSpec((MERGE_TILE, w), row),
                  pl.BlockSpec((MERGE_TILE, w), row),
                  pl.BlockSpec((MERGE_TILE, d), row),
                  pl.BlockSpec((1, SUBLANES, d), mod_map),
                  pl.BlockSpec(pw_bf.shape, lambda j: (0, 0, 0)), pl.BlockSpec((1, w), const),
                  pl.BlockSpec((1, d), const), pl.BlockSpec(wo_bf.shape, const),
                  pl.BlockSpec((1, d), const), pl.BlockSpec(wr.shape, const),
                  pl.BlockSpec((1, LANES), const)],
        out_specs=out_specs, out_shape=out_shape,
        compiler_params=_cparams("arbitrary"),
        name="merge_odd",
    )(up, up, up, hf, hr, gg, xf, modl, pw_bf, ps, on, wo_bf, nf, wr, br)


def _slots_kernel(route_ref, tot_ref, dest_ref, cnt_ref, run_ref, start_ref):
    i = pl.program_id(0)
    rt = route_ref[...]
    m = rt.shape[0]
    lane = lax.broadcasted_iota(jnp.int32, rt.shape, 1).astype(F32)
    oh1 = lane == rt[:, 0:1]
    oh2 = lane == rt[:, 1:2]
    s = jnp.where(oh1 | oh2, 1.0, 0.0)

    @pl.when(i == 0)
    def _():
        tot = tot_ref[0:1, :]
        padded = jnp.floor((tot + (MOE_BLOCK - 1)) / MOE_BLOCK) * MOE_BLOCK
        r = lax.broadcasted_iota(jnp.int32, (LANES, LANES), 0)
        c = lax.broadcasted_iota(jnp.int32, (LANES, LANES), 1)
        upper = jnp.where(r < c, 1.0, 0.0)
        p8 = jnp.broadcast_to(padded, (SUBLANES, LANES))
        start = jnp.dot(p8, upper, precision=HIGHEST, preferred_element_type=F32)[0:1]
        start_ref[...] = start
        run_ref[...] = jnp.zeros_like(run_ref)
        sub = lax.broadcasted_iota(jnp.int32, cnt_ref.shape, 0)
        cnt_ref[...] = jnp.where(sub == 0, tot, jnp.where(sub == 1, start, 0.0))

    r = lax.broadcasted_iota(jnp.int32, (m, m), 0)
    c = lax.broadcasted_iota(jnp.int32, (m, m), 1)
    lower = jnp.where(c < r, 1.0, 0.0).astype(BF16)
    prefix = jnp.dot(lower, s.astype(BF16), preferred_element_type=F32)
    base = prefix + run_ref[...] + start_ref[...]
    d1 = jnp.sum(jnp.where(oh1, base, 0.0), axis=-1, keepdims=True)
    d2 = jnp.sum(jnp.where(oh2, base, 0.0), axis=-1, keepdims=True)
    dest_ref[...] = jnp.where(lane == 0, d1, jnp.where(lane == 1, d2, 0.0)).astype(jnp.int32)
    run_ref[...] += jnp.sum(s, axis=0, keepdims=True)


def _slots(route, totals):
    n_tok = route.shape[0]
    nt = n_tok // TOK_TILE
    return pl.pallas_call(
        _slots_kernel,
        grid=(nt,),
        in_specs=[pl.BlockSpec((TOK_TILE, LANES), lambda i: (i, 0)),
                  pl.BlockSpec((SUBLANES, LANES), lambda i: (0, 0))],
        out_specs=[pl.BlockSpec((TOK_TILE, LANES), lambda i: (i, 0)),
                   pl.BlockSpec((SUBLANES, LANES), lambda i: (0, 0))],
        out_shape=[jax.ShapeDtypeStruct((n_tok, LANES), jnp.int32),
                   jax.ShapeDtypeStruct((SUBLANES, LANES), F32)],
        scratch_shapes=[pltpu.VMEM((1, LANES), F32)] * 2,
        compiler_params=_cparams("arbitrary"),
        name="moe_slots",
    )(route, totals)


def _for_each_row(n_rows, fn):
    def body(g, c):
        g8 = pl.multiple_of(g * SUBLANES, SUBLANES)
        for j in range(SUBLANES):
            fn(g8, j)
        return c
    lax.fori_loop(0, n_rows // SUBLANES, body, 0)


def _dispatch_kernel(dest_ref, pad_lo_ref, pad_hi_ref, tok_ref, xs_ref, zrow, sem, zsem):
    i = pl.program_id(0)
    base = i * TOK_TILE

    def zero_copy(r, n):
        return pltpu.make_async_copy(zrow.at[pl.ds(0, n)], xs_ref.at[pl.ds(r, n)], zsem)

    def for_each_pad_chunk(fn):
        def per_expert(e, c):
            lo, hi = pad_lo_ref[e], pad_hi_ref[e]
            a8 = (lo + SUBLANES - 1) // SUBLANES * SUBLANES
            a64 = (a8 + ZERO_CHUNK - 1) // ZERO_CHUNK * ZERO_CHUNK

            def rows(r, c):
                fn(zero_copy(r, 1))
                return c

            def eights(q, c):
                fn(zero_copy(pl.multiple_of(a8 + q * SUBLANES, SUBLANES), SUBLANES))
                return c

            def chunks(q, c):
                fn(zero_copy(pl.multiple_of(a64 + q * ZERO_CHUNK, ZERO_CHUNK), ZERO_CHUNK))
                return c

            lax.fori_loop(lo, a8, rows, 0)
            lax.fori_loop(0, (a64 - a8) // SUBLANES, eights, 0)
            lax.fori_loop(0, (hi - a64) // ZERO_CHUNK, chunks, 0)
            return c
        lax.fori_loop(0, N_EXPERTS, per_expert, 0)

    @pl.when(i == 0)
    def _():
        zrow[...] = jnp.zeros_like(zrow)
        for_each_pad_chunk(lambda cp: cp.start())

    def issue(g8, j):
        rows = tok_ref.at[pl.ds(g8, SUBLANES)]
        for k in range(TOP_K):
            d = dest_ref[(base + g8) * TOP_K + (j * TOP_K + k)]
            pltpu.make_async_copy(rows.at[pl.ds(j, 1)], xs_ref.at[pl.ds(d, 1)], sem).start(priority=k)

    _for_each_row(TOK_TILE, issue)
    for k in range(TOP_K):
        pltpu.make_async_copy(tok_ref, xs_ref.at[pl.ds(0, TOK_TILE)], sem).wait()

    @pl.when(i == pl.num_programs(0) - 1)
    def _():
        for_each_pad_chunk(lambda cp: cp.wait())


def _dispatch(dest_flat, pad_lo, pad_hi, tok, n_slots):
    n_tok, d = tok.shape
    return pl.pallas_call(
        _dispatch_kernel,
        grid_spec=pltpu.PrefetchScalarGridSpec(
            num_scalar_prefetch=3,
            grid=(n_tok // TOK_TILE,),
            in_specs=[pl.BlockSpec((TOK_TILE, d), lambda i, *_: (i, 0))],
            out_specs=pl.BlockSpec(memory_space=pl.ANY),
            scratch_shapes=[pltpu.VMEM((ZERO_CHUNK, d), tok.dtype),
                            pltpu.SemaphoreType.DMA(()), pltpu.SemaphoreType.DMA(())]),
        out_shape=jax.ShapeDtypeStruct((n_slots, d), tok.dtype),
        compiler_params=_cparams("arbitrary"),
        name="moe_dispatch",
    )(dest_flat, pad_lo, pad_hi, tok)


def _expert_kernel(be_ref, nu_ref, x_ref, w1_ref, w3_ref, w2_ref, y_ref, w1b, w3b, w2b):
    i = pl.program_id(0)

    @pl.when(i < nu_ref[0])
    def _():
        changed = (i == 0) | (be_ref[i] != be_ref[jnp.maximum(i - 1, 0)])

        @pl.when(changed)
        def _():
            w1b[...] = w1_ref[0, 0].astype(BF16)
            w3b[...] = w3_ref[0, 0].astype(BF16)
            w2b[...] = w2_ref[0, 0].astype(BF16)

        x_a, x_b = _unpack_bf16_pair(x_ref[...])
        x_a, x_b = x_a.astype(BF16), x_b.astype(BF16)
        half = x_ref.shape[1]
        a = (jnp.dot(x_a, w1b[0:half, :], preferred_element_type=F32)
             + jnp.dot(x_b, w1b[half:, :], preferred_element_type=F32))
        b = (jnp.dot(x_a, w3b[0:half, :], preferred_element_type=F32)
             + jnp.dot(x_b, w3b[half:, :], preferred_element_type=F32))
        y = _bdot(_silu(a) * b, w2b[...])
        y_ref[...] = _pack_bf16_pair(y[:, :half], y[:, half:])


def _experts(block_e, n_used, xs, w1, w3, w2, layer):
    n_slots, half_d = xs.shape
    d, de = w1.shape[-2], w1.shape[-1]
    n_blocks = n_slots // MOE_BLOCK
    blk = lambda i, be, nu: (jnp.minimum(i, nu[0] - 1), 0)
    wmap = lambda i, be, nu: (layer, be[i], 0, 0)
    return pl.pallas_call(
        _expert_kernel,
        grid_spec=pltpu.PrefetchScalarGridSpec(
            num_scalar_prefetch=2,
            grid=(n_blocks,),
            in_specs=[pl.BlockSpec((MOE_BLOCK, half_d), blk),
                      pl.BlockSpec((1, 1, d, de), wmap), pl.BlockSpec((1, 1, d, de), wmap),
                      pl.BlockSpec((1, 1, de, d), wmap)],
            out_specs=pl.BlockSpec((MOE_BLOCK, half_d), blk),
            scratch_shapes=[pltpu.VMEM((d, de), BF16), pltpu.VMEM((d, de), BF16),
                            pltpu.VMEM((de, d), BF16)]),
        out_shape=jax.ShapeDtypeStruct((n_slots, half_d), jnp.uint32),
        compiler_params=_cparams("arbitrary"),
        name="moe_experts",
    )(block_e, n_used, xs, w1, w3, w2)


def _combine_kernel(dest_ref, x_ref, route_ref, mod_ref, ys_ref, *rest, fuse_next):
    if fuse_next:
        nmod_ref, g_ref, w_ref, o_ref, ux_ref, gg_ref, up_ref, yb0, yb1, sem = rest
    else:
        o_ref, yb0, yb1, sem = rest
    i = pl.program_id(0)
    n = pl.num_programs(0)
    bufs = ((yb0, sem.at[0]), (yb1, sem.at[1]))

    def issue(tile, half):
        ybuf, s = bufs[half]
        base = tile * (TOK_TILE * TOP_K)
        for r in range(TOK_TILE):
            for k in range(TOP_K):
                d = dest_ref[base + (r * TOP_K + k)]
                pltpu.make_async_copy(ys_ref.at[pl.ds(d, 1)], ybuf.at[k, pl.ds(r, 1)], s).start(priority=k)

    def wait(half):
        ybuf, s = bufs[half]
        for k in range(TOP_K):
            pltpu.make_async_copy(ys_ref.at[pl.ds(0, TOK_TILE)], ybuf.at[k], s).wait()

    def compute(half):
        ybuf, _ = bufs[half]
        rows = pl.ds(half * TOK_TILE, TOK_TILE)
        rt = route_ref[rows, :]
        w1, w2 = rt[:, 2:3], rt[:, 3:4]
        y1a, y1b = _unpack_bf16_pair(ybuf[0])
        y2a, y2b = _unpack_bf16_pair(ybuf[1])
        y = jnp.concatenate([w1 * y1a + w2 * y2a, w1 * y1b + w2 * y2b], axis=1)
        x1 = x_ref[rows, :] + mod_ref[0][5:6] * y
        o_ref[rows, :] = x1
        if fuse_next:
            _odd_projection(x1, nmod_ref[0], g_ref, w_ref, ux_ref.at[rows], gg_ref.at[rows], up_ref.at[rows])

    @pl.when(i == 0)
    def _():
        issue(0, 0)

    wait(0)
    issue(2 * i + 1, 1)
    compute(0)
    wait(1)
    issue(jnp.minimum(2 * i + 2, 2 * n - 1), 0)
    compute(1)

    @pl.when(i == n - 1)
    def _():
        wait(0)


def _combine(dest_flat, xn, route, modl, ys, dims, next_odd=None):
    n_tok, d = xn.shape
    n_batch, seq = dims
    blk = 2 * TOK_TILE
    assert seq % blk == 0 and n_tok % blk == 0
    tpb = seq // blk
    row = lambda i, dest: (i, 0)
    const = lambda i, dest: (0, 0)
    mod_spec = pl.BlockSpec((1, SUBLANES, d), lambda i, dest: (jnp.minimum(i // tpb, n_batch), 0, 0))
    in_specs = [pl.BlockSpec((blk, d), row), pl.BlockSpec((blk, LANES), row), mod_spec,
                pl.BlockSpec(memory_space=pl.ANY)]
    out_specs = [pl.BlockSpec((blk, d), row)]
    out_shape = [jax.ShapeDtypeStruct((n_tok, d), F32)]
    args = [dest_flat, xn, route, modl, ys]
    if next_odd is not None:
        nmod, g, w_bf = next_odd
        w = w_bf.shape[1] // 3
        in_specs += [mod_spec, pl.BlockSpec((1, d), const), pl.BlockSpec(w_bf.shape, const)]
        out_specs += [pl.BlockSpec((blk, w), row)] * 3
        out_shape += [jax.ShapeDtypeStruct((n_tok, w), F32)] * 3
        args += [nmod, g, w_bf]
    ybuf = pltpu.VMEM((TOP_K, TOK_TILE, ys.shape[1]), ys.dtype)
    return pl.pallas_call(
        functools.partial(_combine_kernel, fuse_next=next_odd is not None),
        grid_spec=pltpu.PrefetchScalarGridSpec(
            num_scalar_prefetch=1,
            grid=(n_tok // blk,),
            in_specs=in_specs, out_specs=out_specs,
            scratch_shapes=[ybuf, ybuf, pltpu.SemaphoreType.DMA((2,))]),
        out_shape=out_shape,
        compiler_params=_cparams("arbitrary"),
        name="moe_combine",
    )(*args)


def _moe(xn, tok, route, totals, modl, w1, w3, w2, layer, dims, next_odd):
    n_tok = tok.shape[0]
    n = n_tok * TOP_K
    n_blocks = (n + N_EXPERTS * (MOE_BLOCK - 1) + MOE_BLOCK - 1) // MOE_BLOCK
    dest, counts = _slots(route, totals)
    dest_flat = dest[:, :TOP_K].reshape(-1)
    cnt = counts[0, :N_EXPERTS].astype(jnp.int32)
    start = counts[1, :N_EXPERTS].astype(jnp.int32)
    blocks_per_e = (cnt + MOE_BLOCK - 1) // MOE_BLOCK
    blk_end = (start + blocks_per_e * MOE_BLOCK) // MOE_BLOCK
    n_used = blk_end[-1:]
    blk = jnp.minimum(jnp.arange(n_blocks, dtype=jnp.int32), n_used[0] - 1)
    block_e = jnp.sum((blk[:, None] >= blk_end[None, :]).astype(jnp.int32), axis=1)
    block_e = jnp.minimum(block_e, N_EXPERTS - 1)
    xs = _dispatch(dest_flat, start + cnt, blk_end * MOE_BLOCK, tok, n_blocks * MOE_BLOCK)
    ys = _experts(block_e, n_used, xs, w1, w3, w2, layer)
    return _combine(dest_flat, xn, route, modl, ys, dims, next_odd)


def _rope_tables(seq):
    rows = seq // GRID_W
    row = jnp.repeat(jnp.arange(rows), GRID_W).astype(F32)
    col = jnp.tile(jnp.arange(GRID_W), rows).astype(F32)
    inv_freq = ROPE_BASE ** (-jnp.arange(ROPE_PAIRS, dtype=F32) / ROPE_PAIRS)
    ar, ac = row[:, None] * inv_freq, col[:, None] * inv_freq
    cos = jnp.concatenate([jnp.cos(ar), jnp.cos(ar), jnp.cos(ac), jnp.cos(ac)], axis=-1)
    sin = jnp.concatenate([-jnp.sin(ar), jnp.sin(ar), -jnp.sin(ac), jnp.sin(ac)], axis=-1)
    reps = LANES // HEAD_DIM
    cos, sin = jnp.tile(cos, (1, reps)), jnp.tile(sin, (1, reps))
    cos = jnp.concatenate([cos, jnp.ones((TOK_TILE, LANES), F32)], axis=0)
    sin = jnp.concatenate([sin, jnp.zeros((TOK_TILE, LANES), F32)], axis=0)
    return cos, sin


def _head_mean_matrix():
    r = jnp.arange(LANES)
    return jnp.where((r[:, None] // HEAD_DIM) == (r[None, :] // HEAD_DIM), 1.0 / HEAD_DIM, 0.0).astype(BF16)


def _block_diag_chunks(w, chunk):
    dirs, nblk, bw, _ = w.shape
    per = chunk // bw
    w = w.reshape(dirs, nblk // per, per, bw, bw)
    eye = jnp.eye(per, dtype=w.dtype)
    out = jnp.einsum("dcpij,pq->dcpiqj", w, eye)
    return out.reshape(dirs, nblk // per, chunk, chunk).astype(BF16)


def _router_matrix(gw, gb, ew, eb):
    d = gw.shape[0]
    pad = LANES - N_EXPERTS - N_GROUPS
    wr = jnp.concatenate([ew, gw, jnp.zeros((d, pad), F32)], axis=1)
    br = jnp.concatenate([eb, gb, jnp.zeros((pad,), F32)])[None, :]
    hi = wr.astype(BF16)
    lo = (wr - hi.astype(F32)).astype(BF16)
    return jnp.concatenate([hi, lo], axis=1), br


def kernel(x, c, ctx, c_ctx, ada_w, ada_b, norm_mix, norm_ffn, out_norm, w_out, w_in_ab, q_norm, k_norm, attn_sink, sconv_w, sconv_b, w_in_cd, pool_w, pool_scale, lru_conv_w, lru_conv_b, lru_wa, lru_ba, lru_wx, lru_bx, lru_lambda, router_gw, router_gb, router_ew, router_eb, exp_w1, exp_w3, exp_w2):
    n_batch, seq, d = x.shape
    ctx_len = ctx.shape[1]
    depth = ada_w.shape[0]
    assert ctx_len == SEQ_TILE and seq % TOK_TILE == 0 and seq % GRID_W == 0
    assert seq % MERGE_TILE == 0 and (n_batch * ctx_len) % MERGE_TILE == 0 and MERGE_TILE % SEQ_TILE == 0
    assert depth == 2, "context-side odd-layer outputs are not implemented"
    dims = (n_batch, seq)
    n_lat = n_batch * seq

    cc = jnp.concatenate([c, c_ctx[None, :], jnp.zeros((SUBLANES - n_batch - 1, d), F32)], axis=0)
    mods = _ada(cc, ada_w, ada_b)
    mods = mods[:, :n_batch + 1].reshape(depth, n_batch + 1, N_MOD, d)
    mods = jnp.pad(mods, ((0, 0), (0, 0), (0, SUBLANES - N_MOD), (0, 0)))

    xparts = (x.reshape(n_lat, d), ctx.reshape(n_batch * ctx_len, d))
    cos_t, sin_t = _rope_tables(seq)
    pm = _head_mean_matrix()

    projected = None
    for i in range(depth):
        need_ctx = i < depth - 1
        j = i // 2
        modl = mods[i]
        n_out = n_lat + n_batch * ctx_len if need_ctx else n_lat
        wr, br = _router_matrix(router_gw[i], router_gb[i], router_ew[i], router_eb[i])
        wo_bf = w_out[i].astype(BF16)
        nm, nf, on = norm_mix[i][None, :], norm_ffn[i][None, :], out_norm[i][None, :]
        if i % 2 == 0:
            reps = LANES // HEAD_DIM
            q, kvx, gb, u = _inproj_even(xparts, modl, nm, w_in_ab[j].astype(BF16),
                                          jnp.tile(q_norm[j], reps)[None, :], jnp.tile(k_norm[j], reps)[None, :],
                                          cos_t, sin_t, pm, dims)
            att = _attention(q, kvx, attn_sink[j], dims, need_ctx)
            xn, tok, route, totals = _merge_even(att, gb, u, xparts, modl, sconv_w[j], sconv_b[j][None, :], on, wo_bf,
                                         nf, wr, br, dims, n_out)
        else:
            if projected is None:
                projected = _inproj_odd(xparts, modl, nm, w_in_cd[j].astype(BF16), dims)
            ux, gg, up = projected
            chunk = 2 * LANES
            hf, hr = _lru(ux, lru_conv_w[j], lru_conv_b[j][None, :],
                          _block_diag_chunks(0.5 * lru_wa[j], chunk), _block_diag_chunks(0.5 * lru_wx[j], chunk),
                          0.5 * lru_ba[j][:, None, :], 0.5 * lru_bx[j][:, None, :],
                          lru_lambda[j][:, None, :], dims)
            xn, tok, route, totals = _merge_odd(up, hf, hr, gg, xparts[0], modl, pool_w[j].astype(BF16),
                                        pool_scale[j][None, :], on, wo_bf, nf, wr, br, dims)
        next_odd = None
        if need_ctx and (i + 1) % 2 == 1:
            next_odd = (mods[i + 1], norm_mix[i + 1][None, :], w_in_cd[(i + 1) // 2].astype(BF16))
        outs = _moe(xn, tok, route, totals, modl, exp_w1, exp_w3, exp_w2, i, dims, next_odd)
        xparts = (outs[0],)
        projected = tuple(outs[1:]) if next_odd is not None else None
    return xparts[0][:n_lat].reshape(n_batch, seq, d)
```

```python
import functools

import jax
import jax.numpy as jnp
from jax import lax
from jax.experimental import pallas as pl
from jax.experimental.pallas import tpu as pltpu

F32 = jnp.float32
BF16 = jnp.bfloat16
HIGHEST = lax.Precision.HIGHEST
LOG2E = 1.4426950408889634

GRID_W = 64
EPS = 1e-6
HEAD_DIM = 64
N_Q_HEADS = 8
N_KV_HEADS = 2
Q_PER_KV = N_Q_HEADS // N_KV_HEADS
WINDOW = 128
ATT_BLOCK = 128
ROPE_PAIRS = HEAD_DIM // 4
ROPE_BASE = 10000.0
POOL_WINDOWS = (2, 4, 8, 16)
LRU_C = 8.0
N_GROUPS = 4
EXPERTS_PER_GROUP = 8
N_EXPERTS = N_GROUPS * EXPERTS_PER_GROUP
TOP_K = 2
MOE_BLOCK = 512
ZERO_CHUNK = 64
N_MOD = 6

SUBLANES = 8
LANES = 128
SEQ_TILE = 256
TOK_TILE = 512
MERGE_TILE = 512
ATT_TILE = 256
LRU_TILE = 512
NEG = -1e30
VMEM_LIMIT = 48 * 1024 * 1024


def _cparams(*sem):
    return pltpu.CompilerParams(dimension_semantics=sem, vmem_limit_bytes=VMEM_LIMIT)


def _rms(x, g):
    ms = jnp.mean(x * x, axis=-1, keepdims=True)
    return x * lax.rsqrt(ms + EPS) * g


def _modulate(x, g, shift, scale):
    return _rms(x, g) * (1.0 + scale) + shift


def _bdot(a, b):
    return jnp.dot(a.astype(BF16), b, preferred_element_type=F32)


def _silu(x):
    return x * jax.nn.sigmoid(x)


def _pack_bf16_pair(a, b):
    ha = lax.bitcast_convert_type(a.astype(BF16).astype(F32), jnp.uint32)
    hb = lax.bitcast_convert_type(b.astype(BF16).astype(F32), jnp.uint32)
    return ha | (hb >> 16)


def _unpack_bf16_pair(w):
    a = lax.bitcast_convert_type(w & jnp.uint32(0xFFFF0000), F32)
    b = lax.bitcast_convert_type(w << 16, F32)
    return a, b


def _ada_kernel(c_ref, w_ref, b_ref, o_ref):
    s = _silu(c_ref[...])
    o_ref[0] = jnp.dot(s, w_ref[0], precision=HIGHEST, preferred_element_type=F32) + b_ref[0]


def _ada(cc, ada_w, ada_b):
    depth, d, n = ada_w.shape
    tn = n // 2
    return pl.pallas_call(
        _ada_kernel,
        grid=(depth, n // tn),
        in_specs=[pl.BlockSpec((SUBLANES, d), lambda l, j: (0, 0)),
                  pl.BlockSpec((1, d, tn), lambda l, j: (l, 0, j)),
                  pl.BlockSpec((1, 1, tn), lambda l, j: (l, 0, j))],
        out_specs=pl.BlockSpec((1, SUBLANES, tn), lambda l, j: (l, 0, j)),
        out_shape=jax.ShapeDtypeStruct((depth, SUBLANES, n), F32),
        compiler_params=_cparams("arbitrary", "arbitrary"),
        name="ada",
    )(cc, ada_w, ada_b.reshape(depth, 1, n))


def _head_norm_rope(t, pm, g, cos, sin):
    m = t.shape[0]
    t2 = t * t
    hi = t2.astype(BF16)
    lo = (t2 - hi.astype(F32)).astype(BF16)
    ms2 = jnp.dot(jnp.concatenate([hi, lo], axis=0), pm, preferred_element_type=F32)
    ms = ms2[0:m] + ms2[m:2 * m]
    t = t * lax.rsqrt(ms + EPS) * g
    lane = lax.broadcasted_iota(jnp.int32, t.shape, 1)
    first_half = (lane % (2 * ROPE_PAIRS)) < ROPE_PAIRS
    partner = jnp.where(first_half, pltpu.roll(t, LANES - ROPE_PAIRS, 1), pltpu.roll(t, ROPE_PAIRS, 1))
    return t * cos + partner * sin


def _token_operands(xparts, n_lat_tiles, tile):
    d = xparts[0].shape[1]
    lat = pl.BlockSpec((tile, d), lambda j: (jnp.minimum(j, n_lat_tiles - 1), 0))
    if len(xparts) == 2:
        ctx = pl.BlockSpec((tile, d), lambda j: (jnp.maximum(j - n_lat_tiles, 0), 0))
        return list(xparts), [lat, ctx]
    ctx = pl.BlockSpec((tile, d), lambda j: (jnp.maximum(j, n_lat_tiles), 0))
    return [xparts[0], xparts[0]], [lat, ctx]


def _pick_tokens(xl_ref, xc_ref, n_lat_tiles):
    return jnp.where(pl.program_id(0) >= n_lat_tiles, xc_ref[...], xl_ref[...])


def _inproj_even_kernel(xl_ref, xc_ref, mod_ref, g_ref, w_ref, qg_ref, kg_ref, cos_ref, sin_ref, pm_ref,
                        q_ref, kvx_ref, gb_ref, u_ref, *, n_lat_tiles):
    mod = mod_ref[0]
    x = _pick_tokens(xl_ref, xc_ref, n_lat_tiles)
    h = _modulate(x, g_ref[...], mod[0:1], mod[1:2]).astype(BF16)
    cos, sin, pm = cos_ref[...], sin_ref[...], pm_ref[...]
    kv_w = N_KV_HEADS * HEAD_DIM
    att_w = N_Q_HEADS * HEAD_DIM
    sc_w = gb_ref.shape[1]
    kv = jnp.dot(h, w_ref[:, 0:2 * kv_w], preferred_element_type=F32)
    k = _head_norm_rope(kv[:, :kv_w], pm, kg_ref[...], cos, sin)
    v = kv[:, kv_w:]
    kvx_ref[...] = jnp.concatenate([k, pltpu.roll(k, HEAD_DIM, 1), v, pltpu.roll(v, HEAD_DIM, 1)],
                                   axis=-1).astype(BF16)
    q = jnp.dot(h, w_ref[:, 2 * kv_w:2 * kv_w + att_w], preferred_element_type=F32)
    qscale = HEAD_DIM ** -0.5 * LOG2E
    for c in range(att_w // LANES):
        qc = _head_norm_rope(q[:, c * LANES:(c + 1) * LANES], pm, qg_ref[...], cos, sin)
        q_ref[:, c * LANES:(c + 1) * LANES] = (qc * qscale).astype(BF16)
    o = 2 * kv_w + att_w
    gb_ref[...] = jnp.dot(h, w_ref[:, o:o + sc_w], preferred_element_type=F32)
    gc = jnp.dot(h, w_ref[:, o + sc_w:o + 2 * sc_w], preferred_element_type=F32)
    gx = jnp.dot(h, w_ref[:, o + 2 * sc_w:o + 3 * sc_w], preferred_element_type=F32)
    u_ref[...] = gc * gx


def _tile_maps(n_lat_tiles, tiles_per_seq, n_batch):
    def mod_map(j):
        return (jnp.minimum(j // tiles_per_seq, n_batch), 0, 0)

    def pos_map(j):
        return (jnp.where(j < n_lat_tiles, j % tiles_per_seq, tiles_per_seq), 0)

    return mod_map, pos_map


def _inproj_even(xparts, modl, g, w_bf, q_g, k_g, cos_t, sin_t, pm, dims):
    n_tok = sum(p.shape[0] for p in xparts)
    d = xparts[0].shape[1]
    n_batch, seq = dims
    tile = TOK_TILE
    tps = seq // tile
    mod_map, pos_map = _tile_maps(n_batch * tps, tps, n_batch)
    xs, xspecs = _token_operands(xparts, n_batch * tps, tile)
    kv_w, att_w = N_KV_HEADS * HEAD_DIM, N_Q_HEADS * HEAD_DIM
    sc_w = (w_bf.shape[1] - 2 * kv_w - att_w) // 3
    row = lambda j: (j, 0)
    const = lambda j: (0, 0)
    return pl.pallas_call(
        functools.partial(_inproj_even_kernel, n_lat_tiles=n_batch * tps),
        grid=(n_tok // tile,),
        in_specs=xspecs + [
                  pl.BlockSpec((1, SUBLANES, d), mod_map),
                  pl.BlockSpec((1, d), const),
                  pl.BlockSpec(w_bf.shape, const),
                  pl.BlockSpec((1, LANES), const),
                  pl.BlockSpec((1, LANES), const),
                  pl.BlockSpec((tile, LANES), pos_map),
                  pl.BlockSpec((tile, LANES), pos_map),
                  pl.BlockSpec((LANES, LANES), const)],
        out_specs=[pl.BlockSpec((tile, att_w), row),
                   pl.BlockSpec((tile, 4 * kv_w), row),
                   pl.BlockSpec((tile, sc_w), row),
                   pl.BlockSpec((tile, sc_w), row)],
        out_shape=[jax.ShapeDtypeStruct((n_tok, att_w), BF16),
                   jax.ShapeDtypeStruct((n_tok, 4 * kv_w), BF16),
                   jax.ShapeDtypeStruct((n_tok, sc_w), F32),
                   jax.ShapeDtypeStruct((n_tok, sc_w), F32)],
        compiler_params=_cparams("arbitrary"),
        name="inproj_even",
    )(*xs, modl, g, w_bf, q_g, k_g, cos_t, sin_t, pm)


def _attn_heads(q, kvx, mask_fn, sink_ref, o_ref):
    m, n = q.shape[0], kvx.shape[0]
    k_nat, k_swp = kvx[:, 0:LANES], kvx[:, LANES:2 * LANES]
    v_nat, v_swp = kvx[:, 2 * LANES:3 * LANES], kvx[:, 3 * LANES:4 * LANES]
    lo_n = lax.broadcasted_iota(jnp.int32, (n, LANES), 1) < HEAD_DIM
    lo_m = lax.broadcasted_iota(jnp.int32, (2 * m, LANES), 1) < HEAD_DIM
    top = lax.broadcasted_iota(jnp.int32, (2 * m, 1), 0) < m
    zero = jnp.zeros_like(k_nat)
    r2 = lax.broadcasted_iota(jnp.int32, (2 * n, LANES), 0) < n
    l2 = lax.broadcasted_iota(jnp.int32, (2 * n, LANES), 1) < HEAD_DIM
    ones_both = jnp.where(r2 == l2, 1.0, 0.0).astype(BF16)
    outs = []
    for h in range(N_KV_HEADS):
        k_lo, k_hi = (k_nat, k_swp) if h == 0 else (k_swp, k_nat)
        v_lo, v_hi = (v_nat, v_swp) if h == 0 else (v_swp, v_nat)
        g0 = 2 * h
        q2 = jnp.concatenate([q[:, g0 * LANES:(g0 + 1) * LANES],
                              q[:, (g0 + 1) * LANES:(g0 + 2) * LANES]], axis=0)
        k_both = jnp.concatenate([jnp.where(lo_n, k_lo, zero), jnp.where(lo_n, zero, k_hi)], axis=0)
        v_both = jnp.concatenate([jnp.where(lo_n, v_lo, zero), jnp.where(lo_n, zero, v_hi)], axis=0)
        s = lax.dot_general(q2, k_both, (((1,), (1,)), ((), ())), preferred_element_type=F32)
        hq = Q_PER_KV * h
        halves = []
        for par in range(2):
            sp = mask_fn(s[:, par * n:(par + 1) * n])
            snk = jnp.where(top, sink_ref[hq + par], sink_ref[hq + 2 + par]) * LOG2E
            mx = jnp.maximum(jnp.max(sp, axis=-1, keepdims=True), snk)
            halves.append((jnp.exp2(sp - mx), jnp.exp2(snk - mx)))
        (p_lo, e_lo), (p_hi, e_hi) = halves
        p = jnp.concatenate([p_lo, p_hi], axis=1).astype(BF16)
        nd = jnp.dot(p, jnp.concatenate([v_both, ones_both], axis=1), preferred_element_type=F32)
        o = nd[:, :LANES] / (nd[:, LANES:] + jnp.where(lo_m, e_lo, e_hi))
        outs += [o[0:m], o[m:2 * m]]
    o_ref[...] = jnp.concatenate(outs, axis=-1)


def _attn_lat_kernel(sink_ref, q_ref, kp_ref, kc_ref, kn_ref, kx_ref, o_ref):
    i = pl.program_id(1)
    nb = pl.num_programs(1)
    kvx = jnp.concatenate([kp_ref[...], kc_ref[...], kn_ref[...], kx_ref[...]], axis=0)
    m, w = ATT_TILE, WINDOW
    tok = lax.broadcasted_iota(jnp.int32, (2 * m, m), 0) % m
    col = lax.broadcasted_iota(jnp.int32, (2 * m, m), 1)
    tok_w = lax.broadcasted_iota(jnp.int32, (2 * m, w), 0) % m
    col_w = lax.broadcasted_iota(jnp.int32, (2 * m, w), 1)
    ok_prev = (col_w >= tok_w) & (i > 0)
    ok_cur = (col - tok <= w) & (tok - col <= w)
    ok_next = (col_w + (m - w) <= tok_w) & (i < nb - 1)

    def mask_fn(s):
        return jnp.concatenate([jnp.where(ok_prev, s[:, 0:w], NEG), jnp.where(ok_cur, s[:, w:w + m], NEG),
                                jnp.where(ok_next, s[:, w + m:2 * w + m], NEG), s[:, 2 * w + m:]], axis=1)

    _attn_heads(q_ref[...], kvx, mask_fn, sink_ref, o_ref)


def _attn_ctx_kernel(sink_ref, att_in_ref, q_ref, kx_ref, o_ref):
    del att_in_ref
    _attn_heads(q_ref[...], kx_ref[...], lambda s: s, sink_ref, o_ref)


def _attention(q, kvx, sink, dims, need_ctx):
    n_tok = q.shape[0]
    n_batch, seq = dims
    nb = seq // ATT_TILE
    r = ATT_TILE // WINDOW
    nw = seq // WINDOW
    n_lat = n_batch * seq
    ctx_len = (n_tok - n_lat) // n_batch
    att_w, kvx_w = q.shape[1], kvx.shape[1]
    ctx_blk0 = n_lat // ctx_len
    cur = lambda b, i: (b * nb + i, 0)
    prev = lambda b, i: (b * nw + jnp.maximum(i * r - 1, 0), 0)
    nxt = lambda b, i: (b * nw + jnp.minimum((i + 1) * r, nw - 1), 0)
    cx = lambda b, i: (ctx_blk0 + b, 0)
    smem = pl.BlockSpec(memory_space=pltpu.SMEM)
    att = pl.pallas_call(
        _attn_lat_kernel,
        grid=(n_batch, nb),
        in_specs=[smem, pl.BlockSpec((ATT_TILE, att_w), cur),
                  pl.BlockSpec((WINDOW, kvx_w), prev), pl.BlockSpec((ATT_TILE, kvx_w), cur),
                  pl.BlockSpec((WINDOW, kvx_w), nxt), pl.BlockSpec((ctx_len, kvx_w), cx)],
        out_specs=pl.BlockSpec((ATT_TILE, att_w), cur),
        out_shape=jax.ShapeDtypeStruct((n_tok, att_w), F32),
        compiler_params=_cparams("arbitrary", "arbitrary"),
        name="attn_latent",
    )(sink, q, kvx, kvx, kvx, kvx)
    if not need_ctx:
        return att
    cxb = lambda b: (ctx_blk0 + b, 0)
    return pl.pallas_call(
        _attn_ctx_kernel,
        grid=(n_batch,),
        in_specs=[smem, pl.BlockSpec(memory_space=pl.ANY),
                  pl.BlockSpec((ctx_len, att_w), cxb), pl.BlockSpec((ctx_len, kvx_w), cxb)],
        out_specs=pl.BlockSpec((ctx_len, att_w), cxb),
        out_shape=jax.ShapeDtypeStruct((n_tok, att_w), F32),
        input_output_aliases={1: 0},
        compiler_params=_cparams("arbitrary"),
        name="attn_context",
    )(sink, att, q, kvx)


def _route(logits):
    lane = lax.broadcasted_iota(jnp.int32, logits.shape, 1).astype(F32)
    big = float(LANES)
    gl = jnp.where((lane >= N_EXPERTS) & (lane < N_EXPERTS + N_GROUPS), logits, NEG)
    gmax = jnp.max(gl, axis=-1, keepdims=True)
    gidx = jnp.min(jnp.where(gl == gmax, lane, big), axis=-1, keepdims=True) - N_EXPERTS
    p_sel = 1.0 / jnp.sum(jnp.exp(gl - gmax), axis=-1, keepdims=True)
    lo = gidx * EXPERTS_PER_GROUP
    el = jnp.where((lane >= lo) & (lane < lo + EXPERTS_PER_GROUP), logits, NEG)
    m1 = jnp.max(el, axis=-1, keepdims=True)
    i1 = jnp.min(jnp.where(el == m1, lane, big), axis=-1, keepdims=True)
    el2 = jnp.where(lane == i1, NEG, el)
    m2 = jnp.max(el2, axis=-1, keepdims=True)
    i2 = jnp.min(jnp.where(el2 == m2, lane, big), axis=-1, keepdims=True)
    t = jnp.exp(m2 - m1)
    w1 = p_sel / (1.0 + t)
    w2 = p_sel * t / (1.0 + t)
    return jnp.where(lane == 0, i1, jnp.where(lane == 1, i2, jnp.where(lane == 2, w1,
                     jnp.where(lane == 3, w2, 0.0))))


def _tail(n1, n2, gain, wo_ref, x, mod, nf_ref, wr_ref, br_ref, xo_ref, tok_ref, route_ref, cnt_ref):
    half = n1.shape[1]
    y = (_bdot(_rms(n1, gain[:, :half]), wo_ref[0:half, :])
         + _bdot(_rms(n2, gain[:, half:]), wo_ref[half:, :]))
    xn = x + mod[2:3] * y
    xo_ref[...] = xn
    tok = _modulate(xn, nf_ref[...], mod[3:4], mod[4:5])
    half_d = tok.shape[1] // 2
    tok_ref[...] = _pack_bf16_pair(tok[:, :half_d], tok[:, half_d:])
    t_hi = tok.astype(BF16)
    t_lo = (tok - t_hi.astype(F32)).astype(BF16)
    hh = jnp.dot(t_hi, wr_ref[...], preferred_element_type=F32)
    logits = (hh[:, :LANES] + hh[:, LANES:] + br_ref[...]
              + jnp.dot(t_lo, wr_ref[:, 0:LANES], preferred_element_type=F32))
    route = _route(logits)
    route_ref[...] = route
    lane = lax.broadcasted_iota(jnp.int32, route.shape, 1).astype(F32)
    hit = jnp.where((lane == route[:, 0:1]) | (lane == route[:, 1:2]), 1.0, 0.0)

    @pl.when(pl.program_id(0) == 0)
    def _():
        cnt_ref[...] = jnp.zeros_like(cnt_ref)

    cnt_ref[0:1, :] += jnp.sum(hit, axis=0, keepdims=True)


def _seq_flags(j, n_lat_tiles, tiles_per_seq):
    is_ctx = j >= n_lat_tiles
    first = is_ctx | (j % tiles_per_seq == 0)
    last = is_ctx | (j % tiles_per_seq == tiles_per_seq - 1)
    return first, last


def _with_halo(prev_ref, tile, next_ref, first, last):
    p = jnp.where(first, 0.0, prev_ref[...])
    n = jnp.where(last, 0.0, next_ref[...])
    return jnp.concatenate([p, tile, n], axis=0)


def _shift_rows(e, k):
    return pltpu.roll(e, (-k) % e.shape[0], 0)


def _merge_even_kernel(att_ref, gb_ref, u_ref, up_ref, un_ref, xl_ref, xc_ref, mod_ref, cw_ref, cb_ref,
                       on_ref, wo_ref, nf_ref, wr_ref, br_ref,
                       xo_ref, tok_ref, route_ref, cnt_ref, *, n_lat_tiles, tiles_per_seq):
    j = pl.program_id(0)
    first, last = _seq_flags(j, n_lat_tiles, tiles_per_seq)
    e = _with_halo(up_ref, u_ref[...], un_ref, first, last)
    cw = cw_ref[...]
    lo, hi = SUBLANES, SUBLANES + MERGE_TILE
    pos = lax.broadcasted_iota(jnp.int32, (MERGE_TILE, 1), 0) % SEQ_TILE
    is_ctx = j >= n_lat_tiles
    before = jnp.where(is_ctx & (pos == 0), 0.0, _shift_rows(e, -1)[lo:hi])
    after = jnp.where(is_ctx & (pos == SEQ_TILE - 1), 0.0, _shift_rows(e, 1)[lo:hi])
    conv = cw[0:1] * before + cw[1:2] * e[lo:hi] + cw[2:3] * after + cb_ref[...]
    conv = gb_ref[...] * conv
    x = _pick_tokens(xl_ref, xc_ref, n_lat_tiles)
    _tail(att_ref[...], conv, on_ref[...], wo_ref, x, mod_ref[0], nf_ref, wr_ref, br_ref,
          xo_ref, tok_ref, route_ref, cnt_ref)


def _halo_specs(width, n_rows):
    r = MERGE_TILE // SUBLANES
    n_blk = n_rows // SUBLANES
    prev = pl.BlockSpec((SUBLANES, width), lambda j: (jnp.maximum(j * r - 1, 0), 0))
    nxt = pl.BlockSpec((SUBLANES, width), lambda j: (jnp.minimum((j + 1) * r, n_blk - 1), 0))
    return prev, nxt


def _tail_specs(d, n_out_tok):
    row = lambda j: (j, 0)
    out_specs = [pl.BlockSpec((MERGE_TILE, d), row), pl.BlockSpec((MERGE_TILE, d // 2), row),
                 pl.BlockSpec((MERGE_TILE, LANES), row), pl.BlockSpec((SUBLANES, LANES), lambda j: (0, 0))]
    out_shape = [jax.ShapeDtypeStruct((n_out_tok, d), F32), jax.ShapeDtypeStruct((n_out_tok, d // 2), jnp.uint32),
                 jax.ShapeDtypeStruct((n_out_tok, LANES), F32), jax.ShapeDtypeStruct((SUBLANES, LANES), F32)]
    return out_specs, out_shape


def _merge_even(att, gb, u, xparts, modl, cw, cb, on, wo_bf, nf, wr, br, dims, n_out_tok):
    n_tok = u.shape[0]
    d = xparts[0].shape[1]
    n_batch, seq = dims
    tps = seq // MERGE_TILE
    mod_map, _ = _tile_maps(n_batch * tps, tps, n_batch)
    xs, xspecs = _token_operands(xparts, n_batch * tps, MERGE_TILE)
    sc_w = u.shape[1]
    row = lambda j: (j, 0)
    const = lambda j: (0, 0)
    prev, nxt = _halo_specs(sc_w, n_tok)
    out_specs, out_shape = _tail_specs(d, n_out_tok)
    return pl.pallas_call(
        functools.partial(_merge_even_kernel, n_lat_tiles=n_batch * tps, tiles_per_seq=tps),
        grid=(n_out_tok // MERGE_TILE,),
        in_specs=[pl.BlockSpec((MERGE_TILE, att.shape[1]), row),
                  pl.BlockSpec((MERGE_TILE, sc_w), row),
                  pl.BlockSpec((MERGE_TILE, sc_w), row), prev, nxt] + xspecs + [
                  pl.BlockSpec((1, SUBLANES, d), mod_map),
                  pl.BlockSpec(cw.shape, const), pl.BlockSpec((1, sc_w), const),
                  pl.BlockSpec((1, d), const), pl.BlockSpec(wo_bf.shape, const),
                  pl.BlockSpec((1, d), const), pl.BlockSpec(wr.shape, const),
                  pl.BlockSpec((1, LANES), const)],
        out_specs=out_specs, out_shape=out_shape,
        compiler_params=_cparams("arbitrary"),
        name="merge_even",
    )(att, gb, u, u, u, *xs, modl, cw, cb, on, wo_bf, nf, wr, br)


def _odd_projection(x, mod, g_ref, w_ref, ux_ref, gg_ref, up_ref):
    h = _modulate(x, g_ref[...], mod[0:1], mod[1:2]).astype(BF16)
    w = ux_ref.shape[1]
    ux_ref[...] = jnp.dot(h, w_ref[:, 0:w], preferred_element_type=F32)
    ug = jnp.dot(h, w_ref[:, w:2 * w], preferred_element_type=F32)
    gg_ref[...] = jax.nn.gelu(ug, approximate=True)
    up_ref[...] = jnp.dot(h, w_ref[:, 2 * w:3 * w], preferred_element_type=F32)


def _inproj_odd_kernel(xl_ref, xc_ref, mod_ref, g_ref, w_ref, ux_ref, gg_ref, up_ref, *, n_lat_tiles):
    x = _pick_tokens(xl_ref, xc_ref, n_lat_tiles)
    _odd_projection(x, mod_ref[0], g_ref, w_ref, ux_ref, gg_ref, up_ref)


def _inproj_odd(xparts, modl, g, w_bf, dims):
    n_tok = sum(p.shape[0] for p in xparts)
    d = xparts[0].shape[1]
    n_batch, seq = dims
    tile = TOK_TILE
    tps = seq // tile
    mod_map, _ = _tile_maps(n_batch * tps, tps, n_batch)
    xs, xspecs = _token_operands(xparts, n_batch * tps, tile)
    w = w_bf.shape[1] // 3
    row = lambda j: (j, 0)
    const = lambda j: (0, 0)
    return pl.pallas_call(
        functools.partial(_inproj_odd_kernel, n_lat_tiles=n_batch * tps),
        grid=(n_tok // tile,),
        in_specs=xspecs + [
                  pl.BlockSpec((1, SUBLANES, d), mod_map),
                  pl.BlockSpec((1, d), const),
                  pl.BlockSpec(w_bf.shape, const)],
        out_specs=[pl.BlockSpec((tile, w), row)] * 3,
        out_shape=[jax.ShapeDtypeStruct((n_tok, w), F32)] * 3,
        compiler_params=_cparams("arbitrary"),
        name="inproj_odd",
    )(*xs, modl, g, w_bf)


def _lru_conv(e, cw, cb):
    lo, hi = SUBLANES, e.shape[0] - SUBLANES
    return (cw[0:1] * _shift_rows(e, -2)[lo:hi] + cw[1:2] * _shift_rows(e, -1)[lo:hi]
            + cw[2:3] * e[lo:hi] + cw[3:4] * _shift_rows(e, 1)[lo:hi] + cb)


def _lru_coeffs(xl, d, wa_ref, wx_ref, ba_ref, bx_ref, lam_ref, a_ref, b_ref):
    xb = xl.astype(BF16)
    chunk = wa_ref.shape[-1]
    n_chunk = xl.shape[1] // chunk
    pre_a = jnp.concatenate([jnp.dot(xb[:, c * chunk:(c + 1) * chunk], wa_ref[d, c],
                                     preferred_element_type=F32) for c in range(n_chunk)], axis=-1)
    pre_x = jnp.concatenate([jnp.dot(xb[:, c * chunk:(c + 1) * chunk], wx_ref[d, c],
                                     preferred_element_type=F32) for c in range(n_chunk)], axis=-1)
    tr = jnp.tanh(pre_a + ba_ref[d])
    tg = jnp.tanh(pre_x + bx_ref[d])
    c_half = (-0.5 * LRU_C) * jax.nn.softplus(-lam_ref[d])
    log_a = c_half * tr + c_half
    a_ref[...] = jnp.exp(log_a)
    th = jnp.tanh(log_a)
    b_ref[...] = jnp.sqrt(-0.5 * th / (1.0 - th)) * (1.0 + tg) * xl


def _scan_tile(a_ref, b_ref, h_ref, carry, reverse):
    n_grp = a_ref.shape[0] // SUBLANES
    row = lax.broadcasted_iota(jnp.int32, (SUBLANES, a_ref.shape[1]), 0)

    def body(g, carry):
        gi = (n_grp - 1 - g) if reverse else g
        r0 = pl.multiple_of(gi * SUBLANES, SUBLANES)
        a8 = a_ref[pl.ds(r0, SUBLANES), :]
        b8 = b_ref[pl.ds(r0, SUBLANES), :]
        for s in (1, 2, 4):
            if reverse:
                ok = row < SUBLANES - s
                sh = SUBLANES - s
            else:
                ok = row >= s
                sh = s
            a_sh = jnp.where(ok, pltpu.roll(a8, sh, 0), 1.0)
            b_sh = jnp.where(ok, pltpu.roll(b8, sh, 0), 0.0)
            b8 = a8 * b_sh + b8
            a8 = a8 * a_sh
        h8 = a8 * carry + b8
        if h_ref is not None:
            h_ref[pl.ds(r0, SUBLANES), :] = h8
        return h8[0:1] if reverse else h8[SUBLANES - 1:SUBLANES]

    return lax.fori_loop(0, n_grp, body, carry, unroll=4)


def _lru_kernel(uf_ref, ufp_ref, ufn_ref, ur_ref, urp_ref, urn_ref, uc_ref, cw_ref, cb_ref,
                wa_ref, wx_ref, ba_ref, bx_ref, lam_ref, hf_ref, hr_ref,
                cf_ref, cr_ref, a_ref, b_ref):
    i = pl.program_id(1)
    nt = pl.num_programs(1)
    cw, cb = cw_ref[...], cb_ref[...]
    coeffs = functools.partial(_lru_coeffs, wa_ref=wa_ref, wx_ref=wx_ref, ba_ref=ba_ref,
                               bx_ref=bx_ref, lam_ref=lam_ref, a_ref=a_ref, b_ref=b_ref)

    @pl.when(i == 0)
    def _():
        n_ctx = uc_ref.shape[0]
        ac_ref, bc_ref = a_ref.at[pl.ds(0, n_ctx)], b_ref.at[pl.ds(0, n_ctx)]
        z = jnp.zeros((SUBLANES, uc_ref.shape[1]), F32)
        xc = _lru_conv(jnp.concatenate([z, uc_ref[...], z], axis=0), cw, cb)
        zero = jnp.zeros((1, uc_ref.shape[1]), F32)
        coeffs(xc, 0, a_ref=ac_ref, b_ref=bc_ref)
        cf_ref[...] = _scan_tile(ac_ref, bc_ref, None, zero, False)
        coeffs(xc, 1, a_ref=ac_ref, b_ref=bc_ref)
        cr_ref[...] = _scan_tile(ac_ref, bc_ref, None, zero, True)

    xf = _lru_conv(_with_halo(ufp_ref, uf_ref[...], ufn_ref, i == 0, i == nt - 1), cw, cb)
    coeffs(xf, 0)
    cf_ref[...] = _scan_tile(a_ref, b_ref, hf_ref, cf_ref[...], False)
    xr = _lru_conv(_with_halo(urp_ref, ur_ref[...], urn_ref, i == nt - 1, i == 0), cw, cb)
    coeffs(xr, 1)
    cr_ref[...] = _scan_tile(a_ref, b_ref, hr_ref, cr_ref[...], True)


def _lru(ux, cw, cb, wa_bd, wx_bd, ba, bx, lam, dims):
    n_tok, w = ux.shape
    n_batch, seq = dims
    nt = seq // LRU_TILE
    n_lat = n_batch * seq
    r = LRU_TILE // SUBLANES
    n_blk = n_tok // SUBLANES
    fwd = lambda b, i: b * nt + i
    rev = lambda b, i: b * nt + nt - 1 - i

    def specs(tile):
        return [pl.BlockSpec((LRU_TILE, w), lambda b, i: (tile(b, i), 0)),
                pl.BlockSpec((SUBLANES, w), lambda b, i: (jnp.maximum(tile(b, i) * r - 1, 0), 0)),
                pl.BlockSpec((SUBLANES, w), lambda b, i: (jnp.minimum((tile(b, i) + 1) * r, n_blk - 1), 0))]

    const2 = lambda b, i: (0, 0)
    const3 = lambda b, i: (0, 0, 0)
    const4 = lambda b, i: (0, 0, 0, 0)
    return pl.pallas_call(
        _lru_kernel,
        grid=(n_batch, nt),
        in_specs=specs(fwd) + specs(rev) + [
            pl.BlockSpec((SEQ_TILE, w), lambda b, i: (n_lat // SEQ_TILE + b, 0)),
            pl.BlockSpec(cw.shape, const2), pl.BlockSpec((1, w), const2),
            pl.BlockSpec(wa_bd.shape, const4), pl.BlockSpec(wx_bd.shape, const4),
            pl.BlockSpec(ba.shape, const3), pl.BlockSpec(bx.shape, const3),
            pl.BlockSpec(lam.shape, const3)],
        out_specs=[pl.BlockSpec((LRU_TILE, w), lambda b, i: (fwd(b, i), 0)),
                   pl.BlockSpec((LRU_TILE, w), lambda b, i: (rev(b, i), 0))],
        out_shape=[jax.ShapeDtypeStruct((n_lat, w), F32)] * 2,
        scratch_shapes=[pltpu.VMEM((1, w), F32), pltpu.VMEM((1, w), F32),
                        pltpu.VMEM((LRU_TILE, w), F32), pltpu.VMEM((LRU_TILE, w), F32)],
        compiler_params=_cparams("arbitrary", "arbitrary"),
        name="rglru",
    )(ux, ux, ux, ux, ux, ux, ux, cw, cb, wa_bd, wx_bd, ba, bx, lam)


def _merge_odd_kernel(up_ref, upp_ref, upn_ref, hf_ref, hr_ref, gg_ref, x_ref, mod_ref,
                      pw_ref, ps_ref, on_ref, wo_ref, nf_ref, wr_ref, br_ref,
                      xo_ref, tok_ref, route_ref, cnt_ref, *, tiles_per_seq, seq):
    j = pl.program_id(0)
    ts = j % tiles_per_seq
    e = _with_halo(upp_ref, up_ref[...], upn_ref, ts == 0, ts == tiles_per_seq - 1)
    gw = pw_ref.shape[-1]
    lo, hi = SUBLANES, SUBLANES + MERGE_TILE
    tpos = ts * MERGE_TILE + lax.broadcasted_iota(jnp.int32, (MERGE_TILE, 1), 0)
    parts = []
    for g, win in enumerate(POOL_WINDOWS):
        eg = e[:, g * gw:(g + 1) * gw]
        p = eg
        span = 1
        while span < win:
            p = p + _shift_rows(p, span)
            span *= 2
        half = win // 2
        s = _shift_rows(p, -half)[lo:hi]
        cnt = (jnp.minimum(tpos + half, seq) - jnp.maximum(tpos - half, 0)).astype(F32)
        dlt = s / cnt - eg[lo:hi]
        parts.append(_bdot(dlt, pw_ref[g]))
    pool = jnp.concatenate(parts, axis=-1) * ps_ref[...]
    lru = (hf_ref[...] + hr_ref[...]) * gg_ref[...]
    _tail(pool, lru, on_ref[...], wo_ref, x_ref[...], mod_ref[0], nf_ref, wr_ref, br_ref,
          xo_ref, tok_ref, route_ref, cnt_ref)


def _merge_odd(up, hf, hr, gg, xf, modl, pw_bf, ps, on, wo_bf, nf, wr, br, dims):
    d = xf.shape[1]
    n_tok, w = up.shape
    n_batch, seq = dims
    tps = seq // MERGE_TILE
    n_lat = n_batch * seq
    mod_map, _ = _tile_maps(n_batch * tps, tps, n_batch)
    row = lambda j: (j, 0)
    const = lambda j: (0, 0)
    prev, nxt = _halo_specs(w, n_tok)
    out_specs, out_shape = _tail_specs(d, n_lat)
    return pl.pallas_call(
        functools.partial(_merge_odd_kernel, tiles_per_seq=tps, seq=seq),
        grid=(n_lat // MERGE_TILE,),
        in_specs=[pl.BlockSpec((MERGE_TILE, w), row), prev, nxt,
                  pl.BlockSpec((MERGE_TILE, w), row), pl.BlockSpec((MERGE_TILE, w), row),
                  pl.BlockSpec((MERGE_TILE, w), row),
                  pl.BlockSpec((MERGE_TILE, d), row),
                  pl.BlockSpec((1, SUBLANES, d), mod_map),
                  pl.BlockSpec(pw_bf.shape, lambda j: (0, 0, 0)), pl.BlockSpec((1, w), const),
                  pl.BlockSpec((1, d), const), pl.BlockSpec(wo_bf.shape, const),
                  pl.BlockSpec((1, d), const), pl.BlockSpec(wr.shape, const),
                  pl.BlockSpec((1, LANES), const)],
        out_specs=out_specs, out_shape=out_shape,
        compiler_params=_cparams("arbitrary"),
        name="merge_odd",
    )(up, up, up, hf, hr, gg, xf, modl, pw_bf, ps, on, wo_bf, nf, wr, br)


def _slots_kernel(route_ref, tot_ref, dest_ref, cnt_ref, run_ref, start_ref):
    i = pl.program_id(0)
    rt = route_ref[...]
    m = rt.shape[0]
    lane = lax.broadcasted_iota(jnp.int32, rt.shape, 1).astype(F32)
    oh1 = lane == rt[:, 0:1]
    oh2 = lane == rt[:, 1:2]
    s = jnp.where(oh1 | oh2, 1.0, 0.0)

    @pl.when(i == 0)
    def _():
        tot = tot_ref[0:1, :]
        padded = jnp.floor((tot + (MOE_BLOCK - 1)) / MOE_BLOCK) * MOE_BLOCK
        r = lax.broadcasted_iota(jnp.int32, (LANES, LANES), 0)
        c = lax.broadcasted_iota(jnp.int32, (LANES, LANES), 1)
        upper = jnp.where(r < c, 1.0, 0.0)
        p8 = jnp.broadcast_to(padded, (SUBLANES, LANES))
        start = jnp.dot(p8, upper, precision=HIGHEST, preferred_element_type=F32)[0:1]
        start_ref[...] = start
        run_ref[...] = jnp.zeros_like(run_ref)
        sub = lax.broadcasted_iota(jnp.int32, cnt_ref.shape, 0)
        cnt_ref[...] = jnp.where(sub == 0, tot, jnp.where(sub == 1, start, 0.0))

    r = lax.broadcasted_iota(jnp.int32, (m, m), 0)
    c = lax.broadcasted_iota(jnp.int32, (m, m), 1)
    lower = jnp.where(c < r, 1.0, 0.0).astype(BF16)
    prefix = jnp.dot(lower, s.astype(BF16), preferred_element_type=F32)
    base = prefix + run_ref[...] + start_ref[...]
    d1 = jnp.sum(jnp.where(oh1, base, 0.0), axis=-1, keepdims=True)
    d2 = jnp.sum(jnp.where(oh2, base, 0.0), axis=-1, keepdims=True)
    dest_ref[...] = jnp.where(lane == 0, d1, jnp.where(lane == 1, d2, 0.0)).astype(jnp.int32)
    run_ref[...] += jnp.sum(s, axis=0, keepdims=True)


def _slots(route, totals):
    n_tok = route.shape[0]
    nt = n_tok // TOK_TILE
    return pl.pallas_call(
        _slots_kernel,
        grid=(nt,),
        in_specs=[pl.BlockSpec((TOK_TILE, LANES), lambda i: (i, 0)),
                  pl.BlockSpec((SUBLANES, LANES), lambda i: (0, 0))],
        out_specs=[pl.BlockSpec((TOK_TILE, LANES), lambda i: (i, 0)),
                   pl.BlockSpec((SUBLANES, LANES), lambda i: (0, 0))],
        out_shape=[jax.ShapeDtypeStruct((n_tok, LANES), jnp.int32),
                   jax.ShapeDtypeStruct((SUBLANES, LANES), F32)],
        scratch_shapes=[pltpu.VMEM((1, LANES), F32)] * 2,
        compiler_params=_cparams("arbitrary"),
        name="moe_slots",
    )(route, totals)


def _for_each_row(n_rows, fn):
    def body(g, c):
        g8 = pl.multiple_of(g * SUBLANES, SUBLANES)
        for j in range(SUBLANES):
            fn(g8, j)
        return c
    lax.fori_loop(0, n_rows // SUBLANES, body, 0)


def _dispatch_kernel(dest_ref, pad_lo_ref, pad_hi_ref, tok_ref, xs_ref, zrow, sem, zsem):
    i = pl.program_id(0)
    base = i * TOK_TILE

    def zero_copy(r, n):
        return pltpu.make_async_copy(zrow.at[pl.ds(0, n)], xs_ref.at[pl.ds(r, n)], zsem)

    def for_each_pad_chunk(fn):
        def per_expert(e, c):
            lo, hi = pad_lo_ref[e], pad_hi_ref[e]
            a8 = (lo + SUBLANES - 1) // SUBLANES * SUBLANES
            a64 = (a8 + ZERO_CHUNK - 1) // ZERO_CHUNK * ZERO_CHUNK

            def rows(r, c):
                fn(zero_copy(r, 1))
                return c

            def eights(q, c):
                fn(zero_copy(pl.multiple_of(a8 + q * SUBLANES, SUBLANES), SUBLANES))
                return c

            def chunks(q, c):
                fn(zero_copy(pl.multiple_of(a64 + q * ZERO_CHUNK, ZERO_CHUNK), ZERO_CHUNK))
                return c

            lax.fori_loop(lo, a8, rows, 0)
            lax.fori_loop(0, (a64 - a8) // SUBLANES, eights, 0)
            lax.fori_loop(0, (hi - a64) // ZERO_CHUNK, chunks, 0)
            return c
        lax.fori_loop(0, N_EXPERTS, per_expert, 0)

    @pl.when(i == 0)
    def _():
        zrow[...] = jnp.zeros_like(zrow)
        for_each_pad_chunk(lambda cp: cp.start())

    def issue(g8, j):
        rows = tok_ref.at[pl.ds(g8, SUBLANES)]
        for k in range(TOP_K):
            d = dest_ref[(base + g8) * TOP_K + (j * TOP_K + k)]
            pltpu.make_async_copy(rows.at[pl.ds(j, 1)], xs_ref.at[pl.ds(d, 1)], sem).start(priority=k)

    _for_each_row(TOK_TILE, issue)
    for k in range(TOP_K):
        pltpu.make_async_copy(tok_ref, xs_ref.at[pl.ds(0, TOK_TILE)], sem).wait()

    @pl.when(i == pl.num_programs(0) - 1)
    def _():
        for_each_pad_chunk(lambda cp: cp.wait())


def _dispatch(dest_flat, pad_lo, pad_hi, tok, n_slots):
    n_tok, d = tok.shape
    return pl.pallas_call(
        _dispatch_kernel,
        grid_spec=pltpu.PrefetchScalarGridSpec(
            num_scalar_prefetch=3,
            grid=(n_tok // TOK_TILE,),
            in_specs=[pl.BlockSpec((TOK_TILE, d), lambda i, *_: (i, 0))],
            out_specs=pl.BlockSpec(memory_space=pl.ANY),
            scratch_shapes=[pltpu.VMEM((ZERO_CHUNK, d), tok.dtype),
                            pltpu.SemaphoreType.DMA(()), pltpu.SemaphoreType.DMA(())]),
        out_shape=jax.ShapeDtypeStruct((n_slots, d), tok.dtype),
        compiler_params=_cparams("arbitrary"),
        name="moe_dispatch",
    )(dest_flat, pad_lo, pad_hi, tok)


def _expert_kernel(be_ref, nu_ref, rows_ref, x_ref, w1_ref, w3_ref, w2_ref, y_ref, w1b, w3b, w2b):
    i = pl.program_id(0)

    def run(rows):
        x_a, x_b = _unpack_bf16_pair(x_ref[0:rows, :])
        x_a, x_b = x_a.astype(BF16), x_b.astype(BF16)
        half = x_ref.shape[1]
        a = (jnp.dot(x_a, w1b[0:half, :], preferred_element_type=F32)
             + jnp.dot(x_b, w1b[half:, :], preferred_element_type=F32))
        b = (jnp.dot(x_a, w3b[0:half, :], preferred_element_type=F32)
             + jnp.dot(x_b, w3b[half:, :], preferred_element_type=F32))
        y = _bdot(_silu(a) * b, w2b[...])
        y_ref[0:rows, :] = _pack_bf16_pair(y[:, :half], y[:, half:])

    @pl.when(i < nu_ref[0])
    def _():
        changed = (i == 0) | (be_ref[i] != be_ref[jnp.maximum(i - 1, 0)])

        @pl.when(changed)
        def _():
            w1b[...] = w1_ref[0, 0].astype(BF16)
            w3b[...] = w3_ref[0, 0].astype(BF16)
            w2b[...] = w2_ref[0, 0].astype(BF16)

        half_block = MOE_BLOCK // 2

        @pl.when(rows_ref[i] > half_block)
        def _():
            run(MOE_BLOCK)

        @pl.when(rows_ref[i] <= half_block)
        def _():
            run(half_block)


def _experts(block_e, n_used, block_rows, xs, w1, w3, w2, layer):
    n_slots, half_d = xs.shape
    d, de = w1.shape[-2], w1.shape[-1]
    n_blocks = n_slots // MOE_BLOCK
    blk = lambda i, be, nu, rows: (jnp.minimum(i, nu[0] - 1), 0)
    wmap = lambda i, be, nu, rows: (layer, be[i], 0, 0)
    return pl.pallas_call(
        _expert_kernel,
        grid_spec=pltpu.PrefetchScalarGridSpec(
            num_scalar_prefetch=3,
            grid=(n_blocks,),
            in_specs=[pl.BlockSpec((MOE_BLOCK, half_d), blk),
                      pl.BlockSpec((1, 1, d, de), wmap), pl.BlockSpec((1, 1, d, de), wmap),
                      pl.BlockSpec((1, 1, de, d), wmap)],
            out_specs=pl.BlockSpec((MOE_BLOCK, half_d), blk),
            scratch_shapes=[pltpu.VMEM((d, de), BF16), pltpu.VMEM((d, de), BF16),
                            pltpu.VMEM((de, d), BF16)]),
        out_shape=jax.ShapeDtypeStruct((n_slots, half_d), jnp.uint32),
        compiler_params=_cparams("arbitrary"),
        name="moe_experts",
    )(block_e, n_used, block_rows, xs, w1, w3, w2)


def _combine_kernel(dest_ref, x_ref, route_ref, mod_ref, ys_ref, *rest, fuse_next):
    if fuse_next:
        nmod_ref, g_ref, w_ref, o_ref, ux_ref, gg_ref, up_ref, yb0, yb1, sem = rest
    else:
        o_ref, yb0, yb1, sem = rest
    i = pl.program_id(0)
    n = pl.num_programs(0)
    bufs = ((yb0, sem.at[0]), (yb1, sem.at[1]))

    def issue(tile, half):
        ybuf, s = bufs[half]
        base = tile * (TOK_TILE * TOP_K)
        for r in range(TOK_TILE):
            for k in range(TOP_K):
                d = dest_ref[base + (r * TOP_K + k)]
                pltpu.make_async_copy(ys_ref.at[pl.ds(d, 1)], ybuf.at[k, pl.ds(r, 1)], s).start(priority=k)

    def wait(half):
        ybuf, s = bufs[half]
        for k in range(TOP_K):
            pltpu.make_async_copy(ys_ref.at[pl.ds(0, TOK_TILE)], ybuf.at[k], s).wait()

    def compute(half):
        ybuf, _ = bufs[half]
        rows = pl.ds(half * TOK_TILE, TOK_TILE)
        rt = route_ref[rows, :]
        w1, w2 = rt[:, 2:3], rt[:, 3:4]
        y1a, y1b = _unpack_bf16_pair(ybuf[0])
        y2a, y2b = _unpack_bf16_pair(ybuf[1])
        y = jnp.concatenate([w1 * y1a + w2 * y2a, w1 * y1b + w2 * y2b], axis=1)
        x1 = x_ref[rows, :] + mod_ref[0][5:6] * y
        o_ref[rows, :] = x1
        if fuse_next:
            _odd_projection(x1, nmod_ref[0], g_ref, w_ref, ux_ref.at[rows], gg_ref.at[rows], up_ref.at[rows])

    @pl.when(i == 0)
    def _():
        issue(0, 0)

    wait(0)
    issue(2 * i + 1, 1)
    compute(0)
    wait(1)
    issue(jnp.minimum(2 * i + 2, 2 * n - 1), 0)
    compute(1)

    @pl.when(i == n - 1)
    def _():
        wait(0)


def _combine(dest_flat, xn, route, modl, ys, dims, next_odd=None):
    n_tok, d = xn.shape
    n_batch, seq = dims
    blk = 2 * TOK_TILE
    assert seq % blk == 0 and n_tok % blk == 0
    tpb = seq // blk
    row = lambda i, dest: (i, 0)
    const = lambda i, dest: (0, 0)
    mod_spec = pl.BlockSpec((1, SUBLANES, d), lambda i, dest: (jnp.minimum(i // tpb, n_batch), 0, 0))
    in_specs = [pl.BlockSpec((blk, d), row), pl.BlockSpec((blk, LANES), row), mod_spec,
                pl.BlockSpec(memory_space=pl.ANY)]
    out_specs = [pl.BlockSpec((blk, d), row)]
    out_shape = [jax.ShapeDtypeStruct((n_tok, d), F32)]
    args = [dest_flat, xn, route, modl, ys]
    if next_odd is not None:
        nmod, g, w_bf = next_odd
        w = w_bf.shape[1] // 3
        in_specs += [mod_spec, pl.BlockSpec((1, d), const), pl.BlockSpec(w_bf.shape, const)]
        out_specs += [pl.BlockSpec((blk, w), row)] * 3
        out_shape += [jax.ShapeDtypeStruct((n_tok, w), F32)] * 3
        args += [nmod, g, w_bf]
    ybuf = pltpu.VMEM((TOP_K, TOK_TILE, ys.shape[1]), ys.dtype)
    return pl.pallas_call(
        functools.partial(_combine_kernel, fuse_next=next_odd is not None),
        grid_spec=pltpu.PrefetchScalarGridSpec(
            num_scalar_prefetch=1,
            grid=(n_tok // blk,),
            in_specs=in_specs, out_specs=out_specs,
            scratch_shapes=[ybuf, ybuf, pltpu.SemaphoreType.DMA((2,))]),
        out_shape=out_shape,
        compiler_params=_cparams("arbitrary"),
        name="moe_combine",
    )(*args)


def _moe(xn, tok, route, totals, modl, w1, w3, w2, layer, dims, next_odd):
    n_tok = tok.shape[0]
    n = n_tok * TOP_K
    n_blocks = (n + N_EXPERTS * (MOE_BLOCK - 1) + MOE_BLOCK - 1) // MOE_BLOCK
    dest, counts = _slots(route, totals)
    dest_flat = dest[:, :TOP_K].reshape(-1)
    cnt = counts[0, :N_EXPERTS].astype(jnp.int32)
    start = counts[1, :N_EXPERTS].astype(jnp.int32)
    blocks_per_e = (cnt + MOE_BLOCK - 1) // MOE_BLOCK
    blk_end = (start + blocks_per_e * MOE_BLOCK) // MOE_BLOCK
    n_used = blk_end[-1:]
    blk = jnp.minimum(jnp.arange(n_blocks, dtype=jnp.int32), n_used[0] - 1)
    block_e = jnp.sum((blk[:, None] >= blk_end[None, :]).astype(jnp.int32), axis=1)
    block_e = jnp.minimum(block_e, N_EXPERTS - 1)
    xs = _dispatch(dest_flat, start + cnt, blk_end * MOE_BLOCK, tok, n_blocks * MOE_BLOCK)
    block_rows = jnp.clip(start[block_e] + cnt[block_e] - blk * MOE_BLOCK, 0, MOE_BLOCK)
    ys = _experts(block_e, n_used, block_rows, xs, w1, w3, w2, layer)
    return _combine(dest_flat, xn, route, modl, ys, dims, next_odd)


def _rope_tables(seq):
    rows = seq // GRID_W
    row = jnp.repeat(jnp.arange(rows), GRID_W).astype(F32)
    col = jnp.tile(jnp.arange(GRID_W), rows).astype(F32)
    inv_freq = ROPE_BASE ** (-jnp.arange(ROPE_PAIRS, dtype=F32) / ROPE_PAIRS)
    ar, ac = row[:, None] * inv_freq, col[:, None] * inv_freq
    cos = jnp.concatenate([jnp.cos(ar), jnp.cos(ar), jnp.cos(ac), jnp.cos(ac)], axis=-1)
    sin = jnp.concatenate([-jnp.sin(ar), jnp.sin(ar), -jnp.sin(ac), jnp.sin(ac)], axis=-1)
    reps = LANES // HEAD_DIM
    cos, sin = jnp.tile(cos, (1, reps)), jnp.tile(sin, (1, reps))
    cos = jnp.concatenate([cos, jnp.ones((TOK_TILE, LANES), F32)], axis=0)
    sin = jnp.concatenate([sin, jnp.zeros((TOK_TILE, LANES), F32)], axis=0)
    return cos, sin


def _head_mean_matrix():
    r = jnp.arange(LANES)
    return jnp.where((r[:, None] // HEAD_DIM) == (r[None, :] // HEAD_DIM), 1.0 / HEAD_DIM, 0.0).astype(BF16)


def _block_diag_chunks(w, chunk):
    dirs, nblk, bw, _ = w.shape
    per = chunk // bw
    w = w.reshape(dirs, nblk // per, per, bw, bw)
    eye = jnp.eye(per, dtype=w.dtype)
    out = jnp.einsum("dcpij,pq->dcpiqj", w, eye)
    return out.reshape(dirs, nblk // per, chunk, chunk).astype(BF16)


def _router_matrix(gw, gb, ew, eb):
    d = gw.shape[0]
    pad = LANES - N_EXPERTS - N_GROUPS
    wr = jnp.concatenate([ew, gw, jnp.zeros((d, pad), F32)], axis=1)
    br = jnp.concatenate([eb, gb, jnp.zeros((pad,), F32)])[None, :]
    hi = wr.astype(BF16)
    lo = (wr - hi.astype(F32)).astype(BF16)
    return jnp.concatenate([hi, lo], axis=1), br


def kernel(x, c, ctx, c_ctx, ada_w, ada_b, norm_mix, norm_ffn, out_norm, w_out, w_in_ab, q_norm, k_norm, attn_sink, sconv_w, sconv_b, w_in_cd, pool_w, pool_scale, lru_conv_w, lru_conv_b, lru_wa, lru_ba, lru_wx, lru_bx, lru_lambda, router_gw, router_gb, router_ew, router_eb, exp_w1, exp_w3, exp_w2):
    n_batch, seq, d = x.shape
    ctx_len = ctx.shape[1]
    depth = ada_w.shape[0]
    assert ctx_len == SEQ_TILE and seq % TOK_TILE == 0 and seq % GRID_W == 0
    assert seq % MERGE_TILE == 0 and (n_batch * ctx_len) % MERGE_TILE == 0 and MERGE_TILE % SEQ_TILE == 0
    assert depth == 2, "context-side odd-layer outputs are not implemented"
    dims = (n_batch, seq)
    n_lat = n_batch * seq

    cc = jnp.concatenate([c, c_ctx[None, :], jnp.zeros((SUBLANES - n_batch - 1, d), F32)], axis=0)
    mods = _ada(cc, ada_w, ada_b)
    mods = mods[:, :n_batch + 1].reshape(depth, n_batch + 1, N_MOD, d)
    mods = jnp.pad(mods, ((0, 0), (0, 0), (0, SUBLANES - N_MOD), (0, 0)))

    xparts = (x.reshape(n_lat, d), ctx.reshape(n_batch * ctx_len, d))
    cos_t, sin_t = _rope_tables(seq)
    pm = _head_mean_matrix()

    projected = None
    for i in range(depth):
        need_ctx = i < depth - 1
        j = i // 2
        modl = mods[i]
        n_out = n_lat + n_batch * ctx_len if need_ctx else n_lat
        wr, br = _router_matrix(router_gw[i], router_gb[i], router_ew[i], router_eb[i])
        wo_bf = w_out[i].astype(BF16)
        nm, nf, on = norm_mix[i][None, :], norm_ffn[i][None, :], out_norm[i][None, :]
        if i % 2 == 0:
            reps = LANES // HEAD_DIM
            q, kvx, gb, u = _inproj_even(xparts, modl, nm, w_in_ab[j].astype(BF16),
                                          jnp.tile(q_norm[j], reps)[None, :], jnp.tile(k_norm[j], reps)[None, :],
                                          cos_t, sin_t, pm, dims)
            att = _attention(q, kvx, attn_sink[j], dims, need_ctx)
            xn, tok, route, totals = _merge_even(att, gb, u, xparts, modl, sconv_w[j], sconv_b[j][None, :], on, wo_bf,
                                         nf, wr, br, dims, n_out)
        else:
            if projected is None:
                projected = _inproj_odd(xparts, modl, nm, w_in_cd[j].astype(BF16), dims)
            ux, gg, up = projected
            chunk = 2 * LANES
            hf, hr = _lru(ux, lru_conv_w[j], lru_conv_b[j][None, :],
                          _block_diag_chunks(0.5 * lru_wa[j], chunk), _block_diag_chunks(0.5 * lru_wx[j], chunk),
                          0.5 * lru_ba[j][:, None, :], 0.5 * lru_bx[j][:, None, :],
                          lru_lambda[j][:, None, :], dims)
            xn, tok, route, totals = _merge_odd(up, hf, hr, gg, xparts[0], modl, pool_w[j].astype(BF16),
                                        pool_scale[j][None, :], on, wo_bf, nf, wr, br, dims)
        next_odd = None
        if need_ctx and (i + 1) % 2 == 1:
            next_odd = (mods[i + 1], norm_mix[i + 1][None, :], w_in_cd[(i + 1) // 2].astype(BF16))
        outs = _moe(xn, tok, route, totals, modl, exp_w1, exp_w3, exp_w2, i, dims, next_odd)
        xparts = (outs[0],)
        projected = tuple(outs[1:]) if next_odd is not None else None
    return xparts[0][:n_lat].reshape(n_batch, seq, d)
```

```python
import functools

import jax
import jax.numpy as jnp
from jax import lax
from jax.experimental import pallas as pl
from jax.experimental.pallas import tpu as pltpu

F32 = jnp.float32
BF16 = jnp.bfloat16
HIGHEST = lax.Precision.HIGHEST
LOG2E = 1.4426950408889634

GRID_W = 64
EPS = 1e-6
HEAD_DIM = 64
N_Q_HEADS = 8
N_KV_HEADS = 2
Q_PER_KV = N_Q_HEADS // N_KV_HEADS
WINDOW = 128
ATT_BLOCK = 128
ROPE_PAIRS = HEAD_DIM // 4
ROPE_BASE = 10000.0
POOL_WINDOWS = (2, 4, 8, 16)
LRU_C = 8.0
N_GROUPS = 4
EXPERTS_PER_GROUP = 8
N_EXPERTS = N_GROUPS * EXPERTS_PER_GROUP
TOP_K = 2
MOE_BLOCK = 512
ZERO_CHUNK = 64
N_MOD = 6

SUBLANES = 8
LANES = 128
SEQ_TILE = 256
TOK_TILE = 512
SLOT_TILE = 1024
MERGE_TILE = 512
ATT_TILE = 256
LRU_TILE = 512
NEG = -1e30
VMEM_LIMIT = 48 * 1024 * 1024


def _cparams(*sem):
    return pltpu.CompilerParams(dimension_semantics=sem, vmem_limit_bytes=VMEM_LIMIT)


def _rms(x, g):
    ms = jnp.mean(x * x, axis=-1, keepdims=True)
    return x * lax.rsqrt(ms + EPS) * g


def _modulate(x, g, shift, scale):
    return _rms(x, g) * (1.0 + scale) + shift


def _bdot(a, b):
    return jnp.dot(a.astype(BF16), b, preferred_element_type=F32)


def _silu(x):
    return x * jax.nn.sigmoid(x)


def _pack_bf16_pair(a, b):
    ha = lax.bitcast_convert_type(a.astype(BF16).astype(F32), jnp.uint32)
    hb = lax.bitcast_convert_type(b.astype(BF16).astype(F32), jnp.uint32)
    return ha | (hb >> 16)


def _unpack_bf16_pair(w):
    a = lax.bitcast_convert_type(w & jnp.uint32(0xFFFF0000), F32)
    b = lax.bitcast_convert_type(w << 16, F32)
    return a, b


def _ada_kernel(c_ref, w_ref, b_ref, o_ref):
    s = _silu(c_ref[...])
    o_ref[0] = jnp.dot(s, w_ref[0], precision=HIGHEST, preferred_element_type=F32) + b_ref[0]


def _ada(cc, ada_w, ada_b):
    depth, d, n = ada_w.shape
    tn = n // 2
    return pl.pallas_call(
        _ada_kernel,
        grid=(depth, n // tn),
        in_specs=[pl.BlockSpec((SUBLANES, d), lambda l, j: (0, 0)),
                  pl.BlockSpec((1, d, tn), lambda l, j: (l, 0, j)),
                  pl.BlockSpec((1, 1, tn), lambda l, j: (l, 0, j))],
        out_specs=pl.BlockSpec((1, SUBLANES, tn), lambda l, j: (l, 0, j)),
        out_shape=jax.ShapeDtypeStruct((depth, SUBLANES, n), F32),
        compiler_params=_cparams("arbitrary", "arbitrary"),
        name="ada",
    )(cc, ada_w, ada_b.reshape(depth, 1, n))


def _head_norm_rope(t, pm, g, cos, sin):
    m = t.shape[0]
    t2 = t * t
    hi = t2.astype(BF16)
    lo = (t2 - hi.astype(F32)).astype(BF16)
    ms2 = jnp.dot(jnp.concatenate([hi, lo], axis=0), pm, preferred_element_type=F32)
    ms = ms2[0:m] + ms2[m:2 * m]
    t = t * lax.rsqrt(ms + EPS) * g
    lane = lax.broadcasted_iota(jnp.int32, t.shape, 1)
    first_half = (lane % (2 * ROPE_PAIRS)) < ROPE_PAIRS
    partner = jnp.where(first_half, pltpu.roll(t, LANES - ROPE_PAIRS, 1), pltpu.roll(t, ROPE_PAIRS, 1))
    return t * cos + partner * sin


def _token_operands(xparts, n_lat_tiles, tile):
    d = xparts[0].shape[1]
    lat = pl.BlockSpec((tile, d), lambda j: (jnp.minimum(j, n_lat_tiles - 1), 0))
    if len(xparts) == 2:
        ctx = pl.BlockSpec((tile, d), lambda j: (jnp.maximum(j - n_lat_tiles, 0), 0))
        return list(xparts), [lat, ctx]
    ctx = pl.BlockSpec((tile, d), lambda j: (jnp.maximum(j, n_lat_tiles), 0))
    return [xparts[0], xparts[0]], [lat, ctx]


def _pick_tokens(xl_ref, xc_ref, n_lat_tiles):
    return jnp.where(pl.program_id(0) >= n_lat_tiles, xc_ref[...], xl_ref[...])


def _inproj_even_kernel(xl_ref, xc_ref, mod_ref, g_ref, w_ref, qg_ref, kg_ref, cos_ref, sin_ref, pm_ref,
                        q_ref, kvx_ref, gb_ref, u_ref, *, n_lat_tiles):
    mod = mod_ref[0]
    x = _pick_tokens(xl_ref, xc_ref, n_lat_tiles)
    h = _modulate(x, g_ref[...], mod[0:1], mod[1:2]).astype(BF16)
    cos, sin, pm = cos_ref[...], sin_ref[...], pm_ref[...]
    kv_w = N_KV_HEADS * HEAD_DIM
    att_w = N_Q_HEADS * HEAD_DIM
    sc_w = gb_ref.shape[1]
    kv = jnp.dot(h, w_ref[:, 0:2 * kv_w], preferred_element_type=F32)
    k = _head_norm_rope(kv[:, :kv_w], pm, kg_ref[...], cos, sin)
    v = kv[:, kv_w:]
    kvx_ref[...] = jnp.concatenate([k, pltpu.roll(k, HEAD_DIM, 1), v, pltpu.roll(v, HEAD_DIM, 1)],
                                   axis=-1).astype(BF16)
    q = jnp.dot(h, w_ref[:, 2 * kv_w:2 * kv_w + att_w], preferred_element_type=F32)
    qscale = HEAD_DIM ** -0.5 * LOG2E
    for c in range(att_w // LANES):
        qc = _head_norm_rope(q[:, c * LANES:(c + 1) * LANES], pm, qg_ref[...], cos, sin)
        q_ref[:, c * LANES:(c + 1) * LANES] = (qc * qscale).astype(BF16)
    o = 2 * kv_w + att_w
    gb_ref[...] = jnp.dot(h, w_ref[:, o:o + sc_w], preferred_element_type=F32)
    gc = jnp.dot(h, w_ref[:, o + sc_w:o + 2 * sc_w], preferred_element_type=F32)
    gx = jnp.dot(h, w_ref[:, o + 2 * sc_w:o + 3 * sc_w], preferred_element_type=F32)
    u_ref[...] = gc * gx


def _tile_maps(n_lat_tiles, tiles_per_seq, n_batch):
    def mod_map(j):
        return (jnp.minimum(j // tiles_per_seq, n_batch), 0, 0)

    def pos_map(j):
        return (jnp.where(j < n_lat_tiles, j % tiles_per_seq, tiles_per_seq), 0)

    return mod_map, pos_map


def _inproj_even(xparts, modl, g, w_bf, q_g, k_g, cos_t, sin_t, pm, dims):
    n_tok = sum(p.shape[0] for p in xparts)
    d = xparts[0].shape[1]
    n_batch, seq = dims
    tile = TOK_TILE
    tps = seq // tile
    mod_map, pos_map = _tile_maps(n_batch * tps, tps, n_batch)
    xs, xspecs = _token_operands(xparts, n_batch * tps, tile)
    kv_w, att_w = N_KV_HEADS * HEAD_DIM, N_Q_HEADS * HEAD_DIM
    sc_w = (w_bf.shape[1] - 2 * kv_w - att_w) // 3
    row = lambda j: (j, 0)
    const = lambda j: (0, 0)
    return pl.pallas_call(
        functools.partial(_inproj_even_kernel, n_lat_tiles=n_batch * tps),
        grid=(n_tok // tile,),
        in_specs=xspecs + [
                  pl.BlockSpec((1, SUBLANES, d), mod_map),
                  pl.BlockSpec((1, d), const),
                  pl.BlockSpec(w_bf.shape, const),
                  pl.BlockSpec((1, LANES), const),
                  pl.BlockSpec((1, LANES), const),
                  pl.BlockSpec((tile, LANES), pos_map),
                  pl.BlockSpec((tile, LANES), pos_map),
                  pl.BlockSpec((LANES, LANES), const)],
        out_specs=[pl.BlockSpec((tile, att_w), row),
                   pl.BlockSpec((tile, 4 * kv_w), row),
                   pl.BlockSpec((tile, sc_w), row),
                   pl.BlockSpec((tile, sc_w), row)],
        out_shape=[jax.ShapeDtypeStruct((n_tok, att_w), BF16),
                   jax.ShapeDtypeStruct((n_tok, 4 * kv_w), BF16),
                   jax.ShapeDtypeStruct((n_tok, sc_w), F32),
                   jax.ShapeDtypeStruct((n_tok, sc_w), F32)],
        compiler_params=_cparams("arbitrary"),
        name="inproj_even",
    )(*xs, modl, g, w_bf, q_g, k_g, cos_t, sin_t, pm)


def _attn_heads(q, kvx, mask_fn, sink_ref, o_ref):
    m, n = q.shape[0], kvx.shape[0]
    k_nat, k_swp = kvx[:, 0:LANES], kvx[:, LANES:2 * LANES]
    v_nat, v_swp = kvx[:, 2 * LANES:3 * LANES], kvx[:, 3 * LANES:4 * LANES]
    lo_n = lax.broadcasted_iota(jnp.int32, (n, LANES), 1) < HEAD_DIM
    lo_m = lax.broadcasted_iota(jnp.int32, (2 * m, LANES), 1) < HEAD_DIM
    top = lax.broadcasted_iota(jnp.int32, (2 * m, 1), 0) < m
    zero = jnp.zeros_like(k_nat)
    r2 = lax.broadcasted_iota(jnp.int32, (2 * n, LANES), 0) < n
    l2 = lax.broadcasted_iota(jnp.int32, (2 * n, LANES), 1) < HEAD_DIM
    ones_both = jnp.where(r2 == l2, 1.0, 0.0).astype(BF16)
    outs = []
    for h in range(N_KV_HEADS):
        k_lo, k_hi = (k_nat, k_swp) if h == 0 else (k_swp, k_nat)
        v_lo, v_hi = (v_nat, v_swp) if h == 0 else (v_swp, v_nat)
        g0 = 2 * h
        q2 = jnp.concatenate([q[:, g0 * LANES:(g0 + 1) * LANES],
                              q[:, (g0 + 1) * LANES:(g0 + 2) * LANES]], axis=0)
        k_both = jnp.concatenate([jnp.where(lo_n, k_lo, zero), jnp.where(lo_n, zero, k_hi)], axis=0)
        v_both = jnp.concatenate([jnp.where(lo_n, v_lo, zero), jnp.where(lo_n, zero, v_hi)], axis=0)
        s = lax.dot_general(q2, k_both, (((1,), (1,)), ((), ())), preferred_element_type=F32)
        hq = Q_PER_KV * h
        halves = []
        for par in range(2):
            sp = mask_fn(s[:, par * n:(par + 1) * n])
            snk = jnp.where(top, sink_ref[hq + par], sink_ref[hq + 2 + par]) * LOG2E
            mx = jnp.maximum(jnp.max(sp, axis=-1, keepdims=True), snk)
            halves.append((jnp.exp2(sp - mx), jnp.exp2(snk - mx)))
        (p_lo, e_lo), (p_hi, e_hi) = halves
        p = jnp.concatenate([p_lo, p_hi], axis=1).astype(BF16)
        nd = jnp.dot(p, jnp.concatenate([v_both, ones_both], axis=1), preferred_element_type=F32)
        o = nd[:, :LANES] / (nd[:, LANES:] + jnp.where(lo_m, e_lo, e_hi))
        outs += [o[0:m], o[m:2 * m]]
    o_ref[...] = jnp.concatenate(outs, axis=-1)


def _attn_lat_kernel(sink_ref, q_ref, kp_ref, kc_ref, kn_ref, kx_ref, o_ref):
    i = pl.program_id(1)
    nb = pl.num_programs(1)
    kvx = jnp.concatenate([kp_ref[...], kc_ref[...], kn_ref[...], kx_ref[...]], axis=0)
    m, w = ATT_TILE, WINDOW
    tok = lax.broadcasted_iota(jnp.int32, (2 * m, m), 0) % m
    col = lax.broadcasted_iota(jnp.int32, (2 * m, m), 1)
    tok_w = lax.broadcasted_iota(jnp.int32, (2 * m, w), 0) % m
    col_w = lax.broadcasted_iota(jnp.int32, (2 * m, w), 1)
    ok_prev = (col_w >= tok_w) & (i > 0)
    ok_cur = (col - tok <= w) & (tok - col <= w)
    ok_next = (col_w + (m - w) <= tok_w) & (i < nb - 1)

    def mask_fn(s):
        return jnp.concatenate([jnp.where(ok_prev, s[:, 0:w], NEG), jnp.where(ok_cur, s[:, w:w + m], NEG),
                                jnp.where(ok_next, s[:, w + m:2 * w + m], NEG), s[:, 2 * w + m:]], axis=1)

    _attn_heads(q_ref[...], kvx, mask_fn, sink_ref, o_ref)


def _attn_ctx_kernel(sink_ref, att_in_ref, q_ref, kx_ref, o_ref):
    del att_in_ref
    _attn_heads(q_ref[...], kx_ref[...], lambda s: s, sink_ref, o_ref)


def _attention(q, kvx, sink, dims, need_ctx):
    n_tok = q.shape[0]
    n_batch, seq = dims
    nb = seq // ATT_TILE
    r = ATT_TILE // WINDOW
    nw = seq // WINDOW
    n_lat = n_batch * seq
    ctx_len = (n_tok - n_lat) // n_batch
    att_w, kvx_w = q.shape[1], kvx.shape[1]
    ctx_blk0 = n_lat // ctx_len
    cur = lambda b, i: (b * nb + i, 0)
    prev = lambda b, i: (b * nw + jnp.maximum(i * r - 1, 0), 0)
    nxt = lambda b, i: (b * nw + jnp.minimum((i + 1) * r, nw - 1), 0)
    cx = lambda b, i: (ctx_blk0 + b, 0)
    smem = pl.BlockSpec(memory_space=pltpu.SMEM)
    att = pl.pallas_call(
        _attn_lat_kernel,
        grid=(n_batch, nb),
        in_specs=[smem, pl.BlockSpec((ATT_TILE, att_w), cur),
                  pl.BlockSpec((WINDOW, kvx_w), prev), pl.BlockSpec((ATT_TILE, kvx_w), cur),
                  pl.BlockSpec((WINDOW, kvx_w), nxt), pl.BlockSpec((ctx_len, kvx_w), cx)],
        out_specs=pl.BlockSpec((ATT_TILE, att_w), cur),
        out_shape=jax.ShapeDtypeStruct((n_tok, att_w), F32),
        compiler_params=_cparams("arbitrary", "arbitrary"),
        name="attn_latent",
    )(sink, q, kvx, kvx, kvx, kvx)
    if not need_ctx:
        return att
    cxb = lambda b: (ctx_blk0 + b, 0)
    return pl.pallas_call(
        _attn_ctx_kernel,
        grid=(n_batch,),
        in_specs=[smem, pl.BlockSpec(memory_space=pl.ANY),
                  pl.BlockSpec((ctx_len, att_w), cxb), pl.BlockSpec((ctx_len, kvx_w), cxb)],
        out_specs=pl.BlockSpec((ctx_len, att_w), cxb),
        out_shape=jax.ShapeDtypeStruct((n_tok, att_w), F32),
        input_output_aliases={1: 0},
        compiler_params=_cparams("arbitrary"),
        name="attn_context",
    )(sink, att, q, kvx)


def _route(logits):
    lane = lax.broadcasted_iota(jnp.int32, logits.shape, 1).astype(F32)
    big = float(LANES)
    gl = jnp.where((lane >= N_EXPERTS) & (lane < N_EXPERTS + N_GROUPS), logits, NEG)
    gmax = jnp.max(gl, axis=-1, keepdims=True)
    gidx = jnp.min(jnp.where(gl == gmax, lane, big), axis=-1, keepdims=True) - N_EXPERTS
    p_sel = 1.0 / jnp.sum(jnp.exp(gl - gmax), axis=-1, keepdims=True)
    lo = gidx * EXPERTS_PER_GROUP
    el = jnp.where((lane >= lo) & (lane < lo + EXPERTS_PER_GROUP), logits, NEG)
    m1 = jnp.max(el, axis=-1, keepdims=True)
    i1 = jnp.min(jnp.where(el == m1, lane, big), axis=-1, keepdims=True)
    el2 = jnp.where(lane == i1, NEG, el)
    m2 = jnp.max(el2, axis=-1, keepdims=True)
    i2 = jnp.min(jnp.where(el2 == m2, lane, big), axis=-1, keepdims=True)
    t = jnp.exp(m2 - m1)
    w1 = p_sel / (1.0 + t)
    w2 = p_sel * t / (1.0 + t)
    return jnp.where(lane == 0, i1, jnp.where(lane == 1, i2, jnp.where(lane == 2, w1,
                     jnp.where(lane == 3, w2, 0.0))))


def _tail(n1, n2, gain, wo_ref, x, mod, nf_ref, wr_ref, br_ref, xo_ref, tok_ref, route_ref, cnt_ref):
    half = n1.shape[1]
    y = (_bdot(_rms(n1, gain[:, :half]), wo_ref[0:half, :])
         + _bdot(_rms(n2, gain[:, half:]), wo_ref[half:, :]))
    xn = x + mod[2:3] * y
    xo_ref[...] = xn
    tok = _modulate(xn, nf_ref[...], mod[3:4], mod[4:5])
    half_d = tok.shape[1] // 2
    tok_ref[...] = _pack_bf16_pair(tok[:, :half_d], tok[:, half_d:])
    t_hi = tok.astype(BF16)
    t_lo = (tok - t_hi.astype(F32)).astype(BF16)
    hh = jnp.dot(t_hi, wr_ref[...], preferred_element_type=F32)
    logits = (hh[:, :LANES] + hh[:, LANES:] + br_ref[...]
              + jnp.dot(t_lo, wr_ref[:, 0:LANES], preferred_element_type=F32))
    route = _route(logits)
    route_ref[...] = route
    lane = lax.broadcasted_iota(jnp.int32, route.shape, 1).astype(F32)
    hit = jnp.where((lane == route[:, 0:1]) | (lane == route[:, 1:2]), 1.0, 0.0)

    @pl.when(pl.program_id(0) == 0)
    def _():
        cnt_ref[...] = jnp.zeros_like(cnt_ref)

    cnt_ref[0:1, :] += jnp.sum(hit, axis=0, keepdims=True)


def _seq_flags(j, n_lat_tiles, tiles_per_seq):
    is_ctx = j >= n_lat_tiles
    first = is_ctx | (j % tiles_per_seq == 0)
    last = is_ctx | (j % tiles_per_seq == tiles_per_seq - 1)
    return first, last


def _with_halo(prev_ref, tile, next_ref, first, last):
    p = jnp.where(first, 0.0, prev_ref[...])
    n = jnp.where(last, 0.0, next_ref[...])
    return jnp.concatenate([p, tile, n], axis=0)


def _shift_rows(e, k):
    return pltpu.roll(e, (-k) % e.shape[0], 0)


def _merge_even_kernel(att_ref, gb_ref, u_ref, up_ref, un_ref, xl_ref, xc_ref, mod_ref, cw_ref, cb_ref,
                       on_ref, wo_ref, nf_ref, wr_ref, br_ref,
                       xo_ref, tok_ref, route_ref, cnt_ref, *, n_lat_tiles, tiles_per_seq):
    j = pl.program_id(0)
    first, last = _seq_flags(j, n_lat_tiles, tiles_per_seq)
    e = _with_halo(up_ref, u_ref[...], un_ref, first, last)
    cw = cw_ref[...]
    lo, hi = SUBLANES, SUBLANES + MERGE_TILE
    pos = lax.broadcasted_iota(jnp.int32, (MERGE_TILE, 1), 0) % SEQ_TILE
    is_ctx = j >= n_lat_tiles
    before = jnp.where(is_ctx & (pos == 0), 0.0, _shift_rows(e, -1)[lo:hi])
    after = jnp.where(is_ctx & (pos == SEQ_TILE - 1), 0.0, _shift_rows(e, 1)[lo:hi])
    conv = cw[0:1] * before + cw[1:2] * e[lo:hi] + cw[2:3] * after + cb_ref[...]
    conv = gb_ref[...] * conv
    x = _pick_tokens(xl_ref, xc_ref, n_lat_tiles)
    _tail(att_ref[...], conv, on_ref[...], wo_ref, x, mod_ref[0], nf_ref, wr_ref, br_ref,
          xo_ref, tok_ref, route_ref, cnt_ref)


def _halo_specs(width, n_rows):
    r = MERGE_TILE // SUBLANES
    n_blk = n_rows // SUBLANES
    prev = pl.BlockSpec((SUBLANES, width), lambda j: (jnp.maximum(j * r - 1, 0), 0))
    nxt = pl.BlockSpec((SUBLANES, width), lambda j: (jnp.minimum((j + 1) * r, n_blk - 1), 0))
    return prev, nxt


def _tail_specs(d, n_out_tok):
    row = lambda j: (j, 0)
    out_specs = [pl.BlockSpec((MERGE_TILE, d), row), pl.BlockSpec((MERGE_TILE, d // 2), row),
                 pl.BlockSpec((MERGE_TILE, LANES), row), pl.BlockSpec((SUBLANES, LANES), lambda j: (0, 0))]
    out_shape = [jax.ShapeDtypeStruct((n_out_tok, d), F32), jax.ShapeDtypeStruct((n_out_tok, d // 2), jnp.uint32),
                 jax.ShapeDtypeStruct((n_out_tok, LANES), F32), jax.ShapeDtypeStruct((SUBLANES, LANES), F32)]
    return out_specs, out_shape


def _merge_even(att, gb, u, xparts, modl, cw, cb, on, wo_bf, nf, wr, br, dims, n_out_tok):
    n_tok = u.shape[0]
    d = xparts[0].shape[1]
    n_batch, seq = dims
    tps = seq // MERGE_TILE
    mod_map, _ = _tile_maps(n_batch * tps, tps, n_batch)
    xs, xspecs = _token_operands(xparts, n_batch * tps, MERGE_TILE)
    sc_w = u.shape[1]
    row = lambda j: (j, 0)
    const = lambda j: (0, 0)
    prev, nxt = _halo_specs(sc_w, n_tok)
    out_specs, out_shape = _tail_specs(d, n_out_tok)
    return pl.pallas_call(
        functools.partial(_merge_even_kernel, n_lat_tiles=n_batch * tps, tiles_per_seq=tps),
        grid=(n_out_tok // MERGE_TILE,),
        in_specs=[pl.BlockSpec((MERGE_TILE, att.shape[1]), row),
                  pl.BlockSpec((MERGE_TILE, sc_w), row),
                  pl.BlockSpec((MERGE_TILE, sc_w), row), prev, nxt] + xspecs + [
                  pl.BlockSpec((1, SUBLANES, d), mod_map),
                  pl.BlockSpec(cw.shape, const), pl.BlockSpec((1, sc_w), const),
                  pl.BlockSpec((1, d), const), pl.BlockSpec(wo_bf.shape, const),
                  pl.BlockSpec((1, d), const), pl.BlockSpec(wr.shape, const),
                  pl.BlockSpec((1, LANES), const)],
        out_specs=out_specs, out_shape=out_shape,
        compiler_params=_cparams("arbitrary"),
        name="merge_even",
    )(att, gb, u, u, u, *xs, modl, cw, cb, on, wo_bf, nf, wr, br)


def _odd_projection(x, mod, g_ref, w_ref, ux_ref, gg_ref, up_ref):
    h = _modulate(x, g_ref[...], mod[0:1], mod[1:2]).astype(BF16)
    w = ux_ref.shape[1]
    ux_ref[...] = jnp.dot(h, w_ref[:, 0:w], preferred_element_type=F32)
    ug = jnp.dot(h, w_ref[:, w:2 * w], preferred_element_type=F32)
    gg_ref[...] = jax.nn.gelu(ug, approximate=True)
    up_ref[...] = jnp.dot(h, w_ref[:, 2 * w:3 * w], preferred_element_type=F32)


def _inproj_odd_kernel(xl_ref, xc_ref, mod_ref, g_ref, w_ref, ux_ref, gg_ref, up_ref, *, n_lat_tiles):
    x = _pick_tokens(xl_ref, xc_ref, n_lat_tiles)
    _odd_projection(x, mod_ref[0], g_ref, w_ref, ux_ref, gg_ref, up_ref)


def _inproj_odd(xparts, modl, g, w_bf, dims):
    n_tok = sum(p.shape[0] for p in xparts)
    d = xparts[0].shape[1]
    n_batch, seq = dims
    tile = TOK_TILE
    tps = seq // tile
    mod_map, _ = _tile_maps(n_batch * tps, tps, n_batch)
    xs, xspecs = _token_operands(xparts, n_batch * tps, tile)
    w = w_bf.shape[1] // 3
    row = lambda j: (j, 0)
    const = lambda j: (0, 0)
    return pl.pallas_call(
        functools.partial(_inproj_odd_kernel, n_lat_tiles=n_batch * tps),
        grid=(n_tok // tile,),
        in_specs=xspecs + [
                  pl.BlockSpec((1, SUBLANES, d), mod_map),
                  pl.BlockSpec((1, d), const),
                  pl.BlockSpec(w_bf.shape, const)],
        out_specs=[pl.BlockSpec((tile, w), row)] * 3,
        out_shape=[jax.ShapeDtypeStruct((n_tok, w), F32)] * 3,
        compiler_params=_cparams("arbitrary"),
        name="inproj_odd",
    )(*xs, modl, g, w_bf)


def _lru_conv(e, cw, cb):
    lo, hi = SUBLANES, e.shape[0] - SUBLANES
    return (cw[0:1] * _shift_rows(e, -2)[lo:hi] + cw[1:2] * _shift_rows(e, -1)[lo:hi]
            + cw[2:3] * e[lo:hi] + cw[3:4] * _shift_rows(e, 1)[lo:hi] + cb)


def _lru_coeffs(xl, d, wa_ref, wx_ref, ba_ref, bx_ref, lam_ref, a_ref, b_ref):
    xb = xl.astype(BF16)
    chunk = wa_ref.shape[-1]
    n_chunk = xl.shape[1] // chunk
    pre_a = jnp.concatenate([jnp.dot(xb[:, c * chunk:(c + 1) * chunk], wa_ref[d, c],
                                     preferred_element_type=F32) for c in range(n_chunk)], axis=-1)
    pre_x = jnp.concatenate([jnp.dot(xb[:, c * chunk:(c + 1) * chunk], wx_ref[d, c],
                                     preferred_element_type=F32) for c in range(n_chunk)], axis=-1)
    tr = jnp.tanh(pre_a + ba_ref[d])
    tg = jnp.tanh(pre_x + bx_ref[d])
    c_half = (-0.5 * LRU_C) * jax.nn.softplus(-lam_ref[d])
    log_a = c_half * tr + c_half
    a_ref[...] = jnp.exp(log_a)
    th = jnp.tanh(log_a)
    b_ref[...] = jnp.sqrt(-0.5 * th / (1.0 - th)) * (1.0 + tg) * xl


def _scan_tile(a_ref, b_ref, h_ref, carry, reverse):
    n_grp = a_ref.shape[0] // SUBLANES
    row = lax.broadcasted_iota(jnp.int32, (SUBLANES, a_ref.shape[1]), 0)

    def body(g, carry):
        gi = (n_grp - 1 - g) if reverse else g
        r0 = pl.multiple_of(gi * SUBLANES, SUBLANES)
        a8 = a_ref[pl.ds(r0, SUBLANES), :]
        b8 = b_ref[pl.ds(r0, SUBLANES), :]
        for s in (1, 2, 4):
            if reverse:
                ok = row < SUBLANES - s
                sh = SUBLANES - s
            else:
                ok = row >= s
                sh = s
            a_sh = jnp.where(ok, pltpu.roll(a8, sh, 0), 1.0)
            b_sh = jnp.where(ok, pltpu.roll(b8, sh, 0), 0.0)
            b8 = a8 * b_sh + b8
            a8 = a8 * a_sh
        h8 = a8 * carry + b8
        if h_ref is not None:
            h_ref[pl.ds(r0, SUBLANES), :] = h8
        return h8[0:1] if reverse else h8[SUBLANES - 1:SUBLANES]

    return lax.fori_loop(0, n_grp, body, carry, unroll=4)


def _lru_kernel(uf_ref, ufp_ref, ufn_ref, ur_ref, urp_ref, urn_ref, uc_ref, cw_ref, cb_ref,
                wa_ref, wx_ref, ba_ref, bx_ref, lam_ref, hf_ref, hr_ref,
                cf_ref, cr_ref, a_ref, b_ref):
    i = pl.program_id(1)
    nt = pl.num_programs(1)
    cw, cb = cw_ref[...], cb_ref[...]
    coeffs = functools.partial(_lru_coeffs, wa_ref=wa_ref, wx_ref=wx_ref, ba_ref=ba_ref,
                               bx_ref=bx_ref, lam_ref=lam_ref, a_ref=a_ref, b_ref=b_ref)

    @pl.when(i == 0)
    def _():
        n_ctx = uc_ref.shape[0]
        ac_ref, bc_ref = a_ref.at[pl.ds(0, n_ctx)], b_ref.at[pl.ds(0, n_ctx)]
        z = jnp.zeros((SUBLANES, uc_ref.shape[1]), F32)
        xc = _lru_conv(jnp.concatenate([z, uc_ref[...], z], axis=0), cw, cb)
        zero = jnp.zeros((1, uc_ref.shape[1]), F32)
        coeffs(xc, 0, a_ref=ac_ref, b_ref=bc_ref)
        cf_ref[...] = _scan_tile(ac_ref, bc_ref, None, zero, False)
        coeffs(xc, 1, a_ref=ac_ref, b_ref=bc_ref)
        cr_ref[...] = _scan_tile(ac_ref, bc_ref, None, zero, True)

    xf = _lru_conv(_with_halo(ufp_ref, uf_ref[...], ufn_ref, i == 0, i == nt - 1), cw, cb)
    coeffs(xf, 0)
    cf_ref[...] = _scan_tile(a_ref, b_ref, hf_ref, cf_ref[...], False)
    xr = _lru_conv(_with_halo(urp_ref, ur_ref[...], urn_ref, i == nt - 1, i == 0), cw, cb)
    coeffs(xr, 1)
    cr_ref[...] = _scan_tile(a_ref, b_ref, hr_ref, cr_ref[...], True)


def _lru(ux, cw, cb, wa_bd, wx_bd, ba, bx, lam, dims):
    n_tok, w = ux.shape
    n_batch, seq = dims
    nt = seq // LRU_TILE
    n_lat = n_batch * seq
    r = LRU_TILE // SUBLANES
    n_blk = n_tok // SUBLANES
    fwd = lambda b, i: b * nt + i
    rev = lambda b, i: b * nt + nt - 1 - i

    def specs(tile):
        return [pl.BlockSpec((LRU_TILE, w), lambda b, i: (tile(b, i), 0)),
                pl.BlockSpec((SUBLANES, w), lambda b, i: (jnp.maximum(tile(b, i) * r - 1, 0), 0)),
                pl.BlockSpec((SUBLANES, w), lambda b, i: (jnp.minimum((tile(b, i) + 1) * r, n_blk - 1), 0))]

    const2 = lambda b, i: (0, 0)
    const3 = lambda b, i: (0, 0, 0)
    const4 = lambda b, i: (0, 0, 0, 0)
    return pl.pallas_call(
        _lru_kernel,
        grid=(n_batch, nt),
        in_specs=specs(fwd) + specs(rev) + [
            pl.BlockSpec((SEQ_TILE, w), lambda b, i: (n_lat // SEQ_TILE + b, 0)),
            pl.BlockSpec(cw.shape, const2), pl.BlockSpec((1, w), const2),
            pl.BlockSpec(wa_bd.shape, const4), pl.BlockSpec(wx_bd.shape, const4),
            pl.BlockSpec(ba.shape, const3), pl.BlockSpec(bx.shape, const3),
            pl.BlockSpec(lam.shape, const3)],
        out_specs=[pl.BlockSpec((LRU_TILE, w), lambda b, i: (fwd(b, i), 0)),
                   pl.BlockSpec((LRU_TILE, w), lambda b, i: (rev(b, i), 0))],
        out_shape=[jax.ShapeDtypeStruct((n_lat, w), F32)] * 2,
        scratch_shapes=[pltpu.VMEM((1, w), F32), pltpu.VMEM((1, w), F32),
                        pltpu.VMEM((LRU_TILE, w), F32), pltpu.VMEM((LRU_TILE, w), F32)],
        compiler_params=_cparams("arbitrary", "arbitrary"),
        name="rglru",
    )(ux, ux, ux, ux, ux, ux, ux, cw, cb, wa_bd, wx_bd, ba, bx, lam)


def _merge_odd_kernel(up_ref, upp_ref, upn_ref, hf_ref, hr_ref, gg_ref, x_ref, mod_ref,
                      pw_ref, ps_ref, on_ref, wo_ref, nf_ref, wr_ref, br_ref,
                      xo_ref, tok_ref, route_ref, cnt_ref, *, tiles_per_seq, seq):
    j = pl.program_id(0)
    ts = j % tiles_per_seq
    e = _with_halo(upp_ref, up_ref[...], upn_ref, ts == 0, ts == tiles_per_seq - 1)
    gw = pw_ref.shape[-1]
    lo, hi = SUBLANES, SUBLANES + MERGE_TILE
    tpos = ts * MERGE_TILE + lax.broadcasted_iota(jnp.int32, (MERGE_TILE, 1), 0)
    parts = []
    for g, win in enumerate(POOL_WINDOWS):
        eg = e[:, g * gw:(g + 1) * gw]
        p = eg
        span = 1
        while span < win:
            p = p + _shift_rows(p, span)
            span *= 2
        half = win // 2
        s = _shift_rows(p, -half)[lo:hi]
        cnt = (jnp.minimum(tpos + half, seq) - jnp.maximum(tpos - half, 0)).astype(F32)
        dlt = s / cnt - eg[lo:hi]
        parts.append(_bdot(dlt, pw_ref[g]))
    pool = jnp.concatenate(parts, axis=-1) * ps_ref[...]
    lru = (hf_ref[...] + hr_ref[...]) * gg_ref[...]
    _tail(pool, lru, on_ref[...], wo_ref, x_ref[...], mod_ref[0], nf_ref, wr_ref, br_ref,
          xo_ref, tok_ref, route_ref, cnt_ref)


def _merge_odd(up, hf, hr, gg, xf, modl, pw_bf, ps, on, wo_bf, nf, wr, br, dims):
    d = xf.shape[1]
    n_tok, w = up.shape
    n_batch, seq = dims
    tps = seq // MERGE_TILE
    n_lat = n_batch * seq
    mod_map, _ = _tile_maps(n_batch * tps, tps, n_batch)
    row = lambda j: (j, 0)
    const = lambda j: (0, 0)
    prev, nxt = _halo_specs(w, n_tok)
    out_specs, out_shape = _tail_specs(d, n_lat)
    return pl.pallas_call(
        functools.partial(_merge_odd_kernel, tiles_per_seq=tps, seq=seq),
        grid=(n_lat // MERGE_TILE,),
        in_specs=[pl.BlockSpec((MERGE_TILE, w), row), prev, nxt,
                  pl.BlockSpec((MERGE_TILE, w), row), pl.BlockSpec((MERGE_TILE, w), row),
                  pl.BlockSpec((MERGE_TILE, w), row),
                  pl.BlockSpec((MERGE_TILE, d), row),
                  pl.BlockSpec((1, SUBLANES, d), mod_map),
                  pl.BlockSpec(pw_bf.shape, lambda j: (0, 0, 0)), pl.BlockSpec((1, w), const),
                  pl.BlockSpec((1, d), const), pl.BlockSpec(wo_bf.shape, const),
                  pl.BlockSpec((1, d), const), pl.BlockSpec(wr.shape, const),
                  pl.BlockSpec((1, LANES), const)],
        out_specs=out_specs, out_shape=out_shape,
        compiler_params=_cparams("arbitrary"),
        name="merge_odd",
    )(up, up, up, hf, hr, gg, xf, modl, pw_bf, ps, on, wo_bf, nf, wr, br)


def _slots_kernel(route_ref, tot_ref, dest_ref, cnt_ref, run_ref, start_ref):
    i = pl.program_id(0)
    rt = route_ref[...]
    m = rt.shape[0]
    lane = lax.broadcasted_iota(jnp.int32, rt.shape, 1).astype(F32)
    oh1 = lane == rt[:, 0:1]
    oh2 = lane == rt[:, 1:2]
    s = jnp.where(oh1 | oh2, 1.0, 0.0)

    @pl.when(i == 0)
    def _():
        tot = tot_ref[0:1, :]
        padded = jnp.floor((tot + (MOE_BLOCK - 1)) / MOE_BLOCK) * MOE_BLOCK
        r = lax.broadcasted_iota(jnp.int32, (LANES, LANES), 0)
        c = lax.broadcasted_iota(jnp.int32, (LANES, LANES), 1)
        upper = jnp.where(r < c, 1.0, 0.0)
        p8 = jnp.broadcast_to(padded, (SUBLANES, LANES))
        start = jnp.dot(p8, upper, precision=HIGHEST, preferred_element_type=F32)[0:1]
        start_ref[...] = start
        run_ref[...] = jnp.zeros_like(run_ref)
        sub = lax.broadcasted_iota(jnp.int32, cnt_ref.shape, 0)
        cnt_ref[...] = jnp.where(sub == 0, tot, jnp.where(sub == 1, start, 0.0))

    r = lax.broadcasted_iota(jnp.int32, (m, m), 0)
    c = lax.broadcasted_iota(jnp.int32, (m, m), 1)
    lower = jnp.where(c < r, 1.0, 0.0).astype(BF16)
    prefix = jnp.dot(lower, s.astype(BF16), preferred_element_type=F32)
    base = prefix + run_ref[...] + start_ref[...]
    d1 = jnp.sum(jnp.where(oh1, base, 0.0), axis=-1, keepdims=True)
    d2 = jnp.sum(jnp.where(oh2, base, 0.0), axis=-1, keepdims=True)
    dest_ref[...] = jnp.where(lane == 0, d1, jnp.where(lane == 1, d2, 0.0)).astype(jnp.int32)
    run_ref[...] += jnp.sum(s, axis=0, keepdims=True)


def _slots(route, totals):
    n_tok = route.shape[0]
    nt = n_tok // SLOT_TILE
    return pl.pallas_call(
        _slots_kernel,
        grid=(nt,),
        in_specs=[pl.BlockSpec((SLOT_TILE, LANES), lambda i: (i, 0)),
                  pl.BlockSpec((SUBLANES, LANES), lambda i: (0, 0))],
        out_specs=[pl.BlockSpec((SLOT_TILE, LANES), lambda i: (i, 0)),
                   pl.BlockSpec((SUBLANES, LANES), lambda i: (0, 0))],
        out_shape=[jax.ShapeDtypeStruct((n_tok, LANES), jnp.int32),
                   jax.ShapeDtypeStruct((SUBLANES, LANES), F32)],
        scratch_shapes=[pltpu.VMEM((1, LANES), F32)] * 2,
        compiler_params=_cparams("arbitrary"),
        name="moe_slots",
    )(route, totals)


def _for_each_row(n_rows, fn):
    def body(g, c):
        g8 = pl.multiple_of(g * SUBLANES, SUBLANES)
        for j in range(SUBLANES):
            fn(g8, j)
        return c
    lax.fori_loop(0, n_rows // SUBLANES, body, 0)


def _dispatch_kernel(dest_ref, pad_lo_ref, pad_hi_ref, tok_ref, xs_ref, zrow, sem, zsem):
    i = pl.program_id(0)
    base = i * SLOT_TILE

    def zero_copy(r, n):
        return pltpu.make_async_copy(zrow.at[pl.ds(0, n)], xs_ref.at[pl.ds(r, n)], zsem)

    def for_each_pad_chunk(fn):
        def per_expert(e, c):
            lo, hi = pad_lo_ref[e], pad_hi_ref[e]
            a8 = (lo + SUBLANES - 1) // SUBLANES * SUBLANES
            a64 = (a8 + ZERO_CHUNK - 1) // ZERO_CHUNK * ZERO_CHUNK

            def rows(r, c):
                fn(zero_copy(r, 1))
                return c

            def eights(q, c):
                fn(zero_copy(pl.multiple_of(a8 + q * SUBLANES, SUBLANES), SUBLANES))
                return c

            def chunks(q, c):
                fn(zero_copy(pl.multiple_of(a64 + q * ZERO_CHUNK, ZERO_CHUNK), ZERO_CHUNK))
                return c

            lax.fori_loop(lo, a8, rows, 0)
            lax.fori_loop(0, (a64 - a8) // SUBLANES, eights, 0)
            lax.fori_loop(0, (hi - a64) // ZERO_CHUNK, chunks, 0)
            return c
        lax.fori_loop(0, N_EXPERTS, per_expert, 0)

    @pl.when(i == 0)
    def _():
        zrow[...] = jnp.zeros_like(zrow)
        for_each_pad_chunk(lambda cp: cp.start())

    def issue(g8, j):
        rows = tok_ref.at[pl.ds(g8, SUBLANES)]
        for k in range(TOP_K):
            d = dest_ref[(base + g8) * TOP_K + (j * TOP_K + k)]
            pltpu.make_async_copy(rows.at[pl.ds(j, 1)], xs_ref.at[pl.ds(d, 1)], sem).start(priority=k)

    _for_each_row(SLOT_TILE, issue)
    for k in range(TOP_K):
        pltpu.make_async_copy(tok_ref, xs_ref.at[pl.ds(0, SLOT_TILE)], sem).wait()

    @pl.when(i == pl.num_programs(0) - 1)
    def _():
        for_each_pad_chunk(lambda cp: cp.wait())


def _dispatch(dest_flat, pad_lo, pad_hi, tok, n_slots):
    n_tok, d = tok.shape
    return pl.pallas_call(
        _dispatch_kernel,
        grid_spec=pltpu.PrefetchScalarGridSpec(
            num_scalar_prefetch=3,
            grid=(n_tok // SLOT_TILE,),
            in_specs=[pl.BlockSpec((SLOT_TILE, d), lambda i, *_: (i, 0))],
            out_specs=pl.BlockSpec(memory_space=pl.ANY),
            scratch_shapes=[pltpu.VMEM((ZERO_CHUNK, d), tok.dtype),
                            pltpu.SemaphoreType.DMA(()), pltpu.SemaphoreType.DMA(())]),
        out_shape=jax.ShapeDtypeStruct((n_slots, d), tok.dtype),
        compiler_params=_cparams("arbitrary"),
        name="moe_dispatch",
    )(dest_flat, pad_lo, pad_hi, tok)


def _expert_kernel(be_ref, nu_ref, x_ref, w1_ref, w3_ref, w2_ref, y_ref, w1b, w3b, w2b):
    i = pl.program_id(0)

    @pl.when(i < nu_ref[0])
    def _():
        changed = (i == 0) | (be_ref[i] != be_ref[jnp.maximum(i - 1, 0)])

        @pl.when(changed)
        def _():
            w1b[...] = w1_ref[0, 0].astype(BF16)
            w3b[...] = w3_ref[0, 0].astype(BF16)
            w2b[...] = w2_ref[0, 0].astype(BF16)

        x_a, x_b = _unpack_bf16_pair(x_ref[...])
        x_a, x_b = x_a.astype(BF16), x_b.astype(BF16)
        half = x_ref.shape[1]
        a = (jnp.dot(x_a, w1b[0:half, :], preferred_element_type=F32)
             + jnp.dot(x_b, w1b[half:, :], preferred_element_type=F32))
        b = (jnp.dot(x_a, w3b[0:half, :], preferred_element_type=F32)
             + jnp.dot(x_b, w3b[half:, :], preferred_element_type=F32))
        y = _bdot(_silu(a) * b, w2b[...])
        y_ref[...] = _pack_bf16_pair(y[:, :half], y[:, half:])


def _experts(block_e, n_used, xs, w1, w3, w2, layer):
    n_slots, half_d = xs.shape
    d, de = w1.shape[-2], w1.shape[-1]
    n_blocks = n_slots // MOE_BLOCK
    blk = lambda i, be, nu: (jnp.minimum(i, nu[0] - 1), 0)
    wmap = lambda i, be, nu: (layer, be[i], 0, 0)
    return pl.pallas_call(
        _expert_kernel,
        grid_spec=pltpu.PrefetchScalarGridSpec(
            num_scalar_prefetch=2,
            grid=(n_blocks,),
            in_specs=[pl.BlockSpec((MOE_BLOCK, half_d), blk),
                      pl.BlockSpec((1, 1, d, de), wmap), pl.BlockSpec((1, 1, d, de), wmap),
                      pl.BlockSpec((1, 1, de, d), wmap)],
            out_specs=pl.BlockSpec((MOE_BLOCK, half_d), blk),
            scratch_shapes=[pltpu.VMEM((d, de), BF16), pltpu.VMEM((d, de), BF16),
                            pltpu.VMEM((de, d), BF16)]),
        out_shape=jax.ShapeDtypeStruct((n_slots, half_d), jnp.uint32),
        compiler_params=_cparams("arbitrary"),
        name="moe_experts",
    )(block_e, n_used, xs, w1, w3, w2)


def _combine_kernel(dest_ref, x_ref, route_ref, mod_ref, ys_ref, *rest, fuse_next):
    if fuse_next:
        nmod_ref, g_ref, w_ref, o_ref, ux_ref, gg_ref, up_ref, yb0, yb1, sem = rest
    else:
        o_ref, yb0, yb1, sem = rest
    i = pl.program_id(0)
    n = pl.num_programs(0)
    bufs = ((yb0, sem.at[0]), (yb1, sem.at[1]))

    def issue(tile, half):
        ybuf, s = bufs[half]
        base = tile * (TOK_TILE * TOP_K)
        for r in range(TOK_TILE):
            for k in range(TOP_K):
                d = dest_ref[base + (r * TOP_K + k)]
                pltpu.make_async_copy(ys_ref.at[pl.ds(d, 1)], ybuf.at[k, pl.ds(r, 1)], s).start(priority=k)

    def wait(half):
        ybuf, s = bufs[half]
        for k in range(TOP_K):
            pltpu.make_async_copy(ys_ref.at[pl.ds(0, TOK_TILE)], ybuf.at[k], s).wait()

    def compute(half):
        ybuf, _ = bufs[half]
        rows = pl.ds(half * TOK_TILE, TOK_TILE)
        rt = route_ref[rows, :]
        w1, w2 = rt[:, 2:3], rt[:, 3:4]
        y1a, y1b = _unpack_bf16_pair(ybuf[0])
        y2a, y2b = _unpack_bf16_pair(ybuf[1])
        y = jnp.concatenate([w1 * y1a + w2 * y2a, w1 * y1b + w2 * y2b], axis=1)
        x1 = x_ref[rows, :] + mod_ref[0][5:6] * y
        o_ref[rows, :] = x1
        if fuse_next:
            _odd_projection(x1, nmod_ref[0], g_ref, w_ref, ux_ref.at[rows], gg_ref.at[rows], up_ref.at[rows])

    @pl.when(i == 0)
    def _():
        issue(0, 0)

    wait(0)
    issue(2 * i + 1, 1)
    compute(0)
    wait(1)
    issue(jnp.minimum(2 * i + 2, 2 * n - 1), 0)
    compute(1)

    @pl.when(i == n - 1)
    def _():
        wait(0)


def _combine(dest_flat, xn, route, modl, ys, dims, next_odd=None):
    n_tok, d = xn.shape
    n_batch, seq = dims
    blk = 2 * TOK_TILE
    assert seq % blk == 0 and n_tok % blk == 0
    tpb = seq // blk
    row = lambda i, dest: (i, 0)
    const = lambda i, dest: (0, 0)
    mod_spec = pl.BlockSpec((1, SUBLANES, d), lambda i, dest: (jnp.minimum(i // tpb, n_batch), 0, 0))
    in_specs = [pl.BlockSpec((blk, d), row), pl.BlockSpec((blk, LANES), row), mod_spec,
                pl.BlockSpec(memory_space=pl.ANY)]
    out_specs = [pl.BlockSpec((blk, d), row)]
    out_shape = [jax.ShapeDtypeStruct((n_tok, d), F32)]
    args = [dest_flat, xn, route, modl, ys]
    if next_odd is not None:
        nmod, g, w_bf = next_odd
        w = w_bf.shape[1] // 3
        in_specs += [mod_spec, pl.BlockSpec((1, d), const), pl.BlockSpec(w_bf.shape, const)]
        out_specs += [pl.BlockSpec((blk, w), row)] * 3
        out_shape += [jax.ShapeDtypeStruct((n_tok, w), F32)] * 3
        args += [nmod, g, w_bf]
    ybuf = pltpu.VMEM((TOP_K, TOK_TILE, ys.shape[1]), ys.dtype)
    return pl.pallas_call(
        functools.partial(_combine_kernel, fuse_next=next_odd is not None),
        grid_spec=pltpu.PrefetchScalarGridSpec(
            num_scalar_prefetch=1,
            grid=(n_tok // blk,),
            in_specs=in_specs, out_specs=out_specs,
            scratch_shapes=[ybuf, ybuf, pltpu.SemaphoreType.DMA((2,))]),
        out_shape=out_shape,
        compiler_params=_cparams("arbitrary"),
        name="moe_combine",
    )(*args)


def _moe(xn, tok, route, totals, modl, w1, w3, w2, layer, dims, next_odd):
    n_tok = tok.shape[0]
    n = n_tok * TOP_K
    n_blocks = (n + N_EXPERTS * (MOE_BLOCK - 1) + MOE_BLOCK - 1) // MOE_BLOCK
    dest, counts = _slots(route, totals)
    dest_flat = dest[:, :TOP_K].reshape(-1)
    cnt = counts[0, :N_EXPERTS].astype(jnp.int32)
    start = counts[1, :N_EXPERTS].astype(jnp.int32)
    blocks_per_e = (cnt + MOE_BLOCK - 1) // MOE_BLOCK
    blk_end = (start + blocks_per_e * MOE_BLOCK) // MOE_BLOCK
    n_used = blk_end[-1:]
    blk = jnp.minimum(jnp.arange(n_blocks, dtype=jnp.int32), n_used[0] - 1)
    block_e = jnp.sum((blk[:, None] >= blk_end[None, :]).astype(jnp.int32), axis=1)
    block_e = jnp.minimum(block_e, N_EXPERTS - 1)
    xs = _dispatch(dest_flat, start + cnt, blk_end * MOE_BLOCK, tok, n_blocks * MOE_BLOCK)
    ys = _experts(block_e, n_used, xs, w1, w3, w2, layer)
    return _combine(dest_flat, xn, route, modl, ys, dims, next_odd)


def _rope_tables(seq):
    rows = seq // GRID_W
    row = jnp.repeat(jnp.arange(rows), GRID_W).astype(F32)
    col = jnp.tile(jnp.arange(GRID_W), rows).astype(F32)
    inv_freq = ROPE_BASE ** (-jnp.arange(ROPE_PAIRS, dtype=F32) / ROPE_PAIRS)
    ar, ac = row[:, None] * inv_freq, col[:, None] * inv_freq
    cos = jnp.concatenate([jnp.cos(ar), jnp.cos(ar), jnp.cos(ac), jnp.cos(ac)], axis=-1)
    sin = jnp.concatenate([-jnp.sin(ar), jnp.sin(ar), -jnp.sin(ac), jnp.sin(ac)], axis=-1)
    reps = LANES // HEAD_DIM
    cos, sin = jnp.tile(cos, (1, reps)), jnp.tile(sin, (1, reps))
    cos = jnp.concatenate([cos, jnp.ones((TOK_TILE, LANES), F32)], axis=0)
    sin = jnp.concatenate([sin, jnp.zeros((TOK_TILE, LANES), F32)], axis=0)
    return cos, sin


def _head_mean_matrix():
    r = jnp.arange(LANES)
    return jnp.where((r[:, None] // HEAD_DIM) == (r[None, :] // HEAD_DIM), 1.0 / HEAD_DIM, 0.0).astype(BF16)


def _block_diag_chunks(w, chunk):
    dirs, nblk, bw, _ = w.shape
    per = chunk // bw
    w = w.reshape(dirs, nblk // per, per, bw, bw)
    eye = jnp.eye(per, dtype=w.dtype)
    out = jnp.einsum("dcpij,pq->dcpiqj", w, eye)
    return out.reshape(dirs, nblk // per, chunk, chunk).astype(BF16)


def _router_matrix(gw, gb, ew, eb):
    d = gw.shape[0]
    pad = LANES - N_EXPERTS - N_GROUPS
    wr = jnp.concatenate([ew, gw, jnp.zeros((d, pad), F32)], axis=1)
    br = jnp.concatenate([eb, gb, jnp.zeros((pad,), F32)])[None, :]
    hi = wr.astype(BF16)
    lo = (wr - hi.astype(F32)).astype(BF16)
    return jnp.concatenate([hi, lo], axis=1), br


def kernel(x, c, ctx, c_ctx, ada_w, ada_b, norm_mix, norm_ffn, out_norm, w_out, w_in_ab, q_norm, k_norm, attn_sink, sconv_w, sconv_b, w_in_cd, pool_w, pool_scale, lru_conv_w, lru_conv_b, lru_wa, lru_ba, lru_wx, lru_bx, lru_lambda, router_gw, router_gb, router_ew, router_eb, exp_w1, exp_w3, exp_w2):
    n_batch, seq, d = x.shape
    ctx_len = ctx.shape[1]
    depth = ada_w.shape[0]
    assert ctx_len == SEQ_TILE and seq % TOK_TILE == 0 and seq % GRID_W == 0
    assert seq % MERGE_TILE == 0 and (n_batch * ctx_len) % MERGE_TILE == 0 and MERGE_TILE % SEQ_TILE == 0
    assert depth == 2, "context-side odd-layer outputs are not implemented"
    dims = (n_batch, seq)
    n_lat = n_batch * seq

    cc = jnp.concatenate([c, c_ctx[None, :], jnp.zeros((SUBLANES - n_batch - 1, d), F32)], axis=0)
    mods = _ada(cc, ada_w, ada_b)
    mods = mods[:, :n_batch + 1].reshape(depth, n_batch + 1, N_MOD, d)
    mods = jnp.pad(mods, ((0, 0), (0, 0), (0, SUBLANES - N_MOD), (0, 0)))

    xparts = (x.reshape(n_lat, d), ctx.reshape(n_batch * ctx_len, d))
    cos_t, sin_t = _rope_tables(seq)
    pm = _head_mean_matrix()

    projected = None
    for i in range(depth):
        need_ctx = i < depth - 1
        j = i // 2
        modl = mods[i]
        n_out = n_lat + n_batch * ctx_len if need_ctx else n_lat
        wr, br = _router_matrix(router_gw[i], router_gb[i], router_ew[i], router_eb[i])
        wo_bf = w_out[i].astype(BF16)
        nm, nf, on = norm_mix[i][None, :], norm_ffn[i][None, :], out_norm[i][None, :]
        if i % 2 == 0:
            reps = LANES // HEAD_DIM
            q, kvx, gb, u = _inproj_even(xparts, modl, nm, w_in_ab[j].astype(BF16),
                                          jnp.tile(q_norm[j], reps)[None, :], jnp.tile(k_norm[j], reps)[None, :],
                                          cos_t, sin_t, pm, dims)
            att = _attention(q, kvx, attn_sink[j], dims, need_ctx)
            xn, tok, route, totals = _merge_even(att, gb, u, xparts, modl, sconv_w[j], sconv_b[j][None, :], on, wo_bf,
                                         nf, wr, br, dims, n_out)
        else:
            if projected is None:
                projected = _inproj_odd(xparts, modl, nm, w_in_cd[j].astype(BF16), dims)
            ux, gg, up = projected
            chunk = 2 * LANES
            hf, hr = _lru(ux, lru_conv_w[j], lru_conv_b[j][None, :],
                          _block_diag_chunks(0.5 * lru_wa[j], chunk), _block_diag_chunks(0.5 * lru_wx[j], chunk),
                          0.5 * lru_ba[j][:, None, :], 0.5 * lru_bx[j][:, None, :],
                          lru_lambda[j][:, None, :], dims)
            xn, tok, route, totals = _merge_odd(up, hf, hr, gg, xparts[0], modl, pool_w[j].astype(BF16),
                                        pool_scale[j][None, :], on, wo_bf, nf, wr, br, dims)
        next_odd = None
        if need_ctx and (i + 1) % 2 == 1:
            next_odd = (mods[i + 1], norm_mix[i + 1][None, :], w_in_cd[(i + 1) // 2].astype(BF16))
        outs = _moe(xn, tok, route, totals, modl, exp_w1, exp_w3, exp_w2, i, dims, next_odd)
        xparts = (outs[0],)
        projected = tuple(outs[1:]) if next_odd is not None else None
    return xparts[0][:n_lat].reshape(n_batch, seq, d)
```

```python
import functools

import jax
import jax.numpy as jnp
from jax import lax
from jax.experimental import pallas as pl
from jax.experimental.pallas import tpu as pltpu

F32 = jnp.float32
BF16 = jnp.bfloat16
HIGHEST = lax.Precision.HIGHEST
LOG2E = 1.4426950408889634

GRID_W = 64
EPS = 1e-6
HEAD_DIM = 64
N_Q_HEADS = 8
N_KV_HEADS = 2
Q_PER_KV = N_Q_HEADS // N_KV_HEADS
WINDOW = 128
ATT_BLOCK = 128
ROPE_PAIRS = HEAD_DIM // 4
ROPE_BASE = 10000.0
POOL_WINDOWS = (2, 4, 8, 16)
LRU_C = 8.0
N_GROUPS = 4
EXPERTS_PER_GROUP = 8
N_EXPERTS = N_GROUPS * EXPERTS_PER_GROUP
TOP_K = 2
MOE_BLOCK = 512
ZERO_CHUNK = 64
N_MOD = 6

SUBLANES = 8
LANES = 128
SEQ_TILE = 256
TOK_TILE = 512
SLOT_TILE = 1024
MERGE_TILE = 512
ATT_TILE = 256
LRU_TILE = 1024
NEG = -1e30
VMEM_LIMIT = 48 * 1024 * 1024


def _cparams(*sem):
    return pltpu.CompilerParams(dimension_semantics=sem, vmem_limit_bytes=VMEM_LIMIT)


def _rms(x, g):
    ms = jnp.mean(x * x, axis=-1, keepdims=True)
    return x * lax.rsqrt(ms + EPS) * g


def _modulate(x, g, shift, scale):
    return _rms(x, g) * (1.0 + scale) + shift


def _bdot(a, b):
    return jnp.dot(a.astype(BF16), b, preferred_element_type=F32)


def _silu(x):
    return x * jax.nn.sigmoid(x)


def _pack_bf16_pair(a, b):
    ha = lax.bitcast_convert_type(a.astype(BF16).astype(F32), jnp.uint32)
    hb = lax.bitcast_convert_type(b.astype(BF16).astype(F32), jnp.uint32)
    return ha | (hb >> 16)


def _unpack_bf16_pair(w):
    a = lax.bitcast_convert_type(w & jnp.uint32(0xFFFF0000), F32)
    b = lax.bitcast_convert_type(w << 16, F32)
    return a, b


def _ada_kernel(c_ref, w_ref, b_ref, o_ref):
    s = _silu(c_ref[...])
    o_ref[0] = jnp.dot(s, w_ref[0], precision=HIGHEST, preferred_element_type=F32) + b_ref[0]


def _ada(cc, ada_w, ada_b):
    depth, d, n = ada_w.shape
    tn = n // 2
    return pl.pallas_call(
        _ada_kernel,
        grid=(depth, n // tn),
        in_specs=[pl.BlockSpec((SUBLANES, d), lambda l, j: (0, 0)),
                  pl.BlockSpec((1, d, tn), lambda l, j: (l, 0, j)),
                  pl.BlockSpec((1, 1, tn), lambda l, j: (l, 0, j))],
        out_specs=pl.BlockSpec((1, SUBLANES, tn), lambda l, j: (l, 0, j)),
        out_shape=jax.ShapeDtypeStruct((depth, SUBLANES, n), F32),
        compiler_params=_cparams("arbitrary", "arbitrary"),
        name="ada",
    )(cc, ada_w, ada_b.reshape(depth, 1, n))


def _head_norm_rope(t, pm, g, cos, sin):
    m = t.shape[0]
    t2 = t * t
    hi = t2.astype(BF16)
    lo = (t2 - hi.astype(F32)).astype(BF16)
    ms2 = jnp.dot(jnp.concatenate([hi, lo], axis=0), pm, preferred_element_type=F32)
    ms = ms2[0:m] + ms2[m:2 * m]
    t = t * lax.rsqrt(ms + EPS) * g
    lane = lax.broadcasted_iota(jnp.int32, t.shape, 1)
    first_half = (lane % (2 * ROPE_PAIRS)) < ROPE_PAIRS
    partner = jnp.where(first_half, pltpu.roll(t, LANES - ROPE_PAIRS, 1), pltpu.roll(t, ROPE_PAIRS, 1))
    return t * cos + partner * sin


def _token_operands(xparts, n_lat_tiles, tile):
    d = xparts[0].shape[1]
    lat = pl.BlockSpec((tile, d), lambda j: (jnp.minimum(j, n_lat_tiles - 1), 0))
    if len(xparts) == 2:
        ctx = pl.BlockSpec((tile, d), lambda j: (jnp.maximum(j - n_lat_tiles, 0), 0))
        return list(xparts), [lat, ctx]
    ctx = pl.BlockSpec((tile, d), lambda j: (jnp.maximum(j, n_lat_tiles), 0))
    return [xparts[0], xparts[0]], [lat, ctx]


def _pick_tokens(xl_ref, xc_ref, n_lat_tiles):
    return jnp.where(pl.program_id(0) >= n_lat_tiles, xc_ref[...], xl_ref[...])


def _inproj_even_kernel(xl_ref, xc_ref, mod_ref, g_ref, w_ref, qg_ref, kg_ref, cos_ref, sin_ref, pm_ref,
                        q_ref, kvx_ref, gb_ref, u_ref, *, n_lat_tiles):
    mod = mod_ref[0]
    x = _pick_tokens(xl_ref, xc_ref, n_lat_tiles)
    h = _modulate(x, g_ref[...], mod[0:1], mod[1:2]).astype(BF16)
    cos, sin, pm = cos_ref[...], sin_ref[...], pm_ref[...]
    kv_w = N_KV_HEADS * HEAD_DIM
    att_w = N_Q_HEADS * HEAD_DIM
    sc_w = gb_ref.shape[1]
    kv = jnp.dot(h, w_ref[:, 0:2 * kv_w], preferred_element_type=F32)
    k = _head_norm_rope(kv[:, :kv_w], pm, kg_ref[...], cos, sin)
    v = kv[:, kv_w:]
    kvx_ref[...] = jnp.concatenate([k, pltpu.roll(k, HEAD_DIM, 1), v, pltpu.roll(v, HEAD_DIM, 1)],
                                   axis=-1).astype(BF16)
    q = jnp.dot(h, w_ref[:, 2 * kv_w:2 * kv_w + att_w], preferred_element_type=F32)
    qscale = HEAD_DIM ** -0.5 * LOG2E
    for c in range(att_w // LANES):
        qc = _head_norm_rope(q[:, c * LANES:(c + 1) * LANES], pm, qg_ref[...], cos, sin)
        q_ref[:, c * LANES:(c + 1) * LANES] = (qc * qscale).astype(BF16)
    o = 2 * kv_w + att_w
    gb_ref[...] = jnp.dot(h, w_ref[:, o:o + sc_w], preferred_element_type=F32)
    gc = jnp.dot(h, w_ref[:, o + sc_w:o + 2 * sc_w], preferred_element_type=F32)
    gx = jnp.dot(h, w_ref[:, o + 2 * sc_w:o + 3 * sc_w], preferred_element_type=F32)
    u_ref[...] = gc * gx


def _tile_maps(n_lat_tiles, tiles_per_seq, n_batch):
    def mod_map(j):
        return (jnp.minimum(j // tiles_per_seq, n_batch), 0, 0)

    def pos_map(j):
        return (jnp.where(j < n_lat_tiles, j % tiles_per_seq, tiles_per_seq), 0)

    return mod_map, pos_map


def _inproj_even(xparts, modl, g, w_bf, q_g, k_g, cos_t, sin_t, pm, dims):
    n_tok = sum(p.shape[0] for p in xparts)
    d = xparts[0].shape[1]
    n_batch, seq = dims
    tile = TOK_TILE
    tps = seq // tile
    mod_map, pos_map = _tile_maps(n_batch * tps, tps, n_batch)
    xs, xspecs = _token_operands(xparts, n_batch * tps, tile)
    kv_w, att_w = N_KV_HEADS * HEAD_DIM, N_Q_HEADS * HEAD_DIM
    sc_w = (w_bf.shape[1] - 2 * kv_w - att_w) // 3
    row = lambda j: (j, 0)
    const = lambda j: (0, 0)
    return pl.pallas_call(
        functools.partial(_inproj_even_kernel, n_lat_tiles=n_batch * tps),
        grid=(n_tok // tile,),
        in_specs=xspecs + [
                  pl.BlockSpec((1, SUBLANES, d), mod_map),
                  pl.BlockSpec((1, d), const),
                  pl.BlockSpec(w_bf.shape, const),
                  pl.BlockSpec((1, LANES), const),
                  pl.BlockSpec((1, LANES), const),
                  pl.BlockSpec((tile, LANES), pos_map),
                  pl.BlockSpec((tile, LANES), pos_map),
                  pl.BlockSpec((LANES, LANES), const)],
        out_specs=[pl.BlockSpec((tile, att_w), row),
                   pl.BlockSpec((tile, 4 * kv_w), row),
                   pl.BlockSpec((tile, sc_w), row),
                   pl.BlockSpec((tile, sc_w), row)],
        out_shape=[jax.ShapeDtypeStruct((n_tok, att_w), BF16),
                   jax.ShapeDtypeStruct((n_tok, 4 * kv_w), BF16),
                   jax.ShapeDtypeStruct((n_tok, sc_w), F32),
                   jax.ShapeDtypeStruct((n_tok, sc_w), F32)],
        compiler_params=_cparams("arbitrary"),
        name="inproj_even",
    )(*xs, modl, g, w_bf, q_g, k_g, cos_t, sin_t, pm)


def _attn_heads(q, kvx, mask_fn, sink_ref, o_ref):
    m, n = q.shape[0], kvx.shape[0]
    k_nat, k_swp = kvx[:, 0:LANES], kvx[:, LANES:2 * LANES]
    v_nat, v_swp = kvx[:, 2 * LANES:3 * LANES], kvx[:, 3 * LANES:4 * LANES]
    lo_n = lax.broadcasted_iota(jnp.int32, (n, LANES), 1) < HEAD_DIM
    lo_m = lax.broadcasted_iota(jnp.int32, (2 * m, LANES), 1) < HEAD_DIM
    top = lax.broadcasted_iota(jnp.int32, (2 * m, 1), 0) < m
    zero = jnp.zeros_like(k_nat)
    r2 = lax.broadcasted_iota(jnp.int32, (2 * n, LANES), 0) < n
    l2 = lax.broadcasted_iota(jnp.int32, (2 * n, LANES), 1) < HEAD_DIM
    ones_both = jnp.where(r2 == l2, 1.0, 0.0).astype(BF16)
    outs = []
    for h in range(N_KV_HEADS):
        k_lo, k_hi = (k_nat, k_swp) if h == 0 else (k_swp, k_nat)
        v_lo, v_hi = (v_nat, v_swp) if h == 0 else (v_swp, v_nat)
        g0 = 2 * h
        q2 = jnp.concatenate([q[:, g0 * LANES:(g0 + 1) * LANES],
                              q[:, (g0 + 1) * LANES:(g0 + 2) * LANES]], axis=0)
        k_both = jnp.concatenate([jnp.where(lo_n, k_lo, zero), jnp.where(lo_n, zero, k_hi)], axis=0)
        v_both = jnp.concatenate([jnp.where(lo_n, v_lo, zero), jnp.where(lo_n, zero, v_hi)], axis=0)
        s = lax.dot_general(q2, k_both, (((1,), (1,)), ((), ())), preferred_element_type=F32)
        hq = Q_PER_KV * h
        halves = []
        for par in range(2):
            sp = mask_fn(s[:, par * n:(par + 1) * n])
            snk = jnp.where(top, sink_ref[hq + par], sink_ref[hq + 2 + par]) * LOG2E
            mx = jnp.maximum(jnp.max(sp, axis=-1, keepdims=True), snk)
            halves.append((jnp.exp2(sp - mx), jnp.exp2(snk - mx)))
        (p_lo, e_lo), (p_hi, e_hi) = halves
        p = jnp.concatenate([p_lo, p_hi], axis=1).astype(BF16)
        nd = jnp.dot(p, jnp.concatenate([v_both, ones_both], axis=1), preferred_element_type=F32)
        o = nd[:, :LANES] / (nd[:, LANES:] + jnp.where(lo_m, e_lo, e_hi))
        outs += [o[0:m], o[m:2 * m]]
    o_ref[...] = jnp.concatenate(outs, axis=-1)


def _attn_lat_kernel(sink_ref, q_ref, kp_ref, kc_ref, kn_ref, kx_ref, o_ref):
    i = pl.program_id(1)
    nb = pl.num_programs(1)
    kvx = jnp.concatenate([kp_ref[...], kc_ref[...], kn_ref[...], kx_ref[...]], axis=0)
    m, w = ATT_TILE, WINDOW
    tok = lax.broadcasted_iota(jnp.int32, (2 * m, m), 0) % m
    col = lax.broadcasted_iota(jnp.int32, (2 * m, m), 1)
    tok_w = lax.broadcasted_iota(jnp.int32, (2 * m, w), 0) % m
    col_w = lax.broadcasted_iota(jnp.int32, (2 * m, w), 1)
    ok_prev = (col_w >= tok_w) & (i > 0)
    ok_cur = (col - tok <= w) & (tok - col <= w)
    ok_next = (col_w + (m - w) <= tok_w) & (i < nb - 1)

    def mask_fn(s):
        return jnp.concatenate([jnp.where(ok_prev, s[:, 0:w], NEG), jnp.where(ok_cur, s[:, w:w + m], NEG),
                                jnp.where(ok_next, s[:, w + m:2 * w + m], NEG), s[:, 2 * w + m:]], axis=1)

    _attn_heads(q_ref[...], kvx, mask_fn, sink_ref, o_ref)


def _attn_ctx_kernel(sink_ref, att_in_ref, q_ref, kx_ref, o_ref):
    del att_in_ref
    _attn_heads(q_ref[...], kx_ref[...], lambda s: s, sink_ref, o_ref)


def _attention(q, kvx, sink, dims, need_ctx):
    n_tok = q.shape[0]
    n_batch, seq = dims
    nb = seq // ATT_TILE
    r = ATT_TILE // WINDOW
    nw = seq // WINDOW
    n_lat = n_batch * seq
    ctx_len = (n_tok - n_lat) // n_batch
    att_w, kvx_w = q.shape[1], kvx.shape[1]
    ctx_blk0 = n_lat // ctx_len
    cur = lambda b, i: (b * nb + i, 0)
    prev = lambda b, i: (b * nw + jnp.maximum(i * r - 1, 0), 0)
    nxt = lambda b, i: (b * nw + jnp.minimum((i + 1) * r, nw - 1), 0)
    cx = lambda b, i: (ctx_blk0 + b, 0)
    smem = pl.BlockSpec(memory_space=pltpu.SMEM)
    att = pl.pallas_call(
        _attn_lat_kernel,
        grid=(n_batch, nb),
        in_specs=[smem, pl.BlockSpec((ATT_TILE, att_w), cur),
                  pl.BlockSpec((WINDOW, kvx_w), prev), pl.BlockSpec((ATT_TILE, kvx_w), cur),
                  pl.BlockSpec((WINDOW, kvx_w), nxt), pl.BlockSpec((ctx_len, kvx_w), cx)],
        out_specs=pl.BlockSpec((ATT_TILE, att_w), cur),
        out_shape=jax.ShapeDtypeStruct((n_tok, att_w), F32),
        compiler_params=_cparams("arbitrary", "arbitrary"),
        name="attn_latent",
    )(sink, q, kvx, kvx, kvx, kvx)
    if not need_ctx:
        return att
    cxb = lambda b: (ctx_blk0 + b, 0)
    return pl.pallas_call(
        _attn_ctx_kernel,
        grid=(n_batch,),
        in_specs=[smem, pl.BlockSpec(memory_space=pl.ANY),
                  pl.BlockSpec((ctx_len, att_w), cxb), pl.BlockSpec((ctx_len, kvx_w), cxb)],
        out_specs=pl.BlockSpec((ctx_len, att_w), cxb),
        out_shape=jax.ShapeDtypeStruct((n_tok, att_w), F32),
        input_output_aliases={1: 0},
        compiler_params=_cparams("arbitrary"),
        name="attn_context",
    )(sink, att, q, kvx)


def _route(logits):
    lane = lax.broadcasted_iota(jnp.int32, logits.shape, 1).astype(F32)
    big = float(LANES)
    gl = jnp.where((lane >= N_EXPERTS) & (lane < N_EXPERTS + N_GROUPS), logits, NEG)
    gmax = jnp.max(gl, axis=-1, keepdims=True)
    gidx = jnp.min(jnp.where(gl == gmax, lane, big), axis=-1, keepdims=True) - N_EXPERTS
    p_sel = 1.0 / jnp.sum(jnp.exp(gl - gmax), axis=-1, keepdims=True)
    lo = gidx * EXPERTS_PER_GROUP
    el = jnp.where((lane >= lo) & (lane < lo + EXPERTS_PER_GROUP), logits, NEG)
    m1 = jnp.max(el, axis=-1, keepdims=True)
    i1 = jnp.min(jnp.where(el == m1, lane, big), axis=-1, keepdims=True)
    el2 = jnp.where(lane == i1, NEG, el)
    m2 = jnp.max(el2, axis=-1, keepdims=True)
    i2 = jnp.min(jnp.where(el2 == m2, lane, big), axis=-1, keepdims=True)
    t = jnp.exp(m2 - m1)
    w1 = p_sel / (1.0 + t)
    w2 = p_sel * t / (1.0 + t)
    return jnp.where(lane == 0, i1, jnp.where(lane == 1, i2, jnp.where(lane == 2, w1,
                     jnp.where(lane == 3, w2, 0.0))))


def _tail(n1, n2, gain, wo_ref, x, mod, nf_ref, wr_ref, br_ref, xo_ref, tok_ref, route_ref, cnt_ref):
    half = n1.shape[1]
    y = (_bdot(_rms(n1, gain[:, :half]), wo_ref[0:half, :])
         + _bdot(_rms(n2, gain[:, half:]), wo_ref[half:, :]))
    xn = x + mod[2:3] * y
    xo_ref[...] = xn
    tok = _modulate(xn, nf_ref[...], mod[3:4], mod[4:5])
    half_d = tok.shape[1] // 2
    tok_ref[...] = _pack_bf16_pair(tok[:, :half_d], tok[:, half_d:])
    t_hi = tok.astype(BF16)
    t_lo = (tok - t_hi.astype(F32)).astype(BF16)
    hh = jnp.dot(t_hi, wr_ref[...], preferred_element_type=F32)
    logits = (hh[:, :LANES] + hh[:, LANES:] + br_ref[...]
              + jnp.dot(t_lo, wr_ref[:, 0:LANES], preferred_element_type=F32))
    route = _route(logits)
    route_ref[...] = route
    lane = lax.broadcasted_iota(jnp.int32, route.shape, 1).astype(F32)
    hit = jnp.where((lane == route[:, 0:1]) | (lane == route[:, 1:2]), 1.0, 0.0)

    @pl.when(pl.program_id(0) == 0)
    def _():
        cnt_ref[...] = jnp.zeros_like(cnt_ref)

    cnt_ref[0:1, :] += jnp.sum(hit, axis=0, keepdims=True)


def _seq_flags(j, n_lat_tiles, tiles_per_seq):
    is_ctx = j >= n_lat_tiles
    first = is_ctx | (j % tiles_per_seq == 0)
    last = is_ctx | (j % tiles_per_seq == tiles_per_seq - 1)
    return first, last


def _with_halo(prev_ref, tile, next_ref, first, last):
    p = jnp.where(first, 0.0, prev_ref[...])
    n = jnp.where(last, 0.0, next_ref[...])
    return jnp.concatenate([p, tile, n], axis=0)


def _shift_rows(e, k):
    return pltpu.roll(e, (-k) % e.shape[0], 0)


def _merge_even_kernel(att_ref, gb_ref, u_ref, up_ref, un_ref, xl_ref, xc_ref, mod_ref, cw_ref, cb_ref,
                       on_ref, wo_ref, nf_ref, wr_ref, br_ref,
                       xo_ref, tok_ref, route_ref, cnt_ref, *, n_lat_tiles, tiles_per_seq):
    j = pl.program_id(0)
    first, last = _seq_flags(j, n_lat_tiles, tiles_per_seq)
    e = _with_halo(up_ref, u_ref[...], un_ref, first, last)
    cw = cw_ref[...]
    lo, hi = SUBLANES, SUBLANES + MERGE_TILE
    pos = lax.broadcasted_iota(jnp.int32, (MERGE_TILE, 1), 0) % SEQ_TILE
    is_ctx = j >= n_lat_tiles
    before = jnp.where(is_ctx & (pos == 0), 0.0, _shift_rows(e, -1)[lo:hi])
    after = jnp.where(is_ctx & (pos == SEQ_TILE - 1), 0.0, _shift_rows(e, 1)[lo:hi])
    conv = cw[0:1] * before + cw[1:2] * e[lo:hi] + cw[2:3] * after + cb_ref[...]
    conv = gb_ref[...] * conv
    x = _pick_tokens(xl_ref, xc_ref, n_lat_tiles)
    _tail(att_ref[...], conv, on_ref[...], wo_ref, x, mod_ref[0], nf_ref, wr_ref, br_ref,
          xo_ref, tok_ref, route_ref, cnt_ref)


def _halo_specs(width, n_rows):
    r = MERGE_TILE // SUBLANES
    n_blk = n_rows // SUBLANES
    prev = pl.BlockSpec((SUBLANES, width), lambda j: (jnp.maximum(j * r - 1, 0), 0))
    nxt = pl.BlockSpec((SUBLANES, width), lambda j: (jnp.minimum((j + 1) * r, n_blk - 1), 0))
    return prev, nxt


def _tail_specs(d, n_out_tok):
    row = lambda j: (j, 0)
    out_specs = [pl.BlockSpec((MERGE_TILE, d), row), pl.BlockSpec((MERGE_TILE, d // 2), row),
                 pl.BlockSpec((MERGE_TILE, LANES), row), pl.BlockSpec((SUBLANES, LANES), lambda j: (0, 0))]
    out_shape = [jax.ShapeDtypeStruct((n_out_tok, d), F32), jax.ShapeDtypeStruct((n_out_tok, d // 2), jnp.uint32),
                 jax.ShapeDtypeStruct((n_out_tok, LANES), F32), jax.ShapeDtypeStruct((SUBLANES, LANES), F32)]
    return out_specs, out_shape


def _merge_even(att, gb, u, xparts, modl, cw, cb, on, wo_bf, nf, wr, br, dims, n_out_tok):
    n_tok = u.shape[0]
    d = xparts[0].shape[1]
    n_batch, seq = dims
    tps = seq // MERGE_TILE
    mod_map, _ = _tile_maps(n_batch * tps, tps, n_batch)
    xs, xspecs = _token_operands(xparts, n_batch * tps, MERGE_TILE)
    sc_w = u.shape[1]
    row = lambda j: (j, 0)
    const = lambda j: (0, 0)
    prev, nxt = _halo_specs(sc_w, n_tok)
    out_specs, out_shape = _tail_specs(d, n_out_tok)
    return pl.pallas_call(
        functools.partial(_merge_even_kernel, n_lat_tiles=n_batch * tps, tiles_per_seq=tps),
        grid=(n_out_tok // MERGE_TILE,),
        in_specs=[pl.BlockSpec((MERGE_TILE, att.shape[1]), row),
                  pl.BlockSpec((MERGE_TILE, sc_w), row),
                  pl.BlockSpec((MERGE_TILE, sc_w), row), prev, nxt] + xspecs + [
                  pl.BlockSpec((1, SUBLANES, d), mod_map),
                  pl.BlockSpec(cw.shape, const), pl.BlockSpec((1, sc_w), const),
                  pl.BlockSpec((1, d), const), pl.BlockSpec(wo_bf.shape, const),
                  pl.BlockSpec((1, d), const), pl.BlockSpec(wr.shape, const),
                  pl.BlockSpec((1, LANES), const)],
        out_specs=out_specs, out_shape=out_shape,
        compiler_params=_cparams("arbitrary"),
        name="merge_even",
    )(att, gb, u, u, u, *xs, modl, cw, cb, on, wo_bf, nf, wr, br)


def _odd_projection(x, mod, g_ref, w_ref, ux_ref, gg_ref, up_ref):
    h = _modulate(x, g_ref[...], mod[0:1], mod[1:2]).astype(BF16)
    w = ux_ref.shape[1]
    ux_ref[...] = jnp.dot(h, w_ref[:, 0:w], preferred_element_type=F32)
    ug = jnp.dot(h, w_ref[:, w:2 * w], preferred_element_type=F32)
    gg_ref[...] = jax.nn.gelu(ug, approximate=True)
    up_ref[...] = jnp.dot(h, w_ref[:, 2 * w:3 * w], preferred_element_type=F32)


def _inproj_odd_kernel(xl_ref, xc_ref, mod_ref, g_ref, w_ref, ux_ref, gg_ref, up_ref, *, n_lat_tiles):
    x = _pick_tokens(xl_ref, xc_ref, n_lat_tiles)
    _odd_projection(x, mod_ref[0], g_ref, w_ref, ux_ref, gg_ref, up_ref)


def _inproj_odd(xparts, modl, g, w_bf, dims):
    n_tok = sum(p.shape[0] for p in xparts)
    d = xparts[0].shape[1]
    n_batch, seq = dims
    tile = TOK_TILE
    tps = seq // tile
    mod_map, _ = _tile_maps(n_batch * tps, tps, n_batch)
    xs, xspecs = _token_operands(xparts, n_batch * tps, tile)
    w = w_bf.shape[1] // 3
    row = lambda j: (j, 0)
    const = lambda j: (0, 0)
    return pl.pallas_call(
        functools.partial(_inproj_odd_kernel, n_lat_tiles=n_batch * tps),
        grid=(n_tok // tile,),
        in_specs=xspecs + [
                  pl.BlockSpec((1, SUBLANES, d), mod_map),
                  pl.BlockSpec((1, d), const),
                  pl.BlockSpec(w_bf.shape, const)],
        out_specs=[pl.BlockSpec((tile, w), row)] * 3,
        out_shape=[jax.ShapeDtypeStruct((n_tok, w), F32)] * 3,
        compiler_params=_cparams("arbitrary"),
        name="inproj_odd",
    )(*xs, modl, g, w_bf)


def _lru_conv(e, cw, cb):
    lo, hi = SUBLANES, e.shape[0] - SUBLANES
    return (cw[0:1] * _shift_rows(e, -2)[lo:hi] + cw[1:2] * _shift_rows(e, -1)[lo:hi]
            + cw[2:3] * e[lo:hi] + cw[3:4] * _shift_rows(e, 1)[lo:hi] + cb)


def _lru_coeffs(xl, d, wa_ref, wx_ref, ba_ref, bx_ref, lam_ref, a_ref, b_ref):
    xb = xl.astype(BF16)
    chunk = wa_ref.shape[-1]
    n_chunk = xl.shape[1] // chunk
    pre_a = jnp.concatenate([jnp.dot(xb[:, c * chunk:(c + 1) * chunk], wa_ref[d, c],
                                     preferred_element_type=F32) for c in range(n_chunk)], axis=-1)
    pre_x = jnp.concatenate([jnp.dot(xb[:, c * chunk:(c + 1) * chunk], wx_ref[d, c],
                                     preferred_element_type=F32) for c in range(n_chunk)], axis=-1)
    tr = jnp.tanh(pre_a + ba_ref[d])
    tg = jnp.tanh(pre_x + bx_ref[d])
    c_half = (-0.5 * LRU_C) * jax.nn.softplus(-lam_ref[d])
    log_a = c_half * tr + c_half
    a_ref[...] = jnp.exp(log_a)
    th = jnp.tanh(log_a)
    b_ref[...] = jnp.sqrt(-0.5 * th / (1.0 - th)) * (1.0 + tg) * xl


def _scan_tile(a_ref, b_ref, h_ref, carry, reverse):
    n_grp = a_ref.shape[0] // SUBLANES
    row = lax.broadcasted_iota(jnp.int32, (SUBLANES, a_ref.shape[1]), 0)

    def body(g, carry):
        gi = (n_grp - 1 - g) if reverse else g
        r0 = pl.multiple_of(gi * SUBLANES, SUBLANES)
        a8 = a_ref[pl.ds(r0, SUBLANES), :]
        b8 = b_ref[pl.ds(r0, SUBLANES), :]
        for s in (1, 2, 4):
            if reverse:
                ok = row < SUBLANES - s
                sh = SUBLANES - s
            else:
                ok = row >= s
                sh = s
            a_sh = jnp.where(ok, pltpu.roll(a8, sh, 0), 1.0)
            b_sh = jnp.where(ok, pltpu.roll(b8, sh, 0), 0.0)
            b8 = a8 * b_sh + b8
            a8 = a8 * a_sh
        h8 = a8 * carry + b8
        if h_ref is not None:
            h_ref[pl.ds(r0, SUBLANES), :] = h8
        return h8[0:1] if reverse else h8[SUBLANES - 1:SUBLANES]

    return lax.fori_loop(0, n_grp, body, carry, unroll=4)


def _lru_kernel(uf_ref, ufp_ref, ufn_ref, ur_ref, urp_ref, urn_ref, uc_ref, cw_ref, cb_ref,
                wa_ref, wx_ref, ba_ref, bx_ref, lam_ref, hf_ref, hr_ref,
                cf_ref, cr_ref, a_ref, b_ref):
    i = pl.program_id(1)
    nt = pl.num_programs(1)
    cw, cb = cw_ref[...], cb_ref[...]
    coeffs = functools.partial(_lru_coeffs, wa_ref=wa_ref, wx_ref=wx_ref, ba_ref=ba_ref,
                               bx_ref=bx_ref, lam_ref=lam_ref, a_ref=a_ref, b_ref=b_ref)

    @pl.when(i == 0)
    def _():
        n_ctx = uc_ref.shape[0]
        ac_ref, bc_ref = a_ref.at[pl.ds(0, n_ctx)], b_ref.at[pl.ds(0, n_ctx)]
        z = jnp.zeros((SUBLANES, uc_ref.shape[1]), F32)
        xc = _lru_conv(jnp.concatenate([z, uc_ref[...], z], axis=0), cw, cb)
        zero = jnp.zeros((1, uc_ref.shape[1]), F32)
        coeffs(xc, 0, a_ref=ac_ref, b_ref=bc_ref)
        cf_ref[...] = _scan_tile(ac_ref, bc_ref, None, zero, False)
        coeffs(xc, 1, a_ref=ac_ref, b_ref=bc_ref)
        cr_ref[...] = _scan_tile(ac_ref, bc_ref, None, zero, True)

    xf = _lru_conv(_with_halo(ufp_ref, uf_ref[...], ufn_ref, i == 0, i == nt - 1), cw, cb)
    coeffs(xf, 0)
    cf_ref[...] = _scan_tile(a_ref, b_ref, hf_ref, cf_ref[...], False)
    xr = _lru_conv(_with_halo(urp_ref, ur_ref[...], urn_ref, i == nt - 1, i == 0), cw, cb)
    coeffs(xr, 1)
    cr_ref[...] = _scan_tile(a_ref, b_ref, hr_ref, cr_ref[...], True)


def _lru(ux, cw, cb, wa_bd, wx_bd, ba, bx, lam, dims):
    n_tok, w = ux.shape
    n_batch, seq = dims
    nt = seq // LRU_TILE
    n_lat = n_batch * seq
    r = LRU_TILE // SUBLANES
    n_blk = n_tok // SUBLANES
    fwd = lambda b, i: b * nt + i
    rev = lambda b, i: b * nt + nt - 1 - i

    def specs(tile):
        return [pl.BlockSpec((LRU_TILE, w), lambda b, i: (tile(b, i), 0)),
                pl.BlockSpec((SUBLANES, w), lambda b, i: (jnp.maximum(tile(b, i) * r - 1, 0), 0)),
                pl.BlockSpec((SUBLANES, w), lambda b, i: (jnp.minimum((tile(b, i) + 1) * r, n_blk - 1), 0))]

    const2 = lambda b, i: (0, 0)
    const3 = lambda b, i: (0, 0, 0)
    const4 = lambda b, i: (0, 0, 0, 0)
    return pl.pallas_call(
        _lru_kernel,
        grid=(n_batch, nt),
        in_specs=specs(fwd) + specs(rev) + [
            pl.BlockSpec((SEQ_TILE, w), lambda b, i: (n_lat // SEQ_TILE + b, 0)),
            pl.BlockSpec(cw.shape, const2), pl.BlockSpec((1, w), const2),
            pl.BlockSpec(wa_bd.shape, const4), pl.BlockSpec(wx_bd.shape, const4),
            pl.BlockSpec(ba.shape, const3), pl.BlockSpec(bx.shape, const3),
            pl.BlockSpec(lam.shape, const3)],
        out_specs=[pl.BlockSpec((LRU_TILE, w), lambda b, i: (fwd(b, i), 0)),
                   pl.BlockSpec((LRU_TILE, w), lambda b, i: (rev(b, i), 0))],
        out_shape=[jax.ShapeDtypeStruct((n_lat, w), F32)] * 2,
        scratch_shapes=[pltpu.VMEM((1, w), F32), pltpu.VMEM((1, w), F32),
                        pltpu.VMEM((LRU_TILE, w), F32), pltpu.VMEM((LRU_TILE, w), F32)],
        compiler_params=_cparams("arbitrary", "arbitrary"),
        name="rglru",
    )(ux, ux, ux, ux, ux, ux, ux, cw, cb, wa_bd, wx_bd, ba, bx, lam)


def _merge_odd_kernel(up_ref, upp_ref, upn_ref, hf_ref, hr_ref, gg_ref, x_ref, mod_ref,
                      pw_ref, ps_ref, on_ref, wo_ref, nf_ref, wr_ref, br_ref,
                      xo_ref, tok_ref, route_ref, cnt_ref, *, tiles_per_seq, seq):
    j = pl.program_id(0)
    ts = j % tiles_per_seq
    e = _with_halo(upp_ref, up_ref[...], upn_ref, ts == 0, ts == tiles_per_seq - 1)
    gw = pw_ref.shape[-1]
    lo, hi = SUBLANES, SUBLANES + MERGE_TILE
    tpos = ts * MERGE_TILE + lax.broadcasted_iota(jnp.int32, (MERGE_TILE, 1), 0)
    parts = []
    for g, win in enumerate(POOL_WINDOWS):
        eg = e[:, g * gw:(g + 1) * gw]
        p = eg
        span = 1
        while span < win:
            p = p + _shift_rows(p, span)
            span *= 2
        half = win // 2
        s = _shift_rows(p, -half)[lo:hi]
        cnt = (jnp.minimum(tpos + half, seq) - jnp.maximum(tpos - half, 0)).astype(F32)
        dlt = s / cnt - eg[lo:hi]
        parts.append(_bdot(dlt, pw_ref[g]))
    pool = jnp.concatenate(parts, axis=-1) * ps_ref[...]
    lru = (hf_ref[...] + hr_ref[...]) * gg_ref[...]
    _tail(pool, lru, on_ref[...], wo_ref, x_ref[...], mod_ref[0], nf_ref, wr_ref, br_ref,
          xo_ref, tok_ref, route_ref, cnt_ref)


def _merge_odd(up, hf, hr, gg, xf, modl, pw_bf, ps, on, wo_bf, nf, wr, br, dims):
    d = xf.shape[1]
    n_tok, w = up.shape
    n_batch, seq = dims
    tps = seq // MERGE_TILE
    n_lat = n_batch * seq
    mod_map, _ = _tile_maps(n_batch * tps, tps, n_batch)
    row = lambda j: (j, 0)
    const = lambda j: (0, 0)
    prev, nxt = _halo_specs(w, n_tok)
    out_specs, out_shape = _tail_specs(d, n_lat)
    return pl.pallas_call(
        functools.partial(_merge_odd_kernel, tiles_per_seq=tps, seq=seq),
        grid=(n_lat // MERGE_TILE,),
        in_specs=[pl.BlockSpec((MERGE_TILE, w), row), prev, nxt,
                  pl.BlockSpec((MERGE_TILE, w), row), pl.BlockSpec((MERGE_TILE, w), row),
                  pl.BlockSpec((MERGE_TILE, w), row),
                  pl.BlockSpec((MERGE_TILE, d), row),
                  pl.BlockSpec((1, SUBLANES, d), mod_map),
                  pl.BlockSpec(pw_bf.shape, lambda j: (0, 0, 0)), pl.BlockSpec((1, w), const),
                  pl.BlockSpec((1, d), const), pl.BlockSpec(wo_bf.shape, const),
                  pl.BlockSpec((1, d), const), pl.BlockSpec(wr.shape, const),
                  pl.BlockSpec((1, LANES), const)],
        out_specs=out_specs, out_shape=out_shape,
        compiler_params=_cparams("arbitrary"),
        name="merge_odd",
    )(up, up, up, hf, hr, gg, xf, modl, pw_bf, ps, on, wo_bf, nf, wr, br)


def _slots_kernel(route_ref, tot_ref, dest_ref, cnt_ref, run_ref, start_ref):
    i = pl.program_id(0)
    rt = route_ref[...]
    m = rt.shape[0]
    lane = lax.broadcasted_iota(jnp.int32, rt.shape, 1).astype(F32)
    oh1 = lane == rt[:, 0:1]
    oh2 = lane == rt[:, 1:2]
    s = jnp.where(oh1 | oh2, 1.0, 0.0)

    @pl.when(i == 0)
    def _():
        tot = tot_ref[0:1, :]
        padded = jnp.floor((tot + (MOE_BLOCK - 1)) / MOE_BLOCK) * MOE_BLOCK
        r = lax.broadcasted_iota(jnp.int32, (LANES, LANES), 0)
        c = lax.broadcasted_iota(jnp.int32, (LANES, LANES), 1)
        upper = jnp.where(r < c, 1.0, 0.0)
        p8 = jnp.broadcast_to(padded, (SUBLANES, LANES))
        start = jnp.dot(p8, upper, precision=HIGHEST, preferred_element_type=F32)[0:1]
        start_ref[...] = start
        run_ref[...] = jnp.zeros_like(run_ref)
        sub = lax.broadcasted_iota(jnp.int32, cnt_ref.shape, 0)
        cnt_ref[...] = jnp.where(sub == 0, tot, jnp.where(sub == 1, start, 0.0))

    r = lax.broadcasted_iota(jnp.int32, (m, m), 0)
    c = lax.broadcasted_iota(jnp.int32, (m, m), 1)
    lower = jnp.where(c < r, 1.0, 0.0).astype(BF16)
    prefix = jnp.dot(lower, s.astype(BF16), preferred_element_type=F32)
    base = prefix + run_ref[...] + start_ref[...]
    d1 = jnp.sum(jnp.where(oh1, base, 0.0), axis=-1, keepdims=True)
    d2 = jnp.sum(jnp.where(oh2, base, 0.0), axis=-1, keepdims=True)
    dest_ref[...] = jnp.where(lane == 0, d1, jnp.where(lane == 1, d2, 0.0)).astype(jnp.int32)
    run_ref[...] += jnp.sum(s, axis=0, keepdims=True)


def _slots(route, totals):
    n_tok = route.shape[0]
    nt = n_tok // SLOT_TILE
    return pl.pallas_call(
        _slots_kernel,
        grid=(nt,),
        in_specs=[pl.BlockSpec((SLOT_TILE, LANES), lambda i: (i, 0)),
                  pl.BlockSpec((SUBLANES, LANES), lambda i: (0, 0))],
        out_specs=[pl.BlockSpec((SLOT_TILE, LANES), lambda i: (i, 0)),
                   pl.BlockSpec((SUBLANES, LANES), lambda i: (0, 0))],
        out_shape=[jax.ShapeDtypeStruct((n_tok, LANES), jnp.int32),
                   jax.ShapeDtypeStruct((SUBLANES, LANES), F32)],
        scratch_shapes=[pltpu.VMEM((1, LANES), F32)] * 2,
        compiler_params=_cparams("arbitrary"),
        name="moe_slots",
    )(route, totals)


def _for_each_row(n_rows, fn):
    def body(g, c):
        g8 = pl.multiple_of(g * SUBLANES, SUBLANES)
        for j in range(SUBLANES):
            fn(g8, j)
        return c
    lax.fori_loop(0, n_rows // SUBLANES, body, 0)


def _dispatch_kernel(dest_ref, pad_lo_ref, pad_hi_ref, tok_ref, xs_ref, zrow, sem, zsem):
    i = pl.program_id(0)
    base = i * SLOT_TILE

    def zero_copy(r, n):
        return pltpu.make_async_copy(zrow.at[pl.ds(0, n)], xs_ref.at[pl.ds(r, n)], zsem)

    def for_each_pad_chunk(fn):
        def per_expert(e, c):
            lo, hi = pad_lo_ref[e], pad_hi_ref[e]
            a8 = (lo + SUBLANES - 1) // SUBLANES * SUBLANES
            a64 = (a8 + ZERO_CHUNK - 1) // ZERO_CHUNK * ZERO_CHUNK

            def rows(r, c):
                fn(zero_copy(r, 1))
                return c

            def eights(q, c):
                fn(zero_copy(pl.multiple_of(a8 + q * SUBLANES, SUBLANES), SUBLANES))
                return c

            def chunks(q, c):
                fn(zero_copy(pl.multiple_of(a64 + q * ZERO_CHUNK, ZERO_CHUNK), ZERO_CHUNK))
                return c

            lax.fori_loop(lo, a8, rows, 0)
            lax.fori_loop(0, (a64 - a8) // SUBLANES, eights, 0)
            lax.fori_loop(0, (hi - a64) // ZERO_CHUNK, chunks, 0)
            return c
        lax.fori_loop(0, N_EXPERTS, per_expert, 0)

    @pl.when(i == 0)
    def _():
        zrow[...] = jnp.zeros_like(zrow)
        for_each_pad_chunk(lambda cp: cp.start())

    def issue(g8, j):
        rows = tok_ref.at[pl.ds(g8, SUBLANES)]
        for k in range(TOP_K):
            d = dest_ref[(base + g8) * TOP_K + (j * TOP_K + k)]
            pltpu.make_async_copy(rows.at[pl.ds(j, 1)], xs_ref.at[pl.ds(d, 1)], sem).start(priority=k)

    _for_each_row(SLOT_TILE, issue)
    for k in range(TOP_K):
        pltpu.make_async_copy(tok_ref, xs_ref.at[pl.ds(0, SLOT_TILE)], sem).wait()

    @pl.when(i == pl.num_programs(0) - 1)
    def _():
        for_each_pad_chunk(lambda cp: cp.wait())


def _dispatch(dest_flat, pad_lo, pad_hi, tok, n_slots):
    n_tok, d = tok.shape
    return pl.pallas_call(
        _dispatch_kernel,
        grid_spec=pltpu.PrefetchScalarGridSpec(
            num_scalar_prefetch=3,
            grid=(n_tok // SLOT_TILE,),
            in_specs=[pl.BlockSpec((SLOT_TILE, d), lambda i, *_: (i, 0))],
            out_specs=pl.BlockSpec(memory_space=pl.ANY),
            scratch_shapes=[pltpu.VMEM((ZERO_CHUNK, d), tok.dtype),
                            pltpu.SemaphoreType.DMA(()), pltpu.SemaphoreType.DMA(())]),
        out_shape=jax.ShapeDtypeStruct((n_slots, d), tok.dtype),
        compiler_params=_cparams("arbitrary"),
        name="moe_dispatch",
    )(dest_flat, pad_lo, pad_hi, tok)


def _expert_kernel(be_ref, nu_ref, x_ref, w1_ref, w3_ref, w2_ref, y_ref, w1b, w3b, w2b):
    i = pl.program_id(0)

    @pl.when(i < nu_ref[0])
    def _():
        changed = (i == 0) | (be_ref[i] != be_ref[jnp.maximum(i - 1, 0)])

        @pl.when(changed)
        def _():
            w1b[...] = w1_ref[0, 0].astype(BF16)
            w3b[...] = w3_ref[0, 0].astype(BF16)
            w2b[...] = w2_ref[0, 0].astype(BF16)

        x_a, x_b = _unpack_bf16_pair(x_ref[...])
        x_a, x_b = x_a.astype(BF16), x_b.astype(BF16)
        half = x_ref.shape[1]
        a = (jnp.dot(x_a, w1b[0:half, :], preferred_element_type=F32)
             + jnp.dot(x_b, w1b[half:, :], preferred_element_type=F32))
        b = (jnp.dot(x_a, w3b[0:half, :], preferred_element_type=F32)
             + jnp.dot(x_b, w3b[half:, :], preferred_element_type=F32))
        y = _bdot(_silu(a) * b, w2b[...])
        y_ref[...] = _pack_bf16_pair(y[:, :half], y[:, half:])


def _experts(block_e, n_used, xs, w1, w3, w2, layer):
    n_slots, half_d = xs.shape
    d, de = w1.shape[-2], w1.shape[-1]
    n_blocks = n_slots // MOE_BLOCK
    blk = lambda i, be, nu: (jnp.minimum(i, nu[0] - 1), 0)
    wmap = lambda i, be, nu: (layer, be[i], 0, 0)
    return pl.pallas_call(
        _expert_kernel,
        grid_spec=pltpu.PrefetchScalarGridSpec(
            num_scalar_prefetch=2,
            grid=(n_blocks,),
            in_specs=[pl.BlockSpec((MOE_BLOCK, half_d), blk),
                      pl.BlockSpec((1, 1, d, de), wmap), pl.BlockSpec((1, 1, d, de), wmap),
                      pl.BlockSpec((1, 1, de, d), wmap)],
            out_specs=pl.BlockSpec((MOE_BLOCK, half_d), blk),
            scratch_shapes=[pltpu.VMEM((d, de), BF16), pltpu.VMEM((d, de), BF16),
                            pltpu.VMEM((de, d), BF16)]),
        out_shape=jax.ShapeDtypeStruct((n_slots, half_d), jnp.uint32),
        compiler_params=_cparams("arbitrary"),
        name="moe_experts",
    )(block_e, n_used, xs, w1, w3, w2)


def _combine_kernel(dest_ref, x_ref, route_ref, mod_ref, ys_ref, *rest, fuse_next):
    if fuse_next:
        nmod_ref, g_ref, w_ref, o_ref, ux_ref, gg_ref, up_ref, yb0, yb1, sem = rest
    else:
        o_ref, yb0, yb1, sem = rest
    i = pl.program_id(0)
    n = pl.num_programs(0)
    bufs = ((yb0, sem.at[0]), (yb1, sem.at[1]))

    def issue(tile, half):
        ybuf, s = bufs[half]
        base = tile * (TOK_TILE * TOP_K)
        for r in range(TOK_TILE):
            for k in range(TOP_K):
                d = dest_ref[base + (r * TOP_K + k)]
                pltpu.make_async_copy(ys_ref.at[pl.ds(d, 1)], ybuf.at[k, pl.ds(r, 1)], s).start(priority=k)

    def wait(half):
        ybuf, s = bufs[half]
        for k in range(TOP_K):
            pltpu.make_async_copy(ys_ref.at[pl.ds(0, TOK_TILE)], ybuf.at[k], s).wait()

    def compute(half):
        ybuf, _ = bufs[half]
        rows = pl.ds(half * TOK_TILE, TOK_TILE)
        rt = route_ref[rows, :]
        w1, w2 = rt[:, 2:3], rt[:, 3:4]
        y1a, y1b = _unpack_bf16_pair(ybuf[0])
        y2a, y2b = _unpack_bf16_pair(ybuf[1])
        y = jnp.concatenate([w1 * y1a + w2 * y2a, w1 * y1b + w2 * y2b], axis=1)
        x1 = x_ref[rows, :] + mod_ref[0][5:6] * y
        o_ref[rows, :] = x1
        if fuse_next:
            _odd_projection(x1, nmod_ref[0], g_ref, w_ref, ux_ref.at[rows], gg_ref.at[rows], up_ref.at[rows])

    @pl.when(i == 0)
    def _():
        issue(0, 0)

    wait(0)
    issue(2 * i + 1, 1)
    compute(0)
    wait(1)
    issue(jnp.minimum(2 * i + 2, 2 * n - 1), 0)
    compute(1)

    @pl.when(i == n - 1)
    def _():
        wait(0)


def _combine(dest_flat, xn, route, modl, ys, dims, next_odd=None):
    n_tok, d = xn.shape
    n_batch, seq = dims
    blk = 2 * TOK_TILE
    assert seq % blk == 0 and n_tok % blk == 0
    tpb = seq // blk
    row = lambda i, dest: (i, 0)
    const = lambda i, dest: (0, 0)
    mod_spec = pl.BlockSpec((1, SUBLANES, d), lambda i, dest: (jnp.minimum(i // tpb, n_batch), 0, 0))
    in_specs = [pl.BlockSpec((blk, d), row), pl.BlockSpec((blk, LANES), row), mod_spec,
                pl.BlockSpec(memory_space=pl.ANY)]
    out_specs = [pl.BlockSpec((blk, d), row)]
    out_shape = [jax.ShapeDtypeStruct((n_tok, d), F32)]
    args = [dest_flat, xn, route, modl, ys]
    if next_odd is not None:
        nmod, g, w_bf = next_odd
        w = w_bf.shape[1] // 3
        in_specs += [mod_spec, pl.BlockSpec((1, d), const), pl.BlockSpec(w_bf.shape, const)]
        out_specs += [pl.BlockSpec((blk, w), row)] * 3
        out_shape += [jax.ShapeDtypeStruct((n_tok, w), F32)] * 3
        args += [nmod, g, w_bf]
    ybuf = pltpu.VMEM((TOP_K, TOK_TILE, ys.shape[1]), ys.dtype)
    return pl.pallas_call(
        functools.partial(_combine_kernel, fuse_next=next_odd is not None),
        grid_spec=pltpu.PrefetchScalarGridSpec(
            num_scalar_prefetch=1,
            grid=(n_tok // blk,),
            in_specs=in_specs, out_specs=out_specs,
            scratch_shapes=[ybuf, ybuf, pltpu.SemaphoreType.DMA((2,))]),
        out_shape=out_shape,
        compiler_params=_cparams("arbitrary"),
        name="moe_combine",
    )(*args)


def _moe(xn, tok, route, totals, modl, w1, w3, w2, layer, dims, next_odd):
    n_tok = tok.shape[0]
    n = n_tok * TOP_K
    n_blocks = (n + N_EXPERTS * (MOE_BLOCK - 1) + MOE_BLOCK - 1) // MOE_BLOCK
    dest, counts = _slots(route, totals)
    dest_flat = dest[:, :TOP_K].reshape(-1)
    cnt = counts[0, :N_EXPERTS].astype(jnp.int32)
    start = counts[1, :N_EXPERTS].astype(jnp.int32)
    blocks_per_e = (cnt + MOE_BLOCK - 1) // MOE_BLOCK
    blk_end = (start + blocks_per_e * MOE_BLOCK) // MOE_BLOCK
    n_used = blk_end[-1:]
    blk = jnp.minimum(jnp.arange(n_blocks, dtype=jnp.int32), n_used[0] - 1)
    block_e = jnp.sum((blk[:, None] >= blk_end[None, :]).astype(jnp.int32), axis=1)
    block_e = jnp.minimum(block_e, N_EXPERTS - 1)
    xs = _dispatch(dest_flat, start + cnt, blk_end * MOE_BLOCK, tok, n_blocks * MOE_BLOCK)
    ys = _experts(block_e, n_used, xs, w1, w3, w2, layer)
    return _combine(dest_flat, xn, route, modl, ys, dims, next_odd)


def _rope_tables(seq):
    rows = seq // GRID_W
    row = jnp.repeat(jnp.arange(rows), GRID_W).astype(F32)
    col = jnp.tile(jnp.arange(GRID_W), rows).astype(F32)
    inv_freq = ROPE_BASE ** (-jnp.arange(ROPE_PAIRS, dtype=F32) / ROPE_PAIRS)
    ar, ac = row[:, None] * inv_freq, col[:, None] * inv_freq
    cos = jnp.concatenate([jnp.cos(ar), jnp.cos(ar), jnp.cos(ac), jnp.cos(ac)], axis=-1)
    sin = jnp.concatenate([-jnp.sin(ar), jnp.sin(ar), -jnp.sin(ac), jnp.sin(ac)], axis=-1)
    reps = LANES // HEAD_DIM
    cos, sin = jnp.tile(cos, (1, reps)), jnp.tile(sin, (1, reps))
    cos = jnp.concatenate([cos, jnp.ones((TOK_TILE, LANES), F32)], axis=0)
    sin = jnp.concatenate([sin, jnp.zeros((TOK_TILE, LANES), F32)], axis=0)
    return cos, sin


def _head_mean_matrix():
    r = jnp.arange(LANES)
    return jnp.where((r[:, None] // HEAD_DIM) == (r[None, :] // HEAD_DIM), 1.0 / HEAD_DIM, 0.0).astype(BF16)


def _block_diag_chunks(w, chunk):
    dirs, nblk, bw, _ = w.shape
    per = chunk // bw
    w = w.reshape(dirs, nblk // per, per, bw, bw)
    eye = jnp.eye(per, dtype=w.dtype)
    out = jnp.einsum("dcpij,pq->dcpiqj", w, eye)
    return out.reshape(dirs, nblk // per, chunk, chunk).astype(BF16)


def _router_matrix(gw, gb, ew, eb):
    d = gw.shape[0]
    pad = LANES - N_EXPERTS - N_GROUPS
    wr = jnp.concatenate([ew, gw, jnp.zeros((d, pad), F32)], axis=1)
    br = jnp.concatenate([eb, gb, jnp.zeros((pad,), F32)])[None, :]
    hi = wr.astype(BF16)
    lo = (wr - hi.astype(F32)).astype(BF16)
    return jnp.concatenate([hi, lo], axis=1), br


def kernel(x, c, ctx, c_ctx, ada_w, ada_b, norm_mix, norm_ffn, out_norm, w_out, w_in_ab, q_norm, k_norm, attn_sink, sconv_w, sconv_b, w_in_cd, pool_w, pool_scale, lru_conv_w, lru_conv_b, lru_wa, lru_ba, lru_wx, lru_bx, lru_lambda, router_gw, router_gb, router_ew, router_eb, exp_w1, exp_w3, exp_w2):
    n_batch, seq, d = x.shape
    ctx_len = ctx.shape[1]
    depth = ada_w.shape[0]
    assert ctx_len == SEQ_TILE and seq % TOK_TILE == 0 and seq % GRID_W == 0
    assert seq % MERGE_TILE == 0 and (n_batch * ctx_len) % MERGE_TILE == 0 and MERGE_TILE % SEQ_TILE == 0
    assert depth == 2, "context-side odd-layer outputs are not implemented"
    dims = (n_batch, seq)
    n_lat = n_batch * seq

    cc = jnp.concatenate([c, c_ctx[None, :], jnp.zeros((SUBLANES - n_batch - 1, d), F32)], axis=0)
    mods = _ada(cc, ada_w, ada_b)
    mods = mods[:, :n_batch + 1].reshape(depth, n_batch + 1, N_MOD, d)
    mods = jnp.pad(mods, ((0, 0), (0, 0), (0, SUBLANES - N_MOD), (0, 0)))

    xparts = (x.reshape(n_lat, d), ctx.reshape(n_batch * ctx_len, d))
    cos_t, sin_t = _rope_tables(seq)
    pm = _head_mean_matrix()

    projected = None
    for i in range(depth):
        need_ctx = i < depth - 1
        j = i // 2
        modl = mods[i]
        n_out = n_lat + n_batch * ctx_len if need_ctx else n_lat
        wr, br = _router_matrix(router_gw[i], router_gb[i], router_ew[i], router_eb[i])
        wo_bf = w_out[i].astype(BF16)
        nm, nf, on = norm_mix[i][None, :], norm_ffn[i][None, :], out_norm[i][None, :]
        if i % 2 == 0:
            reps = LANES // HEAD_DIM
            q, kvx, gb, u = _inproj_even(xparts, modl, nm, w_in_ab[j].astype(BF16),
                                          jnp.tile(q_norm[j], reps)[None, :], jnp.tile(k_norm[j], reps)[None, :],
                                          cos_t, sin_t, pm, dims)
            att = _attention(q, kvx, attn_sink[j], dims, need_ctx)
            xn, tok, route, totals = _merge_even(att, gb, u, xparts, modl, sconv_w[j], sconv_b[j][None, :], on, wo_bf,
                                         nf, wr, br, dims, n_out)
        else:
            if projected is None:
                projected = _inproj_odd(xparts, modl, nm, w_in_cd[j].astype(BF16), dims)
            ux, gg, up = projected
            chunk = 2 * LANES
            hf, hr = _lru(ux, lru_conv_w[j], lru_conv_b[j][None, :],
                          _block_diag_chunks(0.5 * lru_wa[j], chunk), _block_diag_chunks(0.5 * lru_wx[j], chunk),
                          0.5 * lru_ba[j][:, None, :], 0.5 * lru_bx[j][:, None, :],
                          lru_lambda[j][:, None, :], dims)
            xn, tok, route, totals = _merge_odd(up, hf, hr, gg, xparts[0], modl, pool_w[j].astype(BF16),
                                        pool_scale[j][None, :], on, wo_bf, nf, wr, br, dims)
        next_odd = None
        if need_ctx and (i + 1) % 2 == 1:
            next_odd = (mods[i + 1], norm_mix[i + 1][None, :], w_in_cd[(i + 1) // 2].astype(BF16))
        outs = _moe(xn, tok, route, totals, modl, exp_w1, exp_w3, exp_w2, i, dims, next_odd)
        xparts = (outs[0],)
        projected = tuple(outs[1:]) if next_odd is not None else None
    return xparts[0][:n_lat].reshape(n_batch, seq, d)
```
